```python
import jax, jax.numpy as jnp
from jax import lax
import numpy as np

D_MODEL = 2048
BATCH = 16
SEQ = 256
DEPTH = 4
DEC_BATCH = 2
DEC_SEQ = 4096
PAST_LEN = 512

GRID_W = 64
N_EVEN = (DEPTH + 1) // 2
N_ODD = DEPTH // 2
POOL_WIDTH = D_MODEL // 2
POOL_GROUPS = 4
POOL_GC = POOL_WIDTH // POOL_GROUPS
POOL_WINDOWS = (2, 4, 8, 16)
RET_WIDTH = D_MODEL // 2
H_RET = 8
DK_RET = RET_WIDTH // H_RET
DV_RET = RET_WIDTH // H_RET
ROPE_BASE = 10000.0
H_GLA = 4
GLA_DK_TOTAL = D_MODEL // 2
GLA_DV_TOTAL = D_MODEL
DK_GLA = GLA_DK_TOTAL // H_GLA
DV_GLA = GLA_DV_TOTAL // H_GLA
GLA_GATE_RANK = 16
GLA_TAU = 16.0
D_FF = 5632
CONV_W = 3
CHUNK = 64
EPS = 1e-6
EVEN_IN = POOL_WIDTH + 4 * RET_WIDTH
ODD_IN = 2 * GLA_DK_TOTAL + 2 * GLA_DV_TOTAL

kernel_name = "hybrid_pool_retention_gla_diffusion_step"


def _rms_norm(x, g):
    xf = x.astype(jnp.float32)
    xf = xf * lax.rsqrt(jnp.mean(xf * xf, axis=-1, keepdims=True) + EPS)
    return (xf * g.astype(jnp.float32)).astype(x.dtype)


def _head_rms(o, g):
    B, T, H, Dv = o.shape
    of = o.astype(jnp.float32)
    of = of * lax.rsqrt(jnp.mean(of * of, axis=-1, keepdims=True) + EPS)
    return (of.reshape(B, T, H * Dv) * g.astype(jnp.float32)).astype(o.dtype)


def _rope_tables(T):
    rows = T // GRID_W
    r = jnp.repeat(jnp.arange(rows), GRID_W).astype(jnp.float32)
    col = jnp.tile(jnp.arange(GRID_W), rows).astype(jnp.float32)
    nf = DK_RET // 4
    inv = ROPE_BASE ** (-jnp.arange(nf, dtype=jnp.float32) / nf)
    ang = jnp.stack([r[:, None] * inv, col[:, None] * inv], axis=1)
    return jnp.cos(ang), jnp.sin(ang)


def _apply_rope(x, cos, sin):
    B, T, H, Dk = x.shape
    nf = Dk // 4
    xr = x.astype(jnp.float32).reshape(B, T, H, 2, 2, nf)
    x1, x2 = xr[..., 0, :], xr[..., 1, :]
    c = cos[None, :, None]
    s = sin[None, :, None]
    out = jnp.stack([x1 * c - x2 * s, x1 * s + x2 * c], axis=-2)
    return out.reshape(B, T, H, Dk).astype(x.dtype)


def _chunk_scan(q, k, v, logg, s0):
    B, T, H, Dk = q.shape
    Dv = v.shape[-1]
    C = CHUNK
    N = T // C
    qf = q.astype(jnp.float32).reshape(B, N, C, H, Dk)
    kf = k.astype(jnp.float32).reshape(B, N, C, H, Dk)
    vf = v.astype(jnp.float32).reshape(B, N, C, H, Dv)
    lg = logg.astype(jnp.float32).reshape(logg.shape[0], N, C, H, logg.shape[-1])
    G = jnp.cumsum(lg, axis=2)
    Gt = G[:, :, -1:]
    qd = qf * jnp.exp(G)
    kd = kf * jnp.exp(-G)
    kt = kf * jnp.exp(Gt - G)
    mask = jnp.tril(jnp.ones((C, C), dtype=bool))
    A = jnp.einsum('bnihd,bnjhd->bnhij', qd, kd)
    A = jnp.where(mask, A, 0.0)
    o_intra = jnp.einsum('bnhij,bnjhe->bnihe', A, vf)
    dec = jnp.broadcast_to(jnp.exp(Gt[:, :, 0])[..., None], (B, N, H, Dk, 1))

    def step(S, inp):
        dec_n, kt_n, v_n, qd_n = inp
        o_n = jnp.einsum('bchd,bhde->bche', qd_n, S)
        S = dec_n * S + jnp.einsum('bchd,bche->bhde', kt_n, v_n)
        return S, o_n

    xs = (jnp.moveaxis(dec, 1, 0), jnp.moveaxis(kt, 1, 0), jnp.moveaxis(vf, 1, 0), jnp.moveaxis(qd, 1, 0))
    S_fin, o_inter = lax.scan(step, s0.astype(jnp.float32), xs)
    o = o_intra + jnp.moveaxis(o_inter, 0, 1)
    return o.reshape(B, T, H, Dv).astype(v.dtype), S_fin


def _bidir_scan(q, k, v, logg_f, logg_b, s0_f, s0_b):
    flip = lambda a: jnp.flip(a, axis=1)
    o_f, s_f = _chunk_scan(q, k, v, logg_f, s0_f)
    o_b, s_b = _chunk_scan(flip(q), flip(k), flip(v), flip(logg_b), s0_b)
    return o_f + flip(o_b), s_f, s_b


def _pool_mixer(u, w, scale):
    B, T, _ = u.shape
    uf = u.astype(jnp.float32).reshape(B, T, POOL_GROUPS, POOL_GC)
    cs = jnp.pad(jnp.cumsum(uf, axis=1), ((0, 0), (1, 0), (0, 0), (0, 0)))
    t = jnp.arange(T)
    outs = []
    for gi, win in enumerate(POOL_WINDOWS):
        lo = jnp.clip(t - win // 2, 0, T)
        hi = jnp.clip(t + win - win // 2, 0, T)
        csg = cs[:, :, gi]
        s = jnp.take(csg, hi, axis=1) - jnp.take(csg, lo, axis=1)
        outs.append(s / (hi - lo).astype(jnp.float32)[None, :, None])
    pooled = jnp.stack(outs, axis=2) - uf
    y = jnp.einsum('btgc,gcd->btgd', pooled, w.astype(jnp.float32))
    return (y.reshape(B, T, POOL_WIDTH) * scale.astype(jnp.float32)).astype(u.dtype)


def _dwconv3(a, w, b):
    ap = jnp.pad(a, ((0, 0), (1, 1), (0, 0)))
    return ap[:, :-2] * w[0] + ap[:, 1:-1] * w[1] + ap[:, 2:] * w[2] + b


def _trunk(x, cvec, s_ret, s_gla, rope, ada_w, ada_b, norm1_g, norm2_g, even_w_in, pool_w, pool_scale,
           ret_decay, ret_norm_g, even_w_out, odd_w_in, gla_gw1, gla_gw2, gla_gb, gla_norm_g, odd_w_out,
           ffn_w_up, ffn_conv_w, ffn_conv_b, ffn_w_down, final_g):
    B, T, _ = x.shape
    ret_states, gla_states = [], []
    for l in range(DEPTH):
        mod = (jax.nn.silu(cvec) @ ada_w[l] + ada_b[l])[:, None, :]
        sh1, sc1, g1, sh2, sc2, g2 = jnp.split(mod, 6, axis=-1)
        h = _rms_norm(x, norm1_g[l]) * (1 + sc1) + sh1
        if l % 2 == 0:
            i = l // 2
            proj = h @ even_w_in[i]
            u_pool, q, k, v, g = jnp.split(
                proj, [POOL_WIDTH, POOL_WIDTH + RET_WIDTH, POOL_WIDTH + 2 * RET_WIDTH, POOL_WIDTH + 3 * RET_WIDTH], axis=-1)
            q = q.reshape(B, T, H_RET, DK_RET)
            k = k.reshape(B, T, H_RET, DK_RET) * (DK_RET ** -0.5)
            v = v.reshape(B, T, H_RET, DV_RET)
            if rope is not None:
                q = _apply_rope(q, rope[0], rope[1])
                k = _apply_rope(k, rope[0], rope[1])
            logg = jax.nn.log_sigmoid(ret_decay[i].astype(jnp.float32))
            lf = jnp.broadcast_to(logg[0][None, None, :, None], (1, T, H_RET, 1))
            lb = jnp.broadcast_to(logg[1][None, None, :, None], (1, T, H_RET, 1))
            o, sf, sb = _bidir_scan(q, k, v, lf, lb, s_ret[:, i, 0], s_ret[:, i, 1])
            o = _head_rms(o, ret_norm_g[i]) * jax.nn.silu(g)
            mix = jnp.concatenate([_pool_mixer(u_pool, pool_w[i], pool_scale[i]), o], axis=-1) @ even_w_out[i]
            ret_states.append(jnp.stack([sf, sb], axis=1).astype(x.dtype))
        else:
            j = l // 2
            proj = h @ odd_w_in[j]
            q, k, v, r = jnp.split(proj, [GLA_DK_TOTAL, 2 * GLA_DK_TOTAL, 2 * GLA_DK_TOTAL + GLA_DV_TOTAL], axis=-1)
            q = q.reshape(B, T, H_GLA, DK_GLA) * (DK_GLA ** -0.5)
            k = k.reshape(B, T, H_GLA, DK_GLA)
            v = v.reshape(B, T, H_GLA, DV_GLA)
            z = jnp.einsum('btd,xdr->xbtr', h, gla_gw1[j])
            z = jnp.einsum('xbtr,xrk->xbtk', z, gla_gw2[j]) + gla_gb[j][:, None, None, :]
            loga = (jax.nn.log_sigmoid(z.astype(jnp.float32)) / GLA_TAU).reshape(2, B, T, H_GLA, DK_GLA)
            o, sf, sb = _bidir_scan(q, k, v, loga[0], loga[1], s_gla[:, j, 0], s_gla[:, j, 1])
            o = _head_rms(o, gla_norm_g[j]) * jax.nn.silu(r)
            mix = o @ odd_w_out[j]
            gla_states.append(jnp.stack([sf, sb], axis=1).astype(x.dtype))
        x = x + g1 * mix
        h = _rms_norm(x, norm2_g[l]) * (1 + sc2) + sh2
        a, b = jnp.split(h @ ffn_w_up[l], 2, axis=-1)
        a = _dwconv3(a, ffn_conv_w[l], ffn_conv_b[l])
        x = x + g2 * ((jax.nn.silu(a) * b) @ ffn_w_down[l])
    return _rms_norm(x, final_g), jnp.stack(ret_states, axis=1), jnp.stack(gla_states, axis=1)


def setup_inputs(seed: int = 0) -> dict:
    key = jax.random.key(seed)
    ks = jax.random.split(key, 32)
    nrm = lambda k, shape, s: jax.random.normal(k, shape, jnp.float32) * s
    D = D_MODEL
    gam = 1.0 - 2.0 ** (-5.0 - np.arange(H_RET))
    dec_logit = jnp.asarray(np.log(gam / (1.0 - gam)).astype(np.float32))
    return {
        "x_prompt": nrm(ks[0], (BATCH, SEQ, D), 1.0),
        "x_sample": nrm(ks[1], (DEC_BATCH, DEC_SEQ, D), 1.0),
        "state_ret": nrm(ks[2], (DEC_BATCH, N_EVEN, 2, H_RET, DK_RET, DV_RET), 0.5),
        "state_gla": nrm(ks[3], (DEC_BATCH, N_ODD, 2, H_GLA, DK_GLA, DV_GLA), 0.5),
        "c": nrm(ks[4], (DEC_BATCH, D), 1.0),
        "c_ctx": nrm(ks[5], (D,), 1.0),
        "ada_w": nrm(ks[6], (DEPTH, D, 6 * D), 0.5 * D ** -0.5),
        "ada_b": nrm(ks[7], (DEPTH, 6 * D), 0.01),
        "norm1_g": 1.0 + nrm(ks[8], (DEPTH, D), 0.02),
        "norm2_g": 1.0 + nrm(ks[9], (DEPTH, D), 0.02),
        "even_w_in": nrm(ks[10], (N_EVEN, D, EVEN_IN), D ** -0.5),
        "pool_w": nrm(ks[11], (N_EVEN, POOL_GROUPS, POOL_GC, POOL_GC), POOL_GC ** -0.5),
        "pool_scale": 1.0 + nrm(ks[12], (N_EVEN, POOL_WIDTH), 0.02),
        "ret_decay": dec_logit[None, None, :] + nrm(ks[13], (N_EVEN, 2, H_RET), 0.1),
        "ret_norm_g": 1.0 + nrm(ks[14], (N_EVEN, RET_WIDTH), 0.02),
        "even_w_out": nrm(ks[15], (N_EVEN, POOL_WIDTH + RET_WIDTH, D), (POOL_WIDTH + RET_WIDTH) ** -0.5),
        "odd_w_in": nrm(ks[16], (N_ODD, D, ODD_IN), D ** -0.5),
        "gla_gw1": nrm(ks[17], (N_ODD, 2, D, GLA_GATE_RANK), D ** -0.5),
        "gla_gw2": nrm(ks[18], (N_ODD, 2, GLA_GATE_RANK, GLA_DK_TOTAL), GLA_GATE_RANK ** -0.5),
        "gla_gb": nrm(ks[19], (N_ODD, 2, GLA_DK_TOTAL), 0.01),
        "gla_norm_g": 1.0 + nrm(ks[20], (N_ODD, GLA_DV_TOTAL), 0.02),
        "odd_w_out": nrm(ks[21], (N_ODD, GLA_DV_TOTAL, D), GLA_DV_TOTAL ** -0.5),
        "ffn_w_up": nrm(ks[22], (DEPTH, D, 2 * D_FF), D ** -0.5),
        "ffn_conv_w": nrm(ks[23], (DEPTH, CONV_W, D_FF), CONV_W ** -0.5),
        "ffn_conv_b": nrm(ks[24], (DEPTH, D_FF), 0.01),
        "ffn_w_down": nrm(ks[25], (DEPTH, D_FF, D), D_FF ** -0.5),
        "final_g": 1.0 + nrm(ks[26], (D,), 0.02),
    }


def reference(x_prompt, x_sample, state_ret, state_gla, c, c_ctx, ada_w, ada_b, norm1_g, norm2_g,
              even_w_in, pool_w, pool_scale, ret_decay, ret_norm_g, even_w_out, odd_w_in, gla_gw1,
              gla_gw2, gla_gb, gla_norm_g, odd_w_out, ffn_w_up, ffn_conv_w, ffn_conv_b, ffn_w_down, final_g):
    weights = (ada_w, ada_b, norm1_g, norm2_g, even_w_in, pool_w, pool_scale, ret_decay, ret_norm_g,
               even_w_out, odd_w_in, gla_gw1, gla_gw2, gla_gb, gla_norm_g, odd_w_out, ffn_w_up,
               ffn_conv_w, ffn_conv_b, ffn_w_down, final_g)
    Bp = x_prompt.shape[0]
    z_ret = jnp.zeros((Bp, N_EVEN, 2, H_RET, DK_RET, DV_RET), x_prompt.dtype)
    z_gla = jnp.zeros((Bp, N_ODD, 2, H_GLA, DK_GLA, DV_GLA), x_prompt.dtype)
    y_prompt, new_state_ret, new_state_gla = _trunk(x_prompt, c_ctx[None, :], z_ret, z_gla, None, *weights)
    rope = _rope_tables(x_sample.shape[1])
    y_sample, _, _ = _trunk(x_sample, c, state_ret, state_gla, rope, *weights)
    return (y_prompt, y_sample, new_state_ret, new_state_gla)
```

```python
import functools

import numpy as np
import jax
import jax.numpy as jnp
from jax import lax
from jax.experimental import pallas as pl
from jax.experimental.pallas import tpu as pltpu

F32 = jnp.float32
BF16 = jnp.bfloat16

D_MODEL = 2048
BATCH = 16
SEQ = 256
DEPTH = 4
DEC_BATCH = 2
DEC_SEQ = 4096
GRID_W = 64
N_EVEN = (DEPTH + 1) // 2
N_ODD = DEPTH // 2
POOL_WIDTH = D_MODEL // 2
POOL_GROUPS = 4
POOL_GC = POOL_WIDTH // POOL_GROUPS
POOL_WINDOWS = (2, 4, 8, 16)
RET_WIDTH = D_MODEL // 2
H_RET = 8
DK_RET = RET_WIDTH // H_RET
DV_RET = RET_WIDTH // H_RET
ROPE_BASE = 10000.0
H_GLA = 4
GLA_DK_TOTAL = D_MODEL // 2
GLA_DV_TOTAL = D_MODEL
DK_GLA = GLA_DK_TOTAL // H_GLA
DV_GLA = GLA_DV_TOTAL // H_GLA
GLA_GATE_RANK = 16
GLA_TAU = 16.0
D_FF = 5632
CHUNK = 64
EPS = 1e-6
EVEN_IN = POOL_WIDTH + 4 * RET_WIDTH
ODD_IN = 2 * GLA_DK_TOTAL + 2 * GLA_DV_TOTAL

N_PROMPT = BATCH * SEQ
N_SAMPLE = DEC_BATCH * DEC_SEQ
N_TOK = N_PROMPT + N_SAMPLE
N_COND = 1 + DEC_BATCH
COND_PAD = 8

VMEM_LIMIT = 56 * 1024 * 1024

TM = 512
TN_IN = 1024
TN_OUT = 1024
TF = 512
POOL_TM = 256
POOL_HALO = 16
GLA_DVB = 256
ADA_TN = 1024


def _params(*sem):
    return pltpu.CompilerParams(dimension_semantics=sem, vmem_limit_bytes=VMEM_LIMIT)


def _cond_of_tile(i, tm):
    r0 = i * tm
    return jnp.where(r0 < N_PROMPT, 0, 1 + (r0 - N_PROMPT) // DEC_SEQ)


def _log_sigmoid(x):
    return jnp.minimum(x, 0.0) - jnp.log1p(jnp.exp(-jnp.abs(x)))


def _silu(x):
    return x * jax.nn.sigmoid(x)


def _rms(x):
    return x * lax.rsqrt(jnp.mean(x * x, axis=-1, keepdims=True) + EPS)


def _dot(a, b):
    return jnp.dot(a, b, preferred_element_type=F32)


def _dot_nt(a, b):
    return lax.dot_general(a, b, (((1,), (1,)), ((), ())), preferred_element_type=F32)


def _dot_tn(a, b):
    return lax.dot_general(a, b, (((0,), (0,)), ((), ())), preferred_element_type=F32)


def _split_bf16(x):
    hi = x.astype(BF16)
    lo = (x - hi.astype(F32)).astype(BF16)
    return hi, lo


def _ada_kernel(c_ref, w_ref, b_ref, o_ref):
    s = _silu(c_ref[...]).astype(BF16)
    o_ref[0] = _dot(s, w_ref[0].astype(BF16)) + b_ref[0]


def _ada_mod(cvec, ada_w, ada_b):
    n = 6 * D_MODEL
    mod = pl.pallas_call(
        _ada_kernel,
        grid=(DEPTH, n // ADA_TN),
        in_specs=[
            pl.BlockSpec((COND_PAD, D_MODEL), lambda l, j: (0, 0)),
            pl.BlockSpec((1, D_MODEL, ADA_TN), lambda l, j: (l, 0, j)),
            pl.BlockSpec((1, 1, ADA_TN), lambda l, j: (l, 0, j)),
        ],
        out_specs=pl.BlockSpec((1, COND_PAD, ADA_TN), lambda l, j: (l, 0, j)),
        out_shape=jax.ShapeDtypeStruct((DEPTH, COND_PAD, n), F32),
        compiler_params=_params("parallel", "parallel"),
        name="ada_mod",
    )(cvec, ada_w, ada_b.reshape(DEPTH, 1, n))
    mod = mod[:, :N_COND].reshape(DEPTH, N_COND, 6, D_MODEL).transpose(0, 2, 1, 3)
    return mod.reshape(DEPTH * 6 * N_COND, 1, D_MODEL)


def _mod_spec(layer, part, tm):
    base = (layer * 6 + part) * N_COND
    return pl.BlockSpec((1, 1, D_MODEL), lambda i, j: (base + _cond_of_tile(i, tm), 0, 0))


def _in_proj_kernel(x_ref, g_ref, sh_ref, sc_ref, w_ref, o_ref, h_ref):
    @pl.when(pl.program_id(1) == 0)
    def _():
        h = _rms(x_ref[...]) * g_ref[0]
        h_ref[...] = (h * (1.0 + sc_ref[0]) + sh_ref[0]).astype(BF16)

    o_ref[...] = _dot(h_ref[...], w_ref[...])


def _in_proj(x, norm_g, mod, layer, w, tn):
    m, n = x.shape[0], w.shape[1]
    return pl.pallas_call(
        _in_proj_kernel,
        grid=(m // TM, n // tn),
        in_specs=[
            pl.BlockSpec((TM, D_MODEL), lambda i, j: (i, 0)),
            pl.BlockSpec((1, 1, D_MODEL), lambda i, j: (layer, 0, 0)),
            _mod_spec(layer, 0, TM),
            _mod_spec(layer, 1, TM),
            pl.BlockSpec((D_MODEL, tn), lambda i, j: (0, j)),
        ],
        out_specs=pl.BlockSpec((TM, tn), lambda i, j: (i, j)),
        out_shape=jax.ShapeDtypeStruct((m, n), F32),
        scratch_shapes=[pltpu.VMEM((TM, D_MODEL), BF16)],
        compiler_params=_params("parallel", "arbitrary"),
        name="in_proj",
    )(x, norm_g, mod, mod, w)


def _pool_bands():
    t = POOL_TM
    b0 = np.zeros((POOL_GROUPS, t, t), np.float32)
    bp = np.zeros((POOL_GROUPS, t, POOL_HALO), np.float32)
    bn = np.zeros((POOL_GROUPS, t, POOL_HALO), np.float32)
    for g, win in enumerate(POOL_WINDOWS):
        for r in range(t):
            for s in range(r - win // 2, r + win - win // 2):
                if s < 0:
                    bp[g, r, s + POOL_HALO] = 1.0
                elif s >= t:
                    bn[g, r, s - t] = 1.0
                else:
                    b0[g, r, s] = 1.0
    return jnp.asarray(b0, BF16), jnp.asarray(bp, BF16), jnp.asarray(bn, BF16)


def _pool_kernel(u_ref, up_ref, un_ref, b0_ref, bp_ref, bn_ref, pw_ref, sc_ref, o_ref):
    i = pl.program_id(0)
    tiles_per_seq = DEC_SEQ // POOL_TM
    in_prompt = i < N_PROMPT // POOL_TM
    pos = (i - N_PROMPT // POOL_TM) % tiles_per_seq
    is_start = jnp.logical_or(in_prompt, pos == 0)
    is_end = jnp.logical_or(in_prompt, pos == tiles_per_seq - 1)
    t = lax.broadcasted_iota(jnp.int32, (POOL_TM, POOL_GC), 0)
    for g, win in enumerate(POOL_WINDOWS):
        cols = slice(g * POOL_GC, (g + 1) * POOL_GC)
        u = u_ref[:, cols]
        up = jnp.where(is_start, 0.0, up_ref[:, cols])
        un = jnp.where(is_end, 0.0, un_ref[:, cols])
        s = jnp.zeros((POOL_TM, POOL_GC), F32)
        for band, val in ((b0_ref, u), (bp_ref, up), (bn_ref, un)):
            hi, lo = _split_bf16(val)
            s = s + _dot(band[g], hi) + _dot(band[g], lo)
        cut_lo = jnp.where(is_start, jnp.maximum(win // 2 - t, 0), 0)
        cut_hi = jnp.where(is_end, jnp.maximum(t + (win - win // 2) - POOL_TM, 0), 0)
        cnt = (win - cut_lo - cut_hi).astype(F32)
        pooled = s / cnt - u
        y = _dot(pooled.astype(BF16), pw_ref[g])
        o_ref[:, cols] = y * sc_ref[:, cols]


def _pool_mixer(proj, pool_w, pool_scale):
    b0, bp, bn = _pool_bands()
    halo_blocks = POOL_TM // POOL_HALO
    last = N_TOK // POOL_HALO - 1
    full = lambda shape: pl.BlockSpec(shape, lambda i: (0,) * len(shape))
    return pl.pallas_call(
        _pool_kernel,
        grid=(N_TOK // POOL_TM,),
        in_specs=[
            pl.BlockSpec((POOL_TM, POOL_WIDTH), lambda i: (i, 0)),
            pl.BlockSpec((POOL_HALO, POOL_WIDTH), lambda i: (jnp.maximum(i * halo_blocks - 1, 0), 0)),
            pl.BlockSpec((POOL_HALO, POOL_WIDTH), lambda i: (jnp.minimum((i + 1) * halo_blocks, last), 0)),
            full(b0.shape), full(bp.shape), full(bn.shape),
            full(pool_w.shape),
            full((1, POOL_WIDTH)),
        ],
        out_specs=pl.BlockSpec((POOL_TM, POOL_WIDTH), lambda i: (i, 0)),
        out_shape=jax.ShapeDtypeStruct((N_TOK, POOL_WIDTH), F32),
        compiler_params=_params("parallel"),
        name="pool_mixer",
    )(proj, proj, proj, b0, bp, bn, pool_w, pool_scale.reshape(1, POOL_WIDTH))


def _rope_tables(t):
    nf = DK_RET // 4
    rows = t // GRID_W
    r = jnp.repeat(jnp.arange(rows), GRID_W).astype(F32)
    col = jnp.tile(jnp.arange(GRID_W), rows).astype(F32)
    inv = ROPE_BASE ** (-jnp.arange(nf, dtype=F32) / nf)
    ar, ac = r[:, None] * inv, col[:, None] * inv
    cos = jnp.concatenate([jnp.cos(ar), jnp.cos(ar), jnp.cos(ac), jnp.cos(ac)], axis=1)
    sin = jnp.concatenate([-jnp.sin(ar), jnp.sin(ar), -jnp.sin(ac), jnp.sin(ac)], axis=1)
    return cos, sin


def _ret_kernel(*refs, n_chunks, use_rope, has_s0, emit_state):
    refs = list(refs)
    dec_ref, q_ref, k_ref, v_ref = refs[:4]
    refs = refs[4:]
    if use_rope:
        cos_ref, sin_ref = refs[:2]
        refs = refs[2:]
    if has_s0:
        s0_ref = refs[0]
        refs = refs[1:]
    o_ref = refs[0]
    if emit_state:
        st_ref = refs[1]

    h = pl.program_id(1)
    c = CHUNK
    nf = DK_RET // 4

    def lam(d, shape):
        return _log_sigmoid(jnp.full(shape, dec_ref[d, h], F32))

    row = lax.broadcasted_iota(jnp.int32, (c, DK_RET), 0).astype(F32)
    lam_f, lam_b = lam(0, (c, DK_RET)), lam(1, (c, DK_RET))
    dq_f = jnp.exp((row + 1.0) * lam_f)
    dk_f = jnp.exp((c - 1.0 - row) * lam_f)
    dq_b = jnp.exp((c - row) * lam_b)
    dk_b = jnp.exp(row * lam_b)
    cdec_f = jnp.exp(float(c) * lam(0, (DK_RET, DV_RET)))
    cdec_b = jnp.exp(float(c) * lam(1, (DK_RET, DV_RET)))
    ii = lax.broadcasted_iota(jnp.int32, (c, c), 0)
    jj = lax.broadcasted_iota(jnp.int32, (c, c), 1)
    dist = (ii - jj).astype(F32)
    dmat = jnp.where(ii > jj, jnp.exp(dist * lam(0, (c, c))),
                     jnp.where(ii < jj, jnp.exp(-dist * lam(1, (c, c))), 2.0))
    lane = lax.broadcasted_iota(jnp.int32, (c, DK_RET), 1)
    first_half = (lane % (2 * nf)) < nf

    def rope(x, r0):
        if not use_rope:
            return x
        partner = jnp.where(first_half, pltpu.roll(x, DK_RET - nf, axis=1), pltpu.roll(x, nf, axis=1))
        return x * cos_ref[pl.ds(r0, c), :] + partner * sin_ref[pl.ds(r0, c), :]

    def load(n):
        r0 = pl.multiple_of(n * c, c)
        q = rope(q_ref[pl.ds(r0, c), :], r0)
        k = rope(k_ref[pl.ds(r0, c), :] * (DK_RET ** -0.5), r0)
        return r0, q, k, v_ref[pl.ds(r0, c), :].astype(BF16)

    def fwd(n, s):
        r0, q, k, vb = load(n)
        a = _dot_nt(q.astype(BF16), k.astype(BF16)) * dmat
        o = _dot(a.astype(BF16), vb) + _dot((q * dq_f).astype(BF16), s.astype(BF16))
        o_ref[pl.ds(r0, c), :] = o
        return cdec_f * s + _dot_tn((k * dk_f).astype(BF16), vb)

    def bwd(m, s):
        r0, q, k, vb = load(n_chunks - 1 - m)
        o_ref[pl.ds(r0, c), :] += _dot((q * dq_b).astype(BF16), s.astype(BF16))
        return cdec_b * s + _dot_tn((k * dk_b).astype(BF16), vb)

    zero = jnp.zeros((DK_RET, DV_RET), F32)
    s_f = lax.fori_loop(0, n_chunks, fwd, s0_ref[0, 0, 0, 0] if has_s0 else zero)
    s_b = lax.fori_loop(0, n_chunks, bwd, s0_ref[0, 0, 1, 0] if has_s0 else zero)
    if emit_state:
        st_ref[0, 0, 0] = s_f
        st_ref[0, 1, 0] = s_b


def _ret_scan(proj, decay, t, batch, row_block0, rope, s0, s0_layer, emit_state):
    col0 = POOL_WIDTH // DK_RET
    in_specs = [
        pl.BlockSpec(memory_space=pltpu.SMEM),
        pl.BlockSpec((t, DK_RET), lambda b, h: (row_block0 + b, col0 + h)),
        pl.BlockSpec((t, DK_RET), lambda b, h: (row_block0 + b, col0 + H_RET + h)),
        pl.BlockSpec((t, DV_RET), lambda b, h: (row_block0 + b, col0 + 2 * H_RET + h)),
    ]
    args = [decay, proj, proj, proj]
    if rope is not None:
        in_specs += [pl.BlockSpec((t, DK_RET), lambda b, h: (0, 0))] * 2
        args += list(rope)
    if s0 is not None:
        in_specs.append(pl.BlockSpec((1, 1, 2, 1, DK_RET, DV_RET), lambda b, h: (b, s0_layer, 0, h, 0, 0)))
        args.append(s0)
    out_specs = [pl.BlockSpec((t, DV_RET), lambda b, h: (b, h))]
    out_shape = [jax.ShapeDtypeStruct((batch * t, RET_WIDTH), F32)]
    if emit_state:
        out_specs.append(pl.BlockSpec((1, 2, 1, DK_RET, DV_RET), lambda b, h: (b, 0, h, 0, 0)))
        out_shape.append(jax.ShapeDtypeStruct((batch, 2, H_RET, DK_RET, DV_RET), F32))
    return pl.pallas_call(
        functools.partial(_ret_kernel, n_chunks=t // CHUNK, use_rope=rope is not None,
                          has_s0=s0 is not None, emit_state=emit_state),
        grid=(batch, H_RET),
        in_specs=in_specs,
        out_specs=out_specs,
        out_shape=out_shape,
        compiler_params=_params("parallel", "parallel"),
        name="ret_scan",
    )(*args)


def _gla_kernel(*refs, n_chunks, has_s0, emit_state):
    refs = list(refs)
    q_ref, k_ref, v_ref, z_ref, g2_ref, gb_ref = refs[:6]
    refs = refs[6:]
    if has_s0:
        s0_ref = refs[0]
        refs = refs[1:]
    o_ref = refs[0]
    refs = refs[1:]
    if emit_state:
        st_ref = refs[0]
        refs = refs[1:]
    sf_ref, sb_ref = refs

    c = CHUNK
    ii = lax.broadcasted_iota(jnp.int32, (c, c), 0)
    jj = lax.broadcasted_iota(jnp.int32, (c, c), 1)
    lower = ii >= jj
    tri_f = jnp.where(lower, 1.0, 0.0).astype(BF16)
    tri_b = jnp.where(ii <= jj, 1.0, 0.0).astype(BF16)
    ones = jnp.ones((c, GLA_DVB), BF16)

    def step(n, d, tri, keep, s_ref, first):
        r0 = pl.multiple_of(n * c, c)
        q = q_ref[pl.ds(r0, c), :] * (DK_GLA ** -0.5)
        k = k_ref[pl.ds(r0, c), :]
        vb = v_ref[pl.ds(r0, c), :].astype(BF16)
        z = _dot(z_ref[pl.ds(r0, c), :].astype(BF16), g2_ref[0, d]) + gb_ref[0, d]
        lg_hi, lg_lo = _split_bf16(_log_sigmoid(z) * (1.0 / GLA_TAU))
        g = _dot(tri, lg_hi) + _dot(tri, lg_lo)
        g_tot = g[c - 1:c, :] if d == 0 else g[0:1, :]
        g_tot_col = _dot_tn(lg_hi, ones) + _dot_tn(lg_lo, ones)
        qd = (q * jnp.exp(g)).astype(BF16)
        kd = (k * jnp.exp(-g)).astype(BF16)
        kt = (k * jnp.exp(g_tot - g)).astype(BF16)
        a = jnp.where(keep, _dot_nt(qd, kd), 0.0).astype(BF16)
        s = s_ref[...]
        o = _dot(a, vb) + _dot(qd, s.astype(BF16))
        if first:
            o_ref[pl.ds(r0, c), :] = o
        else:
            o_ref[pl.ds(r0, c), :] += o
        s_ref[...] = jnp.exp(g_tot_col) * s + _dot_tn(kt, vb)

    zero = jnp.zeros((DK_GLA, GLA_DVB), F32)
    sf_ref[...] = s0_ref[0, 0, 0, 0] if has_s0 else zero
    sb_ref[...] = s0_ref[0, 0, 1, 0] if has_s0 else zero

    def fwd(n, carry):
        step(n, 0, tri_f, lower, sf_ref, True)
        return carry

    def bwd(m, carry):
        step(n_chunks - 1 - m, 1, tri_b, ii <= jj, sb_ref, False)
        return carry

    lax.fori_loop(0, n_chunks, fwd, 0)
    lax.fori_loop(0, n_chunks, bwd, 0)
    if emit_state:
        st_ref[0, 0, 0] = sf_ref[...]
        st_ref[0, 1, 0] = sb_ref[...]


def _gla_scan(proj, z1, g2, gb, t, batch, row_block0, s0, s0_layer, emit_state):
    nd = DV_GLA // GLA_DVB
    kcol = GLA_DK_TOTAL // DK_GLA
    vcol = 2 * GLA_DK_TOTAL // GLA_DVB
    in_specs = [
        pl.BlockSpec((t, DK_GLA), lambda b, h, d: (row_block0 + b, h)),
        pl.BlockSpec((t, DK_GLA), lambda b, h, d: (row_block0 + b, kcol + h)),
        pl.BlockSpec((t, GLA_DVB), lambda b, h, d: (row_block0 + b, vcol + h * nd + d)),
        pl.BlockSpec((t, 128), lambda b, h, d: (row_block0 + b, 0)),
        pl.BlockSpec((1, 2, 128, DK_GLA), lambda b, h, d: (h, 0, 0, 0)),
        pl.BlockSpec((1, 2, 1, DK_GLA), lambda b, h, d: (h, 0, 0, 0)),
    ]
    args = [proj, proj, proj, z1, g2, gb]
    if s0 is not None:
        in_specs.append(pl.BlockSpec((1, 1, 2, 1, DK_GLA, GLA_DVB), lambda b, h, d: (b, s0_layer, 0, h, 0, d)))
        args.append(s0)
    out_specs = [pl.BlockSpec((t, GLA_DVB), lambda b, h, d: (b, h * nd + d))]
    out_shape = [jax.ShapeDtypeStruct((batch * t, GLA_DV_TOTAL), F32)]
    if emit_state:
        out_specs.append(pl.BlockSpec((1, 2, 1, DK_GLA, GLA_DVB), lambda b, h, d: (b, 0, h, 0, d)))
        out_shape.append(jax.ShapeDtypeStruct((batch, 2, H_GLA, DK_GLA, DV_GLA), F32))
    return pl.pallas_call(
        functools.partial(_gla_kernel, n_chunks=t // CHUNK, has_s0=s0 is not None, emit_state=emit_state),
        grid=(batch, H_GLA, nd),
        in_specs=in_specs,
        out_specs=out_specs,
        out_shape=out_shape,
        scratch_shapes=[pltpu.VMEM((DK_GLA, GLA_DVB), F32)] * 2,
        compiler_params=_params("parallel", "parallel", "arbitrary"),
        name="gla_scan",
    )(*args)


def _gated_heads(o_ref, gate_ref, gn_ref, a_ref, col0, n_heads, width):
    for hd in range(n_heads):
        cols = slice(hd * width, (hd + 1) * width)
        gt = gate_ref[:, cols]
        val = _rms(o_ref[:, cols]) * gn_ref[:, cols] * _silu(gt)
        a_ref[:, col0 + hd * width:col0 + (hd + 1) * width] = val.astype(BF16)


def _out_even_kernel(yp_ref, o_ref, gate_ref, gn_ref, w_ref, x_ref, g1_ref, out_ref, a_ref):
    @pl.when(pl.program_id(1) == 0)
    def _():
        a_ref[:, :POOL_WIDTH] = yp_ref[...].astype(BF16)
        _gated_heads(o_ref, gate_ref, gn_ref, a_ref, POOL_WIDTH, H_RET, DV_RET)

    out_ref[...] = x_ref[...] + g1_ref[0] * _dot(a_ref[...], w_ref[...])


def _out_odd_kernel(o_ref, gate_ref, gn_ref, w_ref, x_ref, g1_ref, out_ref, a_ref):
    @pl.when(pl.program_id(1) == 0)
    def _():
        _gated_heads(o_ref, gate_ref, gn_ref, a_ref, 0, H_GLA, DV_GLA)

    out_ref[...] = x_ref[...] + g1_ref[0] * _dot(a_ref[...], w_ref[...])


def _out_proj(x, mod, layer, w, gate_proj, gate_col_block, norm_g, o_raw, y_pool=None):
    m = x.shape[0]
    tn = TN_OUT
    width = o_raw.shape[1]
    row = lambda wd: pl.BlockSpec((TM, wd), lambda i, j: (i, 0))
    in_specs = [
        row(width),
        pl.BlockSpec((TM, width), lambda i, j: (i, gate_col_block)),
        pl.BlockSpec((1, width), lambda i, j: (0, 0)),
        pl.BlockSpec((D_MODEL, tn), lambda i, j: (0, j)),
        pl.BlockSpec((TM, tn), lambda i, j: (i, j)),
        pl.BlockSpec((1, 1, tn), lambda i, j: ((layer * 6 + 2) * N_COND + _cond_of_tile(i, TM), 0, j)),
    ]
    args = [o_raw, gate_proj, norm_g.reshape(1, width), w, x, mod]
    body = _out_odd_kernel
    if y_pool is not None:
        in_specs = [row(POOL_WIDTH)] + in_specs
        args = [y_pool] + args
        body = _out_even_kernel
    return pl.pallas_call(
        body,
        grid=(m // TM, D_MODEL // tn),
        in_specs=in_specs,
        out_specs=pl.BlockSpec((TM, tn), lambda i, j: (i, j)),
        out_shape=jax.ShapeDtypeStruct((m, D_MODEL), F32),
        scratch_shapes=[pltpu.VMEM((TM, D_MODEL), BF16)],
        compiler_params=_params("parallel", "arbitrary"),
        name="out_proj",
    )(*args)


def _ffn_kernel(x_ref, xp_ref, xn_ref, gn_ref, sh_ref, sc_ref, gt_ref, wa_ref, wb_ref, cw_ref, cb_ref,
                wd_ref, fg_ref, o_ref, h_ref, hh_ref, acc_ref, *, final):
    i = pl.program_id(0)
    f = pl.program_id(1)

    def modnorm(x):
        return (_rms(x) * gn_ref[0] * (1.0 + sc_ref[0]) + sh_ref[0]).astype(BF16)

    @pl.when(f == 0)
    def _():
        h_ref[...] = modnorm(x_ref[...])
        hh_ref[...] = modnorm(jnp.concatenate([xp_ref[...], xn_ref[...]], axis=0))
        acc_ref[...] = jnp.zeros_like(acc_ref)

    a = _dot(h_ref[...], wa_ref[...])
    b = _dot(h_ref[...], wb_ref[...])
    a_halo = _dot(hh_ref[...], wa_ref[...])

    seq = jnp.where(i * TM < N_PROMPT, SEQ, DEC_SEQ)
    t = lax.broadcasted_iota(jnp.int32, (TM, TF), 0)
    pos = (i * TM + t) & (seq - 1)
    a_prev = jnp.where(t == 0, a_halo[7:8, :], pltpu.roll(a, 1, axis=0))
    a_prev = jnp.where(pos == 0, 0.0, a_prev)
    a_next = jnp.where(t == TM - 1, a_halo[8:9, :], pltpu.roll(a, TM - 1, axis=0))
    a_next = jnp.where(pos == seq - 1, 0.0, a_next)
    conv = a_prev * cw_ref[0:1, :] + a * cw_ref[1:2, :] + a_next * cw_ref[2:3, :] + cb_ref[...]
    acc_ref[...] += _dot((_silu(conv) * b).astype(BF16), wd_ref[...])

    @pl.when(f == pl.num_programs(1) - 1)
    def _():
        y = x_ref[...] + gt_ref[0] * acc_ref[...]
        if final:
            y = _rms(y) * fg_ref[...]
        o_ref[...] = y


def _ffn(x, mod, layer, norm_g, w_up, conv_w, conv_b, w_down, final_g, final):
    m = x.shape[0]
    nf = D_FF // TF
    halo = 8
    last = m // halo - 1
    return pl.pallas_call(
        functools.partial(_ffn_kernel, final=final),
        grid=(m // TM, nf),
        in_specs=[
            pl.BlockSpec((TM, D_MODEL), lambda i, f: (i, 0)),
            pl.BlockSpec((halo, D_MODEL), lambda i, f: (jnp.maximum(i * (TM // halo) - 1, 0), 0)),
            pl.BlockSpec((halo, D_MODEL), lambda i, f: (jnp.minimum((i + 1) * (TM // halo), last), 0)),
            pl.BlockSpec((1, 1, D_MODEL), lambda i, f: (layer, 0, 0)),
            _mod_spec(layer, 3, TM),
            _mod_spec(layer, 4, TM),
            _mod_spec(layer, 5, TM),
            pl.BlockSpec((D_MODEL, TF), lambda i, f: (0, f)),
            pl.BlockSpec((D_MODEL, TF), lambda i, f: (0, nf + f)),
            pl.BlockSpec((3, TF), lambda i, f: (0, f)),
            pl.BlockSpec((1, TF), lambda i, f: (0, f)),
            pl.BlockSpec((TF, D_MODEL), lambda i, f: (f, 0)),
            pl.BlockSpec((1, D_MODEL), lambda i, f: (0, 0)),
        ],
        out_specs=pl.BlockSpec((TM, D_MODEL), lambda i, f: (i, 0)),
        out_shape=jax.ShapeDtypeStruct((m, D_MODEL), F32),
        scratch_shapes=[
            pltpu.VMEM((TM, D_MODEL), BF16),
            pltpu.VMEM((2 * halo, D_MODEL), BF16),
            pltpu.VMEM((TM, D_MODEL), F32),
        ],
        compiler_params=_params("parallel", "arbitrary"),
        name="ffn",
    )(x, x, x, norm_g, mod, mod, mod, w_up, w_up, conv_w, conv_b.reshape(1, D_FF), w_down,
      final_g.reshape(1, D_MODEL))


def _gla_gate_weights(gw2, gb):
    r = GLA_GATE_RANK
    g2 = jnp.zeros((2, 128, GLA_DK_TOTAL), F32)
    g2 = g2.at[0, 0:r].set(gw2[0]).at[1, r:2 * r].set(gw2[1])
    g2 = g2.reshape(2, 128, H_GLA, DK_GLA).transpose(2, 0, 1, 3).astype(BF16)
    gbh = gb.reshape(2, H_GLA, 1, DK_GLA).transpose(1, 0, 2, 3)
    return g2, gbh


def kernel(x_prompt, x_sample, state_ret, state_gla, c, c_ctx, ada_w, ada_b, norm1_g, norm2_g,
           even_w_in, pool_w, pool_scale, ret_decay, ret_norm_g, even_w_out, odd_w_in, gla_gw1,
           gla_gw2, gla_gb, gla_norm_g, odd_w_out, ffn_w_up, ffn_conv_w, ffn_conv_b, ffn_w_down, final_g):
    x = jnp.concatenate([x_prompt.reshape(N_PROMPT, D_MODEL), x_sample.reshape(N_SAMPLE, D_MODEL)], axis=0)
    cvec = jnp.concatenate([c_ctx[None, :], c, jnp.zeros((COND_PAD - N_COND, D_MODEL), F32)], axis=0)
    mod = _ada_mod(cvec, ada_w, ada_b)
    n1 = norm1_g.reshape(DEPTH, 1, D_MODEL)
    n2 = norm2_g.reshape(DEPTH, 1, D_MODEL)
    rope = _rope_tables(DEC_SEQ)
    p_blocks = N_PROMPT // DEC_SEQ

    ret_states, gla_states = [], []
    for l in range(DEPTH):
        if l % 2 == 0:
            i = l // 2
            proj = _in_proj(x, n1, mod, l, even_w_in[i].astype(BF16), TN_IN)
            y_pool = _pool_mixer(proj, pool_w[i].astype(BF16), pool_scale[i])
            o_p, st = _ret_scan(proj, ret_decay[i], SEQ, BATCH, 0, None, None, 0, True)
            (o_s,) = _ret_scan(proj, ret_decay[i], DEC_SEQ, DEC_BATCH, p_blocks, rope, state_ret, i, False)
            o_raw = jnp.concatenate([o_p, o_s], axis=0)
            ret_states.append(st)
            x = _out_proj(x, mod, l, even_w_out[i].astype(BF16), proj, EVEN_IN // RET_WIDTH - 1,
                          ret_norm_g[i], o_raw, y_pool)
        else:
            j = l // 2
            proj = _in_proj(x, n1, mod, l, odd_w_in[j].astype(BF16), TN_IN)
            gw1 = jnp.concatenate([gla_gw1[j, 0], gla_gw1[j, 1],
                                   jnp.zeros((D_MODEL, 128 - 2 * GLA_GATE_RANK), F32)], axis=1)
            z1 = _in_proj(x, n1, mod, l, gw1.astype(BF16), 128)
            g2, gbh = _gla_gate_weights(gla_gw2[j], gla_gb[j])
            o_p, st = _gla_scan(proj, z1, g2, gbh, SEQ, BATCH, 0, None, 0, True)
            (o_s,) = _gla_scan(proj, z1, g2, gbh, DEC_SEQ, DEC_BATCH, p_blocks, state_gla, j, False)
            o_raw = jnp.concatenate([o_p, o_s], axis=0)
            gla_states.append(st)
            x = _out_proj(x, mod, l, odd_w_out[j].astype(BF16), proj, ODD_IN // GLA_DV_TOTAL - 1,
                          gla_norm_g[j], o_raw)
        x = _ffn(x, mod, l, n2, ffn_w_up[l].astype(BF16), ffn_conv_w[l], ffn_conv_b[l],
                 ffn_w_down[l].astype(BF16), final_g, l == DEPTH - 1)

    y_prompt = x[:N_PROMPT].reshape(BATCH, SEQ, D_MODEL)
    y_sample = x[N_PROMPT:].reshape(DEC_BATCH, DEC_SEQ, D_MODEL)
    return (y_prompt, y_sample, jnp.stack(ret_states, axis=1), jnp.stack(gla_states, axis=1))
```

```python
import functools

import numpy as np
import jax
import jax.numpy as jnp
from jax import lax
from jax.experimental import pallas as pl
from jax.experimental.pallas import tpu as pltpu

F32 = jnp.float32
BF16 = jnp.bfloat16

D_MODEL = 2048
BATCH = 16
SEQ = 256
DEPTH = 4
DEC_BATCH = 2
DEC_SEQ = 4096
GRID_W = 64
N_EVEN = (DEPTH + 1) // 2
N_ODD = DEPTH // 2
POOL_WIDTH = D_MODEL // 2
POOL_GROUPS = 4
POOL_GC = POOL_WIDTH // POOL_GROUPS
POOL_WINDOWS = (2, 4, 8, 16)
RET_WIDTH = D_MODEL // 2
H_RET = 8
DK_RET = RET_WIDTH // H_RET
DV_RET = RET_WIDTH // H_RET
ROPE_BASE = 10000.0
H_GLA = 4
GLA_DK_TOTAL = D_MODEL // 2
GLA_DV_TOTAL = D_MODEL
DK_GLA = GLA_DK_TOTAL // H_GLA
DV_GLA = GLA_DV_TOTAL // H_GLA
GLA_GATE_RANK = 16
GLA_TAU = 16.0
D_FF = 5632
CHUNK = 64
EPS = 1e-6
EVEN_IN = POOL_WIDTH + 4 * RET_WIDTH
ODD_IN = 2 * GLA_DK_TOTAL + 2 * GLA_DV_TOTAL

N_PROMPT = BATCH * SEQ
N_SAMPLE = DEC_BATCH * DEC_SEQ
N_TOK = N_PROMPT + N_SAMPLE
N_COND = 1 + DEC_BATCH
COND_PAD = 8

VMEM_LIMIT = 56 * 1024 * 1024

TM = 512
TN_IN = 1024
TN_OUT = 1024
TF = 512
POOL_TM = 256
POOL_HALO = 16
ADA_TN = 1024
RET_CHUNK = 256
GLA_BLK = 256
GLA_DVB_SAMPLE = 256


def _params(*sem):
    return pltpu.CompilerParams(dimension_semantics=sem, vmem_limit_bytes=VMEM_LIMIT)


def _cond_of_tile(i, tm):
    r0 = i * tm
    return jnp.where(r0 < N_PROMPT, 0, 1 + (r0 - N_PROMPT) // DEC_SEQ)


def _log_sigmoid(x):
    return jnp.minimum(x, 0.0) - jnp.log1p(jnp.exp(-jnp.abs(x)))


def _silu(x):
    return x * jax.nn.sigmoid(x)


def _rms(x):
    return x * lax.rsqrt(jnp.mean(x * x, axis=-1, keepdims=True) + EPS)


def _dot(a, b):
    return jnp.dot(a, b, preferred_element_type=F32)


def _dot_nt(a, b):
    return lax.dot_general(a, b, (((1,), (1,)), ((), ())), preferred_element_type=F32)


def _dot_tn(a, b):
    return lax.dot_general(a, b, (((0,), (0,)), ((), ())), preferred_element_type=F32)


def _split_bf16(x):
    hi = x.astype(BF16)
    lo = (x - hi.astype(F32)).astype(BF16)
    return hi, lo


def _loop(n, body, init):
    if n == 1:
        return body(0, init)
    return lax.fori_loop(0, n, body, init)


def _ada_kernel(c_ref, w_ref, b_ref, o_ref):
    s = _silu(c_ref[...]).astype(BF16)
    o_ref[0] = _dot(s, w_ref[0].astype(BF16)) + b_ref[0]


def _ada_mod(cvec, ada_w, ada_b):
    n = 6 * D_MODEL
    mod = pl.pallas_call(
        _ada_kernel,
        grid=(DEPTH, n // ADA_TN),
        in_specs=[
            pl.BlockSpec((COND_PAD, D_MODEL), lambda l, j: (0, 0)),
            pl.BlockSpec((1, D_MODEL, ADA_TN), lambda l, j: (l, 0, j)),
            pl.BlockSpec((1, 1, ADA_TN), lambda l, j: (l, 0, j)),
        ],
        out_specs=pl.BlockSpec((1, COND_PAD, ADA_TN), lambda l, j: (l, 0, j)),
        out_shape=jax.ShapeDtypeStruct((DEPTH, COND_PAD, n), F32),
        compiler_params=_params("parallel", "parallel"),
        name="ada_mod",
    )(cvec, ada_w, ada_b.reshape(DEPTH, 1, n))
    mod = mod[:, :N_COND].reshape(DEPTH, N_COND, 6, D_MODEL).transpose(0, 2, 1, 3)
    return mod.reshape(DEPTH * 6 * N_COND, 1, D_MODEL)


def _mod_spec(layer, part, tm):
    base = (layer * 6 + part) * N_COND
    return pl.BlockSpec((1, 1, D_MODEL), lambda i, j: (base + _cond_of_tile(i, tm), 0, 0))


def _in_proj_kernel(x_ref, g_ref, sh_ref, sc_ref, w_ref, o_ref, h_ref):
    @pl.when(pl.program_id(1) == 0)
    def _():
        h = _rms(x_ref[...]) * g_ref[0]
        h_ref[...] = (h * (1.0 + sc_ref[0]) + sh_ref[0]).astype(BF16)

    o_ref[...] = _dot(h_ref[...], w_ref[...])


def _in_proj(x, norm_g, mod, layer, w, tn):
    m, n = x.shape[0], w.shape[1]
    return pl.pallas_call(
        _in_proj_kernel,
        grid=(m // TM, n // tn),
        in_specs=[
            pl.BlockSpec((TM, D_MODEL), lambda i, j: (i, 0)),
            pl.BlockSpec((1, 1, D_MODEL), lambda i, j: (layer, 0, 0)),
            _mod_spec(layer, 0, TM),
            _mod_spec(layer, 1, TM),
            pl.BlockSpec((D_MODEL, tn), lambda i, j: (0, j)),
        ],
        out_specs=pl.BlockSpec((TM, tn), lambda i, j: (i, j)),
        out_shape=jax.ShapeDtypeStruct((m, n), F32),
        scratch_shapes=[pltpu.VMEM((TM, D_MODEL), BF16)],
        compiler_params=_params("parallel", "arbitrary"),
        name="in_proj",
    )(x, norm_g, mod, mod, w)


def _pool_bands():
    t = POOL_TM
    b0 = np.zeros((POOL_GROUPS, t, t), np.float32)
    bp = np.zeros((POOL_GROUPS, t, POOL_HALO), np.float32)
    bn = np.zeros((POOL_GROUPS, t, POOL_HALO), np.float32)
    for g, win in enumerate(POOL_WINDOWS):
        for r in range(t):
            for s in range(r - win // 2, r + win - win // 2):
                if s < 0:
                    bp[g, r, s + POOL_HALO] = 1.0
                elif s >= t:
                    bn[g, r, s - t] = 1.0
                else:
                    b0[g, r, s] = 1.0
    return jnp.asarray(b0, BF16), jnp.asarray(bp, BF16), jnp.asarray(bn, BF16)


def _pool_kernel(u_ref, up_ref, un_ref, b0_ref, bp_ref, bn_ref, pw_ref, sc_ref, o_ref):
    i = pl.program_id(0)
    tiles_per_seq = DEC_SEQ // POOL_TM
    in_prompt = i < N_PROMPT // POOL_TM
    pos = (i - N_PROMPT // POOL_TM) % tiles_per_seq
    is_start = jnp.logical_or(in_prompt, pos == 0)
    is_end = jnp.logical_or(in_prompt, pos == tiles_per_seq - 1)
    t = lax.broadcasted_iota(jnp.int32, (POOL_TM, POOL_GC), 0)
    for g, win in enumerate(POOL_WINDOWS):
        cols = slice(g * POOL_GC, (g + 1) * POOL_GC)
        u = u_ref[:, cols]
        up = jnp.where(is_start, 0.0, up_ref[:, cols])
        un = jnp.where(is_end, 0.0, un_ref[:, cols])
        s = jnp.zeros((POOL_TM, POOL_GC), F32)
        for band, val in ((b0_ref, u), (bp_ref, up), (bn_ref, un)):
            hi, lo = _split_bf16(val)
            s = s + _dot(band[g], hi) + _dot(band[g], lo)
        cut_lo = jnp.where(is_start, jnp.maximum(win // 2 - t, 0), 0)
        cut_hi = jnp.where(is_end, jnp.maximum(t + (win - win // 2) - POOL_TM, 0), 0)
        cnt = (win - cut_lo - cut_hi).astype(F32)
        pooled = s / cnt - u
        y = _dot(pooled.astype(BF16), pw_ref[g])
        o_ref[:, cols] = y * sc_ref[:, cols]


def _pool_mixer(proj, pool_w, pool_scale):
    b0, bp, bn = _pool_bands()
    halo_blocks = POOL_TM // POOL_HALO
    last = N_TOK // POOL_HALO - 1
    full = lambda shape: pl.BlockSpec(shape, lambda i: (0,) * len(shape))
    return pl.pallas_call(
        _pool_kernel,
        grid=(N_TOK // POOL_TM,),
        in_specs=[
            pl.BlockSpec((POOL_TM, POOL_WIDTH), lambda i: (i, 0)),
            pl.BlockSpec((POOL_HALO, POOL_WIDTH), lambda i: (jnp.maximum(i * halo_blocks - 1, 0), 0)),
            pl.BlockSpec((POOL_HALO, POOL_WIDTH), lambda i: (jnp.minimum((i + 1) * halo_blocks, last), 0)),
            full(b0.shape), full(bp.shape), full(bn.shape),
            full(pool_w.shape),
            full((1, POOL_WIDTH)),
        ],
        out_specs=pl.BlockSpec((POOL_TM, POOL_WIDTH), lambda i: (i, 0)),
        out_shape=jax.ShapeDtypeStruct((N_TOK, POOL_WIDTH), F32),
        compiler_params=_params("parallel"),
        name="pool_mixer",
    )(proj, proj, proj, b0, bp, bn, pool_w, pool_scale.reshape(1, POOL_WIDTH))


def _rope_tables(t):
    nf = DK_RET // 4
    rows = t // GRID_W
    r = jnp.repeat(jnp.arange(rows), GRID_W).astype(F32)
    col = jnp.tile(jnp.arange(GRID_W), rows).astype(F32)
    inv = ROPE_BASE ** (-jnp.arange(nf, dtype=F32) / nf)
    ar, ac = r[:, None] * inv, col[:, None] * inv
    cos = jnp.concatenate([jnp.cos(ar), jnp.cos(ar), jnp.cos(ac), jnp.cos(ac)], axis=1)
    sin = jnp.concatenate([-jnp.sin(ar), jnp.sin(ar), -jnp.sin(ac), jnp.sin(ac)], axis=1)
    return cos, sin


def _ret_kernel(*refs, n_chunks, heads, use_rope, has_s0, emit_state):
    refs = list(refs)
    dec_ref, q_ref, k_ref, v_ref = refs[:4]
    refs = refs[4:]
    if use_rope:
        cos_ref, sin_ref = refs[:2]
        refs = refs[2:]
    if has_s0:
        s0_ref = refs[0]
        refs = refs[1:]
    o_ref = refs[0]
    refs = refs[1:]
    if emit_state:
        st_ref = refs[0]
        refs = refs[1:]
    (qk_ref,) = refs

    c = RET_CHUNK
    nf = DK_RET // 4
    head0 = pl.program_id(1) * heads
    inter = has_s0 or n_chunks > 1

    row = lax.broadcasted_iota(jnp.int32, (c, DK_RET), 0).astype(F32)
    ii = lax.broadcasted_iota(jnp.int32, (c, c), 0)
    jj = lax.broadcasted_iota(jnp.int32, (c, c), 1)
    dist = (ii - jj).astype(F32)
    lane = lax.broadcasted_iota(jnp.int32, (c, DK_RET), 1)
    first_half = (lane % (2 * nf)) < nf

    def rope(x, r0):
        if not use_rope:
            return x
        partner = jnp.where(first_half, pltpu.roll(x, DK_RET - nf, axis=1), pltpu.roll(x, nf, axis=1))
        return x * cos_ref[pl.ds(r0, c), :] + partner * sin_ref[pl.ds(r0, c), :]

    for hh in range(heads):
        cols = slice(hh * DK_RET, (hh + 1) * DK_RET)

        def lam(d, shape):
            return _log_sigmoid(jnp.full(shape, dec_ref[d, head0 + hh], F32))

        lam_f, lam_b = lam(0, (c, DK_RET)), lam(1, (c, DK_RET))
        dq_f = jnp.exp((row + 1.0) * lam_f)
        dk_f = jnp.exp((c - 1.0 - row) * lam_f)
        dq_b = jnp.exp((c - row) * lam_b)
        dk_b = jnp.exp(row * lam_b)
        cdec_f = jnp.exp(float(c) * lam(0, (DK_RET, DV_RET)))
        cdec_b = jnp.exp(float(c) * lam(1, (DK_RET, DV_RET)))
        dmat = jnp.where(ii > jj, jnp.exp(dist * lam(0, (c, c))),
                         jnp.where(ii < jj, jnp.exp(-dist * lam(1, (c, c))), 2.0))

        def prepare(n, carry):
            r0 = n * c if isinstance(n, int) else pl.multiple_of(n * c, c)
            q = rope(q_ref[pl.ds(r0, c), cols], r0)
            k = rope(k_ref[pl.ds(r0, c), cols] * (DK_RET ** -0.5), r0)
            vb = v_ref[pl.ds(r0, c), cols].astype(BF16)
            a = _dot_nt(q.astype(BF16), k.astype(BF16)) * dmat
            o_ref[pl.ds(r0, c), cols] = _dot(a.astype(BF16), vb)
            qk_ref[0, pl.ds(r0, c), cols] = (q * dq_f).astype(BF16)
            qk_ref[1, pl.ds(r0, c), cols] = (q * dq_b).astype(BF16)
            qk_ref[2, pl.ds(r0, c), cols] = (k * dk_f).astype(BF16)
            qk_ref[3, pl.ds(r0, c), cols] = (k * dk_b).astype(BF16)
            return carry

        def advance(n, states):
            out = []
            for d, s, cdec in ((0, states[0], cdec_f), (1, states[1], cdec_b)):
                m = n if d == 0 else n_chunks - 1 - n
                r0 = m * c if isinstance(m, int) else pl.multiple_of(m * c, c)
                vb = v_ref[pl.ds(r0, c), cols].astype(BF16)
                if inter:
                    o_ref[pl.ds(r0, c), cols] += _dot(qk_ref[d, pl.ds(r0, c), cols], s.astype(BF16))
                out.append(cdec * s + _dot_tn(qk_ref[2 + d, pl.ds(r0, c), cols], vb))
            return tuple(out)

        _loop(n_chunks, prepare, 0)
        zero = jnp.zeros((DK_RET, DV_RET), F32)
        init = (s0_ref[0, 0, 0, hh], s0_ref[0, 0, 1, hh]) if has_s0 else (zero, zero)
        s_f, s_b = _loop(n_chunks, advance, init)
        if emit_state:
            st_ref[0, 0, hh] = s_f
            st_ref[0, 1, hh] = s_b


def _ret_scan(proj, decay, t, batch, heads, row_block0, rope, s0, s0_layer, emit_state):
    w = heads * DK_RET
    col0 = POOL_WIDTH // w
    per = RET_WIDTH // w
    in_specs = [
        pl.BlockSpec(memory_space=pltpu.SMEM),
        pl.BlockSpec((t, w), lambda b, h: (row_block0 + b, col0 + h)),
        pl.BlockSpec((t, w), lambda b, h: (row_block0 + b, col0 + per + h)),
        pl.BlockSpec((t, w), lambda b, h: (row_block0 + b, col0 + 2 * per + h)),
    ]
    args = [decay, proj, proj, proj]
    if rope is not None:
        assert heads == 1
        in_specs += [pl.BlockSpec((t, DK_RET), lambda b, h: (0, 0))] * 2
        args += list(rope)
    if s0 is not None:
        in_specs.append(pl.BlockSpec((1, 1, 2, heads, DK_RET, DV_RET), lambda b, h: (b, s0_layer, 0, h, 0, 0)))
        args.append(s0)
    out_specs = [pl.BlockSpec((t, w), lambda b, h: (b, h))]
    out_shape = [jax.ShapeDtypeStruct((batch * t, RET_WIDTH), F32)]
    if emit_state:
        out_specs.append(pl.BlockSpec((1, 2, heads, DK_RET, DV_RET), lambda b, h: (b, 0, h, 0, 0)))
        out_shape.append(jax.ShapeDtypeStruct((batch, 2, H_RET, DK_RET, DV_RET), F32))
    return pl.pallas_call(
        functools.partial(_ret_kernel, n_chunks=t // RET_CHUNK, heads=heads, use_rope=rope is not None,
                          has_s0=s0 is not None, emit_state=emit_state),
        grid=(batch, H_RET // heads),
        in_specs=in_specs,
        out_specs=out_specs,
        out_shape=out_shape,
        scratch_shapes=[pltpu.VMEM((4, t, w), BF16)],
        compiler_params=_params("parallel", "parallel"),
        name="ret_scan",
    )(*args)


def _gla_kernel(*refs, n_blocks, dvb, has_s0, emit_state):
    refs = list(refs)
    q_ref, k_ref, v_ref, z_ref, g2_ref, gb_ref = refs[:6]
    refs = refs[6:]
    if has_s0:
        s0_ref = refs[0]
        refs = refs[1:]
    o_ref = refs[0]
    refs = refs[1:]
    if emit_state:
        st_ref = refs[0]
        refs = refs[1:]
    qd_ref, kt_ref, a_ref, dec_ref, sf_ref, sb_ref = refs

    c = CHUNK
    blk = GLA_BLK
    cpb = blk // c
    n_chunks = n_blocks * cpb

    @pl.when(pl.program_id(2) == 0)
    def _():
        ii = lax.broadcasted_iota(jnp.int32, (blk, blk), 0)
        jj = lax.broadcasted_iota(jnp.int32, (blk, blk), 1)
        same = (ii // c) == (jj // c)
        masks = (jnp.logical_and(same, ii >= jj), jnp.logical_and(same, ii <= jj))
        tris = tuple(jnp.where(m, 1.0, 0.0).astype(BF16) for m in masks)

        def build(bi, carry):
            r0 = bi * blk if isinstance(bi, int) else pl.multiple_of(bi * blk, blk)
            q = q_ref[pl.ds(r0, blk), :] * (DK_GLA ** -0.5)
            k = k_ref[pl.ds(r0, blk), :]
            zb = z_ref[pl.ds(r0, blk), :].astype(BF16)
            scores = None
            for d in (0, 1):
                z = _dot(zb, g2_ref[0, d]) + gb_ref[0, d]
                lg_hi, lg_lo = _split_bf16(_log_sigmoid(z) * (1.0 / GLA_TAU))
                g = _dot(tris[d], lg_hi) + _dot(tris[d], lg_lo)
                edge = c - 1 if d == 0 else 0
                g_tot = jnp.concatenate(
                    [jnp.broadcast_to(g[ci * c + edge:ci * c + edge + 1, :], (c, DK_GLA)) for ci in range(cpb)],
                    axis=0)
                qd = (q * jnp.exp(g)).astype(BF16)
                kd = (k * jnp.exp(-g)).astype(BF16)
                kt = (k * jnp.exp(g_tot - g)).astype(BF16)
                a = jnp.where(masks[d], _dot_nt(qd, kd), 0.0)
                scores = a if scores is None else scores + a
                qd_ref[d, pl.ds(r0, blk), :] = qd
                kt_ref[d, pl.ds(r0, blk), :] = kt
                for ci in range(cpb):
                    dec_ref[d * n_chunks + bi * cpb + ci] = jnp.exp(g_tot[ci * c:ci * c + 8, :])
            a_ref[bi] = scores.astype(BF16)
            return carry

        _loop(n_blocks, build, 0)

    def intra(bi, carry):
        r0 = bi * blk if isinstance(bi, int) else pl.multiple_of(bi * blk, blk)
        o_ref[pl.ds(r0, blk), :] = _dot(a_ref[bi], v_ref[pl.ds(r0, blk), :].astype(BF16))
        return carry

    _loop(n_blocks, intra, 0)

    zero = jnp.zeros((dvb, DK_GLA), F32)
    sf_ref[...] = s0_ref[0, 0, 0, 0].T if has_s0 else zero
    sb_ref[...] = s0_ref[0, 0, 1, 0].T if has_s0 else zero

    def advance(n, carry):
        for d, s_ref in ((0, sf_ref), (1, sb_ref)):
            m = n if d == 0 else n_chunks - 1 - n
            r0 = pl.multiple_of(m * c, c)
            vb = v_ref[pl.ds(r0, c), :].astype(BF16)
            s = s_ref[...]
            o_ref[pl.ds(r0, c), :] += _dot_nt(qd_ref[d, pl.ds(r0, c), :], s.astype(BF16))
            s_ref[...] = s * dec_ref[d * n_chunks + m][0:1, :] + _dot_tn(vb, kt_ref[d, pl.ds(r0, c), :])
        return carry

    lax.fori_loop(0, n_chunks, advance, 0)
    if emit_state:
        st_ref[0, 0, 0] = sf_ref[...].T
        st_ref[0, 1, 0] = sb_ref[...].T


def _gla_scan(proj, z1, g2, gb, t, batch, dvb, row_block0, s0, s0_layer, emit_state):
    nd = DV_GLA // dvb
    kcol = GLA_DK_TOTAL // DK_GLA
    vcol = 2 * GLA_DK_TOTAL // dvb
    n_blocks = t // GLA_BLK
    n_chunks = t // CHUNK
    in_specs = [
        pl.BlockSpec((t, DK_GLA), lambda b, h, d: (row_block0 + b, h)),
        pl.BlockSpec((t, DK_GLA), lambda b, h, d: (row_block0 + b, kcol + h)),
        pl.BlockSpec((t, dvb), lambda b, h, d: (row_block0 + b, vcol + h * nd + d)),
        pl.BlockSpec((t, 128), lambda b, h, d: (row_block0 + b, 0)),
        pl.BlockSpec((1, 2, 128, DK_GLA), lambda b, h, d: (h, 0, 0, 0)),
        pl.BlockSpec((1, 2, 1, DK_GLA), lambda b, h, d: (h, 0, 0, 0)),
    ]
    args = [proj, proj, proj, z1, g2, gb]
    if s0 is not None:
        in_specs.append(pl.BlockSpec((1, 1, 2, 1, DK_GLA, dvb), lambda b, h, d: (b, s0_layer, 0, h, 0, d)))
        args.append(s0)
    out_specs = [pl.BlockSpec((t, dvb), lambda b, h, d: (b, h * nd + d))]
    out_shape = [jax.ShapeDtypeStruct((batch * t, GLA_DV_TOTAL), F32)]
    if emit_state:
        out_specs.append(pl.BlockSpec((1, 2, 1, DK_GLA, dvb), lambda b, h, d: (b, 0, h, 0, d)))
        out_shape.append(jax.ShapeDtypeStruct((batch, 2, H_GLA, DK_GLA, DV_GLA), F32))
    return pl.pallas_call(
        functools.partial(_gla_kernel, n_blocks=n_blocks, dvb=dvb, has_s0=s0 is not None, emit_state=emit_state),
        grid=(batch, H_GLA, nd),
        in_specs=in_specs,
        out_specs=out_specs,
        out_shape=out_shape,
        scratch_shapes=[
            pltpu.VMEM((2, t, DK_GLA), BF16),
            pltpu.VMEM((2, t, DK_GLA), BF16),
            pltpu.VMEM((n_blocks, GLA_BLK, GLA_BLK), BF16),
            pltpu.VMEM((2 * n_chunks, 8, DK_GLA), F32),
            pltpu.VMEM((dvb, DK_GLA), F32),
            pltpu.VMEM((dvb, DK_GLA), F32),
        ],
        compiler_params=_params("parallel", "parallel", "arbitrary"),
        name="gla_scan",
    )(*args)


def _gated_heads(o_ref, gate_ref, gn_ref, a_ref, col0, n_heads, width):
    for hd in range(n_heads):
        cols = slice(hd * width, (hd + 1) * width)
        gt = gate_ref[:, cols]
        val = _rms(o_ref[:, cols]) * gn_ref[:, cols] * _silu(gt)
        a_ref[:, col0 + hd * width:col0 + (hd + 1) * width] = val.astype(BF16)


def _out_even_kernel(yp_ref, o_ref, gate_ref, gn_ref, w_ref, x_ref, g1_ref, out_ref, a_ref):
    @pl.when(pl.program_id(1) == 0)
    def _():
        a_ref[:, :POOL_WIDTH] = yp_ref[...].astype(BF16)
        _gated_heads(o_ref, gate_ref, gn_ref, a_ref, POOL_WIDTH, H_RET, DV_RET)

    out_ref[...] = x_ref[...] + g1_ref[0] * _dot(a_ref[...], w_ref[...])


def _out_odd_kernel(o_ref, gate_ref, gn_ref, w_ref, x_ref, g1_ref, out_ref, a_ref):
    @pl.when(pl.program_id(1) == 0)
    def _():
        _gated_heads(o_ref, gate_ref, gn_ref, a_ref, 0, H_GLA, DV_GLA)

    out_ref[...] = x_ref[...] + g1_ref[0] * _dot(a_ref[...], w_ref[...])


def _out_proj(x, mod, layer, w, gate_proj, gate_col_block, norm_g, o_raw, y_pool=None):
    m = x.shape[0]
    tn = TN_OUT
    width = o_raw.shape[1]
    row = lambda wd: pl.BlockSpec((TM, wd), lambda i, j: (i, 0))
    in_specs = [
        row(width),
        pl.BlockSpec((TM, width), lambda i, j: (i, gate_col_block)),
        pl.BlockSpec((1, width), lambda i, j: (0, 0)),
        pl.BlockSpec((D_MODEL, tn), lambda i, j: (0, j)),
        pl.BlockSpec((TM, tn), lambda i, j: (i, j)),
        pl.BlockSpec((1, 1, tn), lambda i, j: ((layer * 6 + 2) * N_COND + _cond_of_tile(i, TM), 0, j)),
    ]
    args = [o_raw, gate_proj, norm_g.reshape(1, width), w, x, mod]
    body = _out_odd_kernel
    if y_pool is not None:
        in_specs = [row(POOL_WIDTH)] + in_specs
        args = [y_pool] + args
        body = _out_even_kernel
    return pl.pallas_call(
        body,
        grid=(m // TM, D_MODEL // tn),
        in_specs=in_specs,
        out_specs=pl.BlockSpec((TM, tn), lambda i, j: (i, j)),
        out_shape=jax.ShapeDtypeStruct((m, D_MODEL), F32),
        scratch_shapes=[pltpu.VMEM((TM, D_MODEL), BF16)],
        compiler_params=_params("parallel", "arbitrary"),
        name="out_proj",
    )(*args)


def _ffn_kernel(x_ref, xp_ref, xn_ref, gn_ref, sh_ref, sc_ref, gt_ref, wa_ref, wb_ref, cw_ref, cb_ref,
                wd_ref, fg_ref, o_ref, h_ref, hh_ref, acc_ref, *, final):
    i = pl.program_id(0)
    f = pl.program_id(1)

    def modnorm(x):
        return (_rms(x) * gn_ref[0] * (1.0 + sc_ref[0]) + sh_ref[0]).astype(BF16)

    @pl.when(f == 0)
    def _():
        h_ref[...] = modnorm(x_ref[...])
        hh_ref[...] = modnorm(jnp.concatenate([xp_ref[...], xn_ref[...]], axis=0))
        acc_ref[...] = jnp.zeros_like(acc_ref)

    a = _dot(h_ref[...], wa_ref[...])
    b = _dot(h_ref[...], wb_ref[...])
    a_halo = _dot(hh_ref[...], wa_ref[...])

    seq = jnp.where(i * TM < N_PROMPT, SEQ, DEC_SEQ)
    t = lax.broadcasted_iota(jnp.int32, (TM, TF), 0)
    pos = (i * TM + t) & (seq - 1)
    a_prev = jnp.where(t == 0, a_halo[7:8, :], pltpu.roll(a, 1, axis=0))
    a_prev = jnp.where(pos == 0, 0.0, a_prev)
    a_next = jnp.where(t == TM - 1, a_halo[8:9, :], pltpu.roll(a, TM - 1, axis=0))
    a_next = jnp.where(pos == seq - 1, 0.0, a_next)
    conv = a_prev * cw_ref[0:1, :] + a * cw_ref[1:2, :] + a_next * cw_ref[2:3, :] + cb_ref[...]
    acc_ref[...] += _dot((_silu(conv) * b).astype(BF16), wd_ref[...])

    @pl.when(f == pl.num_programs(1) - 1)
    def _():
        y = x_ref[...] + gt_ref[0] * acc_ref[...]
        if final:
            y = _rms(y) * fg_ref[...]
        o_ref[...] = y


def _ffn(x, mod, layer, norm_g, w_up, conv_w, conv_b, w_down, final_g, final):
    m = x.shape[0]
    nf = D_FF // TF
    halo = 8
    last = m // halo - 1
    return pl.pallas_call(
        functools.partial(_ffn_kernel, final=final),
        grid=(m // TM, nf),
        in_specs=[
            pl.BlockSpec((TM, D_MODEL), lambda i, f: (i, 0)),
            pl.BlockSpec((halo, D_MODEL), lambda i, f: (jnp.maximum(i * (TM // halo) - 1, 0), 0)),
            pl.BlockSpec((halo, D_MODEL), lambda i, f: (jnp.minimum((i + 1) * (TM // halo), last), 0)),
            pl.BlockSpec((1, 1, D_MODEL), lambda i, f: (layer, 0, 0)),
            _mod_spec(layer, 3, TM),
            _mod_spec(layer, 4, TM),
            _mod_spec(layer, 5, TM),
            pl.BlockSpec((D_MODEL, TF), lambda i, f: (0, f)),
            pl.BlockSpec((D_MODEL, TF), lambda i, f: (0, nf + f)),
            pl.BlockSpec((3, TF), lambda i, f: (0, f)),
            pl.BlockSpec((1, TF), lambda i, f: (0, f)),
            pl.BlockSpec((TF, D_MODEL), lambda i, f: (f, 0)),
            pl.BlockSpec((1, D_MODEL), lambda i, f: (0, 0)),
        ],
        out_specs=pl.BlockSpec((TM, D_MODEL), lambda i, f: (i, 0)),
        out_shape=jax.ShapeDtypeStruct((m, D_MODEL), F32),
        scratch_shapes=[
            pltpu.VMEM((TM, D_MODEL), BF16),
            pltpu.VMEM((2 * halo, D_MODEL), BF16),
            pltpu.VMEM((TM, D_MODEL), F32),
        ],
        compiler_params=_params("parallel", "arbitrary"),
        name="ffn",
    )(x, x, x, norm_g, mod, mod, mod, w_up, w_up, conv_w, conv_b.reshape(1, D_FF), w_down,
      final_g.reshape(1, D_MODEL))


def _gla_gate_weights(gw2, gb):
    r = GLA_GATE_RANK
    g2 = jnp.zeros((2, 128, GLA_DK_TOTAL), F32)
    g2 = g2.at[0, 0:r].set(gw2[0]).at[1, r:2 * r].set(gw2[1])
    g2 = g2.reshape(2, 128, H_GLA, DK_GLA).transpose(2, 0, 1, 3).astype(BF16)
    gbh = gb.reshape(2, H_GLA, 1, DK_GLA).transpose(1, 0, 2, 3)
    return g2, gbh


def kernel(x_prompt, x_sample, state_ret, state_gla, c, c_ctx, ada_w, ada_b, norm1_g, norm2_g,
           even_w_in, pool_w, pool_scale, ret_decay, ret_norm_g, even_w_out, odd_w_in, gla_gw1,
           gla_gw2, gla_gb, gla_norm_g, odd_w_out, ffn_w_up, ffn_conv_w, ffn_conv_b, ffn_w_down, final_g):
    x = jnp.concatenate([x_prompt.reshape(N_PROMPT, D_MODEL), x_sample.reshape(N_SAMPLE, D_MODEL)], axis=0)
    cvec = jnp.concatenate([c_ctx[None, :], c, jnp.zeros((COND_PAD - N_COND, D_MODEL), F32)], axis=0)
    mod = _ada_mod(cvec, ada_w, ada_b)
    n1 = norm1_g.reshape(DEPTH, 1, D_MODEL)
    n2 = norm2_g.reshape(DEPTH, 1, D_MODEL)
    rope = _rope_tables(DEC_SEQ)
    p_blocks = N_PROMPT // DEC_SEQ

    ret_states, gla_states = [], []
    for l in range(DEPTH):
        if l % 2 == 0:
            i = l // 2
            proj = _in_proj(x, n1, mod, l, even_w_in[i].astype(BF16), TN_IN)
            y_pool = _pool_mixer(proj, pool_w[i].astype(BF16), pool_scale[i])
            o_p, st = _ret_scan(proj, ret_decay[i], SEQ, BATCH, H_RET, 0, None, None, 0, True)
            (o_s,) = _ret_scan(proj, ret_decay[i], DEC_SEQ, DEC_BATCH, 1, p_blocks, rope, state_ret, i, False)
            o_raw = jnp.concatenate([o_p, o_s], axis=0)
            ret_states.append(st)
            x = _out_proj(x, mod, l, even_w_out[i].astype(BF16), proj, EVEN_IN // RET_WIDTH - 1,
                          ret_norm_g[i], o_raw, y_pool)
        else:
            j = l // 2
            proj = _in_proj(x, n1, mod, l, odd_w_in[j].astype(BF16), TN_IN)
            gw1 = jnp.concatenate([gla_gw1[j, 0], gla_gw1[j, 1],
                                   jnp.zeros((D_MODEL, 128 - 2 * GLA_GATE_RANK), F32)], axis=1)
            z1 = _in_proj(x, n1, mod, l, gw1.astype(BF16), 128)
            g2, gbh = _gla_gate_weights(gla_gw2[j], gla_gb[j])
            o_p, st = _gla_scan(proj, z1, g2, gbh, SEQ, BATCH, DV_GLA, 0, None, 0, True)
            (o_s,) = _gla_scan(proj, z1, g2, gbh, DEC_SEQ, DEC_BATCH, GLA_DVB_SAMPLE, p_blocks,
                               state_gla, j, False)
            o_raw = jnp.concatenate([o_p, o_s], axis=0)
            gla_states.append(st)
            x = _out_proj(x, mod, l, odd_w_out[j].astype(BF16), proj, ODD_IN // GLA_DV_TOTAL - 1,
                          gla_norm_g[j], o_raw)
        x = _ffn(x, mod, l, n2, ffn_w_up[l].astype(BF16), ffn_conv_w[l], ffn_conv_b[l],
                 ffn_w_down[l].astype(BF16), final_g, l == DEPTH - 1)

    y_prompt = x[:N_PROMPT].reshape(BATCH, SEQ, D_MODEL)
    y_sample = x[N_PROMPT:].reshape(DEC_BATCH, DEC_SEQ, D_MODEL)
    return (y_prompt, y_sample, jnp.stack(ret_states, axis=1), jnp.stack(gla_states, axis=1))
```

```python
import functools

import numpy as np
import jax
import jax.numpy as jnp
from jax import lax
from jax.experimental import pallas as pl
from jax.experimental.pallas import tpu as pltpu

F32 = jnp.float32
BF16 = jnp.bfloat16

D_MODEL = 2048
BATCH = 16
SEQ = 256
DEPTH = 4
DEC_BATCH = 2
DEC_SEQ = 4096
GRID_W = 64
N_EVEN = (DEPTH + 1) // 2
N_ODD = DEPTH // 2
POOL_WIDTH = D_MODEL // 2
POOL_GROUPS = 4
POOL_GC = POOL_WIDTH // POOL_GROUPS
POOL_WINDOWS = (2, 4, 8, 16)
RET_WIDTH = D_MODEL // 2
H_RET = 8
DK_RET = RET_WIDTH // H_RET
DV_RET = RET_WIDTH // H_RET
ROPE_BASE = 10000.0
H_GLA = 4
GLA_DK_TOTAL = D_MODEL // 2
GLA_DV_TOTAL = D_MODEL
DK_GLA = GLA_DK_TOTAL // H_GLA
DV_GLA = GLA_DV_TOTAL // H_GLA
GLA_GATE_RANK = 16
GLA_TAU = 16.0
D_FF = 5632
CHUNK = 64
EPS = 1e-6
EVEN_IN = POOL_WIDTH + 4 * RET_WIDTH
ODD_IN = 2 * GLA_DK_TOTAL + 2 * GLA_DV_TOTAL

N_PROMPT = BATCH * SEQ
N_SAMPLE = DEC_BATCH * DEC_SEQ
N_TOK = N_PROMPT + N_SAMPLE
N_COND = 1 + DEC_BATCH
COND_PAD = 8

VMEM_LIMIT = 56 * 1024 * 1024

TM = 512
IN_SPLIT = 2
IN_CHUNK = 512
TM_FFN = 1024
TF = 512
POOL_TM = 256
POOL_HALO = 16
ADA_TN = 1024
RET_CHUNK = 256
GLA_BLK = 256


def _params(*sem):
    return pltpu.CompilerParams(dimension_semantics=sem, vmem_limit_bytes=VMEM_LIMIT)


def _cond_of_tile(i, tm):
    r0 = i * tm
    return jnp.where(r0 < N_PROMPT, 0, 1 + (r0 - N_PROMPT) // DEC_SEQ)


def _log_sigmoid(x):
    return jnp.minimum(x, 0.0) - jnp.log1p(jnp.exp(-jnp.abs(x)))


def _silu(x):
    return x * jax.nn.sigmoid(x)


def _rms(x):
    return x * lax.rsqrt(jnp.mean(x * x, axis=-1, keepdims=True) + EPS)


def _dot(a, b):
    return jnp.dot(a, b, preferred_element_type=F32)


def _dot_nt(a, b):
    return lax.dot_general(a, b, (((1,), (1,)), ((), ())), preferred_element_type=F32)


def _dot_tn(a, b):
    return lax.dot_general(a, b, (((0,), (0,)), ((), ())), preferred_element_type=F32)


def _split_bf16(x):
    hi = x.astype(BF16)
    lo = (x - hi.astype(F32)).astype(BF16)
    return hi, lo


def _loop(n, body, init):
    if n == 1:
        return body(0, init)
    return lax.fori_loop(0, n, body, init)


def _ada_kernel(c_ref, w_ref, b_ref, o_ref):
    s = _silu(c_ref[...]).astype(BF16)
    o_ref[0] = _dot(s, w_ref[0].astype(BF16)) + b_ref[0]


def _ada_mod(cvec, ada_w, ada_b):
    n = 6 * D_MODEL
    mod = pl.pallas_call(
        _ada_kernel,
        grid=(DEPTH, n // ADA_TN),
        in_specs=[
            pl.BlockSpec((COND_PAD, D_MODEL), lambda l, j: (0, 0)),
            pl.BlockSpec((1, D_MODEL, ADA_TN), lambda l, j: (l, 0, j)),
            pl.BlockSpec((1, 1, ADA_TN), lambda l, j: (l, 0, j)),
        ],
        out_specs=pl.BlockSpec((1, COND_PAD, ADA_TN), lambda l, j: (l, 0, j)),
        out_shape=jax.ShapeDtypeStruct((DEPTH, COND_PAD, n), F32),
        compiler_params=_params("parallel", "parallel"),
        name="ada_mod",
    )(cvec, ada_w, ada_b.reshape(DEPTH, 1, n))
    mod = mod[:, :N_COND].reshape(DEPTH, N_COND, 6, D_MODEL).transpose(0, 2, 1, 3)
    return mod.reshape(DEPTH * 6 * N_COND, 1, D_MODEL)


def _mod_spec(layer, part, tm, row_axis=0):
    base = (layer * 6 + part) * N_COND
    return pl.BlockSpec((1, 1, D_MODEL), lambda *idx: (base + _cond_of_tile(idx[row_axis], tm), 0, 0))


def _in_proj_kernel(x_ref, g_ref, sh_ref, sc_ref, w_ref, o_ref, h_ref, *, chunk):
    h = _rms(x_ref[...]) * g_ref[0]
    h_ref[...] = (h * (1.0 + sc_ref[0]) + sh_ref[0]).astype(BF16)
    for c0 in range(0, o_ref.shape[1], chunk):
        o_ref[:, c0:c0 + chunk] = _dot(h_ref[...], w_ref[:, c0:c0 + chunk]).astype(o_ref.dtype)


def _in_proj(x, norm_g, mod, layer, w, split):
    m, n = x.shape[0], w.shape[1]
    tn = n // split
    return pl.pallas_call(
        functools.partial(_in_proj_kernel, chunk=min(IN_CHUNK, tn)),
        grid=(split, m // TM),
        in_specs=[
            pl.BlockSpec((TM, D_MODEL), lambda j, i: (i, 0)),
            pl.BlockSpec((1, 1, D_MODEL), lambda j, i: (layer, 0, 0)),
            _mod_spec(layer, 0, TM, 1),
            _mod_spec(layer, 1, TM, 1),
            pl.BlockSpec((D_MODEL, tn), lambda j, i: (0, j)),
        ],
        out_specs=pl.BlockSpec((TM, tn), lambda j, i: (i, j)),
        out_shape=jax.ShapeDtypeStruct((m, n), BF16),
        scratch_shapes=[pltpu.VMEM((TM, D_MODEL), BF16)],
        compiler_params=_params("arbitrary", "arbitrary"),
        name="in_proj",
    )(x, norm_g, mod, mod, w)


def _pool_bands():
    t = POOL_TM
    b0 = np.zeros((POOL_GROUPS, t, t), np.float32)
    bp = np.zeros((POOL_GROUPS, t, POOL_HALO), np.float32)
    bn = np.zeros((POOL_GROUPS, t, POOL_HALO), np.float32)
    for g, win in enumerate(POOL_WINDOWS):
        for r in range(t):
            for s in range(r - win // 2, r + win - win // 2):
                if s < 0:
                    bp[g, r, s + POOL_HALO] = 1.0
                elif s >= t:
                    bn[g, r, s - t] = 1.0
                else:
                    b0[g, r, s] = 1.0
    return jnp.asarray(b0, BF16), jnp.asarray(bp, BF16), jnp.asarray(bn, BF16)


def _pool_kernel(u_ref, up_ref, un_ref, b0_ref, bp_ref, bn_ref, pw_ref, sc_ref, o_ref):
    i = pl.program_id(0)
    tiles_per_seq = DEC_SEQ // POOL_TM
    in_prompt = i < N_PROMPT // POOL_TM
    pos = (i - N_PROMPT // POOL_TM) % tiles_per_seq
    is_start = jnp.logical_or(in_prompt, pos == 0)
    is_end = jnp.logical_or(in_prompt, pos == tiles_per_seq - 1)
    t = lax.broadcasted_iota(jnp.int32, (POOL_TM, POOL_GC), 0)
    for g, win in enumerate(POOL_WINDOWS):
        cols = slice(g * POOL_GC, (g + 1) * POOL_GC)
        ub = u_ref[:, cols]
        up = up_ref[:, cols]
        un = un_ref[:, cols]
        up = jnp.where(is_start, jnp.zeros_like(up), up)
        un = jnp.where(is_end, jnp.zeros_like(un), un)
        s = _dot(b0_ref[g], ub) + _dot(bp_ref[g], up) + _dot(bn_ref[g], un)
        u = ub.astype(F32)
        cut_lo = jnp.where(is_start, jnp.maximum(win // 2 - t, 0), 0)
        cut_hi = jnp.where(is_end, jnp.maximum(t + (win - win // 2) - POOL_TM, 0), 0)
        cnt = (win - cut_lo - cut_hi).astype(F32)
        pooled = s / cnt - u
        y = _dot(pooled.astype(BF16), pw_ref[g])
        o_ref[:, cols] = (y * sc_ref[:, cols]).astype(o_ref.dtype)


def _pool_mixer(proj, pool_w, pool_scale):
    b0, bp, bn = _pool_bands()
    halo_blocks = POOL_TM // POOL_HALO
    last = N_TOK // POOL_HALO - 1
    full = lambda shape: pl.BlockSpec(shape, lambda i: (0,) * len(shape))
    return pl.pallas_call(
        _pool_kernel,
        grid=(N_TOK // POOL_TM,),
        in_specs=[
            pl.BlockSpec((POOL_TM, POOL_WIDTH), lambda i: (i, 0)),
            pl.BlockSpec((POOL_HALO, POOL_WIDTH), lambda i: (jnp.maximum(i * halo_blocks - 1, 0), 0)),
            pl.BlockSpec((POOL_HALO, POOL_WIDTH), lambda i: (jnp.minimum((i + 1) * halo_blocks, last), 0)),
            full(b0.shape), full(bp.shape), full(bn.shape),
            full(pool_w.shape),
            full((1, POOL_WIDTH)),
        ],
        out_specs=pl.BlockSpec((POOL_TM, POOL_WIDTH), lambda i: (i, 0)),
        out_shape=jax.ShapeDtypeStruct((N_TOK, POOL_WIDTH), BF16),
        compiler_params=_params("parallel"),
        name="pool_mixer",
    )(proj, proj, proj, b0, bp, bn, pool_w, pool_scale.reshape(1, POOL_WIDTH))


def _rope_tables(t):
    nf = DK_RET // 4
    rows = t // GRID_W
    r = jnp.repeat(jnp.arange(rows), GRID_W).astype(F32)
    col = jnp.tile(jnp.arange(GRID_W), rows).astype(F32)
    inv = ROPE_BASE ** (-jnp.arange(nf, dtype=F32) / nf)
    ar, ac = r[:, None] * inv, col[:, None] * inv
    cos = jnp.concatenate([jnp.cos(ar), jnp.cos(ar), jnp.cos(ac), jnp.cos(ac)], axis=1)
    sin = jnp.concatenate([-jnp.sin(ar), jnp.sin(ar), -jnp.sin(ac), jnp.sin(ac)], axis=1)
    return cos, sin


def _ret_kernel(*refs, n_chunks, heads, use_rope, has_s0, emit_state):
    refs = list(refs)
    dec_ref, q_ref, k_ref, v_ref = refs[:4]
    refs = refs[4:]
    if use_rope:
        cos_ref, sin_ref = refs[:2]
        refs = refs[2:]
    if has_s0:
        s0_ref = refs[0]
        refs = refs[1:]
    o_ref = refs[0]
    refs = refs[1:]
    if emit_state:
        st_ref = refs[0]
        refs = refs[1:]
    (qk_ref,) = refs

    c = RET_CHUNK
    nf = DK_RET // 4
    head0 = pl.program_id(1) * heads
    inter = has_s0 or n_chunks > 1

    row = lax.broadcasted_iota(jnp.int32, (c, DK_RET), 0).astype(F32)
    ii = lax.broadcasted_iota(jnp.int32, (c, c), 0)
    jj = lax.broadcasted_iota(jnp.int32, (c, c), 1)
    dist = (ii - jj).astype(F32)
    lane = lax.broadcasted_iota(jnp.int32, (c, DK_RET), 1)
    first_half = (lane % (2 * nf)) < nf

    def rope(x, r0):
        if not use_rope:
            return x
        partner = jnp.where(first_half, pltpu.roll(x, DK_RET - nf, axis=1), pltpu.roll(x, nf, axis=1))
        return x * cos_ref[pl.ds(r0, c), :] + partner * sin_ref[pl.ds(r0, c), :]

    for hh in range(heads):
        cols = slice(hh * DK_RET, (hh + 1) * DK_RET)

        def lam(d, shape):
            return _log_sigmoid(jnp.full(shape, dec_ref[d, head0 + hh], F32))

        lam_f, lam_b = lam(0, (c, DK_RET)), lam(1, (c, DK_RET))
        dq_f = jnp.exp((row + 1.0) * lam_f)
        dk_f = jnp.exp((c - 1.0 - row) * lam_f)
        dq_b = jnp.exp((c - row) * lam_b)
        dk_b = jnp.exp(row * lam_b)
        cdec_f = jnp.exp(float(c) * lam(0, (DK_RET, DV_RET)))
        cdec_b = jnp.exp(float(c) * lam(1, (DK_RET, DV_RET)))
        dmat = jnp.where(ii > jj, jnp.exp(dist * lam(0, (c, c))),
                         jnp.where(ii < jj, jnp.exp(-dist * lam(1, (c, c))), 2.0))

        def prepare(n, carry):
            r0 = n * c if isinstance(n, int) else pl.multiple_of(n * c, c)
            q = rope(q_ref[pl.ds(r0, c), cols].astype(F32), r0)
            k = rope(k_ref[pl.ds(r0, c), cols].astype(F32) * (DK_RET ** -0.5), r0)
            vb = v_ref[pl.ds(r0, c), cols]
            a = _dot_nt(q.astype(BF16), k.astype(BF16)) * dmat
            o_ref[pl.ds(r0, c), cols] = _dot(a.astype(BF16), vb)
            qk_ref[0, pl.ds(r0, c), cols] = (q * dq_f).astype(BF16)
            qk_ref[1, pl.ds(r0, c), cols] = (q * dq_b).astype(BF16)
            qk_ref[2, pl.ds(r0, c), cols] = (k * dk_f).astype(BF16)
            qk_ref[3, pl.ds(r0, c), cols] = (k * dk_b).astype(BF16)
            return carry

        def advance(n, states):
            out = []
            for d, s, cdec in ((0, states[0], cdec_f), (1, states[1], cdec_b)):
                m = n if d == 0 else n_chunks - 1 - n
                r0 = m * c if isinstance(m, int) else pl.multiple_of(m * c, c)
                vb = v_ref[pl.ds(r0, c), cols]
                if inter:
                    o_ref[pl.ds(r0, c), cols] += _dot(qk_ref[d, pl.ds(r0, c), cols], s.astype(BF16))
                out.append(cdec * s + _dot_tn(qk_ref[2 + d, pl.ds(r0, c), cols], vb))
            return tuple(out)

        _loop(n_chunks, prepare, 0)
        zero = jnp.zeros((DK_RET, DV_RET), F32)
        init = (s0_ref[0, 0, 0, hh], s0_ref[0, 0, 1, hh]) if has_s0 else (zero, zero)
        s_f, s_b = _loop(n_chunks, advance, init)
        if emit_state:
            st_ref[0, 0, hh] = s_f
            st_ref[0, 1, hh] = s_b


def _ret_scan(proj, decay, t, batch, heads, row_block0, rope, s0, s0_layer, emit_state):
    w = heads * DK_RET
    col0 = POOL_WIDTH // w
    per = RET_WIDTH // w
    in_specs = [
        pl.BlockSpec(memory_space=pltpu.SMEM),
        pl.BlockSpec((t, w), lambda b, h: (row_block0 + b, col0 + h)),
        pl.BlockSpec((t, w), lambda b, h: (row_block0 + b, col0 + per + h)),
        pl.BlockSpec((t, w), lambda b, h: (row_block0 + b, col0 + 2 * per + h)),
    ]
    args = [decay, proj, proj, proj]
    if rope is not None:
        assert heads == 1
        in_specs += [pl.BlockSpec((t, DK_RET), lambda b, h: (0, 0))] * 2
        args += list(rope)
    if s0 is not None:
        in_specs.append(pl.BlockSpec((1, 1, 2, heads, DK_RET, DV_RET), lambda b, h: (b, s0_layer, 0, h, 0, 0)))
        args.append(s0)
    out_specs = [pl.BlockSpec((t, w), lambda b, h: (b, h))]
    out_shape = [jax.ShapeDtypeStruct((batch * t, RET_WIDTH), F32)]
    if emit_state:
        out_specs.append(pl.BlockSpec((1, 2, heads, DK_RET, DV_RET), lambda b, h: (b, 0, h, 0, 0)))
        out_shape.append(jax.ShapeDtypeStruct((batch, 2, H_RET, DK_RET, DV_RET), F32))
    return pl.pallas_call(
        functools.partial(_ret_kernel, n_chunks=t // RET_CHUNK, heads=heads, use_rope=rope is not None,
                          has_s0=s0 is not None, emit_state=emit_state),
        grid=(batch, H_RET // heads),
        in_specs=in_specs,
        out_specs=out_specs,
        out_shape=out_shape,
        scratch_shapes=[pltpu.VMEM((4, t, w), BF16)],
        compiler_params=_params("parallel", "parallel"),
        name="ret_scan",
    )(*args)


def _gla_kernel(*refs, n_blocks, dvb, has_s0, emit_state):
    refs = list(refs)
    q_ref, k_ref, v_ref, z_ref, g2_ref, gb_ref = refs[:6]
    refs = refs[6:]
    if has_s0:
        s0_ref = refs[0]
        refs = refs[1:]
    o_ref = refs[0]
    refs = refs[1:]
    if emit_state:
        st_ref = refs[0]
        refs = refs[1:]
    qd_ref, kt_ref, a_ref, dec_ref, sf_ref, sb_ref = refs

    c = CHUNK
    blk = GLA_BLK
    cpb = blk // c
    n_chunks = n_blocks * cpb

    @pl.when(pl.program_id(2) == 0)
    def _():
        ii = lax.broadcasted_iota(jnp.int32, (blk, blk), 0)
        jj = lax.broadcasted_iota(jnp.int32, (blk, blk), 1)
        same = (ii // c) == (jj // c)
        masks = (jnp.logical_and(same, ii >= jj), jnp.logical_and(same, ii <= jj))
        tris = tuple(jnp.where(m, 1.0, 0.0).astype(BF16) for m in masks)

        def build(bi, carry):
            r0 = bi * blk if isinstance(bi, int) else pl.multiple_of(bi * blk, blk)
            q = q_ref[pl.ds(r0, blk), :].astype(F32) * (DK_GLA ** -0.5)
            k = k_ref[pl.ds(r0, blk), :].astype(F32)
            zb = z_ref[pl.ds(r0, blk), :]
            scores = None
            for d in (0, 1):
                z = _dot(zb, g2_ref[0, d]) + gb_ref[0, d]
                lg_hi, lg_lo = _split_bf16(_log_sigmoid(z) * (1.0 / GLA_TAU))
                g = _dot(tris[d], lg_hi) + _dot(tris[d], lg_lo)
                edge = c - 1 if d == 0 else 0
                g_tot = jnp.concatenate(
                    [jnp.broadcast_to(g[ci * c + edge:ci * c + edge + 1, :], (c, DK_GLA)) for ci in range(cpb)],
                    axis=0)
                qd = (q * jnp.exp(g)).astype(BF16)
                kd = (k * jnp.exp(-g)).astype(BF16)
                kt = (k * jnp.exp(g_tot - g)).astype(BF16)
                a = jnp.where(masks[d], _dot_nt(qd, kd), 0.0)
                scores = a if scores is None else scores + a
                qd_ref[d, pl.ds(r0, blk), :] = qd
                kt_ref[d, pl.ds(r0, blk), :] = kt
                for ci in range(cpb):
                    dec_ref[d * n_chunks + bi * cpb + ci] = jnp.exp(g_tot[ci * c:ci * c + 8, :])
            a_ref[bi] = scores.astype(BF16)
            return carry

        _loop(n_blocks, build, 0)

    def intra(bi, carry):
        r0 = bi * blk if isinstance(bi, int) else pl.multiple_of(bi * blk, blk)
        o_ref[pl.ds(r0, blk), :] = _dot(a_ref[bi], v_ref[pl.ds(r0, blk), :])
        return carry

    _loop(n_blocks, intra, 0)

    zero = jnp.zeros((dvb, DK_GLA), F32)
    sf_ref[...] = s0_ref[0, 0, 0, 0].T if has_s0 else zero
    sb_ref[...] = s0_ref[0, 0, 1, 0].T if has_s0 else zero

    def advance(n, carry):
        for d, s_ref in ((0, sf_ref), (1, sb_ref)):
            m = n if d == 0 else n_chunks - 1 - n
            r0 = pl.multiple_of(m * c, c)
            vb = v_ref[pl.ds(r0, c), :]
            s = s_ref[...]
            o_ref[pl.ds(r0, c), :] += _dot_nt(qd_ref[d, pl.ds(r0, c), :], s.astype(BF16))
            s_ref[...] = s * dec_ref[d * n_chunks + m][0:1, :] + _dot_tn(vb, kt_ref[d, pl.ds(r0, c), :])
        return carry

    lax.fori_loop(0, n_chunks, advance, 0)
    if emit_state:
        st_ref[0, 0, 0] = sf_ref[...].T
        st_ref[0, 1, 0] = sb_ref[...].T


def _gla_scan(proj, z1, g2, gb, t, batch, dvb, row_block0, s0, s0_layer, emit_state):
    nd = DV_GLA // dvb
    kcol = GLA_DK_TOTAL // DK_GLA
    vcol = 2 * GLA_DK_TOTAL // dvb
    n_blocks = t // GLA_BLK
    n_chunks = t // CHUNK
    in_specs = [
        pl.BlockSpec((t, DK_GLA), lambda b, h, d: (row_block0 + b, h)),
        pl.BlockSpec((t, DK_GLA), lambda b, h, d: (row_block0 + b, kcol + h)),
        pl.BlockSpec((t, dvb), lambda b, h, d: (row_block0 + b, vcol + h * nd + d)),
        pl.BlockSpec((t, 128), lambda b, h, d: (row_block0 + b, 0)),
        pl.BlockSpec((1, 2, 128, DK_GLA), lambda b, h, d: (h, 0, 0, 0)),
        pl.BlockSpec((1, 2, 1, DK_GLA), lambda b, h, d: (h, 0, 0, 0)),
    ]
    args = [proj, proj, proj, z1, g2, gb]
    if s0 is not None:
        in_specs.append(pl.BlockSpec((1, 1, 2, 1, DK_GLA, dvb), lambda b, h, d: (b, s0_layer, 0, h, 0, d)))
        args.append(s0)
    out_specs = [pl.BlockSpec((t, dvb), lambda b, h, d: (b, h * nd + d))]
    out_shape = [jax.ShapeDtypeStruct((batch * t, GLA_DV_TOTAL), F32)]
    if emit_state:
        out_specs.append(pl.BlockSpec((1, 2, 1, DK_GLA, dvb), lambda b, h, d: (b, 0, h, 0, d)))
        out_shape.append(jax.ShapeDtypeStruct((batch, 2, H_GLA, DK_GLA, DV_GLA), F32))
    return pl.pallas_call(
        functools.partial(_gla_kernel, n_blocks=n_blocks, dvb=dvb, has_s0=s0 is not None, emit_state=emit_state),
        grid=(batch, H_GLA, nd),
        in_specs=in_specs,
        out_specs=out_specs,
        out_shape=out_shape,
        scratch_shapes=[
            pltpu.VMEM((2, t, DK_GLA), BF16),
            pltpu.VMEM((2, t, DK_GLA), BF16),
            pltpu.VMEM((n_blocks, GLA_BLK, GLA_BLK), BF16),
            pltpu.VMEM((2 * n_chunks, 8, DK_GLA), F32),
            pltpu.VMEM((dvb, DK_GLA), F32),
            pltpu.VMEM((dvb, DK_GLA), F32),
        ],
        compiler_params=_params("parallel", "parallel", "arbitrary"),
        name="gla_scan",
    )(*args)


def _gated_heads(o_ref, gate_ref, gn_ref, a_ref, col0, n_heads, width):
    for hd in range(n_heads):
        cols = slice(hd * width, (hd + 1) * width)
        gt = gate_ref[:, cols].astype(F32)
        val = _rms(o_ref[:, cols]) * gn_ref[:, cols] * _silu(gt)
        a_ref[:, col0 + hd * width:col0 + (hd + 1) * width] = val.astype(BF16)


def _out_proj_kernel(*refs, n_heads, width, with_pool):
    refs = list(refs)
    col0 = 0
    if with_pool:
        yp_ref = refs[0]
        refs = refs[1:]
        col0 = POOL_WIDTH
    op_ref, os_ref, gate_ref, gn_ref, w_ref, x_ref, g1_ref, out_ref, a_ref = refs
    in_prompt = pl.program_id(0) < N_PROMPT // TM

    @pl.when(in_prompt)
    def _():
        _gated_heads(op_ref, gate_ref, gn_ref, a_ref, col0, n_heads, width)

    @pl.when(jnp.logical_not(in_prompt))
    def _():
        _gated_heads(os_ref, gate_ref, gn_ref, a_ref, col0, n_heads, width)

    if with_pool:
        a_ref[:, :POOL_WIDTH] = yp_ref[...]
    out_ref[...] = x_ref[...] + g1_ref[0] * _dot(a_ref[...], w_ref[...])


def _out_proj(x, mod, layer, w, gate_proj, gate_col_block, norm_g, o_prompt, o_sample, n_heads, y_pool=None):
    m = x.shape[0]
    width = o_prompt.shape[1]
    p_tiles = N_PROMPT // TM
    in_specs = [
        pl.BlockSpec((TM, width), lambda i: (jnp.minimum(i, p_tiles - 1), 0)),
        pl.BlockSpec((TM, width), lambda i: (jnp.maximum(i - p_tiles, 0), 0)),
        pl.BlockSpec((TM, width), lambda i: (i, gate_col_block)),
        pl.BlockSpec((1, width), lambda i: (0, 0)),
        pl.BlockSpec((D_MODEL, D_MODEL), lambda i: (0, 0)),
        pl.BlockSpec((TM, D_MODEL), lambda i: (i, 0)),
        _mod_spec(layer, 2, TM),
    ]
    args = [o_prompt, o_sample, gate_proj, norm_g.reshape(1, width), w, x, mod]
    if y_pool is not None:
        in_specs = [pl.BlockSpec((TM, POOL_WIDTH), lambda i: (i, 0))] + in_specs
        args = [y_pool] + args
    return pl.pallas_call(
        functools.partial(_out_proj_kernel, n_heads=n_heads, width=width // n_heads,
                          with_pool=y_pool is not None),
        grid=(m // TM,),
        in_specs=in_specs,
        out_specs=pl.BlockSpec((TM, D_MODEL), lambda i: (i, 0)),
        out_shape=jax.ShapeDtypeStruct((m, D_MODEL), F32),
        scratch_shapes=[pltpu.VMEM((TM, D_MODEL), BF16)],
        compiler_params=_params("parallel"),
        name="out_proj",
    )(*args)


def _ffn_kernel(x_ref, xp_ref, xn_ref, gn_ref, sh_ref, sc_ref, gt_ref, wa_ref, wb_ref, cw_ref, cb_ref,
                wd_ref, fg_ref, o_ref, h_ref, hh_ref, *, final):
    i = pl.program_id(0)
    f = pl.program_id(1)
    tm = TM_FFN

    def modnorm(x):
        return (_rms(x) * gn_ref[0] * (1.0 + sc_ref[0]) + sh_ref[0]).astype(BF16)

    @pl.when(f == 0)
    def _():
        h_ref[...] = modnorm(x_ref[...])
        hh_ref[...] = modnorm(jnp.concatenate([xp_ref[...], xn_ref[...]], axis=0))
        o_ref[...] = jnp.zeros_like(o_ref)

    a = _dot(h_ref[...], wa_ref[...])
    b = _dot(h_ref[...], wb_ref[...])
    a_halo = _dot(hh_ref[...], wa_ref[...])

    seq = jnp.where(i * tm < N_PROMPT, SEQ, DEC_SEQ)
    t = lax.broadcasted_iota(jnp.int32, (tm, TF), 0)
    pos = (i * tm + t) & (seq - 1)
    a_prev = jnp.where(t == 0, a_halo[7:8, :], pltpu.roll(a, 1, axis=0))
    a_prev = jnp.where(pos == 0, 0.0, a_prev)
    a_next = jnp.where(t == tm - 1, a_halo[8:9, :], pltpu.roll(a, tm - 1, axis=0))
    a_next = jnp.where(pos == seq - 1, 0.0, a_next)
    conv = a_prev * cw_ref[0:1, :] + a * cw_ref[1:2, :] + a_next * cw_ref[2:3, :] + cb_ref[...]
    o_ref[...] += _dot((_silu(conv) * b).astype(BF16), wd_ref[...])

    @pl.when(f == pl.num_programs(1) - 1)
    def _():
        y = x_ref[...] + gt_ref[0] * o_ref[...]
        if final:
            y = _rms(y) * fg_ref[...]
        o_ref[...] = y


def _ffn(x, mod, layer, norm_g, w_up, conv_w, conv_b, w_down, final_g, final):
    m = x.shape[0]
    nf = D_FF // TF
    tm = TM_FFN
    halo = 8
    last = m // halo - 1
    return pl.pallas_call(
        functools.partial(_ffn_kernel, final=final),
        grid=(m // tm, nf),
        in_specs=[
            pl.BlockSpec((tm, D_MODEL), lambda i, f: (i, 0), pipeline_mode=pl.Buffered(1)),
            pl.BlockSpec((halo, D_MODEL), lambda i, f: (jnp.maximum(i * (tm // halo) - 1, 0), 0)),
            pl.BlockSpec((halo, D_MODEL), lambda i, f: (jnp.minimum((i + 1) * (tm // halo), last), 0)),
            pl.BlockSpec((1, 1, D_MODEL), lambda i, f: (layer, 0, 0)),
            _mod_spec(layer, 3, tm),
            _mod_spec(layer, 4, tm),
            _mod_spec(layer, 5, tm),
            pl.BlockSpec((D_MODEL, TF), lambda i, f: (0, f)),
            pl.BlockSpec((D_MODEL, TF), lambda i, f: (0, nf + f)),
            pl.BlockSpec((3, TF), lambda i, f: (0, f)),
            pl.BlockSpec((1, TF), lambda i, f: (0, f)),
            pl.BlockSpec((TF, D_MODEL), lambda i, f: (f, 0)),
            pl.BlockSpec((1, D_MODEL), lambda i, f: (0, 0)),
        ],
        out_specs=pl.BlockSpec((tm, D_MODEL), lambda i, f: (i, 0)),
        out_shape=jax.ShapeDtypeStruct((m, D_MODEL), F32),
        scratch_shapes=[
            pltpu.VMEM((tm, D_MODEL), BF16),
            pltpu.VMEM((2 * halo, D_MODEL), BF16),
        ],
        compiler_params=_params("parallel", "arbitrary"),
        name="ffn",
    )(x, x, x, norm_g, mod, mod, mod, w_up, w_up, conv_w, conv_b.reshape(1, D_FF), w_down,
      final_g.reshape(1, D_MODEL))


def _gla_gate_weights(gw2, gb):
    r = GLA_GATE_RANK
    g2 = jnp.zeros((2, 128, GLA_DK_TOTAL), F32)
    g2 = g2.at[0, 0:r].set(gw2[0]).at[1, r:2 * r].set(gw2[1])
    g2 = g2.reshape(2, 128, H_GLA, DK_GLA).transpose(2, 0, 1, 3).astype(BF16)
    gbh = gb.reshape(2, H_GLA, 1, DK_GLA).transpose(1, 0, 2, 3)
    return g2, gbh


def kernel(x_prompt, x_sample, state_ret, state_gla, c, c_ctx, ada_w, ada_b, norm1_g, norm2_g,
           even_w_in, pool_w, pool_scale, ret_decay, ret_norm_g, even_w_out, odd_w_in, gla_gw1,
           gla_gw2, gla_gb, gla_norm_g, odd_w_out, ffn_w_up, ffn_conv_w, ffn_conv_b, ffn_w_down, final_g):
    x = jnp.concatenate([x_prompt.reshape(N_PROMPT, D_MODEL), x_sample.reshape(N_SAMPLE, D_MODEL)], axis=0)
    cvec = jnp.concatenate([c_ctx[None, :], c, jnp.zeros((COND_PAD - N_COND, D_MODEL), F32)], axis=0)
    mod = _ada_mod(cvec, ada_w, ada_b)
    n1 = norm1_g.reshape(DEPTH, 1, D_MODEL)
    n2 = norm2_g.reshape(DEPTH, 1, D_MODEL)
    rope = _rope_tables(DEC_SEQ)
    p_blocks = N_PROMPT // DEC_SEQ

    ret_states, gla_states = [], []
    for l in range(DEPTH):
        if l % 2 == 0:
            i = l // 2
            proj = _in_proj(x, n1, mod, l, even_w_in[i].astype(BF16), IN_SPLIT)
            y_pool = _pool_mixer(proj, pool_w[i].astype(BF16), pool_scale[i])
            o_p, st = _ret_scan(proj, ret_decay[i], SEQ, BATCH, H_RET, 0, None, None, 0, True)
            (o_s,) = _ret_scan(proj, ret_decay[i], DEC_SEQ, DEC_BATCH, 1, p_blocks, rope, state_ret, i, False)
            ret_states.append(st)
            x = _out_proj(x, mod, l, even_w_out[i].astype(BF16), proj, EVEN_IN // RET_WIDTH - 1,
                          ret_norm_g[i], o_p, o_s, H_RET, y_pool)
        else:
            j = l // 2
            proj = _in_proj(x, n1, mod, l, odd_w_in[j].astype(BF16), IN_SPLIT)
            gw1 = jnp.concatenate([gla_gw1[j, 0], gla_gw1[j, 1],
                                   jnp.zeros((D_MODEL, 128 - 2 * GLA_GATE_RANK), F32)], axis=1)
            z1 = _in_proj(x, n1, mod, l, gw1.astype(BF16), 1)
            g2, gbh = _gla_gate_weights(gla_gw2[j], gla_gb[j])
            o_p, st = _gla_scan(proj, z1, g2, gbh, SEQ, BATCH, DV_GLA, 0, None, 0, True)
            (o_s,) = _gla_scan(proj, z1, g2, gbh, DEC_SEQ, DEC_BATCH, DV_GLA, p_blocks, state_gla, j, False)
            gla_states.append(st)
            x = _out_proj(x, mod, l, odd_w_out[j].astype(BF16), proj, ODD_IN // GLA_DV_TOTAL - 1,
                          gla_norm_g[j], o_p, o_s, H_GLA)
        x = _ffn(x, mod, l, n2, ffn_w_up[l].astype(BF16), ffn_conv_w[l], ffn_conv_b[l],
                 ffn_w_down[l].astype(BF16), final_g, l == DEPTH - 1)

    y_prompt = x[:N_PROMPT].reshape(BATCH, SEQ, D_MODEL)
    y_sample = x[N_PROMPT:].reshape(DEC_BATCH, DEC_SEQ, D_MODEL)
    return (y_prompt, y_sample, jnp.stack(ret_states, axis=1), jnp.stack(gla_states, axis=1))
```

```python
import functools

import numpy as np
import jax
import jax.numpy as jnp
from jax import lax
from jax.experimental import pallas as pl
from jax.experimental.pallas import tpu as pltpu

F32 = jnp.float32
BF16 = jnp.bfloat16

D_MODEL = 2048
BATCH = 16
SEQ = 256
DEPTH = 4
DEC_BATCH = 2
DEC_SEQ = 4096
GRID_W = 64
N_EVEN = (DEPTH + 1) // 2
N_ODD = DEPTH // 2
POOL_WIDTH = D_MODEL // 2
POOL_GROUPS = 4
POOL_GC = POOL_WIDTH // POOL_GROUPS
POOL_WINDOWS = (2, 4, 8, 16)
RET_WIDTH = D_MODEL // 2
H_RET = 8
DK_RET = RET_WIDTH // H_RET
DV_RET = RET_WIDTH // H_RET
ROPE_BASE = 10000.0
H_GLA = 4
GLA_DK_TOTAL = D_MODEL // 2
GLA_DV_TOTAL = D_MODEL
DK_GLA = GLA_DK_TOTAL // H_GLA
DV_GLA = GLA_DV_TOTAL // H_GLA
GLA_GATE_RANK = 16
GLA_TAU = 16.0
D_FF = 5632
CHUNK = 64
EPS = 1e-6
EVEN_IN = POOL_WIDTH + 4 * RET_WIDTH
ODD_IN = 2 * GLA_DK_TOTAL + 2 * GLA_DV_TOTAL

N_PROMPT = BATCH * SEQ
N_SAMPLE = DEC_BATCH * DEC_SEQ
N_TOK = N_PROMPT + N_SAMPLE
N_COND = 1 + DEC_BATCH
COND_PAD = 8

VMEM_LIMIT = 56 * 1024 * 1024

TM = 512
IN_SPLIT = 2
IN_CHUNK = 512
TM_FFN = 1024
TF = 512
POOL_TM = 256
POOL_HALO = 16
ADA_TN = 1024
RET_CHUNK = 256
GLA_BLK = 256


def _params(*sem):
    return pltpu.CompilerParams(dimension_semantics=sem, vmem_limit_bytes=VMEM_LIMIT)


def _cond_of_tile(i, tm):
    r0 = i * tm
    return jnp.where(r0 < N_PROMPT, 0, 1 + (r0 - N_PROMPT) // DEC_SEQ)


LOG2E = 1.4426950408889634


def _log_sigmoid(x):
    return jnp.minimum(x, 0.0) - jnp.log(1.0 + jnp.exp2(jnp.abs(x) * (-LOG2E)))


def _silu(x):
    return x * jax.nn.sigmoid(x)


def _rms(x):
    return x * lax.rsqrt(jnp.mean(x * x, axis=-1, keepdims=True) + EPS)


def _dot(a, b):
    return jnp.dot(a, b, preferred_element_type=F32)


def _dot_nt(a, b):
    return lax.dot_general(a, b, (((1,), (1,)), ((), ())), preferred_element_type=F32)


def _dot_tn(a, b):
    return lax.dot_general(a, b, (((0,), (0,)), ((), ())), preferred_element_type=F32)


def _split_bf16(x):
    hi = x.astype(BF16)
    lo = (x - hi.astype(F32)).astype(BF16)
    return hi, lo


def _loop(n, body, init, unroll=1):
    if n == 1:
        return body(0, init)
    return lax.fori_loop(0, n, body, init, unroll=unroll)


def _ada_kernel(c_ref, w_ref, b_ref, o_ref):
    s = _silu(c_ref[...]).astype(BF16)
    o_ref[0] = _dot(s, w_ref[0].astype(BF16)) + b_ref[0]


def _ada_mod(cvec, ada_w, ada_b):
    n = 6 * D_MODEL
    mod = pl.pallas_call(
        _ada_kernel,
        grid=(DEPTH, n // ADA_TN),
        in_specs=[
            pl.BlockSpec((COND_PAD, D_MODEL), lambda l, j: (0, 0)),
            pl.BlockSpec((1, D_MODEL, ADA_TN), lambda l, j: (l, 0, j)),
            pl.BlockSpec((1, 1, ADA_TN), lambda l, j: (l, 0, j)),
        ],
        out_specs=pl.BlockSpec((1, COND_PAD, ADA_TN), lambda l, j: (l, 0, j)),
        out_shape=jax.ShapeDtypeStruct((DEPTH, COND_PAD, n), F32),
        compiler_params=_params("parallel", "parallel"),
        name="ada_mod",
    )(cvec, ada_w, ada_b.reshape(DEPTH, 1, n))
    mod = mod[:, :N_COND].reshape(DEPTH, N_COND, 6, D_MODEL).transpose(0, 2, 1, 3)
    return mod.reshape(DEPTH * 6 * N_COND, 1, D_MODEL)


def _mod_spec(layer, part, tm, row_axis=0):
    base = (layer * 6 + part) * N_COND
    return pl.BlockSpec((1, 1, D_MODEL), lambda *idx: (base + _cond_of_tile(idx[row_axis], tm), 0, 0))


def _in_proj_kernel(x_ref, g_ref, sh_ref, sc_ref, w_ref, o_ref, h_ref, *, chunk):
    h = _rms(x_ref[...]) * g_ref[0]
    h_ref[...] = (h * (1.0 + sc_ref[0]) + sh_ref[0]).astype(BF16)
    for c0 in range(0, o_ref.shape[1], chunk):
        o_ref[:, c0:c0 + chunk] = _dot(h_ref[...], w_ref[:, c0:c0 + chunk]).astype(o_ref.dtype)


def _in_proj(x, norm_g, mod, layer, w, split):
    m, n = x.shape[0], w.shape[1]
    tn = n // split
    return pl.pallas_call(
        functools.partial(_in_proj_kernel, chunk=min(IN_CHUNK, tn)),
        grid=(split, m // TM),
        in_specs=[
            pl.BlockSpec((TM, D_MODEL), lambda j, i: (i, 0)),
            pl.BlockSpec((1, 1, D_MODEL), lambda j, i: (layer, 0, 0)),
            _mod_spec(layer, 0, TM, 1),
            _mod_spec(layer, 1, TM, 1),
            pl.BlockSpec((D_MODEL, tn), lambda j, i: (0, j)),
        ],
        out_specs=pl.BlockSpec((TM, tn), lambda j, i: (i, j)),
        out_shape=jax.ShapeDtypeStruct((m, n), BF16),
        scratch_shapes=[pltpu.VMEM((TM, D_MODEL), BF16)],
        compiler_params=_params("arbitrary", "arbitrary"),
        name="in_proj",
    )(x, norm_g, mod, mod, w)


def _pool_bands():
    t = POOL_TM
    b0 = np.zeros((POOL_GROUPS, t, t), np.float32)
    bp = np.zeros((POOL_GROUPS, t, POOL_HALO), np.float32)
    bn = np.zeros((POOL_GROUPS, t, POOL_HALO), np.float32)
    for g, win in enumerate(POOL_WINDOWS):
        for r in range(t):
            for s in range(r - win // 2, r + win - win // 2):
                if s < 0:
                    bp[g, r, s + POOL_HALO] = 1.0
                elif s >= t:
                    bn[g, r, s - t] = 1.0
                else:
                    b0[g, r, s] = 1.0
    return jnp.asarray(b0, BF16), jnp.asarray(bp, BF16), jnp.asarray(bn, BF16)


def _pool_kernel(u_ref, up_ref, un_ref, b0_ref, bp_ref, bn_ref, pw_ref, sc_ref, o_ref):
    i = pl.program_id(0)
    tiles_per_seq = DEC_SEQ // POOL_TM
    in_prompt = i < N_PROMPT // POOL_TM
    pos = (i - N_PROMPT // POOL_TM) % tiles_per_seq
    is_start = jnp.logical_or(in_prompt, pos == 0)
    is_end = jnp.logical_or(in_prompt, pos == tiles_per_seq - 1)
    t = lax.broadcasted_iota(jnp.int32, (POOL_TM, POOL_GC), 0)
    for g, win in enumerate(POOL_WINDOWS):
        cols = slice(g * POOL_GC, (g + 1) * POOL_GC)
        ub = u_ref[:, cols]
        up = up_ref[:, cols]
        un = un_ref[:, cols]
        up = jnp.where(is_start, jnp.zeros_like(up), up)
        un = jnp.where(is_end, jnp.zeros_like(un), un)
        s = _dot(b0_ref[g], ub) + _dot(bp_ref[g], up) + _dot(bn_ref[g], un)
        u = ub.astype(F32)
        cut_lo = jnp.where(is_start, jnp.maximum(win // 2 - t, 0), 0)
        cut_hi = jnp.where(is_end, jnp.maximum(t + (win - win // 2) - POOL_TM, 0), 0)
        cnt = (win - cut_lo - cut_hi).astype(F32)
        pooled = s / cnt - u
        y = _dot(pooled.astype(BF16), pw_ref[g])
        o_ref[:, cols] = (y * sc_ref[:, cols]).astype(o_ref.dtype)


def _pool_mixer(proj, pool_w, pool_scale):
    b0, bp, bn = _pool_bands()
    halo_blocks = POOL_TM // POOL_HALO
    last = N_TOK // POOL_HALO - 1
    full = lambda shape: pl.BlockSpec(shape, lambda i: (0,) * len(shape))
    return pl.pallas_call(
        _pool_kernel,
        grid=(N_TOK // POOL_TM,),
        in_specs=[
            pl.BlockSpec((POOL_TM, POOL_WIDTH), lambda i: (i, 0)),
            pl.BlockSpec((POOL_HALO, POOL_WIDTH), lambda i: (jnp.maximum(i * halo_blocks - 1, 0), 0)),
            pl.BlockSpec((POOL_HALO, POOL_WIDTH), lambda i: (jnp.minimum((i + 1) * halo_blocks, last), 0)),
            full(b0.shape), full(bp.shape), full(bn.shape),
            full(pool_w.shape),
            full((1, POOL_WIDTH)),
        ],
        out_specs=pl.BlockSpec((POOL_TM, POOL_WIDTH), lambda i: (i, 0)),
        out_shape=jax.ShapeDtypeStruct((N_TOK, POOL_WIDTH), BF16),
        compiler_params=_params("parallel"),
        name="pool_mixer",
    )(proj, proj, proj, b0, bp, bn, pool_w, pool_scale.reshape(1, POOL_WIDTH))


def _rope_tables(t):
    nf = DK_RET // 4
    rows = t // GRID_W
    r = jnp.repeat(jnp.arange(rows), GRID_W).astype(F32)
    col = jnp.tile(jnp.arange(GRID_W), rows).astype(F32)
    inv = ROPE_BASE ** (-jnp.arange(nf, dtype=F32) / nf)
    ar, ac = r[:, None] * inv, col[:, None] * inv
    cos = jnp.concatenate([jnp.cos(ar), jnp.cos(ar), jnp.cos(ac), jnp.cos(ac)], axis=1)
    sin = jnp.concatenate([-jnp.sin(ar), jnp.sin(ar), -jnp.sin(ac), jnp.sin(ac)], axis=1)
    return cos, sin


def _ret_kernel(*refs, n_chunks, heads, use_rope, has_s0, emit_state):
    refs = list(refs)
    dec_ref, q_ref, k_ref, v_ref = refs[:4]
    refs = refs[4:]
    if use_rope:
        cos_ref, sin_ref = refs[:2]
        refs = refs[2:]
    if has_s0:
        s0_ref = refs[0]
        refs = refs[1:]
    o_ref = refs[0]
    refs = refs[1:]
    if emit_state:
        st_ref = refs[0]
        refs = refs[1:]
    (qk_ref,) = refs

    c = RET_CHUNK
    nf = DK_RET // 4
    head0 = pl.program_id(1) * heads
    inter = has_s0 or n_chunks > 1

    row = lax.broadcasted_iota(jnp.int32, (c, DK_RET), 0).astype(F32)
    ii = lax.broadcasted_iota(jnp.int32, (c, c), 0)
    jj = lax.broadcasted_iota(jnp.int32, (c, c), 1)
    dist = (ii - jj).astype(F32)
    lane = lax.broadcasted_iota(jnp.int32, (c, DK_RET), 1)
    first_half = (lane % (2 * nf)) < nf

    def rope(x, r0):
        if not use_rope:
            return x
        partner = jnp.where(first_half, pltpu.roll(x, DK_RET - nf, axis=1), pltpu.roll(x, nf, axis=1))
        return x * cos_ref[pl.ds(r0, c), :] + partner * sin_ref[pl.ds(r0, c), :]

    for hh in range(heads):
        cols = slice(hh * DK_RET, (hh + 1) * DK_RET)

        def lam(d, shape):
            return _log_sigmoid(jnp.full(shape, dec_ref[d, head0 + hh], F32))

        lam_f, lam_b = lam(0, (c, DK_RET)), lam(1, (c, DK_RET))
        dq_f = jnp.exp((row + 1.0) * lam_f)
        dk_f = jnp.exp((c - 1.0 - row) * lam_f)
        dq_b = jnp.exp((c - row) * lam_b)
        dk_b = jnp.exp(row * lam_b)
        cdec_f = jnp.exp(float(c) * lam(0, (DK_RET, DV_RET)))
        cdec_b = jnp.exp(float(c) * lam(1, (DK_RET, DV_RET)))
        dmat = jnp.where(ii > jj, jnp.exp(dist * lam(0, (c, c))),
                         jnp.where(ii < jj, jnp.exp(-dist * lam(1, (c, c))), 2.0))

        def prepare(n, carry):
            r0 = n * c if isinstance(n, int) else pl.multiple_of(n * c, c)
            q = rope(q_ref[pl.ds(r0, c), cols].astype(F32), r0)
            k = rope(k_ref[pl.ds(r0, c), cols].astype(F32) * (DK_RET ** -0.5), r0)
            vb = v_ref[pl.ds(r0, c), cols]
            a = _dot_nt(q.astype(BF16), k.astype(BF16)) * dmat
            o_ref[pl.ds(r0, c), cols] = _dot(a.astype(BF16), vb)
            qk_ref[0, pl.ds(r0, c), cols] = (q * dq_f).astype(BF16)
            qk_ref[1, pl.ds(r0, c), cols] = (q * dq_b).astype(BF16)
            qk_ref[2, pl.ds(r0, c), cols] = (k * dk_f).astype(BF16)
            qk_ref[3, pl.ds(r0, c), cols] = (k * dk_b).astype(BF16)
            return carry

        def advance(n, states):
            out = []
            for d, s, cdec in ((0, states[0], cdec_f), (1, states[1], cdec_b)):
                m = n if d == 0 else n_chunks - 1 - n
                r0 = m * c if isinstance(m, int) else pl.multiple_of(m * c, c)
                vb = v_ref[pl.ds(r0, c), cols]
                if inter:
                    o_ref[pl.ds(r0, c), cols] += _dot(qk_ref[d, pl.ds(r0, c), cols], s.astype(BF16))
                out.append(cdec * s + _dot_tn(qk_ref[2 + d, pl.ds(r0, c), cols], vb))
            return tuple(out)

        _loop(n_chunks, prepare, 0)
        zero = jnp.zeros((DK_RET, DV_RET), F32)
        init = (s0_ref[0, 0, 0, hh], s0_ref[0, 0, 1, hh]) if has_s0 else (zero, zero)
        s_f, s_b = _loop(n_chunks, advance, init)
        if emit_state:
            st_ref[0, 0, hh] = s_f
            st_ref[0, 1, hh] = s_b


def _ret_scan(proj, decay, t, batch, heads, row_block0, rope, s0, s0_layer, emit_state):
    w = heads * DK_RET
    col0 = POOL_WIDTH // w
    per = RET_WIDTH // w
    in_specs = [
        pl.BlockSpec(memory_space=pltpu.SMEM),
        pl.BlockSpec((t, w), lambda b, h: (row_block0 + b, col0 + h)),
        pl.BlockSpec((t, w), lambda b, h: (row_block0 + b, col0 + per + h)),
        pl.BlockSpec((t, w), lambda b, h: (row_block0 + b, col0 + 2 * per + h)),
    ]
    args = [decay, proj, proj, proj]
    if rope is not None:
        assert heads == 1
        in_specs += [pl.BlockSpec((t, DK_RET), lambda b, h: (0, 0))] * 2
        args += list(rope)
    if s0 is not None:
        in_specs.append(pl.BlockSpec((1, 1, 2, heads, DK_RET, DV_RET), lambda b, h: (b, s0_layer, 0, h, 0, 0)))
        args.append(s0)
    out_specs = [pl.BlockSpec((t, w), lambda b, h: (b, h))]
    out_shape = [jax.ShapeDtypeStruct((batch * t, RET_WIDTH), F32)]
    if emit_state:
        out_specs.append(pl.BlockSpec((1, 2, heads, DK_RET, DV_RET), lambda b, h: (b, 0, h, 0, 0)))
        out_shape.append(jax.ShapeDtypeStruct((batch, 2, H_RET, DK_RET, DV_RET), F32))
    return pl.pallas_call(
        functools.partial(_ret_kernel, n_chunks=t // RET_CHUNK, heads=heads, use_rope=rope is not None,
                          has_s0=s0 is not None, emit_state=emit_state),
        grid=(batch, H_RET // heads),
        in_specs=in_specs,
        out_specs=out_specs,
        out_shape=out_shape,
        scratch_shapes=[pltpu.VMEM((4, t, w), BF16)],
        compiler_params=_params("parallel", "parallel"),
        name="ret_scan",
    )(*args)


def _gla_kernel(*refs, n_blocks, dvb, has_s0, emit_state):
    refs = list(refs)
    q_ref, k_ref, v_ref, z_ref, g2_ref, gb_ref = refs[:6]
    refs = refs[6:]
    if has_s0:
        s0_ref = refs[0]
        refs = refs[1:]
    o_ref = refs[0]
    refs = refs[1:]
    if emit_state:
        st_ref = refs[0]
        refs = refs[1:]
    qd_ref, kt_ref, a_ref, dec_ref, sf_ref, sb_ref = refs

    c = CHUNK
    blk = GLA_BLK
    cpb = blk // c
    n_chunks = n_blocks * cpb

    @pl.when(pl.program_id(2) == 0)
    def _():
        ii = lax.broadcasted_iota(jnp.int32, (blk, blk), 0)
        jj = lax.broadcasted_iota(jnp.int32, (blk, blk), 1)
        same = (ii // c) == (jj // c)
        masks = (jnp.logical_and(same, ii >= jj), jnp.logical_and(same, ii <= jj))
        tris = tuple(jnp.tile(jnp.where(m, 1.0, 0.0).astype(BF16), (1, 2)) for m in masks)

        def build(bi, carry):
            r0 = bi * blk if isinstance(bi, int) else pl.multiple_of(bi * blk, blk)
            q = q_ref[pl.ds(r0, blk), :].astype(F32) * (DK_GLA ** -0.5)
            k = k_ref[pl.ds(r0, blk), :].astype(F32)
            zb = z_ref[pl.ds(r0, blk), :]
            scores = None
            for d in (0, 1):
                z = _dot(zb, g2_ref[0, d]) + gb_ref[0, d]
                lg_hi, lg_lo = _split_bf16(_log_sigmoid(z) * (LOG2E / GLA_TAU))
                g = _dot(tris[d], jnp.concatenate([lg_hi, lg_lo], axis=0))
                edge = c - 1 if d == 0 else 0
                g_tot = jnp.concatenate(
                    [jnp.broadcast_to(g[ci * c + edge:ci * c + edge + 1, :], (c, DK_GLA)) for ci in range(cpb)],
                    axis=0)
                qd = (q * jnp.exp2(g)).astype(BF16)
                kd = (k * jnp.exp2(-g)).astype(BF16)
                kt = (k * jnp.exp2(g_tot - g)).astype(BF16)
                a = jnp.where(masks[d], _dot_nt(qd, kd), 0.0)
                scores = a if scores is None else scores + a
                qd_ref[d, pl.ds(r0, blk), :] = qd
                kt_ref[d, pl.ds(r0, blk), :] = kt
                for ci in range(cpb):
                    dec_ref[d * n_chunks + bi * cpb + ci] = jnp.exp2(g_tot[ci * c:ci * c + 8, :])
            a_ref[bi] = scores.astype(BF16)
            return carry

        _loop(n_blocks, build, 0, unroll=2)

    def intra(bi, carry):
        r0 = bi * blk if isinstance(bi, int) else pl.multiple_of(bi * blk, blk)
        o_ref[pl.ds(r0, blk), :] = _dot(a_ref[bi], v_ref[pl.ds(r0, blk), :])
        return carry

    _loop(n_blocks, intra, 0)

    zero = jnp.zeros((dvb, DK_GLA), F32)
    sf_ref[...] = s0_ref[0, 0, 0, 0].T if has_s0 else zero
    sb_ref[...] = s0_ref[0, 0, 1, 0].T if has_s0 else zero

    def advance(n, carry):
        for d, s_ref in ((0, sf_ref), (1, sb_ref)):
            m = n if d == 0 else n_chunks - 1 - n
            r0 = pl.multiple_of(m * c, c)
            vb = v_ref[pl.ds(r0, c), :]
            s = s_ref[...]
            o_ref[pl.ds(r0, c), :] += _dot_nt(qd_ref[d, pl.ds(r0, c), :], s.astype(BF16))
            s_ref[...] = s * dec_ref[d * n_chunks + m][0:1, :] + _dot_tn(vb, kt_ref[d, pl.ds(r0, c), :])
        return carry

    lax.fori_loop(0, n_chunks, advance, 0, unroll=2)
    if emit_state:
        st_ref[0, 0, 0] = sf_ref[...].T
        st_ref[0, 1, 0] = sb_ref[...].T


def _gla_scan(proj, z1, g2, gb, t, batch, dvb, row_block0, s0, s0_layer, emit_state):
    nd = DV_GLA // dvb
    kcol = GLA_DK_TOTAL // DK_GLA
    vcol = 2 * GLA_DK_TOTAL // dvb
    n_blocks = t // GLA_BLK
    n_chunks = t // CHUNK
    in_specs = [
        pl.BlockSpec((t, DK_GLA), lambda b, h, d: (row_block0 + b, h)),
        pl.BlockSpec((t, DK_GLA), lambda b, h, d: (row_block0 + b, kcol + h)),
        pl.BlockSpec((t, dvb), lambda b, h, d: (row_block0 + b, vcol + h * nd + d)),
        pl.BlockSpec((t, 128), lambda b, h, d: (row_block0 + b, 0)),
        pl.BlockSpec((1, 2, 128, DK_GLA), lambda b, h, d: (h, 0, 0, 0)),
        pl.BlockSpec((1, 2, 1, DK_GLA), lambda b, h, d: (h, 0, 0, 0)),
    ]
    args = [proj, proj, proj, z1, g2, gb]
    if s0 is not None:
        in_specs.append(pl.BlockSpec((1, 1, 2, 1, DK_GLA, dvb), lambda b, h, d: (b, s0_layer, 0, h, 0, d)))
        args.append(s0)
    out_specs = [pl.BlockSpec((t, dvb), lambda b, h, d: (b, h * nd + d))]
    out_shape = [jax.ShapeDtypeStruct((batch * t, GLA_DV_TOTAL), F32)]
    if emit_state:
        out_specs.append(pl.BlockSpec((1, 2, 1, DK_GLA, dvb), lambda b, h, d: (b, 0, h, 0, d)))
        out_shape.append(jax.ShapeDtypeStruct((batch, 2, H_GLA, DK_GLA, DV_GLA), F32))
    return pl.pallas_call(
        functools.partial(_gla_kernel, n_blocks=n_blocks, dvb=dvb, has_s0=s0 is not None, emit_state=emit_state),
        grid=(batch, H_GLA, nd),
        in_specs=in_specs,
        out_specs=out_specs,
        out_shape=out_shape,
        scratch_shapes=[
            pltpu.VMEM((2, t, DK_GLA), BF16),
            pltpu.VMEM((2, t, DK_GLA), BF16),
            pltpu.VMEM((n_blocks, GLA_BLK, GLA_BLK), BF16),
            pltpu.VMEM((2 * n_chunks, 8, DK_GLA), F32),
            pltpu.VMEM((dvb, DK_GLA), F32),
            pltpu.VMEM((dvb, DK_GLA), F32),
        ],
        compiler_params=_params("parallel", "parallel", "arbitrary"),
        name="gla_scan",
    )(*args)


def _out_proj_kernel(*refs, n_heads, width, with_pool):
    refs = list(refs)
    if with_pool:
        yp_ref = refs[0]
        refs = refs[1:]
    op_ref, os_ref, gate_ref, gn_ref, w_ref, x_ref, g1_ref, out_ref = refs
    in_prompt = pl.program_id(0) < N_PROMPT // TM

    def run(o_ref):
        parts = [yp_ref[...]] if with_pool else []
        for hd in range(n_heads):
            cols = slice(hd * width, (hd + 1) * width)
            gt = gate_ref[:, cols].astype(F32)
            parts.append((_rms(o_ref[:, cols]) * gn_ref[:, cols] * _silu(gt)).astype(BF16))
        a = jnp.concatenate(parts, axis=1)
        out_ref[...] = x_ref[...] + g1_ref[0] * _dot(a, w_ref[...])

    @pl.when(in_prompt)
    def _():
        run(op_ref)

    @pl.when(jnp.logical_not(in_prompt))
    def _():
        run(os_ref)


def _out_proj(x, mod, layer, w, gate_proj, gate_col_block, norm_g, o_prompt, o_sample, n_heads, y_pool=None):
    m = x.shape[0]
    width = o_prompt.shape[1]
    p_tiles = N_PROMPT // TM
    in_specs = [
        pl.BlockSpec((TM, width), lambda i: (jnp.minimum(i, p_tiles - 1), 0)),
        pl.BlockSpec((TM, width), lambda i: (jnp.maximum(i - p_tiles, 0), 0)),
        pl.BlockSpec((TM, width), lambda i: (i, gate_col_block)),
        pl.BlockSpec((1, width), lambda i: (0, 0)),
        pl.BlockSpec((D_MODEL, D_MODEL), lambda i: (0, 0)),
        pl.BlockSpec((TM, D_MODEL), lambda i: (i, 0)),
        _mod_spec(layer, 2, TM),
    ]
    args = [o_prompt, o_sample, gate_proj, norm_g.reshape(1, width), w, x, mod]
    if y_pool is not None:
        in_specs = [pl.BlockSpec((TM, POOL_WIDTH), lambda i: (i, 0))] + in_specs
        args = [y_pool] + args
    return pl.pallas_call(
        functools.partial(_out_proj_kernel, n_heads=n_heads, width=width // n_heads,
                          with_pool=y_pool is not None),
        grid=(m // TM,),
        in_specs=in_specs,
        out_specs=pl.BlockSpec((TM, D_MODEL), lambda i: (i, 0)),
        out_shape=jax.ShapeDtypeStruct((m, D_MODEL), F32),
        compiler_params=_params("parallel"),
        name="out_proj",
    )(*args)


def _ffn_kernel(x_ref, xp_ref, xn_ref, gn_ref, sh_ref, sc_ref, gt_ref, wa_ref, wb_ref, cw_ref, cb_ref,
                wd_ref, fg_ref, *rest, final):
    if final:
        op_ref, o_ref, h_ref, hh_ref = rest
    else:
        o_ref, h_ref, hh_ref = rest
    i = pl.program_id(0)
    f = pl.program_id(1)
    tm = TM_FFN

    def modnorm(x):
        return (_rms(x) * gn_ref[0] * (1.0 + sc_ref[0]) + sh_ref[0]).astype(BF16)

    @pl.when(f == 0)
    def _():
        h_ref[...] = modnorm(x_ref[...])
        hh_ref[...] = modnorm(jnp.concatenate([xp_ref[...], xn_ref[...]], axis=0))
        o_ref[...] = jnp.zeros_like(o_ref)

    a = _dot(h_ref[...], wa_ref[...])
    b = _dot(h_ref[...], wb_ref[...])
    a_halo = _dot(hh_ref[...], wa_ref[...])

    seq = jnp.where(i * tm < N_PROMPT, SEQ, DEC_SEQ)
    t = lax.broadcasted_iota(jnp.int32, (tm, TF), 0)
    pos = (i * tm + t) & (seq - 1)
    a_prev = jnp.where(t == 0, a_halo[7:8, :], pltpu.roll(a, 1, axis=0))
    a_prev = jnp.where(pos == 0, 0.0, a_prev)
    a_next = jnp.where(t == tm - 1, a_halo[8:9, :], pltpu.roll(a, tm - 1, axis=0))
    a_next = jnp.where(pos == seq - 1, 0.0, a_next)
    conv = a_prev * cw_ref[0:1, :] + a * cw_ref[1:2, :] + a_next * cw_ref[2:3, :] + cb_ref[...]
    o_ref[...] += _dot((_silu(conv) * b).astype(BF16), wd_ref[...])

    @pl.when(f == pl.num_programs(1) - 1)
    def _():
        y = x_ref[...] + gt_ref[0] * o_ref[...]
        if not final:
            o_ref[...] = y
        else:
            y = _rms(y) * fg_ref[...]
            in_prompt = i < N_PROMPT // tm

            @pl.when(in_prompt)
            def _():
                op_ref[...] = y

            @pl.when(jnp.logical_not(in_prompt))
            def _():
                o_ref[...] = y


def _ffn(x, mod, layer, norm_g, w_up, conv_w, conv_b, w_down, final_g, final):
    m = x.shape[0]
    nf = D_FF // TF
    tm = TM_FFN
    halo = 8
    last = m // halo - 1
    if final:
        p_tiles = N_PROMPT // tm
        out_specs = [pl.BlockSpec((tm, D_MODEL), lambda i, f: (jnp.minimum(i, p_tiles - 1), 0),
                                  pipeline_mode=pl.Buffered(1)),
                     pl.BlockSpec((tm, D_MODEL), lambda i, f: (jnp.maximum(i - p_tiles, 0), 0),
                                  pipeline_mode=pl.Buffered(1))]
        out_shape = [jax.ShapeDtypeStruct((N_PROMPT, D_MODEL), F32),
                     jax.ShapeDtypeStruct((N_SAMPLE, D_MODEL), F32)]
    else:
        out_specs = pl.BlockSpec((tm, D_MODEL), lambda i, f: (i, 0))
        out_shape = jax.ShapeDtypeStruct((m, D_MODEL), F32)
    return pl.pallas_call(
        functools.partial(_ffn_kernel, final=final),
        grid=(m // tm, nf),
        in_specs=[
            pl.BlockSpec((tm, D_MODEL), lambda i, f: (i, 0), pipeline_mode=pl.Buffered(1)),
            pl.BlockSpec((halo, D_MODEL), lambda i, f: (jnp.maximum(i * (tm // halo) - 1, 0), 0)),
            pl.BlockSpec((halo, D_MODEL), lambda i, f: (jnp.minimum((i + 1) * (tm // halo), last), 0)),
            pl.BlockSpec((1, 1, D_MODEL), lambda i, f: (layer, 0, 0)),
            _mod_spec(layer, 3, tm),
            _mod_spec(layer, 4, tm),
            _mod_spec(layer, 5, tm),
            pl.BlockSpec((D_MODEL, TF), lambda i, f: (0, f)),
            pl.BlockSpec((D_MODEL, TF), lambda i, f: (0, nf + f)),
            pl.BlockSpec((3, TF), lambda i, f: (0, f)),
            pl.BlockSpec((1, TF), lambda i, f: (0, f)),
            pl.BlockSpec((TF, D_MODEL), lambda i, f: (f, 0)),
            pl.BlockSpec((1, D_MODEL), lambda i, f: (0, 0)),
        ],
        out_specs=out_specs,
        out_shape=out_shape,
        scratch_shapes=[
            pltpu.VMEM((tm, D_MODEL), BF16),
            pltpu.VMEM((2 * halo, D_MODEL), BF16),
        ],
        compiler_params=_params("arbitrary", "arbitrary"),
        name="ffn",
    )(x, x, x, norm_g, mod, mod, mod, w_up, w_up, conv_w, conv_b.reshape(1, D_FF), w_down,
      final_g.reshape(1, D_MODEL))


def _gla_gate_weights(gw2, gb):
    r = GLA_GATE_RANK
    g2 = jnp.zeros((2, 128, GLA_DK_TOTAL), F32)
    g2 = g2.at[0, 0:r].set(gw2[0]).at[1, r:2 * r].set(gw2[1])
    g2 = g2.reshape(2, 128, H_GLA, DK_GLA).transpose(2, 0, 1, 3).astype(BF16)
    gbh = gb.reshape(2, H_GLA, 1, DK_GLA).transpose(1, 0, 2, 3)
    return g2, gbh


def kernel(x_prompt, x_sample, state_ret, state_gla, c, c_ctx, ada_w, ada_b, norm1_g, norm2_g,
           even_w_in, pool_w, pool_scale, ret_decay, ret_norm_g, even_w_out, odd_w_in, gla_gw1,
           gla_gw2, gla_gb, gla_norm_g, odd_w_out, ffn_w_up, ffn_conv_w, ffn_conv_b, ffn_w_down, final_g):
    x = jnp.concatenate([x_prompt.reshape(N_PROMPT, D_MODEL), x_sample.reshape(N_SAMPLE, D_MODEL)], axis=0)
    cvec = jnp.concatenate([c_ctx[None, :], c, jnp.zeros((COND_PAD - N_COND, D_MODEL), F32)], axis=0)
    mod = _ada_mod(cvec, ada_w, ada_b)
    n1 = norm1_g.reshape(DEPTH, 1, D_MODEL)
    n2 = norm2_g.reshape(DEPTH, 1, D_MODEL)
    rope = _rope_tables(DEC_SEQ)
    p_blocks = N_PROMPT // DEC_SEQ

    ret_states, gla_states = [], []
    for l in range(DEPTH):
        if l % 2 == 0:
            i = l // 2
            proj = _in_proj(x, n1, mod, l, even_w_in[i].astype(BF16), IN_SPLIT)
            y_pool = _pool_mixer(proj, pool_w[i].astype(BF16), pool_scale[i])
            o_p, st = _ret_scan(proj, ret_decay[i], SEQ, BATCH, H_RET, 0, None, None, 0, True)
            (o_s,) = _ret_scan(proj, ret_decay[i], DEC_SEQ, DEC_BATCH, 1, p_blocks, rope, state_ret, i, False)
            ret_states.append(st)
            x = _out_proj(x, mod, l, even_w_out[i].astype(BF16), proj, EVEN_IN // RET_WIDTH - 1,
                          ret_norm_g[i], o_p, o_s, H_RET, y_pool)
        else:
            j = l // 2
            proj = _in_proj(x, n1, mod, l, odd_w_in[j].astype(BF16), IN_SPLIT)
            gw1 = jnp.concatenate([gla_gw1[j, 0], gla_gw1[j, 1],
                                   jnp.zeros((D_MODEL, 128 - 2 * GLA_GATE_RANK), F32)], axis=1)
            z1 = _in_proj(x, n1, mod, l, gw1.astype(BF16), 1)
            g2, gbh = _gla_gate_weights(gla_gw2[j], gla_gb[j])
            o_p, st = _gla_scan(proj, z1, g2, gbh, SEQ, BATCH, DV_GLA, 0, None, 0, True)
            (o_s,) = _gla_scan(proj, z1, g2, gbh, DEC_SEQ, DEC_BATCH, DV_GLA, p_blocks, state_gla, j, False)
            gla_states.append(st)
            x = _out_proj(x, mod, l, odd_w_out[j].astype(BF16), proj, ODD_IN // GLA_DV_TOTAL - 1,
                          gla_norm_g[j], o_p, o_s, H_GLA)
        x = _ffn(x, mod, l, n2, ffn_w_up[l].astype(BF16), ffn_conv_w[l], ffn_conv_b[l],
                 ffn_w_down[l].astype(BF16), final_g, l == DEPTH - 1)

    y_prompt = x[0].reshape(BATCH, SEQ, D_MODEL)
    y_sample = x[1].reshape(DEC_BATCH, DEC_SEQ, D_MODEL)
    return (y_prompt, y_sample, jnp.stack(ret_states, axis=1), jnp.stack(gla_states, axis=1))
```

```python
import functools

import numpy as np
import jax
import jax.numpy as jnp
from jax import lax
from jax.experimental import pallas as pl
from jax.experimental.pallas import tpu as pltpu

F32 = jnp.float32
BF16 = jnp.bfloat16

D_MODEL = 2048
BATCH = 16
SEQ = 256
DEPTH = 4
DEC_BATCH = 2
DEC_SEQ = 4096
GRID_W = 64
N_EVEN = (DEPTH + 1) // 2
N_ODD = DEPTH // 2
POOL_WIDTH = D_MODEL // 2
POOL_GROUPS = 4
POOL_GC = POOL_WIDTH // POOL_GROUPS
POOL_WINDOWS = (2, 4, 8, 16)
RET_WIDTH = D_MODEL // 2
H_RET = 8
DK_RET = RET_WIDTH // H_RET
DV_RET = RET_WIDTH // H_RET
ROPE_BASE = 10000.0
H_GLA = 4
GLA_DK_TOTAL = D_MODEL // 2
GLA_DV_TOTAL = D_MODEL
DK_GLA = GLA_DK_TOTAL // H_GLA
DV_GLA = GLA_DV_TOTAL // H_GLA
GLA_GATE_RANK = 16
GLA_TAU = 16.0
D_FF = 5632
CHUNK = 64
EPS = 1e-6
EVEN_IN = POOL_WIDTH + 4 * RET_WIDTH
ODD_IN = 2 * GLA_DK_TOTAL + 2 * GLA_DV_TOTAL

N_PROMPT = BATCH * SEQ
N_SAMPLE = DEC_BATCH * DEC_SEQ
N_TOK = N_PROMPT + N_SAMPLE
N_COND = 1 + DEC_BATCH
COND_PAD = 8

VMEM_LIMIT = 56 * 1024 * 1024

TM = 512
IN_SPLIT = 2
IN_CHUNK = 512
TM_FFN = 1024
TF = 512
POOL_TM = 256
POOL_HALO = 16
ADA_TN = 1024
RET_CHUNK = 256
GLA_BLK = 256


def _params(*sem):
    return pltpu.CompilerParams(dimension_semantics=sem, vmem_limit_bytes=VMEM_LIMIT)


def _cond_of_tile(i, tm):
    r0 = i * tm
    return jnp.where(r0 < N_PROMPT, 0, 1 + (r0 - N_PROMPT) // DEC_SEQ)


LOG2E = 1.4426950408889634


def _log_sigmoid(x):
    return jnp.minimum(x, 0.0) - jnp.log(1.0 + jnp.exp2(jnp.abs(x) * (-LOG2E)))


def _silu(x):
    return x * jax.nn.sigmoid(x)


def _rms(x):
    return x * lax.rsqrt(jnp.mean(x * x, axis=-1, keepdims=True) + EPS)


def _dot(a, b):
    return jnp.dot(a, b, preferred_element_type=F32)


def _dot_nt(a, b):
    return lax.dot_general(a, b, (((1,), (1,)), ((), ())), preferred_element_type=F32)


def _dot_tn(a, b):
    return lax.dot_general(a, b, (((0,), (0,)), ((), ())), preferred_element_type=F32)


def _split_bf16(x):
    hi = x.astype(BF16)
    lo = (x - hi.astype(F32)).astype(BF16)
    return hi, lo


def _loop(n, body, init, unroll=1):
    if n == 1:
        return body(0, init)
    return lax.fori_loop(0, n, body, init, unroll=unroll)


def _ada_kernel(c_ref, w_ref, b_ref, o_ref):
    s = _silu(c_ref[...]).astype(BF16)
    o_ref[0] = _dot(s, w_ref[0].astype(BF16)) + b_ref[0]


def _ada_mod(cvec, ada_w, ada_b):
    n = 6 * D_MODEL
    mod = pl.pallas_call(
        _ada_kernel,
        grid=(DEPTH, n // ADA_TN),
        in_specs=[
            pl.BlockSpec((COND_PAD, D_MODEL), lambda l, j: (0, 0)),
            pl.BlockSpec((1, D_MODEL, ADA_TN), lambda l, j: (l, 0, j)),
            pl.BlockSpec((1, 1, ADA_TN), lambda l, j: (l, 0, j)),
        ],
        out_specs=pl.BlockSpec((1, COND_PAD, ADA_TN), lambda l, j: (l, 0, j)),
        out_shape=jax.ShapeDtypeStruct((DEPTH, COND_PAD, n), F32),
        compiler_params=_params("parallel", "parallel"),
        name="ada_mod",
    )(cvec, ada_w, ada_b.reshape(DEPTH, 1, n))
    mod = mod[:, :N_COND].reshape(DEPTH, N_COND, 6, D_MODEL).transpose(0, 2, 1, 3)
    return mod.reshape(DEPTH * 6 * N_COND, 1, D_MODEL)


def _mod_spec(layer, part, tm, row_axis=0):
    base = (layer * 6 + part) * N_COND
    return pl.BlockSpec((1, 1, D_MODEL), lambda *idx: (base + _cond_of_tile(idx[row_axis], tm), 0, 0))


def _in_proj_kernel(x_ref, g_ref, sh_ref, sc_ref, w_ref, o_ref, h_ref, *, chunk):
    h = _rms(x_ref[...]) * g_ref[0]
    h_ref[...] = (h * (1.0 + sc_ref[0]) + sh_ref[0]).astype(BF16)
    for c0 in range(0, o_ref.shape[1], chunk):
        o_ref[:, c0:c0 + chunk] = _dot(h_ref[...], w_ref[:, c0:c0 + chunk]).astype(o_ref.dtype)


def _in_proj(x, norm_g, mod, layer, w, w_layer, split):
    m, n = x.shape[0], w.shape[2]
    tn = n // split
    return pl.pallas_call(
        functools.partial(_in_proj_kernel, chunk=min(IN_CHUNK, tn)),
        grid=(split, m // TM),
        in_specs=[
            pl.BlockSpec((TM, D_MODEL), lambda j, i: (i, 0)),
            pl.BlockSpec((1, 1, D_MODEL), lambda j, i: (layer, 0, 0)),
            _mod_spec(layer, 0, TM, 1),
            _mod_spec(layer, 1, TM, 1),
            pl.BlockSpec((None, D_MODEL, tn), lambda j, i: (w_layer, 0, j)),
        ],
        out_specs=pl.BlockSpec((TM, tn), lambda j, i: (i, j)),
        out_shape=jax.ShapeDtypeStruct((m, n), BF16),
        scratch_shapes=[pltpu.VMEM((TM, D_MODEL), BF16)],
        compiler_params=_params("arbitrary", "arbitrary"),
        name="in_proj",
    )(x, norm_g, mod, mod, w)


def _pool_bands():
    t = POOL_TM
    b0 = np.zeros((POOL_GROUPS, t, t), np.float32)
    bp = np.zeros((POOL_GROUPS, t, POOL_HALO), np.float32)
    bn = np.zeros((POOL_GROUPS, t, POOL_HALO), np.float32)
    for g, win in enumerate(POOL_WINDOWS):
        for r in range(t):
            for s in range(r - win // 2, r + win - win // 2):
                if s < 0:
                    bp[g, r, s + POOL_HALO] = 1.0
                elif s >= t:
                    bn[g, r, s - t] = 1.0
                else:
                    b0[g, r, s] = 1.0
    return jnp.asarray(b0, BF16), jnp.asarray(bp, BF16), jnp.asarray(bn, BF16)


def _pool_kernel(u_ref, up_ref, un_ref, b0_ref, bp_ref, bn_ref, pw_ref, sc_ref, o_ref):
    i = pl.program_id(0)
    tiles_per_seq = DEC_SEQ // POOL_TM
    in_prompt = i < N_PROMPT // POOL_TM
    pos = (i - N_PROMPT // POOL_TM) % tiles_per_seq
    is_start = jnp.logical_or(in_prompt, pos == 0)
    is_end = jnp.logical_or(in_prompt, pos == tiles_per_seq - 1)
    t = lax.broadcasted_iota(jnp.int32, (POOL_TM, POOL_GC), 0)
    for g, win in enumerate(POOL_WINDOWS):
        cols = slice(g * POOL_GC, (g + 1) * POOL_GC)
        ub = u_ref[:, cols]
        up = up_ref[:, cols]
        un = un_ref[:, cols]
        up = jnp.where(is_start, jnp.zeros_like(up), up)
        un = jnp.where(is_end, jnp.zeros_like(un), un)
        s = _dot(b0_ref[g], ub) + _dot(bp_ref[g], up) + _dot(bn_ref[g], un)
        u = ub.astype(F32)
        cut_lo = jnp.where(is_start, jnp.maximum(win // 2 - t, 0), 0)
        cut_hi = jnp.where(is_end, jnp.maximum(t + (win - win // 2) - POOL_TM, 0), 0)
        cnt = (win - cut_lo - cut_hi).astype(F32)
        pooled = s / cnt - u
        y = _dot(pooled.astype(BF16), pw_ref[g])
        o_ref[:, cols] = (y * sc_ref[:, cols]).astype(o_ref.dtype)


def _pool_mixer(proj, pool_w, pool_scale, w_layer):
    b0, bp, bn = _pool_bands()
    halo_blocks = POOL_TM // POOL_HALO
    last = N_TOK // POOL_HALO - 1
    full = lambda shape: pl.BlockSpec(shape, lambda i: (0,) * len(shape))
    return pl.pallas_call(
        _pool_kernel,
        grid=(N_TOK // POOL_TM,),
        in_specs=[
            pl.BlockSpec((POOL_TM, POOL_WIDTH), lambda i: (i, 0)),
            pl.BlockSpec((POOL_HALO, POOL_WIDTH), lambda i: (jnp.maximum(i * halo_blocks - 1, 0), 0)),
            pl.BlockSpec((POOL_HALO, POOL_WIDTH), lambda i: (jnp.minimum((i + 1) * halo_blocks, last), 0)),
            full(b0.shape), full(bp.shape), full(bn.shape),
            pl.BlockSpec((None,) + pool_w.shape[1:], lambda i: (w_layer, 0, 0, 0)),
            pl.BlockSpec((None, 1, POOL_WIDTH), lambda i: (w_layer, 0, 0)),
        ],
        out_specs=pl.BlockSpec((POOL_TM, POOL_WIDTH), lambda i: (i, 0)),
        out_shape=jax.ShapeDtypeStruct((N_TOK, POOL_WIDTH), BF16),
        compiler_params=_params("parallel"),
        name="pool_mixer",
    )(proj, proj, proj, b0, bp, bn, pool_w, pool_scale.reshape(N_EVEN, 1, POOL_WIDTH))


def _rope_tables(t):
    nf = DK_RET // 4
    rows = t // GRID_W
    r = jnp.repeat(jnp.arange(rows), GRID_W).astype(F32)
    col = jnp.tile(jnp.arange(GRID_W), rows).astype(F32)
    inv = ROPE_BASE ** (-jnp.arange(nf, dtype=F32) / nf)
    ar, ac = r[:, None] * inv, col[:, None] * inv
    cos = jnp.concatenate([jnp.cos(ar), jnp.cos(ar), jnp.cos(ac), jnp.cos(ac)], axis=1)
    sin = jnp.concatenate([-jnp.sin(ar), jnp.sin(ar), -jnp.sin(ac), jnp.sin(ac)], axis=1)
    return cos, sin


def _ret_kernel(*refs, n_chunks, heads, use_rope, has_s0, emit_state):
    refs = list(refs)
    dec_ref, q_ref, k_ref, v_ref = refs[:4]
    refs = refs[4:]
    if use_rope:
        cos_ref, sin_ref = refs[:2]
        refs = refs[2:]
    if has_s0:
        s0_ref = refs[0]
        refs = refs[1:]
    o_ref = refs[0]
    refs = refs[1:]
    if emit_state:
        st_ref = refs[0]
        refs = refs[1:]
    (qk_ref,) = refs

    c = RET_CHUNK
    nf = DK_RET // 4
    head0 = pl.program_id(1) * heads
    inter = has_s0 or n_chunks > 1

    row = lax.broadcasted_iota(jnp.int32, (c, DK_RET), 0).astype(F32)
    ii = lax.broadcasted_iota(jnp.int32, (c, c), 0)
    jj = lax.broadcasted_iota(jnp.int32, (c, c), 1)
    dist = (ii - jj).astype(F32)
    lane = lax.broadcasted_iota(jnp.int32, (c, DK_RET), 1)
    first_half = (lane % (2 * nf)) < nf

    def rope(x, r0):
        if not use_rope:
            return x
        partner = jnp.where(first_half, pltpu.roll(x, DK_RET - nf, axis=1), pltpu.roll(x, nf, axis=1))
        return x * cos_ref[pl.ds(r0, c), :] + partner * sin_ref[pl.ds(r0, c), :]

    for hh in range(heads):
        cols = slice(hh * DK_RET, (hh + 1) * DK_RET)

        def lam(d, shape):
            return _log_sigmoid(jnp.full(shape, dec_ref[d, head0 + hh], F32))

        lam_f, lam_b = lam(0, (c, DK_RET)), lam(1, (c, DK_RET))
        dq_f = jnp.exp((row + 1.0) * lam_f)
        dk_f = jnp.exp((c - 1.0 - row) * lam_f)
        dq_b = jnp.exp((c - row) * lam_b)
        dk_b = jnp.exp(row * lam_b)
        cdec_f = jnp.exp(float(c) * lam(0, (DK_RET, DV_RET)))
        cdec_b = jnp.exp(float(c) * lam(1, (DK_RET, DV_RET)))
        dmat = jnp.where(ii > jj, jnp.exp(dist * lam(0, (c, c))),
                         jnp.where(ii < jj, jnp.exp(-dist * lam(1, (c, c))), 2.0))

        group = 2 if n_chunks % 2 == 0 else 1

        def prepare(gi, carry):
            ns = [gi * group + j for j in range(group)]
            r0s = [n * c if isinstance(n, int) else pl.multiple_of(n * c, c) for n in ns]
            q = [rope(q_ref[pl.ds(r0, c), cols].astype(F32), r0) for r0 in r0s]
            k = [rope(k_ref[pl.ds(r0, c), cols].astype(F32) * (DK_RET ** -0.5), r0) for r0 in r0s]
            a = [(_dot_nt(q[j].astype(BF16), k[j].astype(BF16)) * dmat).astype(BF16) for j in range(group)]
            for j, r0 in enumerate(r0s):
                qk_ref[0, pl.ds(r0, c), cols] = (q[j] * dq_f).astype(BF16)
                qk_ref[1, pl.ds(r0, c), cols] = (q[j] * dq_b).astype(BF16)
                qk_ref[2, pl.ds(r0, c), cols] = (k[j] * dk_f).astype(BF16)
                qk_ref[3, pl.ds(r0, c), cols] = (k[j] * dk_b).astype(BF16)
            o = [_dot(a[j], v_ref[pl.ds(r0s[j], c), cols]) for j in range(group)]
            for j, r0 in enumerate(r0s):
                o_ref[pl.ds(r0, c), cols] = o[j]
            return carry

        def advance(n, states):
            ms = (n, n_chunks - 1 - n)
            r0s = [m * c if isinstance(m, int) else pl.multiple_of(m * c, c) for m in ms]
            upd = [_dot_tn(qk_ref[2 + d, pl.ds(r0s[d], c), cols], v_ref[pl.ds(r0s[d], c), cols]) for d in (0, 1)]
            out = []
            for d, cdec in ((0, cdec_f), (1, cdec_b)):
                if inter:
                    o_ref[pl.ds(r0s[d], c), cols] += _dot(qk_ref[d, pl.ds(r0s[d], c), cols],
                                                          states[d].astype(BF16))
                out.append(cdec * states[d] + upd[d])
            return tuple(out)

        _loop(n_chunks // group, prepare, 0)
        zero = jnp.zeros((DK_RET, DV_RET), F32)
        init = (s0_ref[0, 0, 0, hh], s0_ref[0, 0, 1, hh]) if has_s0 else (zero, zero)
        s_f, s_b = _loop(n_chunks, advance, init)
        if emit_state:
            st_ref[0, 0, hh] = s_f
            st_ref[0, 1, hh] = s_b


def _ret_scan(proj, decay, t, batch, heads, row_block0, rope, s0, s0_layer, emit_state):
    w = heads * DK_RET
    col0 = POOL_WIDTH // w
    per = RET_WIDTH // w
    in_specs = [
        pl.BlockSpec(memory_space=pltpu.SMEM),
        pl.BlockSpec((t, w), lambda b, h: (row_block0 + b, col0 + h)),
        pl.BlockSpec((t, w), lambda b, h: (row_block0 + b, col0 + per + h)),
        pl.BlockSpec((t, w), lambda b, h: (row_block0 + b, col0 + 2 * per + h)),
    ]
    args = [decay, proj, proj, proj]
    if rope is not None:
        assert heads == 1
        in_specs += [pl.BlockSpec((t, DK_RET), lambda b, h: (0, 0))] * 2
        args += list(rope)
    if s0 is not None:
        in_specs.append(pl.BlockSpec((1, 1, 2, heads, DK_RET, DV_RET), lambda b, h: (b, s0_layer, 0, h, 0, 0)))
        args.append(s0)
    out_specs = [pl.BlockSpec((t, w), lambda b, h: (b, h))]
    out_shape = [jax.ShapeDtypeStruct((batch * t, RET_WIDTH), F32)]
    if emit_state:
        out_specs.append(pl.BlockSpec((1, 2, heads, DK_RET, DV_RET), lambda b, h: (b, 0, h, 0, 0)))
        out_shape.append(jax.ShapeDtypeStruct((batch, 2, H_RET, DK_RET, DV_RET), F32))
    return pl.pallas_call(
        functools.partial(_ret_kernel, n_chunks=t // RET_CHUNK, heads=heads, use_rope=rope is not None,
                          has_s0=s0 is not None, emit_state=emit_state),
        grid=(batch, H_RET // heads),
        in_specs=in_specs,
        out_specs=out_specs,
        out_shape=out_shape,
        scratch_shapes=[pltpu.VMEM((4, t, w), BF16)],
        compiler_params=_params("parallel", "parallel"),
        name="ret_scan",
    )(*args)


def _gla_kernel(*refs, n_blocks, dvb, has_s0, emit_state):
    refs = list(refs)
    q_ref, k_ref, v_ref, z_ref, g2_ref, gb_ref = refs[:6]
    refs = refs[6:]
    if has_s0:
        s0_ref = refs[0]
        refs = refs[1:]
    o_ref = refs[0]
    refs = refs[1:]
    if emit_state:
        st_ref = refs[0]
        refs = refs[1:]
    qd_ref, kt_ref, a_ref, dec_ref, sf_ref, sb_ref = refs

    c = CHUNK
    blk = GLA_BLK
    cpb = blk // c
    n_chunks = n_blocks * cpb

    @pl.when(pl.program_id(2) == 0)
    def _():
        ii = lax.broadcasted_iota(jnp.int32, (blk, blk), 0)
        jj = lax.broadcasted_iota(jnp.int32, (blk, blk), 1)
        same = (ii // c) == (jj // c)
        masks = (jnp.logical_and(same, ii >= jj), jnp.logical_and(same, ii <= jj))
        tris = tuple(jnp.tile(jnp.where(m, 1.0, 0.0).astype(BF16), (1, 2)) for m in masks)

        group = 2 if n_blocks % 2 == 0 else 1
        chains = [(j, d) for j in range(group) for d in (0, 1)]

        def build(gi, carry):
            bis = [gi * group + j for j in range(group)]
            r0s = [bi * blk if isinstance(bi, int) else pl.multiple_of(bi * blk, blk) for bi in bis]
            q = [q_ref[pl.ds(r0, blk), :].astype(F32) * (DK_GLA ** -0.5) for r0 in r0s]
            k = [k_ref[pl.ds(r0, blk), :].astype(F32) for r0 in r0s]
            z = [_dot(z_ref[pl.ds(r0s[j], blk), :], g2_ref[0, d]) + gb_ref[0, d] for j, d in chains]
            lg = [jnp.concatenate(_split_bf16(_log_sigmoid(zz) * (LOG2E / GLA_TAU)), axis=0) for zz in z]
            g = [_dot(tris[d], lg[n]) for n, (j, d) in enumerate(chains)]
            qds, kds = [], []
            for n, (j, d) in enumerate(chains):
                edge = c - 1 if d == 0 else 0
                g_tot = jnp.concatenate(
                    [jnp.broadcast_to(g[n][ci * c + edge:ci * c + edge + 1, :], (c, DK_GLA))
                     for ci in range(cpb)], axis=0)
                qd = (q[j] * jnp.exp2(g[n])).astype(BF16)
                kds.append((k[j] * jnp.exp2(-g[n])).astype(BF16))
                qds.append(qd)
                qd_ref[d, pl.ds(r0s[j], blk), :] = qd
                kt_ref[d, pl.ds(r0s[j], blk), :] = (k[j] * jnp.exp2(g_tot - g[n])).astype(BF16)
                for ci in range(cpb):
                    dec_ref[d * n_chunks + bis[j] * cpb + ci] = jnp.exp2(g_tot[ci * c:ci * c + 8, :])
            a = [_dot_nt(qds[n], kds[n]) for n in range(len(chains))]
            for j in range(group):
                a_ref[bis[j]] = (jnp.where(masks[0], a[2 * j], 0.0) + jnp.where(masks[1], a[2 * j + 1], 0.0)).astype(BF16)
            return carry

        _loop(n_blocks // group, build, 0)

    def intra(bi, carry):
        r0 = bi * blk if isinstance(bi, int) else pl.multiple_of(bi * blk, blk)
        o_ref[pl.ds(r0, blk), :] = _dot(a_ref[bi], v_ref[pl.ds(r0, blk), :])
        return carry

    _loop(n_blocks, intra, 0)

    zero = jnp.zeros((dvb, DK_GLA), F32)
    sf_ref[...] = s0_ref[0, 0, 0, 0].T if has_s0 else zero
    sb_ref[...] = s0_ref[0, 0, 1, 0].T if has_s0 else zero

    def advance(n, carry):
        ms = (n, n_chunks - 1 - n)
        r0s = [pl.multiple_of(m * c, c) for m in ms]
        upd = [_dot_tn(v_ref[pl.ds(r0s[d], c), :], kt_ref[d, pl.ds(r0s[d], c), :]) for d in (0, 1)]
        for d, s_ref in ((0, sf_ref), (1, sb_ref)):
            s = s_ref[...]
            o_ref[pl.ds(r0s[d], c), :] += _dot_nt(qd_ref[d, pl.ds(r0s[d], c), :], s.astype(BF16))
            s_ref[...] = s * dec_ref[d * n_chunks + ms[d]][0:1, :] + upd[d]
        return carry

    lax.fori_loop(0, n_chunks, advance, 0, unroll=2)
    if emit_state:
        st_ref[0, 0, 0] = sf_ref[...].T
        st_ref[0, 1, 0] = sb_ref[...].T


def _gla_scan(proj, z1, g2, gb, t, batch, dvb, row_block0, s0, s0_layer, emit_state):
    nd = DV_GLA // dvb
    kcol = GLA_DK_TOTAL // DK_GLA
    vcol = 2 * GLA_DK_TOTAL // dvb
    n_blocks = t // GLA_BLK
    n_chunks = t // CHUNK
    in_specs = [
        pl.BlockSpec((t, DK_GLA), lambda b, h, d: (row_block0 + b, h)),
        pl.BlockSpec((t, DK_GLA), lambda b, h, d: (row_block0 + b, kcol + h)),
        pl.BlockSpec((t, dvb), lambda b, h, d: (row_block0 + b, vcol + h * nd + d)),
        pl.BlockSpec((t, 128), lambda b, h, d: (row_block0 + b, 0)),
        pl.BlockSpec((1, 2, 128, DK_GLA), lambda b, h, d: (h, 0, 0, 0)),
        pl.BlockSpec((1, 2, 1, DK_GLA), lambda b, h, d: (h, 0, 0, 0)),
    ]
    args = [proj, proj, proj, z1, g2, gb]
    if s0 is not None:
        in_specs.append(pl.BlockSpec((1, 1, 2, 1, DK_GLA, dvb), lambda b, h, d: (b, s0_layer, 0, h, 0, d)))
        args.append(s0)
    out_specs = [pl.BlockSpec((t, dvb), lambda b, h, d: (b, h * nd + d))]
    out_shape = [jax.ShapeDtypeStruct((batch * t, GLA_DV_TOTAL), F32)]
    if emit_state:
        out_specs.append(pl.BlockSpec((1, 2, 1, DK_GLA, dvb), lambda b, h, d: (b, 0, h, 0, d)))
        out_shape.append(jax.ShapeDtypeStruct((batch, 2, H_GLA, DK_GLA, DV_GLA), F32))
    return pl.pallas_call(
        functools.partial(_gla_kernel, n_blocks=n_blocks, dvb=dvb, has_s0=s0 is not None, emit_state=emit_state),
        grid=(batch, H_GLA, nd),
        in_specs=in_specs,
        out_specs=out_specs,
        out_shape=out_shape,
        scratch_shapes=[
            pltpu.VMEM((2, t, DK_GLA), BF16),
            pltpu.VMEM((2, t, DK_GLA), BF16),
            pltpu.VMEM((n_blocks, GLA_BLK, GLA_BLK), BF16),
            pltpu.VMEM((2 * n_chunks, 8, DK_GLA), F32),
            pltpu.VMEM((dvb, DK_GLA), F32),
            pltpu.VMEM((dvb, DK_GLA), F32),
        ],
        compiler_params=_params("parallel", "parallel", "arbitrary"),
        name="gla_scan",
    )(*args)


def _out_proj_kernel(*refs, n_heads, width, with_pool):
    refs = list(refs)
    if with_pool:
        yp_ref = refs[0]
        refs = refs[1:]
    op_ref, os_ref, gate_ref, gn_ref, w_ref, x_ref, g1_ref, out_ref = refs
    in_prompt = pl.program_id(0) < N_PROMPT // TM

    def run(o_ref):
        parts = [yp_ref[...]] if with_pool else []
        for hd in range(n_heads):
            cols = slice(hd * width, (hd + 1) * width)
            gt = gate_ref[:, cols].astype(F32)
            parts.append((_rms(o_ref[:, cols]) * gn_ref[:, cols] * _silu(gt)).astype(BF16))
        a = jnp.concatenate(parts, axis=1)
        out_ref[...] = x_ref[...] + g1_ref[0] * _dot(a, w_ref[...])

    @pl.when(in_prompt)
    def _():
        run(op_ref)

    @pl.when(jnp.logical_not(in_prompt))
    def _():
        run(os_ref)


def _out_proj(x, mod, layer, w, w_layer, gate_proj, gate_col_block, norm_g, o_prompt, o_sample, n_heads,
              y_pool=None):
    m = x.shape[0]
    width = o_prompt.shape[1]
    p_tiles = N_PROMPT // TM
    in_specs = [
        pl.BlockSpec((TM, width), lambda i: (jnp.minimum(i, p_tiles - 1), 0)),
        pl.BlockSpec((TM, width), lambda i: (jnp.maximum(i - p_tiles, 0), 0)),
        pl.BlockSpec((TM, width), lambda i: (i, gate_col_block)),
        pl.BlockSpec((1, width), lambda i: (0, 0)),
        pl.BlockSpec((None, D_MODEL, D_MODEL), lambda i: (w_layer, 0, 0)),
        pl.BlockSpec((TM, D_MODEL), lambda i: (i, 0)),
        _mod_spec(layer, 2, TM),
    ]
    args = [o_prompt, o_sample, gate_proj, norm_g.reshape(1, width), w, x, mod]
    if y_pool is not None:
        in_specs = [pl.BlockSpec((TM, POOL_WIDTH), lambda i: (i, 0))] + in_specs
        args = [y_pool] + args
    return pl.pallas_call(
        functools.partial(_out_proj_kernel, n_heads=n_heads, width=width // n_heads,
                          with_pool=y_pool is not None),
        grid=(m // TM,),
        in_specs=in_specs,
        out_specs=pl.BlockSpec((TM, D_MODEL), lambda i: (i, 0)),
        out_shape=jax.ShapeDtypeStruct((m, D_MODEL), F32),
        compiler_params=_params("parallel"),
        name="out_proj",
    )(*args)


def _ffn_kernel(x_ref, xp_ref, xn_ref, gn_ref, sh_ref, sc_ref, gt_ref, wa_ref, wb_ref, cw_ref, cb_ref,
                wd_ref, fg_ref, *rest, final):
    if final:
        op_ref, o_ref, h_ref, hh_ref = rest
    else:
        o_ref, h_ref, hh_ref = rest
    i = pl.program_id(0)
    f = pl.program_id(1)
    tm = TM_FFN

    def modnorm(x):
        return (_rms(x) * gn_ref[0] * (1.0 + sc_ref[0]) + sh_ref[0]).astype(BF16)

    @pl.when(f == 0)
    def _():
        h_ref[...] = modnorm(x_ref[...])
        hh_ref[...] = modnorm(jnp.concatenate([xp_ref[...], xn_ref[...]], axis=0))
        o_ref[...] = jnp.zeros_like(o_ref)

    a = _dot(h_ref[...], wa_ref[...])
    b = _dot(h_ref[...], wb_ref[...])
    a_halo = _dot(hh_ref[...], wa_ref[...])

    seq = jnp.where(i * tm < N_PROMPT, SEQ, DEC_SEQ)
    t = lax.broadcasted_iota(jnp.int32, (tm, TF), 0)
    pos = (i * tm + t) & (seq - 1)
    a_prev = jnp.where(t == 0, a_halo[7:8, :], pltpu.roll(a, 1, axis=0))
    a_prev = jnp.where(pos == 0, 0.0, a_prev)
    a_next = jnp.where(t == tm - 1, a_halo[8:9, :], pltpu.roll(a, tm - 1, axis=0))
    a_next = jnp.where(pos == seq - 1, 0.0, a_next)
    conv = a_prev * cw_ref[0:1, :] + a * cw_ref[1:2, :] + a_next * cw_ref[2:3, :] + cb_ref[...]
    o_ref[...] += _dot((_silu(conv) * b).astype(BF16), wd_ref[...])

    @pl.when(f == pl.num_programs(1) - 1)
    def _():
        y = x_ref[...] + gt_ref[0] * o_ref[...]
        if not final:
            o_ref[...] = y
        else:
            y = _rms(y) * fg_ref[...]
            in_prompt = i < N_PROMPT // tm

            @pl.when(in_prompt)
            def _():
                op_ref[...] = y

            @pl.when(jnp.logical_not(in_prompt))
            def _():
                o_ref[...] = y


def _ffn(x, mod, layer, norm_g, w_up, conv_w, conv_b, w_down, final_g, final):
    m = x.shape[0]
    nf = D_FF // TF
    tm = TM_FFN
    halo = 8
    last = m // halo - 1
    if final:
        p_tiles = N_PROMPT // tm
        out_specs = [pl.BlockSpec((tm, D_MODEL), lambda i, f: (jnp.minimum(i, p_tiles - 1), 0),
                                  pipeline_mode=pl.Buffered(1)),
                     pl.BlockSpec((tm, D_MODEL), lambda i, f: (jnp.maximum(i - p_tiles, 0), 0),
                                  pipeline_mode=pl.Buffered(1))]
        out_shape = [jax.ShapeDtypeStruct((N_PROMPT, D_MODEL), F32),
                     jax.ShapeDtypeStruct((N_SAMPLE, D_MODEL), F32)]
    else:
        out_specs = pl.BlockSpec((tm, D_MODEL), lambda i, f: (i, 0))
        out_shape = jax.ShapeDtypeStruct((m, D_MODEL), F32)
    return pl.pallas_call(
        functools.partial(_ffn_kernel, final=final),
        grid=(m // tm, nf),
        in_specs=[
            pl.BlockSpec((tm, D_MODEL), lambda i, f: (i, 0), pipeline_mode=pl.Buffered(1)),
            pl.BlockSpec((halo, D_MODEL), lambda i, f: (jnp.maximum(i * (tm // halo) - 1, 0), 0)),
            pl.BlockSpec((halo, D_MODEL), lambda i, f: (jnp.minimum((i + 1) * (tm // halo), last), 0)),
            pl.BlockSpec((1, 1, D_MODEL), lambda i, f: (layer, 0, 0)),
            _mod_spec(layer, 3, tm),
            _mod_spec(layer, 4, tm),
            _mod_spec(layer, 5, tm),
            pl.BlockSpec((None, D_MODEL, TF), lambda i, f: (layer, 0, f)),
            pl.BlockSpec((None, D_MODEL, TF), lambda i, f: (layer, 0, nf + f)),
            pl.BlockSpec((None, 3, TF), lambda i, f: (layer, 0, f)),
            pl.BlockSpec((None, 1, TF), lambda i, f: (layer, 0, f)),
            pl.BlockSpec((None, TF, D_MODEL), lambda i, f: (layer, f, 0)),
            pl.BlockSpec((1, D_MODEL), lambda i, f: (0, 0)),
        ],
        out_specs=out_specs,
        out_shape=out_shape,
        scratch_shapes=[
            pltpu.VMEM((tm, D_MODEL), BF16),
            pltpu.VMEM((2 * halo, D_MODEL), BF16),
        ],
        compiler_params=_params("arbitrary", "arbitrary"),
        name="ffn",
    )(x, x, x, norm_g, mod, mod, mod, w_up, w_up, conv_w, conv_b.reshape(DEPTH, 1, D_FF), w_down,
      final_g.reshape(1, D_MODEL))


def _gla_gate_weights(gw2, gb):
    r = GLA_GATE_RANK
    g2 = jnp.zeros((2, 128, GLA_DK_TOTAL), F32)
    g2 = g2.at[0, 0:r].set(gw2[0]).at[1, r:2 * r].set(gw2[1])
    g2 = g2.reshape(2, 128, H_GLA, DK_GLA).transpose(2, 0, 1, 3).astype(BF16)
    gbh = gb.reshape(2, H_GLA, 1, DK_GLA).transpose(1, 0, 2, 3)
    return g2, gbh


def kernel(x_prompt, x_sample, state_ret, state_gla, c, c_ctx, ada_w, ada_b, norm1_g, norm2_g,
           even_w_in, pool_w, pool_scale, ret_decay, ret_norm_g, even_w_out, odd_w_in, gla_gw1,
           gla_gw2, gla_gb, gla_norm_g, odd_w_out, ffn_w_up, ffn_conv_w, ffn_conv_b, ffn_w_down, final_g):
    x = jnp.concatenate([x_prompt.reshape(N_PROMPT, D_MODEL), x_sample.reshape(N_SAMPLE, D_MODEL)], axis=0)
    cvec = jnp.concatenate([c_ctx[None, :], c, jnp.zeros((COND_PAD - N_COND, D_MODEL), F32)], axis=0)
    mod = _ada_mod(cvec, ada_w, ada_b)
    n1 = norm1_g.reshape(DEPTH, 1, D_MODEL)
    n2 = norm2_g.reshape(DEPTH, 1, D_MODEL)
    rope = _rope_tables(DEC_SEQ)
    p_blocks = N_PROMPT // DEC_SEQ

    w_even_in, w_even_out = even_w_in.astype(BF16), even_w_out.astype(BF16)
    w_odd_in, w_odd_out = odd_w_in.astype(BF16), odd_w_out.astype(BF16)
    w_up, w_down, w_pool = ffn_w_up.astype(BF16), ffn_w_down.astype(BF16), pool_w.astype(BF16)
    w_gate1 = jnp.concatenate([gla_gw1[:, 0], gla_gw1[:, 1],
                               jnp.zeros((N_ODD, D_MODEL, 128 - 2 * GLA_GATE_RANK), F32)], axis=2).astype(BF16)

    ret_states, gla_states = [], []
    for l in range(DEPTH):
        if l % 2 == 0:
            i = l // 2
            proj = _in_proj(x, n1, mod, l, w_even_in, i, IN_SPLIT)
            y_pool = _pool_mixer(proj, w_pool, pool_scale, i)
            o_p, st = _ret_scan(proj, ret_decay[i], SEQ, BATCH, H_RET, 0, None, None, 0, True)
            (o_s,) = _ret_scan(proj, ret_decay[i], DEC_SEQ, DEC_BATCH, 1, p_blocks, rope, state_ret, i, False)
            ret_states.append(st)
            x = _out_proj(x, mod, l, w_even_out, i, proj, EVEN_IN // RET_WIDTH - 1,
                          ret_norm_g[i], o_p, o_s, H_RET, y_pool)
        else:
            j = l // 2
            proj = _in_proj(x, n1, mod, l, w_odd_in, j, IN_SPLIT)
            z1 = _in_proj(x, n1, mod, l, w_gate1, j, 1)
            g2, gbh = _gla_gate_weights(gla_gw2[j], gla_gb[j])
            o_p, st = _gla_scan(proj, z1, g2, gbh, SEQ, BATCH, DV_GLA, 0, None, 0, True)
            (o_s,) = _gla_scan(proj, z1, g2, gbh, DEC_SEQ, DEC_BATCH, DV_GLA, p_blocks, state_gla, j, False)
            gla_states.append(st)
            x = _out_proj(x, mod, l, w_odd_out, j, proj, ODD_IN // GLA_DV_TOTAL - 1,
                          gla_norm_g[j], o_p, o_s, H_GLA)
        x = _ffn(x, mod, l, n2, w_up, ffn_conv_w, ffn_conv_b, w_down, final_g, l == DEPTH - 1)

    y_prompt = x[0].reshape(BATCH, SEQ, D_MODEL)
    y_sample = x[1].reshape(DEC_BATCH, DEC_SEQ, D_MODEL)
    return (y_prompt, y_sample, jnp.stack(ret_states, axis=1), jnp.stack(gla_states, axis=1))
```

```python
import functools

import numpy as np
import jax
import jax.numpy as jnp
from jax import lax
from jax.experimental import pallas as pl
from jax.experimental.pallas import tpu as pltpu

F32 = jnp.float32
BF16 = jnp.bfloat16

D_MODEL = 2048
BATCH = 16
SEQ = 256
DEPTH = 4
DEC_BATCH = 2
DEC_SEQ = 4096
GRID_W = 64
N_EVEN = (DEPTH + 1) // 2
N_ODD = DEPTH // 2
POOL_WIDTH = D_MODEL // 2
POOL_GROUPS = 4
POOL_GC = POOL_WIDTH // POOL_GROUPS
POOL_WINDOWS = (2, 4, 8, 16)
RET_WIDTH = D_MODEL // 2
H_RET = 8
DK_RET = RET_WIDTH // H_RET
DV_RET = RET_WIDTH // H_RET
ROPE_BASE = 10000.0
H_GLA = 4
GLA_DK_TOTAL = D_MODEL // 2
GLA_DV_TOTAL = D_MODEL
DK_GLA = GLA_DK_TOTAL // H_GLA
DV_GLA = GLA_DV_TOTAL // H_GLA
GLA_GATE_RANK = 16
GLA_TAU = 16.0
D_FF = 5632
CHUNK = 64
EPS = 1e-6
EVEN_IN = POOL_WIDTH + 4 * RET_WIDTH
ODD_IN = 2 * GLA_DK_TOTAL + 2 * GLA_DV_TOTAL

N_PROMPT = BATCH * SEQ
N_SAMPLE = DEC_BATCH * DEC_SEQ
N_TOK = N_PROMPT + N_SAMPLE
N_COND = 1 + DEC_BATCH
COND_PAD = 8

VMEM_LIMIT = 56 * 1024 * 1024

TM = 512
IN_SPLIT = 2
IN_CHUNK = 640
ODD_IN_PAD = ODD_IN + 256
TM_FFN = 1024
TF = 512
POOL_TM = 256
POOL_HALO = 16
ADA_TN = 1024
RET_CHUNK = 256
GLA_BLK = 256


def _params(*sem):
    return pltpu.CompilerParams(dimension_semantics=sem, vmem_limit_bytes=VMEM_LIMIT)


def _cond_of_tile(i, tm):
    r0 = i * tm
    return jnp.where(r0 < N_PROMPT, 0, 1 + (r0 - N_PROMPT) // DEC_SEQ)


LOG2E = 1.4426950408889634


def _log_sigmoid(x):
    return jnp.minimum(x, 0.0) - jnp.log(1.0 + jnp.exp2(jnp.abs(x) * (-LOG2E)))


def _silu(x):
    return x * jax.nn.sigmoid(x)


def _rms(x):
    return x * lax.rsqrt(jnp.mean(x * x, axis=-1, keepdims=True) + EPS)


def _dot(a, b):
    return jnp.dot(a, b, preferred_element_type=F32)


def _dot_nt(a, b):
    return lax.dot_general(a, b, (((1,), (1,)), ((), ())), preferred_element_type=F32)


def _dot_tn(a, b):
    return lax.dot_general(a, b, (((0,), (0,)), ((), ())), preferred_element_type=F32)


def _split_bf16(x):
    hi = x.astype(BF16)
    lo = (x - hi.astype(F32)).astype(BF16)
    return hi, lo


def _loop(n, body, init, unroll=1):
    if n == 1:
        return body(0, init)
    return lax.fori_loop(0, n, body, init, unroll=unroll)


def _ada_kernel(c_ref, w_ref, b_ref, o_ref):
    s = _silu(c_ref[...]).astype(BF16)
    o_ref[0] = _dot(s, w_ref[0].astype(BF16)) + b_ref[0]


def _ada_mod(cvec, ada_w, ada_b):
    n = 6 * D_MODEL
    mod = pl.pallas_call(
        _ada_kernel,
        grid=(DEPTH, n // ADA_TN),
        in_specs=[
            pl.BlockSpec((COND_PAD, D_MODEL), lambda l, j: (0, 0)),
            pl.BlockSpec((1, D_MODEL, ADA_TN), lambda l, j: (l, 0, j)),
            pl.BlockSpec((1, 1, ADA_TN), lambda l, j: (l, 0, j)),
        ],
        out_specs=pl.BlockSpec((1, COND_PAD, ADA_TN), lambda l, j: (l, 0, j)),
        out_shape=jax.ShapeDtypeStruct((DEPTH, COND_PAD, n), F32),
        compiler_params=_params("parallel", "parallel"),
        name="ada_mod",
    )(cvec, ada_w, ada_b.reshape(DEPTH, 1, n))
    mod = mod[:, :N_COND].reshape(DEPTH, N_COND, 6, D_MODEL).transpose(0, 2, 1, 3)
    return mod.reshape(DEPTH * 6 * N_COND, 1, D_MODEL)


def _mod_spec(layer, part, tm, row_axis=0):
    base = (layer * 6 + part) * N_COND
    return pl.BlockSpec((1, 1, D_MODEL), lambda *idx: (base + _cond_of_tile(idx[row_axis], tm), 0, 0))


def _in_proj_kernel(*refs, chunk):
    *x_refs, g_ref, sh_ref, sc_ref, w_ref, o_ref, h_ref = refs

    def norm(x_ref):
        h = _rms(x_ref[...]) * g_ref[0]
        h_ref[...] = (h * (1.0 + sc_ref[0]) + sh_ref[0]).astype(BF16)

    if len(x_refs) == 1:
        norm(x_refs[0])
    else:
        in_prompt = pl.program_id(1) < N_PROMPT // TM
        pl.when(in_prompt)(lambda: norm(x_refs[0]))
        pl.when(jnp.logical_not(in_prompt))(lambda: norm(x_refs[1]))
    for c0 in range(0, o_ref.shape[1], chunk):
        o_ref[:, c0:c0 + chunk] = _dot(h_ref[...], w_ref[:, c0:c0 + chunk]).astype(o_ref.dtype)


def _in_proj(xs, norm_g, mod, layer, w, w_layer, split):
    n = w.shape[2]
    tn = n // split
    x_specs = [pl.BlockSpec((TM, D_MODEL), lambda j, i: (i, 0))] if len(xs) == 1 else _group_specs(TM, D_MODEL)
    return pl.pallas_call(
        functools.partial(_in_proj_kernel, chunk=min(IN_CHUNK, tn)),
        grid=(split, N_TOK // TM),
        in_specs=x_specs + [
            pl.BlockSpec((1, 1, D_MODEL), lambda j, i: (layer, 0, 0)),
            _mod_spec(layer, 0, TM, 1),
            _mod_spec(layer, 1, TM, 1),
            pl.BlockSpec((None, D_MODEL, tn), lambda j, i: (w_layer, 0, j)),
        ],
        out_specs=pl.BlockSpec((TM, tn), lambda j, i: (i, j)),
        out_shape=jax.ShapeDtypeStruct((N_TOK, n), BF16),
        scratch_shapes=[pltpu.VMEM((TM, D_MODEL), BF16)],
        compiler_params=_params("arbitrary", "arbitrary"),
        name="in_proj",
    )(*xs, norm_g, mod, mod, w)


def _pool_bands():
    t = POOL_TM
    b0 = np.zeros((POOL_GROUPS, t, t), np.float32)
    bp = np.zeros((POOL_GROUPS, t, POOL_HALO), np.float32)
    bn = np.zeros((POOL_GROUPS, t, POOL_HALO), np.float32)
    for g, win in enumerate(POOL_WINDOWS):
        for r in range(t):
            for s in range(r - win // 2, r + win - win // 2):
                if s < 0:
                    bp[g, r, s + POOL_HALO] = 1.0
                elif s >= t:
                    bn[g, r, s - t] = 1.0
                else:
                    b0[g, r, s] = 1.0
    return jnp.asarray(b0, BF16), jnp.asarray(bp, BF16), jnp.asarray(bn, BF16)


def _pool_tiles(tiles, b0_ref, bp_ref, bn_ref, pw_ref, sc_ref):
    tiles_per_seq = DEC_SEQ // POOL_TM
    t = lax.broadcasted_iota(jnp.int32, (POOL_TM, POOL_GC), 0)
    sums, cnts, us = [], [], []
    for tile, u, up, un in tiles:
        in_prompt = tile < N_PROMPT // POOL_TM
        pos = (tile - N_PROMPT // POOL_TM) % tiles_per_seq
        is_start = jnp.logical_or(in_prompt, pos == 0)
        is_end = jnp.logical_or(in_prompt, pos == tiles_per_seq - 1)
        up = jnp.where(is_start, jnp.zeros_like(up), up)
        un = jnp.where(is_end, jnp.zeros_like(un), un)
        for g, win in enumerate(POOL_WINDOWS):
            cols = slice(g * POOL_GC, (g + 1) * POOL_GC)
            us.append(u[:, cols])
            sums.append(_dot(b0_ref[g], u[:, cols]) + _dot(bp_ref[g], up[:, cols]) + _dot(bn_ref[g], un[:, cols]))
            cut_lo = jnp.where(is_start, jnp.maximum(win // 2 - t, 0), 0)
            cut_hi = jnp.where(is_end, jnp.maximum(t + (win - win // 2) - POOL_TM, 0), 0)
            cnts.append((win - cut_lo - cut_hi).astype(F32))
    pooled = [(s / cnt - u.astype(F32)).astype(BF16) for s, cnt, u in zip(sums, cnts, us)]
    ys = [_dot(p, pw_ref[n % POOL_GROUPS]) for n, p in enumerate(pooled)]
    out = []
    for j in range(len(tiles)):
        out.append(jnp.concatenate(
            [(ys[j * POOL_GROUPS + g] * sc_ref[:, g * POOL_GC:(g + 1) * POOL_GC]).astype(BF16)
             for g in range(POOL_GROUPS)], axis=1))
    return out


def _rope_tables(t):
    nf = DK_RET // 4
    rows = t // GRID_W
    r = jnp.repeat(jnp.arange(rows), GRID_W).astype(F32)
    col = jnp.tile(jnp.arange(GRID_W), rows).astype(F32)
    inv = ROPE_BASE ** (-jnp.arange(nf, dtype=F32) / nf)
    ar, ac = r[:, None] * inv, col[:, None] * inv
    cos = jnp.concatenate([jnp.cos(ar), jnp.cos(ar), jnp.cos(ac), jnp.cos(ac)], axis=1)
    sin = jnp.concatenate([-jnp.sin(ar), jnp.sin(ar), -jnp.sin(ac), jnp.sin(ac)], axis=1)
    return cos, sin


def _ret_kernel(*refs, n_chunks, heads, use_rope, has_s0, emit_state):
    refs = list(refs)
    dec_ref, q_ref, k_ref, v_ref = refs[:4]
    refs = refs[4:]
    if use_rope:
        cos_ref, sin_ref = refs[:2]
        refs = refs[2:]
    if has_s0:
        s0_ref = refs[0]
        refs = refs[1:]
    o_ref = refs[0]
    refs = refs[1:]
    if emit_state:
        st_ref = refs[0]
        refs = refs[1:]
    (qk_ref,) = refs

    c = RET_CHUNK
    nf = DK_RET // 4
    head0 = pl.program_id(1) * heads
    inter = has_s0 or n_chunks > 1

    row = lax.broadcasted_iota(jnp.int32, (c, DK_RET), 0).astype(F32)
    ii = lax.broadcasted_iota(jnp.int32, (c, c), 0)
    jj = lax.broadcasted_iota(jnp.int32, (c, c), 1)
    dist = (ii - jj).astype(F32)
    lane = lax.broadcasted_iota(jnp.int32, (c, DK_RET), 1)
    first_half = (lane % (2 * nf)) < nf

    def rope(x, r0):
        if not use_rope:
            return x
        partner = jnp.where(first_half, pltpu.roll(x, DK_RET - nf, axis=1), pltpu.roll(x, nf, axis=1))
        return x * cos_ref[pl.ds(r0, c), :] + partner * sin_ref[pl.ds(r0, c), :]

    for hh in range(heads):
        cols = slice(hh * DK_RET, (hh + 1) * DK_RET)

        def lam(d, shape):
            return _log_sigmoid(jnp.full(shape, dec_ref[d, head0 + hh], F32))

        lam_f, lam_b = lam(0, (c, DK_RET)), lam(1, (c, DK_RET))
        dq_f = jnp.exp((row + 1.0) * lam_f)
        dk_f = jnp.exp((c - 1.0 - row) * lam_f)
        dq_b = jnp.exp((c - row) * lam_b)
        dk_b = jnp.exp(row * lam_b)
        cdec_f = jnp.exp(float(c) * lam(0, (DK_RET, DV_RET)))
        cdec_b = jnp.exp(float(c) * lam(1, (DK_RET, DV_RET)))
        dmat = jnp.where(ii > jj, jnp.exp(dist * lam(0, (c, c))),
                         jnp.where(ii < jj, jnp.exp(-dist * lam(1, (c, c))), 2.0))

        group = 2 if n_chunks % 2 == 0 else 1

        def prepare(gi, carry):
            ns = [gi * group + j for j in range(group)]
            r0s = [n * c if isinstance(n, int) else pl.multiple_of(n * c, c) for n in ns]
            q = [rope(q_ref[pl.ds(r0, c), cols].astype(F32), r0) for r0 in r0s]
            k = [rope(k_ref[pl.ds(r0, c), cols].astype(F32) * (DK_RET ** -0.5), r0) for r0 in r0s]
            a = [(_dot_nt(q[j].astype(BF16), k[j].astype(BF16)) * dmat).astype(BF16) for j in range(group)]
            for j, r0 in enumerate(r0s):
                qk_ref[0, pl.ds(r0, c), cols] = (q[j] * dq_f).astype(BF16)
                qk_ref[1, pl.ds(r0, c), cols] = (q[j] * dq_b).astype(BF16)
                qk_ref[2, pl.ds(r0, c), cols] = (k[j] * dk_f).astype(BF16)
                qk_ref[3, pl.ds(r0, c), cols] = (k[j] * dk_b).astype(BF16)
            o = [_dot(a[j], v_ref[pl.ds(r0s[j], c), cols]) for j in range(group)]
            for j, r0 in enumerate(r0s):
                o_ref[pl.ds(r0, c), cols] = o[j]
            return carry

        def advance(n, states):
            ms = (n, n_chunks - 1 - n)
            r0s = [m * c if isinstance(m, int) else pl.multiple_of(m * c, c) for m in ms]
            upd = [_dot_tn(qk_ref[2 + d, pl.ds(r0s[d], c), cols], v_ref[pl.ds(r0s[d], c), cols]) for d in (0, 1)]
            out = []
            for d, cdec in ((0, cdec_f), (1, cdec_b)):
                if inter:
                    o_ref[pl.ds(r0s[d], c), cols] += _dot(qk_ref[d, pl.ds(r0s[d], c), cols],
                                                          states[d].astype(BF16))
                out.append(cdec * states[d] + upd[d])
            return tuple(out)

        _loop(n_chunks // group, prepare, 0)
        zero = jnp.zeros((DK_RET, DV_RET), F32)
        init = (s0_ref[0, 0, 0, hh], s0_ref[0, 0, 1, hh]) if has_s0 else (zero, zero)
        s_f, s_b = _loop(n_chunks, advance, init)
        if emit_state:
            st_ref[0, 0, hh] = s_f
            st_ref[0, 1, hh] = s_b


def _ret_scan(proj, decay, t, batch, heads, row_block0, rope, s0, s0_layer, emit_state):
    w = heads * DK_RET
    col0 = POOL_WIDTH // w
    per = RET_WIDTH // w
    in_specs = [
        pl.BlockSpec(memory_space=pltpu.SMEM),
        pl.BlockSpec((t, w), lambda b, h: (row_block0 + b, col0 + h)),
        pl.BlockSpec((t, w), lambda b, h: (row_block0 + b, col0 + per + h)),
        pl.BlockSpec((t, w), lambda b, h: (row_block0 + b, col0 + 2 * per + h)),
    ]
    args = [decay, proj, proj, proj]
    if rope is not None:
        assert heads == 1
        in_specs += [pl.BlockSpec((t, DK_RET), lambda b, h: (0, 0))] * 2
        args += list(rope)
    if s0 is not None:
        in_specs.append(pl.BlockSpec((1, 1, 2, heads, DK_RET, DV_RET), lambda b, h: (b, s0_layer, 0, h, 0, 0)))
        args.append(s0)
    out_specs = [pl.BlockSpec((t, w), lambda b, h: (b, h))]
    out_shape = [jax.ShapeDtypeStruct((batch * t, RET_WIDTH), F32)]
    if emit_state:
        out_specs.append(pl.BlockSpec((1, 2, heads, DK_RET, DV_RET), lambda b, h: (b, 0, h, 0, 0)))
        out_shape.append(jax.ShapeDtypeStruct((batch, 2, H_RET, DK_RET, DV_RET), F32))
    return pl.pallas_call(
        functools.partial(_ret_kernel, n_chunks=t // RET_CHUNK, heads=heads, use_rope=rope is not None,
                          has_s0=s0 is not None, emit_state=emit_state),
        grid=(batch, H_RET // heads),
        in_specs=in_specs,
        out_specs=out_specs,
        out_shape=out_shape,
        scratch_shapes=[pltpu.VMEM((4, t, w), BF16)],
        compiler_params=_params("parallel", "parallel"),
        name="ret_scan",
    )(*args)


def _gla_kernel(*refs, n_blocks, dvb, has_s0, emit_state):
    refs = list(refs)
    q_ref, k_ref, v_ref, z_ref, g2_ref, gb_ref = refs[:6]
    refs = refs[6:]
    if has_s0:
        s0_ref = refs[0]
        refs = refs[1:]
    o_ref = refs[0]
    refs = refs[1:]
    if emit_state:
        st_ref = refs[0]
        refs = refs[1:]
    qd_ref, kt_ref, a_ref, dec_ref, sf_ref, sb_ref = refs

    c = CHUNK
    blk = GLA_BLK
    cpb = blk // c
    n_chunks = n_blocks * cpb

    @pl.when(pl.program_id(2) == 0)
    def _():
        ii = lax.broadcasted_iota(jnp.int32, (blk, blk), 0)
        jj = lax.broadcasted_iota(jnp.int32, (blk, blk), 1)
        same = (ii // c) == (jj // c)
        masks = (jnp.logical_and(same, ii >= jj), jnp.logical_and(same, ii <= jj))
        tris = tuple(jnp.tile(jnp.where(m, 1.0, 0.0).astype(BF16), (1, 2)) for m in masks)

        group = 2 if n_blocks % 2 == 0 else 1
        chains = [(j, d) for j in range(group) for d in (0, 1)]

        def build(gi, carry):
            bis = [gi * group + j for j in range(group)]
            r0s = [bi * blk if isinstance(bi, int) else pl.multiple_of(bi * blk, blk) for bi in bis]
            q = [q_ref[pl.ds(r0, blk), :].astype(F32) * (DK_GLA ** -0.5) for r0 in r0s]
            k = [k_ref[pl.ds(r0, blk), :].astype(F32) for r0 in r0s]
            z = [_dot(z_ref[pl.ds(r0s[j], blk), :], g2_ref[0, d]) + gb_ref[0, d] for j, d in chains]
            lg = [jnp.concatenate(_split_bf16(_log_sigmoid(zz) * (LOG2E / GLA_TAU)), axis=0) for zz in z]
            g = [_dot(tris[d], lg[n]) for n, (j, d) in enumerate(chains)]
            qds, kds = [], []
            for n, (j, d) in enumerate(chains):
                edge = c - 1 if d == 0 else 0
                g_tot = jnp.concatenate(
                    [jnp.broadcast_to(g[n][ci * c + edge:ci * c + edge + 1, :], (c, DK_GLA))
                     for ci in range(cpb)], axis=0)
                qd = (q[j] * jnp.exp2(g[n])).astype(BF16)
                kds.append((k[j] * jnp.exp2(-g[n])).astype(BF16))
                qds.append(qd)
                qd_ref[d, pl.ds(r0s[j], blk), :] = qd
                kt_ref[d, pl.ds(r0s[j], blk), :] = (k[j] * jnp.exp2(g_tot - g[n])).astype(BF16)
                for ci in range(cpb):
                    dec_ref[d * n_chunks + bis[j] * cpb + ci] = jnp.exp2(g_tot[ci * c:ci * c + 8, :])
            a = [_dot_nt(qds[n], kds[n]) for n in range(len(chains))]
            for j in range(group):
                a_ref[bis[j]] = (jnp.where(masks[0], a[2 * j], 0.0) + jnp.where(masks[1], a[2 * j + 1], 0.0)).astype(BF16)
            return carry

        _loop(n_blocks // group, build, 0)

    def intra(bi, carry):
        r0 = bi * blk if isinstance(bi, int) else pl.multiple_of(bi * blk, blk)
        o_ref[pl.ds(r0, blk), :] = _dot(a_ref[bi], v_ref[pl.ds(r0, blk), :])
        return carry

    _loop(n_blocks, intra, 0)

    zero = jnp.zeros((dvb, DK_GLA), F32)
    sf_ref[...] = s0_ref[0, 0, 0, 0].T if has_s0 else zero
    sb_ref[...] = s0_ref[0, 0, 1, 0].T if has_s0 else zero

    def advance(n, carry):
        ms = (n, n_chunks - 1 - n)
        r0s = [pl.multiple_of(m * c, c) for m in ms]
        upd = [_dot_tn(v_ref[pl.ds(r0s[d], c), :], kt_ref[d, pl.ds(r0s[d], c), :]) for d in (0, 1)]
        for d, s_ref in ((0, sf_ref), (1, sb_ref)):
            s = s_ref[...]
            o_ref[pl.ds(r0s[d], c), :] += _dot_nt(qd_ref[d, pl.ds(r0s[d], c), :], s.astype(BF16))
            s_ref[...] = s * dec_ref[d * n_chunks + ms[d]][0:1, :] + upd[d]
        return carry

    lax.fori_loop(0, n_chunks, advance, 0, unroll=2)
    if emit_state:
        st_ref[0, 0, 0] = sf_ref[...].T
        st_ref[0, 1, 0] = sb_ref[...].T


def _gla_scan(proj, g2, gb, t, batch, dvb, row_block0, s0, s0_layer, emit_state):
    nd = DV_GLA // dvb
    kcol = GLA_DK_TOTAL // DK_GLA
    vcol = 2 * GLA_DK_TOTAL // dvb
    n_blocks = t // GLA_BLK
    n_chunks = t // CHUNK
    in_specs = [
        pl.BlockSpec((t, DK_GLA), lambda b, h, d: (row_block0 + b, h)),
        pl.BlockSpec((t, DK_GLA), lambda b, h, d: (row_block0 + b, kcol + h)),
        pl.BlockSpec((t, dvb), lambda b, h, d: (row_block0 + b, vcol + h * nd + d)),
        pl.BlockSpec((t, 128), lambda b, h, d: (row_block0 + b, ODD_IN // 128)),
        pl.BlockSpec((1, 2, 128, DK_GLA), lambda b, h, d: (h, 0, 0, 0)),
        pl.BlockSpec((1, 2, 1, DK_GLA), lambda b, h, d: (h, 0, 0, 0)),
    ]
    args = [proj, proj, proj, proj, g2, gb]
    if s0 is not None:
        in_specs.append(pl.BlockSpec((1, 1, 2, 1, DK_GLA, dvb), lambda b, h, d: (b, s0_layer, 0, h, 0, d)))
        args.append(s0)
    out_specs = [pl.BlockSpec((t, dvb), lambda b, h, d: (b, h * nd + d))]
    out_shape = [jax.ShapeDtypeStruct((batch * t, GLA_DV_TOTAL), F32)]
    if emit_state:
        out_specs.append(pl.BlockSpec((1, 2, 1, DK_GLA, dvb), lambda b, h, d: (b, 0, h, 0, d)))
        out_shape.append(jax.ShapeDtypeStruct((batch, 2, H_GLA, DK_GLA, DV_GLA), F32))
    return pl.pallas_call(
        functools.partial(_gla_kernel, n_blocks=n_blocks, dvb=dvb, has_s0=s0 is not None, emit_state=emit_state),
        grid=(batch, H_GLA, nd),
        in_specs=in_specs,
        out_specs=out_specs,
        out_shape=out_shape,
        scratch_shapes=[
            pltpu.VMEM((2, t, DK_GLA), BF16),
            pltpu.VMEM((2, t, DK_GLA), BF16),
            pltpu.VMEM((n_blocks, GLA_BLK, GLA_BLK), BF16),
            pltpu.VMEM((2 * n_chunks, 8, DK_GLA), F32),
            pltpu.VMEM((dvb, DK_GLA), F32),
            pltpu.VMEM((dvb, DK_GLA), F32),
        ],
        compiler_params=_params("parallel", "parallel", "arbitrary"),
        name="gla_scan",
    )(*args)


def _group_specs(tm, width, col_block=0):
    p_tiles = N_PROMPT // tm
    return [pl.BlockSpec((tm, width), lambda *idx: (jnp.minimum(idx[-1], p_tiles - 1), col_block)),
            pl.BlockSpec((tm, width), lambda *idx: (jnp.maximum(idx[-1] - p_tiles, 0), col_block))]


def _out_proj_kernel(*refs, n_heads, width, with_pool):
    refs = list(refs)
    if with_pool:
        u_ref, up_ref, un_ref, b0_ref, bp_ref, bn_ref, pw_ref, psc_ref = refs[:8]
        refs = refs[8:]
    op_ref, os_ref, gate_ref, gn_ref, w_ref = refs[:5]
    *x_refs, g1_ref, out_ref = refs[5:]
    xp_ref, xs_ref = x_refs if len(x_refs) == 2 else (x_refs[0], x_refs[0])
    i = pl.program_id(0)
    in_prompt = i < N_PROMPT // TM

    pooled = []
    if with_pool:
        sub = TM // POOL_TM
        tiles = []
        for s in range(sub):
            r0 = s * POOL_TM
            up = up_ref[...] if s == 0 else u_ref[r0 - POOL_HALO:r0, :]
            un = un_ref[...] if s == sub - 1 else u_ref[r0 + POOL_TM:r0 + POOL_TM + POOL_HALO, :]
            tiles.append((i * sub + s, u_ref[r0:r0 + POOL_TM, :], up, un))
        pooled = [jnp.concatenate(_pool_tiles(tiles, b0_ref, bp_ref, bn_ref, pw_ref, psc_ref), axis=0)]

    def run(o_ref, x_ref):
        parts = list(pooled)
        for hd in range(n_heads):
            cols = slice(hd * width, (hd + 1) * width)
            gt = gate_ref[:, cols].astype(F32)
            parts.append((_rms(o_ref[:, cols]) * gn_ref[:, cols] * _silu(gt)).astype(BF16))
        a = jnp.concatenate(parts, axis=1)
        out_ref[...] = x_ref[...] + g1_ref[0] * _dot(a, w_ref[...])

    @pl.when(in_prompt)
    def _():
        run(op_ref, xp_ref)

    @pl.when(jnp.logical_not(in_prompt))
    def _():
        run(os_ref, xs_ref)


def _out_proj(xs, mod, layer, w, w_layer, proj, gate_col_block, norm_g, o_prompt, o_sample, n_heads, pool=None):
    width = o_prompt.shape[1]
    x_specs = [pl.BlockSpec((TM, D_MODEL), lambda i: (i, 0))] if len(xs) == 1 else _group_specs(TM, D_MODEL)
    in_specs = _group_specs(TM, width) + [
        pl.BlockSpec((TM, width), lambda i: (i, gate_col_block)),
        pl.BlockSpec((1, width), lambda i: (0, 0)),
        pl.BlockSpec((None, D_MODEL, D_MODEL), lambda i: (w_layer, 0, 0)),
    ] + x_specs + [_mod_spec(layer, 2, TM)]
    args = [o_prompt, o_sample, proj, norm_g.reshape(1, width), w, *xs, mod]
    if pool is not None:
        b0, bp, bn = _pool_bands()
        halo_blocks = TM // POOL_HALO
        last = N_TOK // POOL_HALO - 1
        full = lambda shape: pl.BlockSpec(shape, lambda i: (0,) * len(shape))
        in_specs = [
            pl.BlockSpec((TM, POOL_WIDTH), lambda i: (i, 0)),
            pl.BlockSpec((POOL_HALO, POOL_WIDTH), lambda i: (jnp.maximum(i * halo_blocks - 1, 0), 0)),
            pl.BlockSpec((POOL_HALO, POOL_WIDTH), lambda i: (jnp.minimum((i + 1) * halo_blocks, last), 0)),
            full(b0.shape), full(bp.shape), full(bn.shape),
            pl.BlockSpec((None,) + pool[0].shape[1:], lambda i: (w_layer, 0, 0, 0)),
            pl.BlockSpec((None, 1, POOL_WIDTH), lambda i: (w_layer, 0, 0)),
        ] + in_specs
        args = [proj, proj, proj, b0, bp, bn, pool[0], pool[1].reshape(N_EVEN, 1, POOL_WIDTH)] + args
    return pl.pallas_call(
        functools.partial(_out_proj_kernel, n_heads=n_heads, width=width // n_heads, with_pool=pool is not None),
        grid=(N_TOK // TM,),
        in_specs=in_specs,
        out_specs=pl.BlockSpec((TM, D_MODEL), lambda i: (i, 0)),
        out_shape=jax.ShapeDtypeStruct((N_TOK, D_MODEL), F32),
        compiler_params=_params("parallel"),
        name="out_proj",
    )(*args)


def _ffn_kernel(x_ref, xp_ref, xn_ref, gn_ref, sh_ref, sc_ref, gt_ref, wa_ref, wb_ref, cw_ref, cb_ref,
                wd_ref, fg_ref, *rest, final):
    if final:
        op_ref, o_ref, h_ref, hh_ref = rest
    else:
        o_ref, h_ref, hh_ref = rest
    i = pl.program_id(0)
    f = pl.program_id(1)
    tm = TM_FFN

    def modnorm(x):
        return (_rms(x) * gn_ref[0] * (1.0 + sc_ref[0]) + sh_ref[0]).astype(BF16)

    @pl.when(f == 0)
    def _():
        h_ref[...] = modnorm(x_ref[...])
        hh_ref[...] = modnorm(jnp.concatenate([xp_ref[...], xn_ref[...]], axis=0))
        o_ref[...] = jnp.zeros_like(o_ref)

    a = _dot(h_ref[...], wa_ref[...])
    b = _dot(h_ref[...], wb_ref[...])
    a_halo = _dot(hh_ref[...], wa_ref[...])

    seq = jnp.where(i * tm < N_PROMPT, SEQ, DEC_SEQ)
    t = lax.broadcasted_iota(jnp.int32, (tm, TF), 0)
    pos = (i * tm + t) & (seq - 1)
    a_prev = jnp.where(t == 0, a_halo[7:8, :], pltpu.roll(a, 1, axis=0))
    a_prev = jnp.where(pos == 0, 0.0, a_prev)
    a_next = jnp.where(t == tm - 1, a_halo[8:9, :], pltpu.roll(a, tm - 1, axis=0))
    a_next = jnp.where(pos == seq - 1, 0.0, a_next)
    conv = a_prev * cw_ref[0:1, :] + a * cw_ref[1:2, :] + a_next * cw_ref[2:3, :] + cb_ref[...]
    o_ref[...] += _dot((_silu(conv) * b).astype(BF16), wd_ref[...])

    @pl.when(f == pl.num_programs(1) - 1)
    def _():
        y = x_ref[...] + gt_ref[0] * o_ref[...]
        if not final:
            o_ref[...] = y
        else:
            y = _rms(y) * fg_ref[...]
            in_prompt = i < N_PROMPT // tm

            @pl.when(in_prompt)
            def _():
                op_ref[...] = y

            @pl.when(jnp.logical_not(in_prompt))
            def _():
                o_ref[...] = y


def _ffn(x, mod, layer, norm_g, w_up, conv_w, conv_b, w_down, final_g, final):
    m = x.shape[0]
    nf = D_FF // TF
    tm = TM_FFN
    halo = 8
    last = m // halo - 1
    if final:
        p_tiles = N_PROMPT // tm
        out_specs = [pl.BlockSpec((tm, D_MODEL), lambda i, f: (jnp.minimum(i, p_tiles - 1), 0),
                                  pipeline_mode=pl.Buffered(1)),
                     pl.BlockSpec((tm, D_MODEL), lambda i, f: (jnp.maximum(i - p_tiles, 0), 0),
                                  pipeline_mode=pl.Buffered(1))]
        out_shape = [jax.ShapeDtypeStruct((N_PROMPT, D_MODEL), F32),
                     jax.ShapeDtypeStruct((N_SAMPLE, D_MODEL), F32)]
    else:
        out_specs = pl.BlockSpec((tm, D_MODEL), lambda i, f: (i, 0))
        out_shape = jax.ShapeDtypeStruct((m, D_MODEL), F32)
    return pl.pallas_call(
        functools.partial(_ffn_kernel, final=final),
        grid=(m // tm, nf),
        in_specs=[
            pl.BlockSpec((tm, D_MODEL), lambda i, f: (i, 0), pipeline_mode=pl.Buffered(1)),
            pl.BlockSpec((halo, D_MODEL), lambda i, f: (jnp.maximum(i * (tm // halo) - 1, 0), 0)),
            pl.BlockSpec((halo, D_MODEL), lambda i, f: (jnp.minimum((i + 1) * (tm // halo), last), 0)),
            pl.BlockSpec((1, 1, D_MODEL), lambda i, f: (layer, 0, 0)),
            _mod_spec(layer, 3, tm),
            _mod_spec(layer, 4, tm),
            _mod_spec(layer, 5, tm),
            pl.BlockSpec((None, D_MODEL, TF), lambda i, f: (layer, 0, f)),
            pl.BlockSpec((None, D_MODEL, TF), lambda i, f: (layer, 0, nf + f)),
            pl.BlockSpec((None, 3, TF), lambda i, f: (layer, 0, f)),
            pl.BlockSpec((None, 1, TF), lambda i, f: (layer, 0, f)),
            pl.BlockSpec((None, TF, D_MODEL), lambda i, f: (layer, f, 0)),
            pl.BlockSpec((1, D_MODEL), lambda i, f: (0, 0)),
        ],
        out_specs=out_specs,
        out_shape=out_shape,
        scratch_shapes=[
            pltpu.VMEM((tm, D_MODEL), BF16),
            pltpu.VMEM((2 * halo, D_MODEL), BF16),
        ],
        compiler_params=_params("arbitrary", "arbitrary"),
        name="ffn",
    )(x, x, x, norm_g, mod, mod, mod, w_up, w_up, conv_w, conv_b.reshape(DEPTH, 1, D_FF), w_down,
      final_g.reshape(1, D_MODEL))


def _gla_gate_weights(gw2, gb):
    r = GLA_GATE_RANK
    g2 = jnp.zeros((2, 128, GLA_DK_TOTAL), F32)
    g2 = g2.at[0, 0:r].set(gw2[0]).at[1, r:2 * r].set(gw2[1])
    g2 = g2.reshape(2, 128, H_GLA, DK_GLA).transpose(2, 0, 1, 3).astype(BF16)
    gbh = gb.reshape(2, H_GLA, 1, DK_GLA).transpose(1, 0, 2, 3)
    return g2, gbh


def kernel(x_prompt, x_sample, state_ret, state_gla, c, c_ctx, ada_w, ada_b, norm1_g, norm2_g,
           even_w_in, pool_w, pool_scale, ret_decay, ret_norm_g, even_w_out, odd_w_in, gla_gw1,
           gla_gw2, gla_gb, gla_norm_g, odd_w_out, ffn_w_up, ffn_conv_w, ffn_conv_b, ffn_w_down, final_g):
    xs = (x_prompt.reshape(N_PROMPT, D_MODEL), x_sample.reshape(N_SAMPLE, D_MODEL))
    cvec = jnp.concatenate([c_ctx[None, :], c, jnp.zeros((COND_PAD - N_COND, D_MODEL), F32)], axis=0)
    mod = _ada_mod(cvec, ada_w, ada_b)
    n1 = norm1_g.reshape(DEPTH, 1, D_MODEL)
    n2 = norm2_g.reshape(DEPTH, 1, D_MODEL)
    rope = _rope_tables(DEC_SEQ)
    p_blocks = N_PROMPT // DEC_SEQ

    w_even_in, w_even_out = even_w_in.astype(BF16), even_w_out.astype(BF16)
    w_odd_out = odd_w_out.astype(BF16)
    w_up, w_down, w_pool = ffn_w_up.astype(BF16), ffn_w_down.astype(BF16), pool_w.astype(BF16)
    w_odd_in = jnp.concatenate(
        [odd_w_in, gla_gw1[:, 0], gla_gw1[:, 1],
         jnp.zeros((N_ODD, D_MODEL, ODD_IN_PAD - ODD_IN - 2 * GLA_GATE_RANK), F32)], axis=2).astype(BF16)

    ret_states, gla_states = [], []
    for l in range(DEPTH):
        if l % 2 == 0:
            i = l // 2
            proj = _in_proj(xs, n1, mod, l, w_even_in, i, IN_SPLIT)
            o_p, st = _ret_scan(proj, ret_decay[i], SEQ, BATCH, H_RET, 0, None, None, 0, True)
            (o_s,) = _ret_scan(proj, ret_decay[i], DEC_SEQ, DEC_BATCH, 1, p_blocks, rope, state_ret, i, False)
            ret_states.append(st)
            x = _out_proj(xs, mod, l, w_even_out, i, proj, EVEN_IN // RET_WIDTH - 1,
                          ret_norm_g[i], o_p, o_s, H_RET, (w_pool, pool_scale))
        else:
            j = l // 2
            proj = _in_proj(xs, n1, mod, l, w_odd_in, j, IN_SPLIT)
            g2, gbh = _gla_gate_weights(gla_gw2[j], gla_gb[j])
            o_p, st = _gla_scan(proj, g2, gbh, SEQ, BATCH, DV_GLA, 0, None, 0, True)
            (o_s,) = _gla_scan(proj, g2, gbh, DEC_SEQ, DEC_BATCH, DV_GLA, p_blocks, state_gla, j, False)
            gla_states.append(st)
            x = _out_proj(xs, mod, l, w_odd_out, j, proj, ODD_IN // GLA_DV_TOTAL - 1,
                          gla_norm_g[j], o_p, o_s, H_GLA)
        xs = _ffn(x, mod, l, n2, w_up, ffn_conv_w, ffn_conv_b, w_down, final_g, l == DEPTH - 1)
        xs = tuple(xs) if l == DEPTH - 1 else (xs,)

    y_prompt = xs[0].reshape(BATCH, SEQ, D_MODEL)
    y_sample = xs[1].reshape(DEC_BATCH, DEC_SEQ, D_MODEL)
    return (y_prompt, y_sample, jnp.stack(ret_states, axis=1), jnp.stack(gla_states, axis=1))
```

```python
import functools

import numpy as np
import jax
import jax.numpy as jnp
from jax import lax
from jax.experimental import pallas as pl
from jax.experimental.pallas import tpu as pltpu

F32 = jnp.float32
BF16 = jnp.bfloat16

D_MODEL = 2048
BATCH = 16
SEQ = 256
DEPTH = 4
DEC_BATCH = 2
DEC_SEQ = 4096
GRID_W = 64
N_EVEN = (DEPTH + 1) // 2
N_ODD = DEPTH // 2
POOL_WIDTH = D_MODEL // 2
POOL_GROUPS = 4
POOL_GC = POOL_WIDTH // POOL_GROUPS
POOL_WINDOWS = (2, 4, 8, 16)
RET_WIDTH = D_MODEL // 2
H_RET = 8
DK_RET = RET_WIDTH // H_RET
DV_RET = RET_WIDTH // H_RET
ROPE_BASE = 10000.0
H_GLA = 4
GLA_DK_TOTAL = D_MODEL // 2
GLA_DV_TOTAL = D_MODEL
DK_GLA = GLA_DK_TOTAL // H_GLA
DV_GLA = GLA_DV_TOTAL // H_GLA
GLA_GATE_RANK = 16
GLA_TAU = 16.0
D_FF = 5632
CHUNK = 64
EPS = 1e-6
EVEN_IN = POOL_WIDTH + 4 * RET_WIDTH
ODD_IN = 2 * GLA_DK_TOTAL + 2 * GLA_DV_TOTAL

N_PROMPT = BATCH * SEQ
N_SAMPLE = DEC_BATCH * DEC_SEQ
N_TOK = N_PROMPT + N_SAMPLE
N_COND = 1 + DEC_BATCH
COND_PAD = 8

VMEM_LIMIT = 56 * 1024 * 1024

TM = 512
IN_SPLIT = 2
IN_CHUNK = 512
TM_FFN = 1024
TF = 512
POOL_TM = 256
POOL_HALO = 16
ADA_TN = 1024
RET_CHUNK = 256
GLA_BLK = 256


def _params(*sem):
    return pltpu.CompilerParams(dimension_semantics=sem, vmem_limit_bytes=VMEM_LIMIT)


def _cond_of_tile(i, tm):
    r0 = i * tm
    return jnp.where(r0 < N_PROMPT, 0, 1 + (r0 - N_PROMPT) // DEC_SEQ)


LOG2E = 1.4426950408889634


def _log_sigmoid(x):
    return jnp.minimum(x, 0.0) - jnp.log(1.0 + jnp.exp2(jnp.abs(x) * (-LOG2E)))


def _silu(x):
    return x * jax.nn.sigmoid(x)


def _rms(x):
    return x * lax.rsqrt(jnp.mean(x * x, axis=-1, keepdims=True) + EPS)


def _dot(a, b):
    return jnp.dot(a, b, preferred_element_type=F32)


def _dot_nt(a, b):
    return lax.dot_general(a, b, (((1,), (1,)), ((), ())), preferred_element_type=F32)


def _dot_tn(a, b):
    return lax.dot_general(a, b, (((0,), (0,)), ((), ())), preferred_element_type=F32)


def _split_bf16(x):
    hi = x.astype(BF16)
    lo = (x - hi.astype(F32)).astype(BF16)
    return hi, lo


def _loop(n, body, init, unroll=1):
    if n == 1:
        return body(0, init)
    return lax.fori_loop(0, n, body, init, unroll=unroll)


def _ada_kernel(c_ref, w_ref, b_ref, o_ref):
    s = _silu(c_ref[...]).astype(BF16)
    o_ref[0] = _dot(s, w_ref[0].astype(BF16)) + b_ref[0]


def _ada_mod(cvec, ada_w, ada_b):
    n = 6 * D_MODEL
    mod = pl.pallas_call(
        _ada_kernel,
        grid=(DEPTH, n // ADA_TN),
        in_specs=[
            pl.BlockSpec((COND_PAD, D_MODEL), lambda l, j: (0, 0)),
            pl.BlockSpec((1, D_MODEL, ADA_TN), lambda l, j: (l, 0, j)),
            pl.BlockSpec((1, 1, ADA_TN), lambda l, j: (l, 0, j)),
        ],
        out_specs=pl.BlockSpec((1, COND_PAD, ADA_TN), lambda l, j: (l, 0, j)),
        out_shape=jax.ShapeDtypeStruct((DEPTH, COND_PAD, n), F32),
        compiler_params=_params("parallel", "parallel"),
        name="ada_mod",
    )(cvec, ada_w, ada_b.reshape(DEPTH, 1, n))
    mod = mod[:, :N_COND].reshape(DEPTH, N_COND, 6, D_MODEL).transpose(0, 2, 1, 3)
    return mod.reshape(DEPTH * 6 * N_COND, 1, D_MODEL)


def _mod_spec(layer, part, tm, row_axis=0):
    base = (layer * 6 + part) * N_COND
    return pl.BlockSpec((1, 1, D_MODEL), lambda *idx: (base + _cond_of_tile(idx[row_axis], tm), 0, 0))


def _in_proj_kernel(*refs, chunk):
    *x_refs, g_ref, sh_ref, sc_ref, w_ref, o_ref, h_ref = refs

    if len(x_refs) == 1:
        x = x_refs[0][...]
    else:
        x = jnp.where(pl.program_id(1) < N_PROMPT // TM, x_refs[0][...], x_refs[1][...])
    h = _rms(x) * g_ref[0]
    h_ref[...] = (h * (1.0 + sc_ref[0]) + sh_ref[0]).astype(BF16)
    for c0 in range(0, o_ref.shape[1], chunk):
        o_ref[:, c0:c0 + chunk] = _dot(h_ref[...], w_ref[:, c0:c0 + chunk]).astype(o_ref.dtype)


def _in_proj(xs, norm_g, mod, layer, w, w_layer, split):
    n = w.shape[2]
    tn = n // split
    x_specs = [pl.BlockSpec((TM, D_MODEL), lambda j, i: (i, 0))] if len(xs) == 1 else _group_specs(TM, D_MODEL)
    return pl.pallas_call(
        functools.partial(_in_proj_kernel, chunk=min(IN_CHUNK, tn)),
        grid=(split, N_TOK // TM),
        in_specs=x_specs + [
            pl.BlockSpec((1, 1, D_MODEL), lambda j, i: (layer, 0, 0)),
            _mod_spec(layer, 0, TM, 1),
            _mod_spec(layer, 1, TM, 1),
            pl.BlockSpec((None, D_MODEL, tn), lambda j, i: (w_layer, 0, j)),
        ],
        out_specs=pl.BlockSpec((TM, tn), lambda j, i: (i, j)),
        out_shape=jax.ShapeDtypeStruct((N_TOK, n), BF16),
        scratch_shapes=[pltpu.VMEM((TM, D_MODEL), BF16)],
        compiler_params=_params("arbitrary", "arbitrary"),
        name="in_proj",
    )(*xs, norm_g, mod, mod, w)


def _pool_bands():
    t = POOL_TM
    b0 = np.zeros((POOL_GROUPS, t, t), np.float32)
    bp = np.zeros((POOL_GROUPS, t, POOL_HALO), np.float32)
    bn = np.zeros((POOL_GROUPS, t, POOL_HALO), np.float32)
    for g, win in enumerate(POOL_WINDOWS):
        for r in range(t):
            for s in range(r - win // 2, r + win - win // 2):
                if s < 0:
                    bp[g, r, s + POOL_HALO] = 1.0
                elif s >= t:
                    bn[g, r, s - t] = 1.0
                else:
                    b0[g, r, s] = 1.0
    return jnp.asarray(b0, BF16), jnp.asarray(bp, BF16), jnp.asarray(bn, BF16)


def _pool_tiles(tiles, b0_ref, bp_ref, bn_ref, pw_ref, sc_ref):
    tiles_per_seq = DEC_SEQ // POOL_TM
    t = lax.broadcasted_iota(jnp.int32, (POOL_TM, POOL_GC), 0)
    sums, cnts, us = [], [], []
    for tile, u, up, un in tiles:
        in_prompt = tile < N_PROMPT // POOL_TM
        pos = (tile - N_PROMPT // POOL_TM) % tiles_per_seq
        is_start = jnp.logical_or(in_prompt, pos == 0)
        is_end = jnp.logical_or(in_prompt, pos == tiles_per_seq - 1)
        up = jnp.where(is_start, jnp.zeros_like(up), up)
        un = jnp.where(is_end, jnp.zeros_like(un), un)
        for g, win in enumerate(POOL_WINDOWS):
            cols = slice(g * POOL_GC, (g + 1) * POOL_GC)
            us.append(u[:, cols])
            sums.append(_dot(b0_ref[g], u[:, cols]) + _dot(bp_ref[g], up[:, cols]) + _dot(bn_ref[g], un[:, cols]))
            cut_lo = jnp.where(is_start, jnp.maximum(win // 2 - t, 0), 0)
            cut_hi = jnp.where(is_end, jnp.maximum(t + (win - win // 2) - POOL_TM, 0), 0)
            cnts.append((win - cut_lo - cut_hi).astype(F32))
    pooled = [(s / cnt - u.astype(F32)).astype(BF16) for s, cnt, u in zip(sums, cnts, us)]
    ys = [_dot(p, pw_ref[n % POOL_GROUPS]) for n, p in enumerate(pooled)]
    out = []
    for j in range(len(tiles)):
        out.append(jnp.concatenate(
            [(ys[j * POOL_GROUPS + g] * sc_ref[:, g * POOL_GC:(g + 1) * POOL_GC]).astype(BF16)
             for g in range(POOL_GROUPS)], axis=1))
    return out


def _rope_tables(t):
    nf = DK_RET // 4
    rows = t // GRID_W
    r = jnp.repeat(jnp.arange(rows), GRID_W).astype(F32)
    col = jnp.tile(jnp.arange(GRID_W), rows).astype(F32)
    inv = ROPE_BASE ** (-jnp.arange(nf, dtype=F32) / nf)
    ar, ac = r[:, None] * inv, col[:, None] * inv
    cos = jnp.concatenate([jnp.cos(ar), jnp.cos(ar), jnp.cos(ac), jnp.cos(ac)], axis=1)
    sin = jnp.concatenate([-jnp.sin(ar), jnp.sin(ar), -jnp.sin(ac), jnp.sin(ac)], axis=1)
    return cos, sin


def _ret_kernel(*refs, n_chunks, heads, use_rope, has_s0, emit_state, aliased):
    refs = list(refs)
    dec_ref, q_ref, k_ref, v_ref = refs[:4]
    refs = refs[4:]
    if use_rope:
        cos_ref, sin_ref = refs[:2]
        refs = refs[2:]
    if has_s0:
        s0_ref = refs[0]
        refs = refs[1:]
    if aliased:
        refs = refs[1:]
    o_ref = refs[0]
    refs = refs[1:]
    if emit_state:
        st_ref = refs[0]
        refs = refs[1:]
    (qk_ref,) = refs

    c = RET_CHUNK
    nf = DK_RET // 4
    head0 = pl.program_id(1) * heads
    inter = has_s0 or n_chunks > 1

    row = lax.broadcasted_iota(jnp.int32, (c, DK_RET), 0).astype(F32)
    ii = lax.broadcasted_iota(jnp.int32, (c, c), 0)
    jj = lax.broadcasted_iota(jnp.int32, (c, c), 1)
    dist = (ii - jj).astype(F32)
    lane = lax.broadcasted_iota(jnp.int32, (c, DK_RET), 1)
    first_half = (lane % (2 * nf)) < nf

    def rope(x, r0):
        if not use_rope:
            return x
        partner = jnp.where(first_half, pltpu.roll(x, DK_RET - nf, axis=1), pltpu.roll(x, nf, axis=1))
        return x * cos_ref[pl.ds(r0, c), :] + partner * sin_ref[pl.ds(r0, c), :]

    for hh in range(heads):
        cols = slice(hh * DK_RET, (hh + 1) * DK_RET)

        def lam(d, shape):
            return _log_sigmoid(jnp.full(shape, dec_ref[d, head0 + hh], F32))

        lam_f, lam_b = lam(0, (c, DK_RET)), lam(1, (c, DK_RET))
        dq_f = jnp.exp((row + 1.0) * lam_f)
        dk_f = jnp.exp((c - 1.0 - row) * lam_f)
        dq_b = jnp.exp((c - row) * lam_b)
        dk_b = jnp.exp(row * lam_b)
        cdec_f = jnp.exp(float(c) * lam(0, (DK_RET, DV_RET)))
        cdec_b = jnp.exp(float(c) * lam(1, (DK_RET, DV_RET)))
        dmat = jnp.where(ii > jj, jnp.exp(dist * lam(0, (c, c))),
                         jnp.where(ii < jj, jnp.exp(-dist * lam(1, (c, c))), 2.0))

        group = 2 if n_chunks % 2 == 0 else 1

        def prepare(gi, carry):
            ns = [gi * group + j for j in range(group)]
            r0s = [n * c if isinstance(n, int) else pl.multiple_of(n * c, c) for n in ns]
            q = [rope(q_ref[pl.ds(r0, c), cols].astype(F32), r0) for r0 in r0s]
            k = [rope(k_ref[pl.ds(r0, c), cols].astype(F32) * (DK_RET ** -0.5), r0) for r0 in r0s]
            a = [(_dot_nt(q[j].astype(BF16), k[j].astype(BF16)) * dmat).astype(BF16) for j in range(group)]
            for j, r0 in enumerate(r0s):
                qk_ref[0, pl.ds(r0, c), cols] = (q[j] * dq_f).astype(BF16)
                qk_ref[1, pl.ds(r0, c), cols] = (q[j] * dq_b).astype(BF16)
                qk_ref[2, pl.ds(r0, c), cols] = (k[j] * dk_f).astype(BF16)
                qk_ref[3, pl.ds(r0, c), cols] = (k[j] * dk_b).astype(BF16)
            o = [_dot(a[j], v_ref[pl.ds(r0s[j], c), cols]) for j in range(group)]
            for j, r0 in enumerate(r0s):
                o_ref[pl.ds(r0, c), cols] = o[j]
            return carry

        def advance(n, states):
            ms = (n, n_chunks - 1 - n)
            r0s = [m * c if isinstance(m, int) else pl.multiple_of(m * c, c) for m in ms]
            upd = [_dot_tn(qk_ref[2 + d, pl.ds(r0s[d], c), cols], v_ref[pl.ds(r0s[d], c), cols]) for d in (0, 1)]
            out = []
            for d, cdec in ((0, cdec_f), (1, cdec_b)):
                if inter:
                    o_ref[pl.ds(r0s[d], c), cols] += _dot(qk_ref[d, pl.ds(r0s[d], c), cols],
                                                          states[d].astype(BF16))
                out.append(cdec * states[d] + upd[d])
            return tuple(out)

        _loop(n_chunks // group, prepare, 0)
        zero = jnp.zeros((DK_RET, DV_RET), F32)
        init = (s0_ref[0, 0, 0, hh], s0_ref[0, 0, 1, hh]) if has_s0 else (zero, zero)
        s_f, s_b = _loop(n_chunks, advance, init)
        if emit_state:
            st_ref[0, 0, hh] = s_f
            st_ref[0, 1, hh] = s_b


def _ret_scan(proj, decay, t, batch, heads, row_block0, rope, s0, s0_layer, state_buf, state_layer):
    w = heads * DK_RET
    col0 = POOL_WIDTH // w
    per = RET_WIDTH // w
    in_specs = [
        pl.BlockSpec(memory_space=pltpu.SMEM),
        pl.BlockSpec((t, w), lambda b, h: (row_block0 + b, col0 + h)),
        pl.BlockSpec((t, w), lambda b, h: (row_block0 + b, col0 + per + h)),
        pl.BlockSpec((t, w), lambda b, h: (row_block0 + b, col0 + 2 * per + h)),
    ]
    args = [decay, proj, proj, proj]
    if rope is not None:
        assert heads == 1
        in_specs += [pl.BlockSpec((t, DK_RET), lambda b, h: (0, 0))] * 2
        args += list(rope)
    if s0 is not None:
        in_specs.append(pl.BlockSpec((1, 1, 2, heads, DK_RET, DV_RET), lambda b, h: (b, s0_layer, 0, h, 0, 0)))
        args.append(s0)
    out_specs = [pl.BlockSpec((t, w), lambda b, h: (b, h))]
    out_shape = [jax.ShapeDtypeStruct((batch * t, RET_WIDTH), F32)]
    aliases = {}
    if state_layer is not None:
        out_specs.append(pl.BlockSpec((1, None, 2, heads, DK_RET, DV_RET), lambda b, h: (b, state_layer, 0, h, 0, 0)))
        out_shape.append(jax.ShapeDtypeStruct((batch, N_EVEN, 2, H_RET, DK_RET, DV_RET), F32))
        if state_buf is not None:
            aliases = {len(args): 1}
            in_specs.append(pl.BlockSpec(memory_space=pl.ANY))
            args.append(state_buf)
    return pl.pallas_call(
        functools.partial(_ret_kernel, n_chunks=t // RET_CHUNK, heads=heads, use_rope=rope is not None,
                          has_s0=s0 is not None, emit_state=state_layer is not None, aliased=bool(aliases)),
        grid=(batch, H_RET // heads),
        in_specs=in_specs,
        out_specs=out_specs,
        out_shape=out_shape,
        input_output_aliases=aliases,
        scratch_shapes=[pltpu.VMEM((4, t, w), BF16)],
        compiler_params=_params("parallel", "parallel"),
        name="ret_scan",
    )(*args)


def _gla_kernel(*refs, n_blocks, dvb, has_s0, emit_state, aliased):
    refs = list(refs)
    q_ref, k_ref, v_ref, z_ref, g2_ref, gb_ref = refs[:6]
    refs = refs[6:]
    if has_s0:
        s0_ref = refs[0]
        refs = refs[1:]
    if aliased:
        refs = refs[1:]
    o_ref = refs[0]
    refs = refs[1:]
    if emit_state:
        st_ref = refs[0]
        refs = refs[1:]
    qd_ref, kt_ref, a_ref, dec_ref, sf_ref, sb_ref = refs

    c = CHUNK
    blk = GLA_BLK
    cpb = blk // c
    n_chunks = n_blocks * cpb

    @pl.when(pl.program_id(2) == 0)
    def _():
        ii = lax.broadcasted_iota(jnp.int32, (blk, blk), 0)
        jj = lax.broadcasted_iota(jnp.int32, (blk, blk), 1)
        same = (ii // c) == (jj // c)
        masks = (jnp.logical_and(same, ii >= jj), jnp.logical_and(same, ii <= jj))
        tris = tuple(jnp.tile(jnp.where(m, 1.0, 0.0).astype(BF16), (1, 2)) for m in masks)

        group = 2 if n_blocks % 2 == 0 else 1
        chains = [(j, d) for j in range(group) for d in (0, 1)]

        def build(gi, carry):
            bis = [gi * group + j for j in range(group)]
            r0s = [bi * blk if isinstance(bi, int) else pl.multiple_of(bi * blk, blk) for bi in bis]
            q = [q_ref[pl.ds(r0, blk), :].astype(F32) * (DK_GLA ** -0.5) for r0 in r0s]
            k = [k_ref[pl.ds(r0, blk), :].astype(F32) for r0 in r0s]
            z = [_dot(z_ref[pl.ds(r0s[j], blk), :], g2_ref[0, d]) + gb_ref[0, d] for j, d in chains]
            lg = [jnp.concatenate(_split_bf16(_log_sigmoid(zz) * (LOG2E / GLA_TAU)), axis=0) for zz in z]
            g = [_dot(tris[d], lg[n]) for n, (j, d) in enumerate(chains)]
            qds, kds = [], []
            for n, (j, d) in enumerate(chains):
                edge = c - 1 if d == 0 else 0
                g_tot = jnp.concatenate(
                    [jnp.broadcast_to(g[n][ci * c + edge:ci * c + edge + 1, :], (c, DK_GLA))
                     for ci in range(cpb)], axis=0)
                qd = (q[j] * jnp.exp2(g[n])).astype(BF16)
                kds.append((k[j] * jnp.exp2(-g[n])).astype(BF16))
                qds.append(qd)
                qd_ref[d, pl.ds(r0s[j], blk), :] = qd
                kt_ref[d, pl.ds(r0s[j], blk), :] = (k[j] * jnp.exp2(g_tot - g[n])).astype(BF16)
                for ci in range(cpb):
                    dec_ref[d * n_chunks + bis[j] * cpb + ci] = jnp.exp2(g_tot[ci * c:ci * c + 8, :])
            a = [_dot_nt(qds[n], kds[n]) for n in range(len(chains))]
            for j in range(group):
                a_ref[bis[j]] = (jnp.where(masks[0], a[2 * j], 0.0) + jnp.where(masks[1], a[2 * j + 1], 0.0)).astype(BF16)
            return carry

        _loop(n_blocks // group, build, 0)

    def intra(bi, carry):
        r0 = bi * blk if isinstance(bi, int) else pl.multiple_of(bi * blk, blk)
        o_ref[pl.ds(r0, blk), :] = _dot(a_ref[bi], v_ref[pl.ds(r0, blk), :])
        return carry

    _loop(n_blocks, intra, 0)

    zero = jnp.zeros((dvb, DK_GLA), F32)
    sf_ref[...] = s0_ref[0, 0, 0, 0].T if has_s0 else zero
    sb_ref[...] = s0_ref[0, 0, 1, 0].T if has_s0 else zero

    def advance(n, carry):
        ms = (n, n_chunks - 1 - n)
        r0s = [pl.multiple_of(m * c, c) for m in ms]
        upd = [_dot_tn(v_ref[pl.ds(r0s[d], c), :], kt_ref[d, pl.ds(r0s[d], c), :]) for d in (0, 1)]
        for d, s_ref in ((0, sf_ref), (1, sb_ref)):
            s = s_ref[...]
            o_ref[pl.ds(r0s[d], c), :] += _dot_nt(qd_ref[d, pl.ds(r0s[d], c), :], s.astype(BF16))
            s_ref[...] = s * dec_ref[d * n_chunks + ms[d]][0:1, :] + upd[d]
        return carry

    lax.fori_loop(0, n_chunks, advance, 0, unroll=2)
    if emit_state:
        st_ref[0, 0, 0] = sf_ref[...].T
        st_ref[0, 1, 0] = sb_ref[...].T


def _gla_scan(proj, z1, g2, gb, t, batch, dvb, row_block0, s0, s0_layer, state_buf, state_layer):
    nd = DV_GLA // dvb
    kcol = GLA_DK_TOTAL // DK_GLA
    vcol = 2 * GLA_DK_TOTAL // dvb
    n_blocks = t // GLA_BLK
    n_chunks = t // CHUNK
    in_specs = [
        pl.BlockSpec((t, DK_GLA), lambda b, h, d: (row_block0 + b, h)),
        pl.BlockSpec((t, DK_GLA), lambda b, h, d: (row_block0 + b, kcol + h)),
        pl.BlockSpec((t, dvb), lambda b, h, d: (row_block0 + b, vcol + h * nd + d)),
        pl.BlockSpec((t, 128), lambda b, h, d: (row_block0 + b, 0)),
        pl.BlockSpec((1, 2, 128, DK_GLA), lambda b, h, d: (h, 0, 0, 0)),
        pl.BlockSpec((1, 2, 1, DK_GLA), lambda b, h, d: (h, 0, 0, 0)),
    ]
    args = [proj, proj, proj, z1, g2, gb]
    if s0 is not None:
        in_specs.append(pl.BlockSpec((1, 1, 2, 1, DK_GLA, dvb), lambda b, h, d: (b, s0_layer, 0, h, 0, d)))
        args.append(s0)
    out_specs = [pl.BlockSpec((t, dvb), lambda b, h, d: (b, h * nd + d))]
    out_shape = [jax.ShapeDtypeStruct((batch * t, GLA_DV_TOTAL), F32)]
    aliases = {}
    if state_layer is not None:
        out_specs.append(pl.BlockSpec((1, None, 2, 1, DK_GLA, dvb), lambda b, h, d: (b, state_layer, 0, h, 0, d)))
        out_shape.append(jax.ShapeDtypeStruct((batch, N_ODD, 2, H_GLA, DK_GLA, DV_GLA), F32))
        if state_buf is not None:
            aliases = {len(args): 1}
            in_specs.append(pl.BlockSpec(memory_space=pl.ANY))
            args.append(state_buf)
    return pl.pallas_call(
        functools.partial(_gla_kernel, n_blocks=n_blocks, dvb=dvb, has_s0=s0 is not None,
                          emit_state=state_layer is not None, aliased=bool(aliases)),
        grid=(batch, H_GLA, nd),
        in_specs=in_specs,
        out_specs=out_specs,
        out_shape=out_shape,
        input_output_aliases=aliases,
        scratch_shapes=[
            pltpu.VMEM((2, t, DK_GLA), BF16),
            pltpu.VMEM((2, t, DK_GLA), BF16),
            pltpu.VMEM((n_blocks, GLA_BLK, GLA_BLK), BF16),
            pltpu.VMEM((2 * n_chunks, 8, DK_GLA), F32),
            pltpu.VMEM((dvb, DK_GLA), F32),
            pltpu.VMEM((dvb, DK_GLA), F32),
        ],
        compiler_params=_params("parallel", "parallel", "arbitrary"),
        name="gla_scan",
    )(*args)


def _group_specs(tm, width, col_block=0):
    p_tiles = N_PROMPT // tm
    return [pl.BlockSpec((tm, width), lambda *idx: (jnp.minimum(idx[-1], p_tiles - 1), col_block)),
            pl.BlockSpec((tm, width), lambda *idx: (jnp.maximum(idx[-1] - p_tiles, 0), col_block))]


def _out_proj_kernel(*refs, n_heads, width, with_pool):
    refs = list(refs)
    if with_pool:
        u_ref, up_ref, un_ref, b0_ref, bp_ref, bn_ref, pw_ref, psc_ref = refs[:8]
        refs = refs[8:]
    op_ref, os_ref, gate_ref, gn_ref, w_ref = refs[:5]
    *x_refs, g1_ref, out_ref = refs[5:]
    xp_ref, xs_ref = x_refs if len(x_refs) == 2 else (x_refs[0], x_refs[0])
    i = pl.program_id(0)
    in_prompt = i < N_PROMPT // TM

    pooled = []
    if with_pool:
        sub = TM // POOL_TM
        tiles = []
        for s in range(sub):
            r0 = s * POOL_TM
            up = up_ref[...] if s == 0 else u_ref[r0 - POOL_HALO:r0, :]
            un = un_ref[...] if s == sub - 1 else u_ref[r0 + POOL_TM:r0 + POOL_TM + POOL_HALO, :]
            tiles.append((i * sub + s, u_ref[r0:r0 + POOL_TM, :], up, un))
        pooled = [jnp.concatenate(_pool_tiles(tiles, b0_ref, bp_ref, bn_ref, pw_ref, psc_ref), axis=0)]

    def run(o_ref, x_ref):
        parts = list(pooled)
        for hd in range(n_heads):
            cols = slice(hd * width, (hd + 1) * width)
            gt = gate_ref[:, cols].astype(F32)
            parts.append((_rms(o_ref[:, cols]) * gn_ref[:, cols] * _silu(gt)).astype(BF16))
        a = jnp.concatenate(parts, axis=1)
        out_ref[...] = x_ref[...] + g1_ref[0] * _dot(a, w_ref[...])

    @pl.when(in_prompt)
    def _():
        run(op_ref, xp_ref)

    @pl.when(jnp.logical_not(in_prompt))
    def _():
        run(os_ref, xs_ref)


def _out_proj(xs, mod, layer, w, w_layer, proj, gate_col_block, norm_g, o_prompt, o_sample, n_heads, pool=None):
    width = o_prompt.shape[1]
    x_specs = [pl.BlockSpec((TM, D_MODEL), lambda i: (i, 0))] if len(xs) == 1 else _group_specs(TM, D_MODEL)
    in_specs = _group_specs(TM, width) + [
        pl.BlockSpec((TM, width), lambda i: (i, gate_col_block)),
        pl.BlockSpec((1, width), lambda i: (0, 0)),
        pl.BlockSpec((None, D_MODEL, D_MODEL), lambda i: (w_layer, 0, 0)),
    ] + x_specs + [_mod_spec(layer, 2, TM)]
    args = [o_prompt, o_sample, proj, norm_g.reshape(1, width), w, *xs, mod]
    if pool is not None:
        b0, bp, bn = _pool_bands()
        halo_blocks = TM // POOL_HALO
        last = N_TOK // POOL_HALO - 1
        full = lambda shape: pl.BlockSpec(shape, lambda i: (0,) * len(shape))
        in_specs = [
            pl.BlockSpec((TM, POOL_WIDTH), lambda i: (i, 0)),
            pl.BlockSpec((POOL_HALO, POOL_WIDTH), lambda i: (jnp.maximum(i * halo_blocks - 1, 0), 0)),
            pl.BlockSpec((POOL_HALO, POOL_WIDTH), lambda i: (jnp.minimum((i + 1) * halo_blocks, last), 0)),
            full(b0.shape), full(bp.shape), full(bn.shape),
            pl.BlockSpec((None,) + pool[0].shape[1:], lambda i: (w_layer, 0, 0, 0)),
            pl.BlockSpec((None, 1, POOL_WIDTH), lambda i: (w_layer, 0, 0)),
        ] + in_specs
        args = [proj, proj, proj, b0, bp, bn, pool[0], pool[1].reshape(N_EVEN, 1, POOL_WIDTH)] + args
    return pl.pallas_call(
        functools.partial(_out_proj_kernel, n_heads=n_heads, width=width // n_heads, with_pool=pool is not None),
        grid=(N_TOK // TM,),
        in_specs=in_specs,
        out_specs=pl.BlockSpec((TM, D_MODEL), lambda i: (i, 0)),
        out_shape=jax.ShapeDtypeStruct((N_TOK, D_MODEL), F32),
        compiler_params=_params("parallel"),
        name="out_proj",
    )(*args)


def _ffn_kernel(x_ref, xp_ref, xn_ref, gn_ref, sh_ref, sc_ref, gt_ref, wa_ref, wb_ref, cw_ref, cb_ref,
                wd_ref, fg_ref, *rest, final):
    if final:
        op_ref, o_ref, h_ref, hh_ref = rest
    else:
        o_ref, h_ref, hh_ref = rest
    i = pl.program_id(0)
    f = pl.program_id(1)
    tm = TM_FFN

    def modnorm(x):
        return (_rms(x) * gn_ref[0] * (1.0 + sc_ref[0]) + sh_ref[0]).astype(BF16)

    @pl.when(f == 0)
    def _():
        h_ref[...] = modnorm(x_ref[...])
        hh_ref[...] = modnorm(jnp.concatenate([xp_ref[...], xn_ref[...]], axis=0))
        o_ref[...] = jnp.zeros_like(o_ref)

    a = _dot(h_ref[...], wa_ref[...])
    b = _dot(h_ref[...], wb_ref[...])
    a_halo = _dot(hh_ref[...], wa_ref[...])

    seq = jnp.where(i * tm < N_PROMPT, SEQ, DEC_SEQ)
    t = lax.broadcasted_iota(jnp.int32, (tm, TF), 0)
    pos = (i * tm + t) & (seq - 1)
    a_prev = jnp.where(t == 0, a_halo[7:8, :], pltpu.roll(a, 1, axis=0))
    a_prev = jnp.where(pos == 0, 0.0, a_prev)
    a_next = jnp.where(t == tm - 1, a_halo[8:9, :], pltpu.roll(a, tm - 1, axis=0))
    a_next = jnp.where(pos == seq - 1, 0.0, a_next)
    conv = a_prev * cw_ref[0:1, :] + a * cw_ref[1:2, :] + a_next * cw_ref[2:3, :] + cb_ref[...]
    o_ref[...] += _dot((_silu(conv) * b).astype(BF16), wd_ref[...])

    @pl.when(f == pl.num_programs(1) - 1)
    def _():
        y = x_ref[...] + gt_ref[0] * o_ref[...]
        if not final:
            o_ref[...] = y
        else:
            y = _rms(y) * fg_ref[...]
            in_prompt = i < N_PROMPT // tm

            @pl.when(in_prompt)
            def _():
                op_ref[...] = y

            @pl.when(jnp.logical_not(in_prompt))
            def _():
                o_ref[...] = y


def _ffn(x, mod, layer, norm_g, w_up, conv_w, conv_b, w_down, final_g, final):
    m = x.shape[0]
    nf = D_FF // TF
    tm = TM_FFN
    halo = 8
    last = m // halo - 1
    if final:
        p_tiles = N_PROMPT // tm
        out_specs = [pl.BlockSpec((tm, D_MODEL), lambda i, f: (jnp.minimum(i, p_tiles - 1), 0),
                                  pipeline_mode=pl.Buffered(1)),
                     pl.BlockSpec((tm, D_MODEL), lambda i, f: (jnp.maximum(i - p_tiles, 0), 0),
                                  pipeline_mode=pl.Buffered(1))]
        out_shape = [jax.ShapeDtypeStruct((N_PROMPT, D_MODEL), F32),
                     jax.ShapeDtypeStruct((N_SAMPLE, D_MODEL), F32)]
    else:
        out_specs = pl.BlockSpec((tm, D_MODEL), lambda i, f: (i, 0))
        out_shape = jax.ShapeDtypeStruct((m, D_MODEL), F32)
    return pl.pallas_call(
        functools.partial(_ffn_kernel, final=final),
        grid=(m // tm, nf),
        in_specs=[
            pl.BlockSpec((tm, D_MODEL), lambda i, f: (i, 0), pipeline_mode=pl.Buffered(1)),
            pl.BlockSpec((halo, D_MODEL), lambda i, f: (jnp.maximum(i * (tm // halo) - 1, 0), 0)),
            pl.BlockSpec((halo, D_MODEL), lambda i, f: (jnp.minimum((i + 1) * (tm // halo), last), 0)),
            pl.BlockSpec((1, 1, D_MODEL), lambda i, f: (layer, 0, 0)),
            _mod_spec(layer, 3, tm),
            _mod_spec(layer, 4, tm),
            _mod_spec(layer, 5, tm),
            pl.BlockSpec((None, D_MODEL, TF), lambda i, f: (layer, 0, f)),
            pl.BlockSpec((None, D_MODEL, TF), lambda i, f: (layer, 0, nf + f)),
            pl.BlockSpec((None, 3, TF), lambda i, f: (layer, 0, f)),
            pl.BlockSpec((None, 1, TF), lambda i, f: (layer, 0, f)),
            pl.BlockSpec((None, TF, D_MODEL), lambda i, f: (layer, f, 0)),
            pl.BlockSpec((1, D_MODEL), lambda i, f: (0, 0)),
        ],
        out_specs=out_specs,
        out_shape=out_shape,
        scratch_shapes=[
            pltpu.VMEM((tm, D_MODEL), BF16),
            pltpu.VMEM((2 * halo, D_MODEL), BF16),
        ],
        compiler_params=_params("arbitrary", "arbitrary"),
        name="ffn",
    )(x, x, x, norm_g, mod, mod, mod, w_up, w_up, conv_w, conv_b.reshape(DEPTH, 1, D_FF), w_down,
      final_g.reshape(1, D_MODEL))


def _gla_gate_weights(gw2, gb):
    r = GLA_GATE_RANK
    g2 = jnp.zeros((2, 128, GLA_DK_TOTAL), F32)
    g2 = g2.at[0, 0:r].set(gw2[0]).at[1, r:2 * r].set(gw2[1])
    g2 = g2.reshape(2, 128, H_GLA, DK_GLA).transpose(2, 0, 1, 3).astype(BF16)
    gbh = gb.reshape(2, H_GLA, 1, DK_GLA).transpose(1, 0, 2, 3)
    return g2, gbh


def kernel(x_prompt, x_sample, state_ret, state_gla, c, c_ctx, ada_w, ada_b, norm1_g, norm2_g,
           even_w_in, pool_w, pool_scale, ret_decay, ret_norm_g, even_w_out, odd_w_in, gla_gw1,
           gla_gw2, gla_gb, gla_norm_g, odd_w_out, ffn_w_up, ffn_conv_w, ffn_conv_b, ffn_w_down, final_g):
    xs = (x_prompt.reshape(N_PROMPT, D_MODEL), x_sample.reshape(N_SAMPLE, D_MODEL))
    cvec = jnp.concatenate([c_ctx[None, :], c, jnp.zeros((COND_PAD - N_COND, D_MODEL), F32)], axis=0)
    mod = _ada_mod(cvec, ada_w, ada_b)
    n1 = norm1_g.reshape(DEPTH, 1, D_MODEL)
    n2 = norm2_g.reshape(DEPTH, 1, D_MODEL)
    rope = _rope_tables(DEC_SEQ)
    p_blocks = N_PROMPT // DEC_SEQ

    w_even_in, w_even_out = even_w_in.astype(BF16), even_w_out.astype(BF16)
    w_odd_in, w_odd_out = odd_w_in.astype(BF16), odd_w_out.astype(BF16)
    w_up, w_down, w_pool = ffn_w_up.astype(BF16), ffn_w_down.astype(BF16), pool_w.astype(BF16)
    w_gate1 = jnp.concatenate([gla_gw1[:, 0], gla_gw1[:, 1],
                               jnp.zeros((N_ODD, D_MODEL, 128 - 2 * GLA_GATE_RANK), F32)], axis=2).astype(BF16)

    ret_states = gla_states = None
    for l in range(DEPTH):
        if l % 2 == 0:
            i = l // 2
            proj = _in_proj(xs, n1, mod, l, w_even_in, i, IN_SPLIT)
            o_p, ret_states = _ret_scan(proj, ret_decay[i], SEQ, BATCH, H_RET, 0, None, None, 0, ret_states, i)
            (o_s,) = _ret_scan(proj, ret_decay[i], DEC_SEQ, DEC_BATCH, 1, p_blocks, rope, state_ret, i, None, None)
            x = _out_proj(xs, mod, l, w_even_out, i, proj, EVEN_IN // RET_WIDTH - 1,
                          ret_norm_g[i], o_p, o_s, H_RET, (w_pool, pool_scale))
        else:
            j = l // 2
            proj = _in_proj(xs, n1, mod, l, w_odd_in, j, IN_SPLIT)
            z1 = _in_proj(xs, n1, mod, l, w_gate1, j, 1)
            g2, gbh = _gla_gate_weights(gla_gw2[j], gla_gb[j])
            o_p, gla_states = _gla_scan(proj, z1, g2, gbh, SEQ, BATCH, DV_GLA, 0, None, 0, gla_states, j)
            (o_s,) = _gla_scan(proj, z1, g2, gbh, DEC_SEQ, DEC_BATCH, DV_GLA, p_blocks, state_gla, j, None, None)
            x = _out_proj(xs, mod, l, w_odd_out, j, proj, ODD_IN // GLA_DV_TOTAL - 1,
                          gla_norm_g[j], o_p, o_s, H_GLA)
        xs = _ffn(x, mod, l, n2, w_up, ffn_conv_w, ffn_conv_b, w_down, final_g, l == DEPTH - 1)
        xs = tuple(xs) if l == DEPTH - 1 else (xs,)

    y_prompt = xs[0].reshape(BATCH, SEQ, D_MODEL)
    y_sample = xs[1].reshape(DEC_BATCH, DEC_SEQ, D_MODEL)
    return (y_prompt, y_sample, ret_states, gla_states)
```

```python
import functools

import numpy as np
import jax
import jax.numpy as jnp
from jax import lax
from jax.experimental import pallas as pl
from jax.experimental.pallas import tpu as pltpu

F32 = jnp.float32
BF16 = jnp.bfloat16

D_MODEL = 2048
BATCH = 16
SEQ = 256
DEPTH = 4
DEC_BATCH = 2
DEC_SEQ = 4096
GRID_W = 64
N_EVEN = (DEPTH + 1) // 2
N_ODD = DEPTH // 2
POOL_WIDTH = D_MODEL // 2
POOL_GROUPS = 4
POOL_GC = POOL_WIDTH // POOL_GROUPS
POOL_WINDOWS = (2, 4, 8, 16)
RET_WIDTH = D_MODEL // 2
H_RET = 8
DK_RET = RET_WIDTH // H_RET
DV_RET = RET_WIDTH // H_RET
ROPE_BASE = 10000.0
H_GLA = 4
GLA_DK_TOTAL = D_MODEL // 2
GLA_DV_TOTAL = D_MODEL
DK_GLA = GLA_DK_TOTAL // H_GLA
DV_GLA = GLA_DV_TOTAL // H_GLA
GLA_GATE_RANK = 16
GLA_TAU = 16.0
D_FF = 5632
CHUNK = 64
EPS = 1e-6
EVEN_IN = POOL_WIDTH + 4 * RET_WIDTH
ODD_IN = 2 * GLA_DK_TOTAL + 2 * GLA_DV_TOTAL

N_PROMPT = BATCH * SEQ
N_SAMPLE = DEC_BATCH * DEC_SEQ
N_TOK = N_PROMPT + N_SAMPLE
N_COND = 1 + DEC_BATCH
COND_PAD = 8

VMEM_LIMIT = 56 * 1024 * 1024

TM = 512
IN_SPLIT = 2
IN_CHUNK = 512
TM_FFN = 1024
TF = 512
POOL_TM = 256
POOL_HALO = 16
ADA_TN = 1024
RET_CHUNK = 256
GLA_BLK = 256
GLA_PROMPT_SEQS = 2


def _params(*sem):
    return pltpu.CompilerParams(dimension_semantics=sem, vmem_limit_bytes=VMEM_LIMIT)


def _cond_of_tile(i, tm):
    r0 = i * tm
    return jnp.where(r0 < N_PROMPT, 0, 1 + (r0 - N_PROMPT) // DEC_SEQ)


LOG2E = 1.4426950408889634


def _log_sigmoid(x):
    return jnp.minimum(x, 0.0) - jnp.log(1.0 + jnp.exp2(jnp.abs(x) * (-LOG2E)))


def _silu(x):
    return x * jax.nn.sigmoid(x)


def _rms(x):
    return x * lax.rsqrt(jnp.mean(x * x, axis=-1, keepdims=True) + EPS)


def _dot(a, b):
    return jnp.dot(a, b, preferred_element_type=F32)


def _dot_nt(a, b):
    return lax.dot_general(a, b, (((1,), (1,)), ((), ())), preferred_element_type=F32)


def _dot_tn(a, b):
    return lax.dot_general(a, b, (((0,), (0,)), ((), ())), preferred_element_type=F32)


def _split_bf16(x):
    hi = x.astype(BF16)
    lo = (x - hi.astype(F32)).astype(BF16)
    return hi, lo


def _loop(n, body, init, unroll=1):
    if n == 1:
        return body(0, init)
    return lax.fori_loop(0, n, body, init, unroll=unroll)


def _ada_kernel(c_ref, w_ref, b_ref, o_ref):
    s = _silu(c_ref[...]).astype(BF16)
    o_ref[0] = _dot(s, w_ref[0].astype(BF16)) + b_ref[0]


def _ada_mod(cvec, ada_w, ada_b):
    n = 6 * D_MODEL
    mod = pl.pallas_call(
        _ada_kernel,
        grid=(DEPTH, n // ADA_TN),
        in_specs=[
            pl.BlockSpec((COND_PAD, D_MODEL), lambda l, j: (0, 0)),
            pl.BlockSpec((1, D_MODEL, ADA_TN), lambda l, j: (l, 0, j)),
            pl.BlockSpec((1, 1, ADA_TN), lambda l, j: (l, 0, j)),
        ],
        out_specs=pl.BlockSpec((1, COND_PAD, ADA_TN), lambda l, j: (l, 0, j)),
        out_shape=jax.ShapeDtypeStruct((DEPTH, COND_PAD, n), F32),
        compiler_params=_params("parallel", "parallel"),
        name="ada_mod",
    )(cvec, ada_w, ada_b.reshape(DEPTH, 1, n))
    mod = mod[:, :N_COND].reshape(DEPTH, N_COND, 6, D_MODEL).transpose(0, 2, 1, 3)
    return mod.reshape(DEPTH * 6 * N_COND, 1, D_MODEL)


def _mod_spec(layer, part, tm, row_axis=0):
    base = (layer * 6 + part) * N_COND
    return pl.BlockSpec((1, 1, D_MODEL), lambda *idx: (base + _cond_of_tile(idx[row_axis], tm), 0, 0))


def _in_proj_kernel(*refs, chunk):
    *x_refs, g_ref, sh_ref, sc_ref, w_ref, o_ref, h_ref = refs

    if len(x_refs) == 1:
        x = x_refs[0][...]
    else:
        x = jnp.where(pl.program_id(1) < N_PROMPT // TM, x_refs[0][...], x_refs[1][...])
    h = _rms(x) * g_ref[0]
    h_ref[...] = (h * (1.0 + sc_ref[0]) + sh_ref[0]).astype(BF16)
    for c0 in range(0, o_ref.shape[1], chunk):
        o_ref[:, c0:c0 + chunk] = _dot(h_ref[...], w_ref[:, c0:c0 + chunk]).astype(o_ref.dtype)


def _in_proj(xs, norm_g, mod, layer, w, w_layer, split):
    n = w.shape[2]
    tn = n // split
    x_specs = [pl.BlockSpec((TM, D_MODEL), lambda j, i: (i, 0))] if len(xs) == 1 else _group_specs(TM, D_MODEL)
    return pl.pallas_call(
        functools.partial(_in_proj_kernel, chunk=min(IN_CHUNK, tn)),
        grid=(split, N_TOK // TM),
        in_specs=x_specs + [
            pl.BlockSpec((1, 1, D_MODEL), lambda j, i: (layer, 0, 0)),
            _mod_spec(layer, 0, TM, 1),
            _mod_spec(layer, 1, TM, 1),
            pl.BlockSpec((None, D_MODEL, tn), lambda j, i: (w_layer, 0, j)),
        ],
        out_specs=pl.BlockSpec((TM, tn), lambda j, i: (i, j)),
        out_shape=jax.ShapeDtypeStruct((N_TOK, n), BF16),
        scratch_shapes=[pltpu.VMEM((TM, D_MODEL), BF16)],
        compiler_params=_params("arbitrary", "arbitrary"),
        name="in_proj",
    )(*xs, norm_g, mod, mod, w)


def _pool_bands():
    t = POOL_TM
    b0 = np.zeros((POOL_GROUPS, t, t), np.float32)
    bp = np.zeros((POOL_GROUPS, t, POOL_HALO), np.float32)
    bn = np.zeros((POOL_GROUPS, t, POOL_HALO), np.float32)
    for g, win in enumerate(POOL_WINDOWS):
        for r in range(t):
            for s in range(r - win // 2, r + win - win // 2):
                if s < 0:
                    bp[g, r, s + POOL_HALO] = 1.0
                elif s >= t:
                    bn[g, r, s - t] = 1.0
                else:
                    b0[g, r, s] = 1.0
    return jnp.asarray(b0, BF16), jnp.asarray(bp, BF16), jnp.asarray(bn, BF16)


def _pool_tiles(tiles, b0_ref, bp_ref, bn_ref, pw_ref, sc_ref):
    tiles_per_seq = DEC_SEQ // POOL_TM
    t = lax.broadcasted_iota(jnp.int32, (POOL_TM, POOL_GC), 0)
    sums, cnts, us = [], [], []
    for tile, u, up, un in tiles:
        in_prompt = tile < N_PROMPT // POOL_TM
        pos = (tile - N_PROMPT // POOL_TM) % tiles_per_seq
        is_start = jnp.logical_or(in_prompt, pos == 0)
        is_end = jnp.logical_or(in_prompt, pos == tiles_per_seq - 1)
        up = jnp.where(is_start, jnp.zeros_like(up), up)
        un = jnp.where(is_end, jnp.zeros_like(un), un)
        for g, win in enumerate(POOL_WINDOWS):
            cols = slice(g * POOL_GC, (g + 1) * POOL_GC)
            us.append(u[:, cols])
            sums.append(_dot(b0_ref[g], u[:, cols]) + _dot(bp_ref[g], up[:, cols]) + _dot(bn_ref[g], un[:, cols]))
            cut_lo = jnp.where(is_start, jnp.maximum(win // 2 - t, 0), 0)
            cut_hi = jnp.where(is_end, jnp.maximum(t + (win - win // 2) - POOL_TM, 0), 0)
            cnts.append((win - cut_lo - cut_hi).astype(F32))
    pooled = [(s / cnt - u.astype(F32)).astype(BF16) for s, cnt, u in zip(sums, cnts, us)]
    ys = [_dot(p, pw_ref[n % POOL_GROUPS]) for n, p in enumerate(pooled)]
    out = []
    for j in range(len(tiles)):
        out.append(jnp.concatenate(
            [(ys[j * POOL_GROUPS + g] * sc_ref[:, g * POOL_GC:(g + 1) * POOL_GC]).astype(BF16)
             for g in range(POOL_GROUPS)], axis=1))
    return out


def _rope_tables(t):
    nf = DK_RET // 4
    rows = t // GRID_W
    r = jnp.repeat(jnp.arange(rows), GRID_W).astype(F32)
    col = jnp.tile(jnp.arange(GRID_W), rows).astype(F32)
    inv = ROPE_BASE ** (-jnp.arange(nf, dtype=F32) / nf)
    ar, ac = r[:, None] * inv, col[:, None] * inv
    cos = jnp.concatenate([jnp.cos(ar), jnp.cos(ar), jnp.cos(ac), jnp.cos(ac)], axis=1)
    sin = jnp.concatenate([-jnp.sin(ar), jnp.sin(ar), -jnp.sin(ac), jnp.sin(ac)], axis=1)
    return cos, sin


def _ret_kernel(*refs, n_chunks, heads, use_rope, has_s0, emit_state, aliased):
    refs = list(refs)
    dec_ref, q_ref, k_ref, v_ref = refs[:4]
    refs = refs[4:]
    if use_rope:
        cos_ref, sin_ref = refs[:2]
        refs = refs[2:]
    if has_s0:
        s0_ref = refs[0]
        refs = refs[1:]
    if aliased:
        refs = refs[1:]
    o_ref = refs[0]
    refs = refs[1:]
    if emit_state:
        st_ref = refs[0]
        refs = refs[1:]
    (qk_ref,) = refs

    c = RET_CHUNK
    nf = DK_RET // 4
    head0 = pl.program_id(1) * heads
    inter = has_s0 or n_chunks > 1

    row = lax.broadcasted_iota(jnp.int32, (c, DK_RET), 0).astype(F32)
    ii = lax.broadcasted_iota(jnp.int32, (c, c), 0)
    jj = lax.broadcasted_iota(jnp.int32, (c, c), 1)
    dist = (ii - jj).astype(F32)
    lane = lax.broadcasted_iota(jnp.int32, (c, DK_RET), 1)
    first_half = (lane % (2 * nf)) < nf

    def rope(x, r0):
        if not use_rope:
            return x
        partner = jnp.where(first_half, pltpu.roll(x, DK_RET - nf, axis=1), pltpu.roll(x, nf, axis=1))
        return x * cos_ref[pl.ds(r0, c), :] + partner * sin_ref[pl.ds(r0, c), :]

    for hh in range(heads):
        cols = slice(hh * DK_RET, (hh + 1) * DK_RET)

        def lam(d, shape):
            return _log_sigmoid(jnp.full(shape, dec_ref[d, head0 + hh], F32))

        lam_f, lam_b = lam(0, (c, DK_RET)), lam(1, (c, DK_RET))
        dq_f = jnp.exp((row + 1.0) * lam_f)
        dk_f = jnp.exp((c - 1.0 - row) * lam_f)
        dq_b = jnp.exp((c - row) * lam_b)
        dk_b = jnp.exp(row * lam_b)
        cdec_f = jnp.exp(float(c) * lam(0, (DK_RET, DV_RET)))
        cdec_b = jnp.exp(float(c) * lam(1, (DK_RET, DV_RET)))
        dmat = jnp.where(ii > jj, jnp.exp(dist * lam(0, (c, c))),
                         jnp.where(ii < jj, jnp.exp(-dist * lam(1, (c, c))), 2.0))

        group = 2 if n_chunks % 2 == 0 else 1

        def prepare(gi, carry):
            ns = [gi * group + j for j in range(group)]
            r0s = [n * c if isinstance(n, int) else pl.multiple_of(n * c, c) for n in ns]
            q = [rope(q_ref[pl.ds(r0, c), cols].astype(F32), r0) for r0 in r0s]
            k = [rope(k_ref[pl.ds(r0, c), cols].astype(F32) * (DK_RET ** -0.5), r0) for r0 in r0s]
            a = [(_dot_nt(q[j].astype(BF16), k[j].astype(BF16)) * dmat).astype(BF16) for j in range(group)]
            for j, r0 in enumerate(r0s):
                qk_ref[0, pl.ds(r0, c), cols] = (q[j] * dq_f).astype(BF16)
                qk_ref[1, pl.ds(r0, c), cols] = (q[j] * dq_b).astype(BF16)
                qk_ref[2, pl.ds(r0, c), cols] = (k[j] * dk_f).astype(BF16)
                qk_ref[3, pl.ds(r0, c), cols] = (k[j] * dk_b).astype(BF16)
            o = [_dot(a[j], v_ref[pl.ds(r0s[j], c), cols]) for j in range(group)]
            for j, r0 in enumerate(r0s):
                o_ref[pl.ds(r0, c), cols] = o[j]
            return carry

        def advance(n, states):
            ms = (n, n_chunks - 1 - n)
            r0s = [m * c if isinstance(m, int) else pl.multiple_of(m * c, c) for m in ms]
            upd = [_dot_tn(qk_ref[2 + d, pl.ds(r0s[d], c), cols], v_ref[pl.ds(r0s[d], c), cols]) for d in (0, 1)]
            out = []
            for d, cdec in ((0, cdec_f), (1, cdec_b)):
                if inter:
                    o_ref[pl.ds(r0s[d], c), cols] += _dot(qk_ref[d, pl.ds(r0s[d], c), cols],
                                                          states[d].astype(BF16))
                out.append(cdec * states[d] + upd[d])
            return tuple(out)

        _loop(n_chunks // group, prepare, 0)
        zero = jnp.zeros((DK_RET, DV_RET), F32)
        init = (s0_ref[0, 0, 0, hh], s0_ref[0, 0, 1, hh]) if has_s0 else (zero, zero)
        s_f, s_b = _loop(n_chunks, advance, init)
        if emit_state:
            st_ref[0, 0, hh] = s_f
            st_ref[0, 1, hh] = s_b


def _ret_scan(proj, decay, t, batch, heads, row_block0, rope, s0, s0_layer, state_buf, state_layer):
    w = heads * DK_RET
    col0 = POOL_WIDTH // w
    per = RET_WIDTH // w
    in_specs = [
        pl.BlockSpec(memory_space=pltpu.SMEM),
        pl.BlockSpec((t, w), lambda b, h: (row_block0 + b, col0 + h)),
        pl.BlockSpec((t, w), lambda b, h: (row_block0 + b, col0 + per + h)),
        pl.BlockSpec((t, w), lambda b, h: (row_block0 + b, col0 + 2 * per + h)),
    ]
    args = [decay, proj, proj, proj]
    if rope is not None:
        assert heads == 1
        in_specs += [pl.BlockSpec((t, DK_RET), lambda b, h: (0, 0))] * 2
        args += list(rope)
    if s0 is not None:
        in_specs.append(pl.BlockSpec((1, 1, 2, heads, DK_RET, DV_RET), lambda b, h: (b, s0_layer, 0, h, 0, 0)))
        args.append(s0)
    out_specs = [pl.BlockSpec((t, w), lambda b, h: (b, h))]
    out_shape = [jax.ShapeDtypeStruct((batch * t, RET_WIDTH), F32)]
    aliases = {}
    if state_layer is not None:
        out_specs.append(pl.BlockSpec((1, None, 2, heads, DK_RET, DV_RET), lambda b, h: (b, state_layer, 0, h, 0, 0)))
        out_shape.append(jax.ShapeDtypeStruct((batch, N_EVEN, 2, H_RET, DK_RET, DV_RET), F32))
        if state_buf is not None:
            aliases = {len(args): 1}
            in_specs.append(pl.BlockSpec(memory_space=pl.ANY))
            args.append(state_buf)
    return pl.pallas_call(
        functools.partial(_ret_kernel, n_chunks=t // RET_CHUNK, heads=heads, use_rope=rope is not None,
                          has_s0=s0 is not None, emit_state=state_layer is not None, aliased=bool(aliases)),
        grid=(batch, H_RET // heads),
        in_specs=in_specs,
        out_specs=out_specs,
        out_shape=out_shape,
        input_output_aliases=aliases,
        scratch_shapes=[pltpu.VMEM((4, t, w), BF16)],
        compiler_params=_params("parallel", "parallel"),
        name="ret_scan",
    )(*args)


def _gla_kernel(*refs, n_blocks, seqs, dvb, has_s0, emit_state, aliased):
    refs = list(refs)
    q_ref, k_ref, v_ref, z_ref, g2_ref, gb_ref = refs[:6]
    refs = refs[6:]
    if has_s0:
        s0_ref = refs[0]
        refs = refs[1:]
    if aliased:
        refs = refs[1:]
    o_ref = refs[0]
    refs = refs[1:]
    if emit_state:
        st_ref = refs[0]
        refs = refs[1:]
    qd_ref, kt_ref, a_ref, dec_ref, s_ref = refs

    c = CHUNK
    blk = GLA_BLK
    cpb = blk // c
    n_chunks = n_blocks * cpb
    tot_blocks = seqs * n_blocks
    tot_chunks = seqs * n_chunks

    @pl.when(pl.program_id(2) == 0)
    def _():
        ii = lax.broadcasted_iota(jnp.int32, (blk, blk), 0)
        jj = lax.broadcasted_iota(jnp.int32, (blk, blk), 1)
        same = (ii // c) == (jj // c)
        masks = (jnp.logical_and(same, ii >= jj), jnp.logical_and(same, ii <= jj))
        tris = tuple(jnp.tile(jnp.where(m, 1.0, 0.0).astype(BF16), (1, 2)) for m in masks)

        group = 2 if tot_blocks % 2 == 0 else 1
        chains = [(j, d) for j in range(group) for d in (0, 1)]

        def build(gi, carry):
            bis = [gi * group + j for j in range(group)]
            r0s = [bi * blk if isinstance(bi, int) else pl.multiple_of(bi * blk, blk) for bi in bis]
            q = [q_ref[pl.ds(r0, blk), :].astype(F32) * (DK_GLA ** -0.5) for r0 in r0s]
            k = [k_ref[pl.ds(r0, blk), :].astype(F32) for r0 in r0s]
            z = [_dot(z_ref[pl.ds(r0s[j], blk), :], g2_ref[0, d]) + gb_ref[0, d] for j, d in chains]
            lg = [jnp.concatenate(_split_bf16(_log_sigmoid(zz) * (LOG2E / GLA_TAU)), axis=0) for zz in z]
            g = [_dot(tris[d], lg[n]) for n, (j, d) in enumerate(chains)]
            qds, kds = [], []
            for n, (j, d) in enumerate(chains):
                edge = c - 1 if d == 0 else 0
                g_tot = jnp.concatenate(
                    [jnp.broadcast_to(g[n][ci * c + edge:ci * c + edge + 1, :], (c, DK_GLA))
                     for ci in range(cpb)], axis=0)
                qd = (q[j] * jnp.exp2(g[n])).astype(BF16)
                kds.append((k[j] * jnp.exp2(-g[n])).astype(BF16))
                qds.append(qd)
                qd_ref[d, pl.ds(r0s[j], blk), :] = qd
                kt_ref[d, pl.ds(r0s[j], blk), :] = (k[j] * jnp.exp2(g_tot - g[n])).astype(BF16)
                for ci in range(cpb):
                    dec_ref[d * tot_chunks + bis[j] * cpb + ci] = jnp.exp2(g_tot[ci * c:ci * c + 8, :])
            a = [_dot_nt(qds[n], kds[n]) for n in range(len(chains))]
            for j in range(group):
                a_ref[bis[j]] = (jnp.where(masks[0], a[2 * j], 0.0) + jnp.where(masks[1], a[2 * j + 1], 0.0)).astype(BF16)
            return carry

        _loop(tot_blocks // group, build, 0)

    def intra(bi, carry):
        r0 = bi * blk if isinstance(bi, int) else pl.multiple_of(bi * blk, blk)
        o_ref[pl.ds(r0, blk), :] = _dot(a_ref[bi], v_ref[pl.ds(r0, blk), :])
        return carry

    _loop(tot_blocks, intra, 0)

    runs = [(sq, d) for sq in range(seqs) for d in (0, 1)]
    for sq, d in runs:
        s_ref[2 * sq + d] = s0_ref[sq, 0, d, 0].T if has_s0 else jnp.zeros((dvb, DK_GLA), F32)

    def advance(n, carry):
        ms = [sq * n_chunks + (n if d == 0 else n_chunks - 1 - n) for sq, d in runs]
        r0s = [pl.multiple_of(m * c, c) for m in ms]
        upd = [_dot_tn(v_ref[pl.ds(r0s[n_], c), :], kt_ref[d, pl.ds(r0s[n_], c), :])
               for n_, (sq, d) in enumerate(runs)]
        for n_, (sq, d) in enumerate(runs):
            s = s_ref[2 * sq + d]
            o_ref[pl.ds(r0s[n_], c), :] += _dot_nt(qd_ref[d, pl.ds(r0s[n_], c), :], s.astype(BF16))
            s_ref[2 * sq + d] = s * dec_ref[d * tot_chunks + ms[n_]][0:1, :] + upd[n_]
        return carry

    lax.fori_loop(0, n_chunks, advance, 0, unroll=2 if seqs == 1 else 1)
    if emit_state:
        for sq, d in runs:
            st_ref[sq, d, 0] = s_ref[2 * sq + d].T


def _gla_scan(proj, z1, g2, gb, t, batch, seqs, dvb, row_block0, s0, s0_layer, state_buf, state_layer):
    nd = DV_GLA // dvb
    kcol = GLA_DK_TOTAL // DK_GLA
    vcol = 2 * GLA_DK_TOTAL // dvb
    n_blocks = t // GLA_BLK
    rows = seqs * t
    in_specs = [
        pl.BlockSpec((rows, DK_GLA), lambda b, h, d: (row_block0 + b, h)),
        pl.BlockSpec((rows, DK_GLA), lambda b, h, d: (row_block0 + b, kcol + h)),
        pl.BlockSpec((rows, dvb), lambda b, h, d: (row_block0 + b, vcol + h * nd + d)),
        pl.BlockSpec((rows, 128), lambda b, h, d: (row_block0 + b, 0)),
        pl.BlockSpec((1, 2, 128, DK_GLA), lambda b, h, d: (h, 0, 0, 0)),
        pl.BlockSpec((1, 2, 1, DK_GLA), lambda b, h, d: (h, 0, 0, 0)),
    ]
    args = [proj, proj, proj, z1, g2, gb]
    if s0 is not None:
        assert seqs == 1
        in_specs.append(pl.BlockSpec((1, 1, 2, 1, DK_GLA, dvb), lambda b, h, d: (b, s0_layer, 0, h, 0, d)))
        args.append(s0)
    out_specs = [pl.BlockSpec((rows, dvb), lambda b, h, d: (b, h * nd + d))]
    out_shape = [jax.ShapeDtypeStruct((batch * t, GLA_DV_TOTAL), F32)]
    aliases = {}
    if state_layer is not None:
        out_specs.append(pl.BlockSpec((seqs, None, 2, 1, DK_GLA, dvb), lambda b, h, d: (b, state_layer, 0, h, 0, d)))
        out_shape.append(jax.ShapeDtypeStruct((batch, N_ODD, 2, H_GLA, DK_GLA, DV_GLA), F32))
        if state_buf is not None:
            aliases = {len(args): 1}
            in_specs.append(pl.BlockSpec(memory_space=pl.ANY))
            args.append(state_buf)
    return pl.pallas_call(
        functools.partial(_gla_kernel, n_blocks=n_blocks, seqs=seqs, dvb=dvb, has_s0=s0 is not None,
                          emit_state=state_layer is not None, aliased=bool(aliases)),
        grid=(batch // seqs, H_GLA, nd),
        in_specs=in_specs,
        out_specs=out_specs,
        out_shape=out_shape,
        input_output_aliases=aliases,
        scratch_shapes=[
            pltpu.VMEM((2, rows, DK_GLA), BF16),
            pltpu.VMEM((2, rows, DK_GLA), BF16),
            pltpu.VMEM((rows // GLA_BLK, GLA_BLK, GLA_BLK), BF16),
            pltpu.VMEM((2 * rows // CHUNK, 8, DK_GLA), F32),
            pltpu.VMEM((2 * seqs, dvb, DK_GLA), F32),
        ],
        compiler_params=_params("parallel", "parallel", "arbitrary"),
        name="gla_scan",
    )(*args)


def _group_specs(tm, width, col_block=0):
    p_tiles = N_PROMPT // tm
    return [pl.BlockSpec((tm, width), lambda *idx: (jnp.minimum(idx[-1], p_tiles - 1), col_block)),
            pl.BlockSpec((tm, width), lambda *idx: (jnp.maximum(idx[-1] - p_tiles, 0), col_block))]


def _out_proj_kernel(*refs, n_heads, width, with_pool):
    refs = list(refs)
    if with_pool:
        u_ref, up_ref, un_ref, b0_ref, bp_ref, bn_ref, pw_ref, psc_ref = refs[:8]
        refs = refs[8:]
    op_ref, os_ref, gate_ref, gn_ref, w_ref = refs[:5]
    *x_refs, g1_ref, out_ref = refs[5:]
    xp_ref, xs_ref = x_refs if len(x_refs) == 2 else (x_refs[0], x_refs[0])
    i = pl.program_id(0)
    in_prompt = i < N_PROMPT // TM

    pooled = []
    if with_pool:
        sub = TM // POOL_TM
        tiles = []
        for s in range(sub):
            r0 = s * POOL_TM
            up = up_ref[...] if s == 0 else u_ref[r0 - POOL_HALO:r0, :]
            un = un_ref[...] if s == sub - 1 else u_ref[r0 + POOL_TM:r0 + POOL_TM + POOL_HALO, :]
            tiles.append((i * sub + s, u_ref[r0:r0 + POOL_TM, :], up, un))
        pooled = [jnp.concatenate(_pool_tiles(tiles, b0_ref, bp_ref, bn_ref, pw_ref, psc_ref), axis=0)]

    def run(o_ref, x_ref):
        parts = list(pooled)
        for hd in range(n_heads):
            cols = slice(hd * width, (hd + 1) * width)
            gt = gate_ref[:, cols].astype(F32)
            parts.append((_rms(o_ref[:, cols]) * gn_ref[:, cols] * _silu(gt)).astype(BF16))
        a = jnp.concatenate(parts, axis=1)
        out_ref[...] = x_ref[...] + g1_ref[0] * _dot(a, w_ref[...])

    @pl.when(in_prompt)
    def _():
        run(op_ref, xp_ref)

    @pl.when(jnp.logical_not(in_prompt))
    def _():
        run(os_ref, xs_ref)


def _out_proj(xs, mod, layer, w, w_layer, proj, gate_col_block, norm_g, o_prompt, o_sample, n_heads, pool=None):
    width = o_prompt.shape[1]
    x_specs = [pl.BlockSpec((TM, D_MODEL), lambda i: (i, 0))] if len(xs) == 1 else _group_specs(TM, D_MODEL)
    in_specs = _group_specs(TM, width) + [
        pl.BlockSpec((TM, width), lambda i: (i, gate_col_block)),
        pl.BlockSpec((1, width), lambda i: (0, 0)),
        pl.BlockSpec((None, D_MODEL, D_MODEL), lambda i: (w_layer, 0, 0)),
    ] + x_specs + [_mod_spec(layer, 2, TM)]
    args = [o_prompt, o_sample, proj, norm_g.reshape(1, width), w, *xs, mod]
    if pool is not None:
        b0, bp, bn = _pool_bands()
        halo_blocks = TM // POOL_HALO
        last = N_TOK // POOL_HALO - 1
        full = lambda shape: pl.BlockSpec(shape, lambda i: (0,) * len(shape))
        in_specs = [
            pl.BlockSpec((TM, POOL_WIDTH), lambda i: (i, 0)),
            pl.BlockSpec((POOL_HALO, POOL_WIDTH), lambda i: (jnp.maximum(i * halo_blocks - 1, 0), 0)),
            pl.BlockSpec((POOL_HALO, POOL_WIDTH), lambda i: (jnp.minimum((i + 1) * halo_blocks, last), 0)),
            full(b0.shape), full(bp.shape), full(bn.shape),
            pl.BlockSpec((None,) + pool[0].shape[1:], lambda i: (w_layer, 0, 0, 0)),
            pl.BlockSpec((None, 1, POOL_WIDTH), lambda i: (w_layer, 0, 0)),
        ] + in_specs
        args = [proj, proj, proj, b0, bp, bn, pool[0], pool[1].reshape(N_EVEN, 1, POOL_WIDTH)] + args
    return pl.pallas_call(
        functools.partial(_out_proj_kernel, n_heads=n_heads, width=width // n_heads, with_pool=pool is not None),
        grid=(N_TOK // TM,),
        in_specs=in_specs,
        out_specs=pl.BlockSpec((TM, D_MODEL), lambda i: (i, 0)),
        out_shape=jax.ShapeDtypeStruct((N_TOK, D_MODEL), F32),
        compiler_params=_params("parallel"),
        name="out_proj",
    )(*args)


def _ffn_kernel(x_ref, xp_ref, xn_ref, gn_ref, sh_ref, sc_ref, gt_ref, wa_ref, wb_ref, cw_ref, cb_ref,
                wd_ref, fg_ref, *rest, final):
    if final:
        op_ref, o_ref, h_ref, hh_ref = rest
    else:
        o_ref, h_ref, hh_ref = rest
    i = pl.program_id(0)
    f = pl.program_id(1)
    tm = TM_FFN

    def modnorm(x):
        return (_rms(x) * gn_ref[0] * (1.0 + sc_ref[0]) + sh_ref[0]).astype(BF16)

    @pl.when(f == 0)
    def _():
        h_ref[...] = modnorm(x_ref[...])
        hh_ref[...] = modnorm(jnp.concatenate([xp_ref[...], xn_ref[...]], axis=0))
        o_ref[...] = jnp.zeros_like(o_ref)

    a = _dot(h_ref[...], wa_ref[...])
    b = _dot(h_ref[...], wb_ref[...])
    a_halo = _dot(hh_ref[...], wa_ref[...])

    seq = jnp.where(i * tm < N_PROMPT, SEQ, DEC_SEQ)
    t = lax.broadcasted_iota(jnp.int32, (tm, TF), 0)
    pos = (i * tm + t) & (seq - 1)
    a_prev = jnp.where(t == 0, a_halo[7:8, :], pltpu.roll(a, 1, axis=0))
    a_prev = jnp.where(pos == 0, 0.0, a_prev)
    a_next = jnp.where(t == tm - 1, a_halo[8:9, :], pltpu.roll(a, tm - 1, axis=0))
    a_next = jnp.where(pos == seq - 1, 0.0, a_next)
    conv = a_prev * cw_ref[0:1, :] + a * cw_ref[1:2, :] + a_next * cw_ref[2:3, :] + cb_ref[...]
    o_ref[...] += _dot((_silu(conv) * b).astype(BF16), wd_ref[...])

    @pl.when(f == pl.num_programs(1) - 1)
    def _():
        y = x_ref[...] + gt_ref[0] * o_ref[...]
        if not final:
            o_ref[...] = y
        else:
            y = _rms(y) * fg_ref[...]
            in_prompt = i < N_PROMPT // tm

            @pl.when(in_prompt)
            def _():
                op_ref[...] = y

            @pl.when(jnp.logical_not(in_prompt))
            def _():
                o_ref[...] = y


def _ffn(x, mod, layer, norm_g, w_up, conv_w, conv_b, w_down, final_g, final):
    m = x.shape[0]
    nf = D_FF // TF
    tm = TM_FFN
    halo = 8
    last = m // halo - 1
    if final:
        p_tiles = N_PROMPT // tm
        out_specs = [pl.BlockSpec((tm, D_MODEL), lambda i, f: (jnp.minimum(i, p_tiles - 1), 0),
                                  pipeline_mode=pl.Buffered(1)),
                     pl.BlockSpec((tm, D_MODEL), lambda i, f: (jnp.maximum(i - p_tiles, 0), 0),
                                  pipeline_mode=pl.Buffered(1))]
        out_shape = [jax.ShapeDtypeStruct((N_PROMPT, D_MODEL), F32),
                     jax.ShapeDtypeStruct((N_SAMPLE, D_MODEL), F32)]
    else:
        out_specs = pl.BlockSpec((tm, D_MODEL), lambda i, f: (i, 0))
        out_shape = jax.ShapeDtypeStruct((m, D_MODEL), F32)
    return pl.pallas_call(
        functools.partial(_ffn_kernel, final=final),
        grid=(m // tm, nf),
        in_specs=[
            pl.BlockSpec((tm, D_MODEL), lambda i, f: (i, 0), pipeline_mode=pl.Buffered(1)),
            pl.BlockSpec((halo, D_MODEL), lambda i, f: (jnp.maximum(i * (tm // halo) - 1, 0), 0)),
            pl.BlockSpec((halo, D_MODEL), lambda i, f: (jnp.minimum((i + 1) * (tm // halo), last), 0)),
            pl.BlockSpec((1, 1, D_MODEL), lambda i, f: (layer, 0, 0)),
            _mod_spec(layer, 3, tm),
            _mod_spec(layer, 4, tm),
            _mod_spec(layer, 5, tm),
            pl.BlockSpec((None, None, D_MODEL, TF), lambda i, f: (layer, f, 0, 0)),
            pl.BlockSpec((None, None, D_MODEL, TF), lambda i, f: (layer, nf + f, 0, 0)),
            pl.BlockSpec((None, 3, TF), lambda i, f: (layer, 0, f)),
            pl.BlockSpec((None, 1, TF), lambda i, f: (layer, 0, f)),
            pl.BlockSpec((None, TF, D_MODEL), lambda i, f: (layer, f, 0)),
            pl.BlockSpec((1, D_MODEL), lambda i, f: (0, 0)),
        ],
        out_specs=out_specs,
        out_shape=out_shape,
        scratch_shapes=[
            pltpu.VMEM((tm, D_MODEL), BF16),
            pltpu.VMEM((2 * halo, D_MODEL), BF16),
        ],
        compiler_params=_params("arbitrary", "arbitrary"),
        name="ffn",
    )(x, x, x, norm_g, mod, mod, mod, w_up, w_up, conv_w, conv_b.reshape(DEPTH, 1, D_FF), w_down,
      final_g.reshape(1, D_MODEL))


def _gla_gate_weights(gw2, gb):
    r = GLA_GATE_RANK
    g2 = jnp.zeros((2, 128, GLA_DK_TOTAL), F32)
    g2 = g2.at[0, 0:r].set(gw2[0]).at[1, r:2 * r].set(gw2[1])
    g2 = g2.reshape(2, 128, H_GLA, DK_GLA).transpose(2, 0, 1, 3).astype(BF16)
    gbh = gb.reshape(2, H_GLA, 1, DK_GLA).transpose(1, 0, 2, 3)
    return g2, gbh


def kernel(x_prompt, x_sample, state_ret, state_gla, c, c_ctx, ada_w, ada_b, norm1_g, norm2_g,
           even_w_in, pool_w, pool_scale, ret_decay, ret_norm_g, even_w_out, odd_w_in, gla_gw1,
           gla_gw2, gla_gb, gla_norm_g, odd_w_out, ffn_w_up, ffn_conv_w, ffn_conv_b, ffn_w_down, final_g):
    xs = (x_prompt.reshape(N_PROMPT, D_MODEL), x_sample.reshape(N_SAMPLE, D_MODEL))
    cvec = jnp.concatenate([c_ctx[None, :], c, jnp.zeros((COND_PAD - N_COND, D_MODEL), F32)], axis=0)
    mod = _ada_mod(cvec, ada_w, ada_b)
    n1 = norm1_g.reshape(DEPTH, 1, D_MODEL)
    n2 = norm2_g.reshape(DEPTH, 1, D_MODEL)
    rope = _rope_tables(DEC_SEQ)
    p_blocks = N_PROMPT // DEC_SEQ

    w_even_in, w_even_out = even_w_in.astype(BF16), even_w_out.astype(BF16)
    w_odd_in, w_odd_out = odd_w_in.astype(BF16), odd_w_out.astype(BF16)
    w_down, w_pool = ffn_w_down.astype(BF16), pool_w.astype(BF16)
    w_up = ffn_w_up.astype(BF16).reshape(DEPTH, D_MODEL, 2 * D_FF // TF, TF).transpose(0, 2, 1, 3)
    w_gate1 = jnp.concatenate([gla_gw1[:, 0], gla_gw1[:, 1],
                               jnp.zeros((N_ODD, D_MODEL, 128 - 2 * GLA_GATE_RANK), F32)], axis=2).astype(BF16)

    ret_states = gla_states = None
    for l in range(DEPTH):
        if l % 2 == 0:
            i = l // 2
            proj = _in_proj(xs, n1, mod, l, w_even_in, i, IN_SPLIT)
            o_p, ret_states = _ret_scan(proj, ret_decay[i], SEQ, BATCH, H_RET, 0, None, None, 0, ret_states, i)
            (o_s,) = _ret_scan(proj, ret_decay[i], DEC_SEQ, DEC_BATCH, 1, p_blocks, rope, state_ret, i, None, None)
            x = _out_proj(xs, mod, l, w_even_out, i, proj, EVEN_IN // RET_WIDTH - 1,
                          ret_norm_g[i], o_p, o_s, H_RET, (w_pool, pool_scale))
        else:
            j = l // 2
            proj = _in_proj(xs, n1, mod, l, w_odd_in, j, IN_SPLIT)
            z1 = _in_proj(xs, n1, mod, l, w_gate1, j, 1)
            g2, gbh = _gla_gate_weights(gla_gw2[j], gla_gb[j])
            o_p, gla_states = _gla_scan(proj, z1, g2, gbh, SEQ, BATCH, GLA_PROMPT_SEQS, DV_GLA, 0, None, 0,
                                        gla_states, j)
            (o_s,) = _gla_scan(proj, z1, g2, gbh, DEC_SEQ, DEC_BATCH, 1, DV_GLA, p_blocks, state_gla, j,
                               None, None)
            x = _out_proj(xs, mod, l, w_odd_out, j, proj, ODD_IN // GLA_DV_TOTAL - 1,
                          gla_norm_g[j], o_p, o_s, H_GLA)
        xs = _ffn(x, mod, l, n2, w_up, ffn_conv_w, ffn_conv_b, w_down, final_g, l == DEPTH - 1)
        xs = tuple(xs) if l == DEPTH - 1 else (xs,)

    y_prompt = xs[0].reshape(BATCH, SEQ, D_MODEL)
    y_sample = xs[1].reshape(DEC_BATCH, DEC_SEQ, D_MODEL)
    return (y_prompt, y_sample, ret_states, gla_states)
```

```python
import functools

import numpy as np
import jax
import jax.numpy as jnp
from jax import lax
from jax.experimental import pallas as pl
from jax.experimental.pallas import tpu as pltpu

F32 = jnp.float32
BF16 = jnp.bfloat16

D_MODEL = 2048
BATCH = 16
SEQ = 256
DEPTH = 4
DEC_BATCH = 2
DEC_SEQ = 4096
GRID_W = 64
N_EVEN = (DEPTH + 1) // 2
N_ODD = DEPTH // 2
POOL_WIDTH = D_MODEL // 2
POOL_GROUPS = 4
POOL_GC = POOL_WIDTH // POOL_GROUPS
POOL_WINDOWS = (2, 4, 8, 16)
RET_WIDTH = D_MODEL // 2
H_RET = 8
DK_RET = RET_WIDTH // H_RET
DV_RET = RET_WIDTH // H_RET
ROPE_BASE = 10000.0
H_GLA = 4
GLA_DK_TOTAL = D_MODEL // 2
GLA_DV_TOTAL = D_MODEL
DK_GLA = GLA_DK_TOTAL // H_GLA
DV_GLA = GLA_DV_TOTAL // H_GLA
GLA_GATE_RANK = 16
GLA_TAU = 16.0
D_FF = 5632
CHUNK = 64
EPS = 1e-6
EVEN_IN = POOL_WIDTH + 4 * RET_WIDTH
ODD_IN = 2 * GLA_DK_TOTAL + 2 * GLA_DV_TOTAL

N_PROMPT = BATCH * SEQ
N_SAMPLE = DEC_BATCH * DEC_SEQ
N_TOK = N_PROMPT + N_SAMPLE
N_COND = 1 + DEC_BATCH
COND_PAD = 8

VMEM_LIMIT = 56 * 1024 * 1024

TM = 512
IN_SPLIT = 2
IN_CHUNK = 512
TM_FFN = 1024
TF = 512
POOL_TM = 256
POOL_HALO = 16
ADA_TN = 1024
RET_CHUNK = 256
GLA_BLK = 256
GLA_PROMPT_SEQS = 2


def _params(*sem):
    return pltpu.CompilerParams(dimension_semantics=sem, vmem_limit_bytes=VMEM_LIMIT)


def _cond_of_tile(i, tm):
    r0 = i * tm
    return jnp.where(r0 < N_PROMPT, 0, 1 + (r0 - N_PROMPT) // DEC_SEQ)


LOG2E = 1.4426950408889634


def _log_sigmoid(x):
    return jnp.minimum(x, 0.0) - jnp.log(1.0 + jnp.exp2(jnp.abs(x) * (-LOG2E)))


def _silu(x):
    return x * jax.nn.sigmoid(x)


def _rms(x):
    return x * lax.rsqrt(jnp.mean(x * x, axis=-1, keepdims=True) + EPS)


def _dot(a, b):
    return jnp.dot(a, b, preferred_element_type=F32)


def _dot_nt(a, b):
    return lax.dot_general(a, b, (((1,), (1,)), ((), ())), preferred_element_type=F32)


def _dot_tn(a, b):
    return lax.dot_general(a, b, (((0,), (0,)), ((), ())), preferred_element_type=F32)


def _split_bf16(x):
    hi = x.astype(BF16)
    lo = (x - hi.astype(F32)).astype(BF16)
    return hi, lo


def _loop(n, body, init, unroll=1):
    if n == 1:
        return body(0, init)
    return lax.fori_loop(0, n, body, init, unroll=unroll)


def _ada_kernel(c_ref, w_ref, b_ref, o_ref):
    s = _silu(c_ref[...]).astype(BF16)
    o_ref[0] = _dot(s, w_ref[0].astype(BF16)) + b_ref[0]


def _ada_mod(cvec, ada_w, ada_b):
    n = 6 * D_MODEL
    mod = pl.pallas_call(
        _ada_kernel,
        grid=(DEPTH, n // ADA_TN),
        in_specs=[
            pl.BlockSpec((COND_PAD, D_MODEL), lambda l, j: (0, 0)),
            pl.BlockSpec((1, D_MODEL, ADA_TN), lambda l, j: (l, 0, j)),
            pl.BlockSpec((1, 1, ADA_TN), lambda l, j: (l, 0, j)),
        ],
        out_specs=pl.BlockSpec((1, COND_PAD, ADA_TN), lambda l, j: (l, 0, j)),
        out_shape=jax.ShapeDtypeStruct((DEPTH, COND_PAD, n), F32),
        compiler_params=_params("parallel", "parallel"),
        name="ada_mod",
    )(cvec, ada_w, ada_b.reshape(DEPTH, 1, n))
    mod = mod[:, :N_COND].reshape(DEPTH, N_COND, 6, D_MODEL).transpose(0, 2, 1, 3)
    return mod.reshape(DEPTH * 6 * N_COND, 1, D_MODEL)


def _mod_spec(layer, part, tm, row_axis=0):
    base = (layer * 6 + part) * N_COND
    return pl.BlockSpec((1, 1, D_MODEL), lambda *idx: (base + _cond_of_tile(idx[row_axis], tm), 0, 0))


def _in_proj_kernel(*refs, chunk):
    *x_refs, g_ref, sh_ref, sc_ref, w_ref, o_ref, h_ref = refs

    if len(x_refs) == 1:
        x = x_refs[0][...]
    else:
        x = jnp.where(pl.program_id(1) < N_PROMPT // TM, x_refs[0][...], x_refs[1][...])
    h = _rms(x) * g_ref[0]
    h_ref[...] = (h * (1.0 + sc_ref[0]) + sh_ref[0]).astype(BF16)
    for c0 in range(0, o_ref.shape[1], chunk):
        o_ref[:, c0:c0 + chunk] = _dot(h_ref[...], w_ref[:, c0:c0 + chunk]).astype(o_ref.dtype)


def _in_proj(xs, norm_g, mod, layer, w, w_layer, split):
    n = w.shape[2]
    tn = n // split
    x_specs = [pl.BlockSpec((TM, D_MODEL), lambda j, i: (i, 0))] if len(xs) == 1 else _group_specs(TM, D_MODEL)
    return pl.pallas_call(
        functools.partial(_in_proj_kernel, chunk=min(IN_CHUNK, tn)),
        grid=(split, N_TOK // TM),
        in_specs=x_specs + [
            pl.BlockSpec((1, 1, D_MODEL), lambda j, i: (layer, 0, 0)),
            _mod_spec(layer, 0, TM, 1),
            _mod_spec(layer, 1, TM, 1),
            pl.BlockSpec((None, D_MODEL, tn), lambda j, i: (w_layer, 0, j)),
        ],
        out_specs=pl.BlockSpec((TM, tn), lambda j, i: (i, j)),
        out_shape=jax.ShapeDtypeStruct((N_TOK, n), BF16),
        scratch_shapes=[pltpu.VMEM((TM, D_MODEL), BF16)],
        compiler_params=_params("arbitrary", "arbitrary"),
        name="in_proj",
    )(*xs, norm_g, mod, mod, w)


def _pool_bands():
    t = POOL_TM
    b0 = np.zeros((POOL_GROUPS, t, t), np.float32)
    bp = np.zeros((POOL_GROUPS, t, POOL_HALO), np.float32)
    bn = np.zeros((POOL_GROUPS, t, POOL_HALO), np.float32)
    for g, win in enumerate(POOL_WINDOWS):
        for r in range(t):
            for s in range(r - win // 2, r + win - win // 2):
                if s < 0:
                    bp[g, r, s + POOL_HALO] = 1.0
                elif s >= t:
                    bn[g, r, s - t] = 1.0
                else:
                    b0[g, r, s] = 1.0
    return jnp.asarray(b0, BF16), jnp.asarray(bp, BF16), jnp.asarray(bn, BF16)


def _pool_tiles(tiles, b0_ref, bp_ref, bn_ref, pw_ref, sc_ref):
    tiles_per_seq = DEC_SEQ // POOL_TM
    t = lax.broadcasted_iota(jnp.int32, (POOL_TM, POOL_GC), 0)
    sums, cnts, us = [], [], []
    for tile, u, up, un in tiles:
        in_prompt = tile < N_PROMPT // POOL_TM
        pos = (tile - N_PROMPT // POOL_TM) % tiles_per_seq
        is_start = jnp.logical_or(in_prompt, pos == 0)
        is_end = jnp.logical_or(in_prompt, pos == tiles_per_seq - 1)
        up = jnp.where(is_start, jnp.zeros_like(up), up)
        un = jnp.where(is_end, jnp.zeros_like(un), un)
        for g, win in enumerate(POOL_WINDOWS):
            cols = slice(g * POOL_GC, (g + 1) * POOL_GC)
            us.append(u[:, cols])
            sums.append(_dot(b0_ref[g], u[:, cols]) + _dot(bp_ref[g], up[:, cols]) + _dot(bn_ref[g], un[:, cols]))
            cut_lo = jnp.where(is_start, jnp.maximum(win // 2 - t, 0), 0)
            cut_hi = jnp.where(is_end, jnp.maximum(t + (win - win // 2) - POOL_TM, 0), 0)
            cnts.append((win - cut_lo - cut_hi).astype(F32))
    pooled = [(s / cnt - u.astype(F32)).astype(BF16) for s, cnt, u in zip(sums, cnts, us)]
    ys = [_dot(p, pw_ref[n % POOL_GROUPS]) for n, p in enumerate(pooled)]
    out = []
    for j in range(len(tiles)):
        out.append(jnp.concatenate(
            [(ys[j * POOL_GROUPS + g] * sc_ref[:, g * POOL_GC:(g + 1) * POOL_GC]).astype(BF16)
             for g in range(POOL_GROUPS)], axis=1))
    return out


def _rope_tables(t):
    nf = DK_RET // 4
    rows = t // GRID_W
    r = jnp.repeat(jnp.arange(rows), GRID_W).astype(F32)
    col = jnp.tile(jnp.arange(GRID_W), rows).astype(F32)
    inv = ROPE_BASE ** (-jnp.arange(nf, dtype=F32) / nf)
    ar, ac = r[:, None] * inv, col[:, None] * inv
    cos = jnp.concatenate([jnp.cos(ar), jnp.cos(ar), jnp.cos(ac), jnp.cos(ac)], axis=1)
    sin = jnp.concatenate([-jnp.sin(ar), jnp.sin(ar), -jnp.sin(ac), jnp.sin(ac)], axis=1)
    return cos, sin


def _ret_kernel(*refs, n_chunks, heads, use_rope, has_s0, emit_state, aliased):
    refs = list(refs)
    dec_ref, q_ref, k_ref, v_ref = refs[:4]
    refs = refs[4:]
    if use_rope:
        cos_ref, sin_ref = refs[:2]
        refs = refs[2:]
    if has_s0:
        s0_ref = refs[0]
        refs = refs[1:]
    if aliased:
        refs = refs[1:]
    o_ref = refs[0]
    refs = refs[1:]
    if emit_state:
        st_ref = refs[0]
        refs = refs[1:]
    (qk_ref,) = refs

    c = RET_CHUNK
    nf = DK_RET // 4
    head0 = pl.program_id(1) * heads
    inter = has_s0 or n_chunks > 1

    row = lax.broadcasted_iota(jnp.int32, (c, DK_RET), 0).astype(F32)
    ii = lax.broadcasted_iota(jnp.int32, (c, c), 0)
    jj = lax.broadcasted_iota(jnp.int32, (c, c), 1)
    dist = (ii - jj).astype(F32)
    lane = lax.broadcasted_iota(jnp.int32, (c, DK_RET), 1)
    first_half = (lane % (2 * nf)) < nf

    def rope(x, r0):
        if not use_rope:
            return x
        partner = jnp.where(first_half, pltpu.roll(x, DK_RET - nf, axis=1), pltpu.roll(x, nf, axis=1))
        return x * cos_ref[pl.ds(r0, c), :] + partner * sin_ref[pl.ds(r0, c), :]

    for hh in range(heads):
        cols = slice(hh * DK_RET, (hh + 1) * DK_RET)

        def lam(d, shape):
            return _log_sigmoid(jnp.full(shape, dec_ref[d, head0 + hh], F32))

        lam_f, lam_b = lam(0, (c, DK_RET)), lam(1, (c, DK_RET))
        dq_f = jnp.exp((row + 1.0) * lam_f)
        dk_f = jnp.exp((c - 1.0 - row) * lam_f)
        dq_b = jnp.exp((c - row) * lam_b)
        dk_b = jnp.exp(row * lam_b)
        cdec_f = jnp.exp(float(c) * lam(0, (DK_RET, DV_RET)))
        cdec_b = jnp.exp(float(c) * lam(1, (DK_RET, DV_RET)))
        dmat = jnp.where(ii > jj, jnp.exp(dist * lam(0, (c, c))),
                         jnp.where(ii < jj, jnp.exp(-dist * lam(1, (c, c))), 2.0))

        group = 2 if n_chunks % 2 == 0 else 1

        def prepare(gi, carry):
            ns = [gi * group + j for j in range(group)]
            r0s = [n * c if isinstance(n, int) else pl.multiple_of(n * c, c) for n in ns]
            q = [rope(q_ref[pl.ds(r0, c), cols].astype(F32), r0) for r0 in r0s]
            k = [rope(k_ref[pl.ds(r0, c), cols].astype(F32) * (DK_RET ** -0.5), r0) for r0 in r0s]
            a = [(_dot_nt(q[j].astype(BF16), k[j].astype(BF16)) * dmat).astype(BF16) for j in range(group)]
            for j, r0 in enumerate(r0s):
                qk_ref[0, pl.ds(r0, c), cols] = (q[j] * dq_f).astype(BF16)
                qk_ref[1, pl.ds(r0, c), cols] = (q[j] * dq_b).astype(BF16)
                qk_ref[2, pl.ds(r0, c), cols] = (k[j] * dk_f).astype(BF16)
                qk_ref[3, pl.ds(r0, c), cols] = (k[j] * dk_b).astype(BF16)
            o = [_dot(a[j], v_ref[pl.ds(r0s[j], c), cols]) for j in range(group)]
            for j, r0 in enumerate(r0s):
                o_ref[pl.ds(r0, c), cols] = o[j]
            return carry

        def advance(n, states):
            ms = (n, n_chunks - 1 - n)
            r0s = [m * c if isinstance(m, int) else pl.multiple_of(m * c, c) for m in ms]
            upd = [_dot_tn(qk_ref[2 + d, pl.ds(r0s[d], c), cols], v_ref[pl.ds(r0s[d], c), cols]) for d in (0, 1)]
            out = []
            for d, cdec in ((0, cdec_f), (1, cdec_b)):
                if inter:
                    o_ref[pl.ds(r0s[d], c), cols] += _dot(qk_ref[d, pl.ds(r0s[d], c), cols],
                                                          states[d].astype(BF16))
                out.append(cdec * states[d] + upd[d])
            return tuple(out)

        _loop(n_chunks // group, prepare, 0)
        zero = jnp.zeros((DK_RET, DV_RET), F32)
        init = (s0_ref[0, 0, 0, hh], s0_ref[0, 0, 1, hh]) if has_s0 else (zero, zero)
        s_f, s_b = _loop(n_chunks, advance, init)
        if emit_state:
            st_ref[0, 0, hh] = s_f
            st_ref[0, 1, hh] = s_b


def _ret_scan(proj, decay, t, batch, heads, row_block0, rope, s0, s0_layer, state_buf, state_layer):
    w = heads * DK_RET
    col0 = POOL_WIDTH // w
    per = RET_WIDTH // w
    in_specs = [
        pl.BlockSpec(memory_space=pltpu.SMEM),
        pl.BlockSpec((t, w), lambda b, h: (row_block0 + b, col0 + h)),
        pl.BlockSpec((t, w), lambda b, h: (row_block0 + b, col0 + per + h)),
        pl.BlockSpec((t, w), lambda b, h: (row_block0 + b, col0 + 2 * per + h)),
    ]
    args = [decay, proj, proj, proj]
    if rope is not None:
        assert heads == 1
        in_specs += [pl.BlockSpec((t, DK_RET), lambda b, h: (0, 0))] * 2
        args += list(rope)
    if s0 is not None:
        in_specs.append(pl.BlockSpec((1, 1, 2, heads, DK_RET, DV_RET), lambda b, h: (b, s0_layer, 0, h, 0, 0)))
        args.append(s0)
    out_specs = [pl.BlockSpec((t, w), lambda b, h: (b, h))]
    out_shape = [jax.ShapeDtypeStruct((batch * t, RET_WIDTH), F32)]
    aliases = {}
    if state_layer is not None:
        out_specs.append(pl.BlockSpec((1, None, 2, heads, DK_RET, DV_RET), lambda b, h: (b, state_layer, 0, h, 0, 0)))
        out_shape.append(jax.ShapeDtypeStruct((batch, N_EVEN, 2, H_RET, DK_RET, DV_RET), F32))
        if state_buf is not None:
            aliases = {len(args): 1}
            in_specs.append(pl.BlockSpec(memory_space=pl.ANY))
            args.append(state_buf)
    return pl.pallas_call(
        functools.partial(_ret_kernel, n_chunks=t // RET_CHUNK, heads=heads, use_rope=rope is not None,
                          has_s0=s0 is not None, emit_state=state_layer is not None, aliased=bool(aliases)),
        grid=(batch, H_RET // heads),
        in_specs=in_specs,
        out_specs=out_specs,
        out_shape=out_shape,
        input_output_aliases=aliases,
        scratch_shapes=[pltpu.VMEM((4, t, w), BF16)],
        compiler_params=_params("parallel", "parallel"),
        name="ret_scan",
    )(*args)


def _gla_kernel(*refs, n_blocks, seqs, dvb, has_s0, emit_state, aliased):
    refs = list(refs)
    q_ref, k_ref, v_ref, z_ref, g2_ref, gb_ref = refs[:6]
    refs = refs[6:]
    if has_s0:
        s0_ref = refs[0]
        refs = refs[1:]
    if aliased:
        refs = refs[1:]
    o_ref = refs[0]
    refs = refs[1:]
    if emit_state:
        st_ref = refs[0]
        refs = refs[1:]
    qd_ref, kt_ref, a_ref, dec_ref, s_ref = refs

    c = CHUNK
    blk = GLA_BLK
    cpb = blk // c
    tot_blocks = seqs * n_blocks

    @pl.when(pl.program_id(2) == 0)
    def _():
        ii = lax.broadcasted_iota(jnp.int32, (blk, blk), 0)
        jj = lax.broadcasted_iota(jnp.int32, (blk, blk), 1)
        same = (ii // c) == (jj // c)
        masks = (jnp.logical_and(same, ii >= jj), jnp.logical_and(same, ii <= jj))
        tris = tuple(jnp.tile(jnp.where(m, 1.0, 0.0).astype(BF16), (1, 2)) for m in masks)

        group = 2 if tot_blocks % 2 == 0 else 1
        chains = [(j, d) for j in range(group) for d in (0, 1)]

        def build(gi, carry):
            bis = [gi * group + j for j in range(group)]
            r0s = [bi * blk if isinstance(bi, int) else pl.multiple_of(bi * blk, blk) for bi in bis]
            q = [q_ref[pl.ds(r0, blk), :].astype(F32) * (DK_GLA ** -0.5) for r0 in r0s]
            k = [k_ref[pl.ds(r0, blk), :].astype(F32) for r0 in r0s]
            z = [_dot(z_ref[pl.ds(r0s[j], blk), :], g2_ref[0, d]) + gb_ref[0, d] for j, d in chains]
            lg = [jnp.concatenate(_split_bf16(_log_sigmoid(zz) * (LOG2E / GLA_TAU)), axis=0) for zz in z]
            g = [_dot(tris[d], lg[n]) for n, (j, d) in enumerate(chains)]
            qds, kds, cross = [], [], []
            for n, (j, d) in enumerate(chains):
                edge = c - 1 if d == 0 else 0
                gt = [g[n][ci * c + edge:ci * c + edge + 1, :] for ci in range(cpb)]
                g_tot = jnp.concatenate([jnp.broadcast_to(gt[ci], (c, DK_GLA)) for ci in range(cpb)], axis=0)
                qd_f32 = q[j] * jnp.exp2(g[n])
                qd = qd_f32.astype(BF16)
                kds.append((k[j] * jnp.exp2(-g[n])).astype(BF16))
                qds.append(qd)
                kt = k[j] * jnp.exp2(g_tot - g[n])
                scan = list(range(cpb)) if d == 0 else list(reversed(range(cpb)))
                pos = {ci: p for p, ci in enumerate(scan)}

                def span(lo, hi):
                    parts = [gt[scan[p]] for p in range(lo, hi)]
                    return functools.reduce(lambda x, y: x + y, parts) if parts else None

                def scaled(x, e):
                    return x if e is None else x * jnp.exp2(jnp.broadcast_to(e, x.shape))

                rows = lambda x, ci: x[ci * c:(ci + 1) * c, :]
                qd_ref[d, pl.ds(r0s[j], blk), :] = jnp.concatenate(
                    [scaled(rows(qd_f32, ci), span(0, pos[ci])) for ci in range(cpb)], axis=0).astype(BF16)
                kt_ref[d, pl.ds(r0s[j], blk), :] = jnp.concatenate(
                    [scaled(rows(kt, ci), span(pos[ci] + 1, cpb)) for ci in range(cpb)], axis=0).astype(BF16)
                dec_ref[d * tot_blocks + bis[j]] = jnp.exp2(jnp.broadcast_to(span(0, cpb), (8, DK_GLA)))
                zero_rows = jnp.zeros((c, DK_GLA), BF16)
                for p in range(1, cpb):
                    src = jnp.concatenate(
                        [scaled(rows(kt, cj), span(pos[cj] + 1, p)).astype(BF16) if pos[cj] < p else zero_rows
                         for cj in range(cpb)], axis=0)
                    cross.append((n, scan[p], qd[scan[p] * c:(scan[p] + 1) * c, :], src))
            a = [_dot_nt(qds[n], kds[n]) for n in range(len(chains))]
            cross = [(n, ci, _dot_nt(qrows, src)) for n, ci, qrows, src in cross]
            for j in range(group):
                total = jnp.where(masks[0], a[2 * j], 0.0) + jnp.where(masks[1], a[2 * j + 1], 0.0)
                for d in (0, 1):
                    by_chunk = {ci: sc for n, ci, sc in cross if n == 2 * j + d}
                    total = total + jnp.concatenate(
                        [by_chunk.get(ci, jnp.zeros((c, blk), F32)) for ci in range(cpb)], axis=0)
                a_ref[bis[j]] = total.astype(BF16)
            return carry

        _loop(tot_blocks // group, build, 0)

    def intra(bi, carry):
        r0 = bi * blk if isinstance(bi, int) else pl.multiple_of(bi * blk, blk)
        o_ref[pl.ds(r0, blk), :] = _dot(a_ref[bi], v_ref[pl.ds(r0, blk), :])
        return carry

    _loop(tot_blocks, intra, 0)

    runs = [(sq, d) for sq in range(seqs) for d in (0, 1)]
    for sq, d in runs:
        s_ref[2 * sq + d] = s0_ref[sq, 0, d, 0].T if has_s0 else jnp.zeros((dvb, DK_GLA), F32)

    inter = has_s0 or n_blocks > 1

    def advance(n, carry):
        ms = [sq * n_blocks + (n if d == 0 else n_blocks - 1 - n) for sq, d in runs]
        r0s = [m * blk if isinstance(m, int) else pl.multiple_of(m * blk, blk) for m in ms]
        upd = [_dot_tn(v_ref[pl.ds(r0s[n_], blk), :], kt_ref[d, pl.ds(r0s[n_], blk), :])
               for n_, (sq, d) in enumerate(runs)]
        for n_, (sq, d) in enumerate(runs):
            s = s_ref[2 * sq + d]
            if inter:
                o_ref[pl.ds(r0s[n_], blk), :] += _dot_nt(qd_ref[d, pl.ds(r0s[n_], blk), :], s.astype(BF16))
            s_ref[2 * sq + d] = s * dec_ref[d * tot_blocks + ms[n_]][0:1, :] + upd[n_]
        return carry

    _loop(n_blocks, advance, 0)
    if emit_state:
        for sq, d in runs:
            st_ref[sq, d, 0] = s_ref[2 * sq + d].T


def _gla_scan(proj, z1, g2, gb, t, batch, seqs, dvb, row_block0, s0, s0_layer, state_buf, state_layer):
    nd = DV_GLA // dvb
    kcol = GLA_DK_TOTAL // DK_GLA
    vcol = 2 * GLA_DK_TOTAL // dvb
    n_blocks = t // GLA_BLK
    rows = seqs * t
    in_specs = [
        pl.BlockSpec((rows, DK_GLA), lambda b, h, d: (row_block0 + b, h)),
        pl.BlockSpec((rows, DK_GLA), lambda b, h, d: (row_block0 + b, kcol + h)),
        pl.BlockSpec((rows, dvb), lambda b, h, d: (row_block0 + b, vcol + h * nd + d)),
        pl.BlockSpec((rows, 128), lambda b, h, d: (row_block0 + b, 0)),
        pl.BlockSpec((1, 2, 128, DK_GLA), lambda b, h, d: (h, 0, 0, 0)),
        pl.BlockSpec((1, 2, 1, DK_GLA), lambda b, h, d: (h, 0, 0, 0)),
    ]
    args = [proj, proj, proj, z1, g2, gb]
    if s0 is not None:
        assert seqs == 1
        in_specs.append(pl.BlockSpec((1, 1, 2, 1, DK_GLA, dvb), lambda b, h, d: (b, s0_layer, 0, h, 0, d)))
        args.append(s0)
    out_specs = [pl.BlockSpec((rows, dvb), lambda b, h, d: (b, h * nd + d))]
    out_shape = [jax.ShapeDtypeStruct((batch * t, GLA_DV_TOTAL), F32)]
    aliases = {}
    if state_layer is not None:
        out_specs.append(pl.BlockSpec((seqs, None, 2, 1, DK_GLA, dvb), lambda b, h, d: (b, state_layer, 0, h, 0, d)))
        out_shape.append(jax.ShapeDtypeStruct((batch, N_ODD, 2, H_GLA, DK_GLA, DV_GLA), F32))
        if state_buf is not None:
            aliases = {len(args): 1}
            in_specs.append(pl.BlockSpec(memory_space=pl.ANY))
            args.append(state_buf)
    return pl.pallas_call(
        functools.partial(_gla_kernel, n_blocks=n_blocks, seqs=seqs, dvb=dvb, has_s0=s0 is not None,
                          emit_state=state_layer is not None, aliased=bool(aliases)),
        grid=(batch // seqs, H_GLA, nd),
        in_specs=in_specs,
        out_specs=out_specs,
        out_shape=out_shape,
        input_output_aliases=aliases,
        scratch_shapes=[
            pltpu.VMEM((2, rows, DK_GLA), BF16),
            pltpu.VMEM((2, rows, DK_GLA), BF16),
            pltpu.VMEM((rows // GLA_BLK, GLA_BLK, GLA_BLK), BF16),
            pltpu.VMEM((2 * rows // GLA_BLK, 8, DK_GLA), F32),
            pltpu.VMEM((2 * seqs, dvb, DK_GLA), F32),
        ],
        compiler_params=_params("parallel", "parallel", "arbitrary"),
        name="gla_scan",
    )(*args)


def _group_specs(tm, width, col_block=0):
    p_tiles = N_PROMPT // tm
    return [pl.BlockSpec((tm, width), lambda *idx: (jnp.minimum(idx[-1], p_tiles - 1), col_block)),
            pl.BlockSpec((tm, width), lambda *idx: (jnp.maximum(idx[-1] - p_tiles, 0), col_block))]


def _out_proj_kernel(*refs, n_heads, width, with_pool):
    refs = list(refs)
    if with_pool:
        u_ref, up_ref, un_ref, b0_ref, bp_ref, bn_ref, pw_ref, psc_ref = refs[:8]
        refs = refs[8:]
    op_ref, os_ref, gate_ref, gn_ref, w_ref = refs[:5]
    *x_refs, g1_ref, out_ref = refs[5:]
    xp_ref, xs_ref = x_refs if len(x_refs) == 2 else (x_refs[0], x_refs[0])
    i = pl.program_id(0)
    in_prompt = i < N_PROMPT // TM

    pooled = []
    if with_pool:
        sub = TM // POOL_TM
        tiles = []
        for s in range(sub):
            r0 = s * POOL_TM
            up = up_ref[...] if s == 0 else u_ref[r0 - POOL_HALO:r0, :]
            un = un_ref[...] if s == sub - 1 else u_ref[r0 + POOL_TM:r0 + POOL_TM + POOL_HALO, :]
            tiles.append((i * sub + s, u_ref[r0:r0 + POOL_TM, :], up, un))
        pooled = [jnp.concatenate(_pool_tiles(tiles, b0_ref, bp_ref, bn_ref, pw_ref, psc_ref), axis=0)]

    def run(o_ref, x_ref):
        parts = list(pooled)
        for hd in range(n_heads):
            cols = slice(hd * width, (hd + 1) * width)
            gt = gate_ref[:, cols].astype(F32)
            parts.append((_rms(o_ref[:, cols]) * gn_ref[:, cols] * _silu(gt)).astype(BF16))
        a = jnp.concatenate(parts, axis=1)
        out_ref[...] = x_ref[...] + g1_ref[0] * _dot(a, w_ref[...])

    @pl.when(in_prompt)
    def _():
        run(op_ref, xp_ref)

    @pl.when(jnp.logical_not(in_prompt))
    def _():
        run(os_ref, xs_ref)


def _out_proj(xs, mod, layer, w, w_layer, proj, gate_col_block, norm_g, o_prompt, o_sample, n_heads, pool=None):
    width = o_prompt.shape[1]
    x_specs = [pl.BlockSpec((TM, D_MODEL), lambda i: (i, 0))] if len(xs) == 1 else _group_specs(TM, D_MODEL)
    in_specs = _group_specs(TM, width) + [
        pl.BlockSpec((TM, width), lambda i: (i, gate_col_block)),
        pl.BlockSpec((1, width), lambda i: (0, 0)),
        pl.BlockSpec((None, D_MODEL, D_MODEL), lambda i: (w_layer, 0, 0)),
    ] + x_specs + [_mod_spec(layer, 2, TM)]
    args = [o_prompt, o_sample, proj, norm_g.reshape(1, width), w, *xs, mod]
    if pool is not None:
        b0, bp, bn = _pool_bands()
        halo_blocks = TM // POOL_HALO
        last = N_TOK // POOL_HALO - 1
        full = lambda shape: pl.BlockSpec(shape, lambda i: (0,) * len(shape))
        in_specs = [
            pl.BlockSpec((TM, POOL_WIDTH), lambda i: (i, 0)),
            pl.BlockSpec((POOL_HALO, POOL_WIDTH), lambda i: (jnp.maximum(i * halo_blocks - 1, 0), 0)),
            pl.BlockSpec((POOL_HALO, POOL_WIDTH), lambda i: (jnp.minimum((i + 1) * halo_blocks, last), 0)),
            full(b0.shape), full(bp.shape), full(bn.shape),
            pl.BlockSpec((None,) + pool[0].shape[1:], lambda i: (w_layer, 0, 0, 0)),
            pl.BlockSpec((None, 1, POOL_WIDTH), lambda i: (w_layer, 0, 0)),
        ] + in_specs
        args = [proj, proj, proj, b0, bp, bn, pool[0], pool[1].reshape(N_EVEN, 1, POOL_WIDTH)] + args
    return pl.pallas_call(
        functools.partial(_out_proj_kernel, n_heads=n_heads, width=width // n_heads, with_pool=pool is not None),
        grid=(N_TOK // TM,),
        in_specs=in_specs,
        out_specs=pl.BlockSpec((TM, D_MODEL), lambda i: (i, 0)),
        out_shape=jax.ShapeDtypeStruct((N_TOK, D_MODEL), F32),
        compiler_params=_params("parallel"),
        name="out_proj",
    )(*args)


def _ffn_kernel(x_ref, xp_ref, xn_ref, gn_ref, sh_ref, sc_ref, gt_ref, wa_ref, wb_ref, cw_ref, cb_ref,
                wd_ref, fg_ref, *rest, final):
    if final:
        op_ref, o_ref, h_ref, hh_ref = rest
    else:
        o_ref, h_ref, hh_ref = rest
    i = pl.program_id(0)
    f = pl.program_id(1)
    tm = TM_FFN

    def modnorm(x):
        return (_rms(x) * gn_ref[0] * (1.0 + sc_ref[0]) + sh_ref[0]).astype(BF16)

    @pl.when(f == 0)
    def _():
        h_ref[...] = modnorm(x_ref[...])
        hh_ref[...] = modnorm(jnp.concatenate([xp_ref[...], xn_ref[...]], axis=0))
        o_ref[...] = jnp.zeros_like(o_ref)

    a = _dot(h_ref[...], wa_ref[...])
    b = _dot(h_ref[...], wb_ref[...])
    a_halo = _dot(hh_ref[...], wa_ref[...])

    seq = jnp.where(i * tm < N_PROMPT, SEQ, DEC_SEQ)
    t = lax.broadcasted_iota(jnp.int32, (tm, TF), 0)
    pos = (i * tm + t) & (seq - 1)
    a_prev = jnp.where(t == 0, a_halo[7:8, :], pltpu.roll(a, 1, axis=0))
    a_prev = jnp.where(pos == 0, 0.0, a_prev)
    a_next = jnp.where(t == tm - 1, a_halo[8:9, :], pltpu.roll(a, tm - 1, axis=0))
    a_next = jnp.where(pos == seq - 1, 0.0, a_next)
    conv = a_prev * cw_ref[0:1, :] + a * cw_ref[1:2, :] + a_next * cw_ref[2:3, :] + cb_ref[...]
    o_ref[...] += _dot((_silu(conv) * b).astype(BF16), wd_ref[...])

    @pl.when(f == pl.num_programs(1) - 1)
    def _():
        y = x_ref[...] + gt_ref[0] * o_ref[...]
        if not final:
            o_ref[...] = y
        else:
            y = _rms(y) * fg_ref[...]
            in_prompt = i < N_PROMPT // tm

            @pl.when(in_prompt)
            def _():
                op_ref[...] = y

            @pl.when(jnp.logical_not(in_prompt))
            def _():
                o_ref[...] = y


def _ffn(x, mod, layer, norm_g, w_up, conv_w, conv_b, w_down, final_g, final):
    m = x.shape[0]
    nf = D_FF // TF
    tm = TM_FFN
    halo = 8
    last = m // halo - 1
    if final:
        p_tiles = N_PROMPT // tm
        out_specs = [pl.BlockSpec((tm, D_MODEL), lambda i, f: (jnp.minimum(i, p_tiles - 1), 0),
                                  pipeline_mode=pl.Buffered(1)),
                     pl.BlockSpec((tm, D_MODEL), lambda i, f: (jnp.maximum(i - p_tiles, 0), 0),
                                  pipeline_mode=pl.Buffered(1))]
        out_shape = [jax.ShapeDtypeStruct((N_PROMPT, D_MODEL), F32),
                     jax.ShapeDtypeStruct((N_SAMPLE, D_MODEL), F32)]
    else:
        out_specs = pl.BlockSpec((tm, D_MODEL), lambda i, f: (i, 0))
        out_shape = jax.ShapeDtypeStruct((m, D_MODEL), F32)
    return pl.pallas_call(
        functools.partial(_ffn_kernel, final=final),
        grid=(m // tm, nf),
        in_specs=[
            pl.BlockSpec((tm, D_MODEL), lambda i, f: (i, 0), pipeline_mode=pl.Buffered(1)),
            pl.BlockSpec((halo, D_MODEL), lambda i, f: (jnp.maximum(i * (tm // halo) - 1, 0), 0)),
            pl.BlockSpec((halo, D_MODEL), lambda i, f: (jnp.minimum((i + 1) * (tm // halo), last), 0)),
            pl.BlockSpec((1, 1, D_MODEL), lambda i, f: (layer, 0, 0)),
            _mod_spec(layer, 3, tm),
            _mod_spec(layer, 4, tm),
            _mod_spec(layer, 5, tm),
            pl.BlockSpec((None, D_MODEL, TF), lambda i, f: (layer, 0, f)),
            pl.BlockSpec((None, D_MODEL, TF), lambda i, f: (layer, 0, nf + f)),
            pl.BlockSpec((None, 3, TF), lambda i, f: (layer, 0, f)),
            pl.BlockSpec((None, 1, TF), lambda i, f: (layer, 0, f)),
            pl.BlockSpec((None, TF, D_MODEL), lambda i, f: (layer, f, 0)),
            pl.BlockSpec((1, D_MODEL), lambda i, f: (0, 0)),
        ],
        out_specs=out_specs,
        out_shape=out_shape,
        scratch_shapes=[
            pltpu.VMEM((tm, D_MODEL), BF16),
            pltpu.VMEM((2 * halo, D_MODEL), BF16),
        ],
        compiler_params=_params("arbitrary", "arbitrary"),
        name="ffn",
    )(x, x, x, norm_g, mod, mod, mod, w_up, w_up, conv_w, conv_b.reshape(DEPTH, 1, D_FF), w_down,
      final_g.reshape(1, D_MODEL))


def _gla_gate_weights(gw2, gb):
    r = GLA_GATE_RANK
    g2 = jnp.zeros((2, 128, GLA_DK_TOTAL), F32)
    g2 = g2.at[0, 0:r].set(gw2[0]).at[1, r:2 * r].set(gw2[1])
    g2 = g2.reshape(2, 128, H_GLA, DK_GLA).transpose(2, 0, 1, 3).astype(BF16)
    gbh = gb.reshape(2, H_GLA, 1, DK_GLA).transpose(1, 0, 2, 3)
    return g2, gbh


def kernel(x_prompt, x_sample, state_ret, state_gla, c, c_ctx, ada_w, ada_b, norm1_g, norm2_g,
           even_w_in, pool_w, pool_scale, ret_decay, ret_norm_g, even_w_out, odd_w_in, gla_gw1,
           gla_gw2, gla_gb, gla_norm_g, odd_w_out, ffn_w_up, ffn_conv_w, ffn_conv_b, ffn_w_down, final_g):
    xs = (x_prompt.reshape(N_PROMPT, D_MODEL), x_sample.reshape(N_SAMPLE, D_MODEL))
    cvec = jnp.concatenate([c_ctx[None, :], c, jnp.zeros((COND_PAD - N_COND, D_MODEL), F32)], axis=0)
    mod = _ada_mod(cvec, ada_w, ada_b)
    n1 = norm1_g.reshape(DEPTH, 1, D_MODEL)
    n2 = norm2_g.reshape(DEPTH, 1, D_MODEL)
    rope = _rope_tables(DEC_SEQ)
    p_blocks = N_PROMPT // DEC_SEQ

    w_even_in, w_even_out = even_w_in.astype(BF16), even_w_out.astype(BF16)
    w_odd_in, w_odd_out = odd_w_in.astype(BF16), odd_w_out.astype(BF16)
    w_up, w_down, w_pool = ffn_w_up.astype(BF16), ffn_w_down.astype(BF16), pool_w.astype(BF16)
    w_gate1 = jnp.concatenate([gla_gw1[:, 0], gla_gw1[:, 1],
                               jnp.zeros((N_ODD, D_MODEL, 128 - 2 * GLA_GATE_RANK), F32)], axis=2).astype(BF16)

    ret_states = gla_states = None
    for l in range(DEPTH):
        if l % 2 == 0:
            i = l // 2
            proj = _in_proj(xs, n1, mod, l, w_even_in, i, IN_SPLIT)
            o_p, ret_states = _ret_scan(proj, ret_decay[i], SEQ, BATCH, H_RET, 0, None, None, 0, ret_states, i)
            (o_s,) = _ret_scan(proj, ret_decay[i], DEC_SEQ, DEC_BATCH, 1, p_blocks, rope, state_ret, i, None, None)
            x = _out_proj(xs, mod, l, w_even_out, i, proj, EVEN_IN // RET_WIDTH - 1,
                          ret_norm_g[i], o_p, o_s, H_RET, (w_pool, pool_scale))
        else:
            j = l // 2
            proj = _in_proj(xs, n1, mod, l, w_odd_in, j, IN_SPLIT)
            z1 = _in_proj(xs, n1, mod, l, w_gate1, j, 1)
            g2, gbh = _gla_gate_weights(gla_gw2[j], gla_gb[j])
            o_p, gla_states = _gla_scan(proj, z1, g2, gbh, SEQ, BATCH, GLA_PROMPT_SEQS, DV_GLA, 0, None, 0,
                                        gla_states, j)
            (o_s,) = _gla_scan(proj, z1, g2, gbh, DEC_SEQ, DEC_BATCH, 1, DV_GLA, p_blocks, state_gla, j,
                               None, None)
            x = _out_proj(xs, mod, l, w_odd_out, j, proj, ODD_IN // GLA_DV_TOTAL - 1,
                          gla_norm_g[j], o_p, o_s, H_GLA)
        xs = _ffn(x, mod, l, n2, w_up, ffn_conv_w, ffn_conv_b, w_down, final_g, l == DEPTH - 1)
        xs = tuple(xs) if l == DEPTH - 1 else (xs,)

    y_prompt = xs[0].reshape(BATCH, SEQ, D_MODEL)
    y_sample = xs[1].reshape(DEC_BATCH, DEC_SEQ, D_MODEL)
    return (y_prompt, y_sample, ret_states, gla_states)
```

```python
import functools

import numpy as np
import jax
import jax.numpy as jnp
from jax import lax
from jax.experimental import pallas as pl
from jax.experimental.pallas import tpu as pltpu

F32 = jnp.float32
BF16 = jnp.bfloat16

D_MODEL = 2048
BATCH = 16
SEQ = 256
DEPTH = 4
DEC_BATCH = 2
DEC_SEQ = 4096
GRID_W = 64
N_EVEN = (DEPTH + 1) // 2
N_ODD = DEPTH // 2
POOL_WIDTH = D_MODEL // 2
POOL_GROUPS = 4
POOL_GC = POOL_WIDTH // POOL_GROUPS
POOL_WINDOWS = (2, 4, 8, 16)
RET_WIDTH = D_MODEL // 2
H_RET = 8
DK_RET = RET_WIDTH // H_RET
DV_RET = RET_WIDTH // H_RET
ROPE_BASE = 10000.0
H_GLA = 4
GLA_DK_TOTAL = D_MODEL // 2
GLA_DV_TOTAL = D_MODEL
DK_GLA = GLA_DK_TOTAL // H_GLA
DV_GLA = GLA_DV_TOTAL // H_GLA
GLA_GATE_RANK = 16
GLA_TAU = 16.0
D_FF = 5632
CHUNK = 64
EPS = 1e-6
EVEN_IN = POOL_WIDTH + 4 * RET_WIDTH
ODD_IN = 2 * GLA_DK_TOTAL + 2 * GLA_DV_TOTAL

N_PROMPT = BATCH * SEQ
N_SAMPLE = DEC_BATCH * DEC_SEQ
N_TOK = N_PROMPT + N_SAMPLE
N_COND = 1 + DEC_BATCH
COND_PAD = 8

VMEM_LIMIT = 56 * 1024 * 1024

TM = 512
IN_SPLIT = 2
IN_CHUNK = 512
TM_FFN = 1024
TF = 512
POOL_TM = 256
POOL_HALO = 16
ADA_TN = 1024
RET_CHUNK = 256
GLA_BLK = 256
GLA_PROMPT_SEQS = 2


def _params(*sem):
    return pltpu.CompilerParams(dimension_semantics=sem, vmem_limit_bytes=VMEM_LIMIT)


def _cond_of_tile(i, tm):
    r0 = i * tm
    return jnp.where(r0 < N_PROMPT, 0, 1 + (r0 - N_PROMPT) // DEC_SEQ)


LOG2E = 1.4426950408889634


def _log_sigmoid(x):
    return jnp.minimum(x, 0.0) - jnp.log(1.0 + jnp.exp2(jnp.abs(x) * (-LOG2E)))


def _silu(x):
    return x * jax.nn.sigmoid(x)


def _rms(x):
    return x * lax.rsqrt(jnp.mean(x * x, axis=-1, keepdims=True) + EPS)


def _dot(a, b):
    return jnp.dot(a, b, preferred_element_type=F32)


def _dot_nt(a, b):
    return lax.dot_general(a, b, (((1,), (1,)), ((), ())), preferred_element_type=F32)


def _dot_tn(a, b):
    return lax.dot_general(a, b, (((0,), (0,)), ((), ())), preferred_element_type=F32)


def _split_bf16(x):
    hi = x.astype(BF16)
    lo = (x - hi.astype(F32)).astype(BF16)
    return hi, lo


def _loop(n, body, init, unroll=1):
    if n == 1:
        return body(0, init)
    return lax.fori_loop(0, n, body, init, unroll=unroll)


def _ada_kernel(c_ref, w_ref, b_ref, o_ref):
    s = _silu(c_ref[...]).astype(BF16)
    o_ref[0] = _dot(s, w_ref[0].astype(BF16)) + b_ref[0]


def _ada_mod(cvec, ada_w, ada_b):
    n = 6 * D_MODEL
    mod = pl.pallas_call(
        _ada_kernel,
        grid=(DEPTH, n // ADA_TN),
        in_specs=[
            pl.BlockSpec((COND_PAD, D_MODEL), lambda l, j: (0, 0)),
            pl.BlockSpec((1, D_MODEL, ADA_TN), lambda l, j: (l, 0, j)),
            pl.BlockSpec((1, 1, ADA_TN), lambda l, j: (l, 0, j)),
        ],
        out_specs=pl.BlockSpec((1, COND_PAD, ADA_TN), lambda l, j: (l, 0, j)),
        out_shape=jax.ShapeDtypeStruct((DEPTH, COND_PAD, n), F32),
        compiler_params=_params("parallel", "parallel"),
        name="ada_mod",
    )(cvec, ada_w, ada_b.reshape(DEPTH, 1, n))
    mod = mod[:, :N_COND].reshape(DEPTH, N_COND, 6, D_MODEL).transpose(0, 2, 1, 3)
    return mod.reshape(DEPTH * 6 * N_COND, 1, D_MODEL)


def _mod_spec(layer, part, tm, row_axis=0):
    base = (layer * 6 + part) * N_COND
    return pl.BlockSpec((1, 1, D_MODEL), lambda *idx: (base + _cond_of_tile(idx[row_axis], tm), 0, 0))


def _in_proj_kernel(*refs, chunk):
    *x_refs, g_ref, sh_ref, sc_ref, w_ref, o_ref, h_ref = refs

    if len(x_refs) == 1:
        x = x_refs[0][...]
    else:
        x = jnp.where(pl.program_id(1) < N_PROMPT // TM, x_refs[0][...], x_refs[1][...])
    h = _rms(x) * g_ref[0]
    h_ref[...] = (h * (1.0 + sc_ref[0]) + sh_ref[0]).astype(BF16)
    for c0 in range(0, o_ref.shape[1], chunk):
        o_ref[:, c0:c0 + chunk] = _dot(h_ref[...], w_ref[:, c0:c0 + chunk]).astype(o_ref.dtype)


def _in_proj(xs, norm_g, mod, layer, w, w_layer, split):
    n = w.shape[2]
    tn = n // split
    x_specs = [pl.BlockSpec((TM, D_MODEL), lambda j, i: (i, 0))] if len(xs) == 1 else _group_specs(TM, D_MODEL)
    return pl.pallas_call(
        functools.partial(_in_proj_kernel, chunk=min(IN_CHUNK, tn)),
        grid=(split, N_TOK // TM),
        in_specs=x_specs + [
            pl.BlockSpec((1, 1, D_MODEL), lambda j, i: (layer, 0, 0)),
            _mod_spec(layer, 0, TM, 1),
            _mod_spec(layer, 1, TM, 1),
            pl.BlockSpec((None, D_MODEL, tn), lambda j, i: (w_layer, 0, j)),
        ],
        out_specs=pl.BlockSpec((TM, tn), lambda j, i: (i, j)),
        out_shape=jax.ShapeDtypeStruct((N_TOK, n), BF16),
        scratch_shapes=[pltpu.VMEM((TM, D_MODEL), BF16)],
        compiler_params=_params("arbitrary", "arbitrary"),
        name="in_proj",
    )(*xs, norm_g, mod, mod, w)


def _pool_bands():
    t = POOL_TM
    b0 = np.zeros((POOL_GROUPS, t, t), np.float32)
    bp = np.zeros((POOL_GROUPS, t, POOL_HALO), np.float32)
    bn = np.zeros((POOL_GROUPS, t, POOL_HALO), np.float32)
    for g, win in enumerate(POOL_WINDOWS):
        for r in range(t):
            for s in range(r - win // 2, r + win - win // 2):
                if s < 0:
                    bp[g, r, s + POOL_HALO] = 1.0
                elif s >= t:
                    bn[g, r, s - t] = 1.0
                else:
                    b0[g, r, s] = 1.0
    return jnp.asarray(b0, BF16), jnp.asarray(bp, BF16), jnp.asarray(bn, BF16)


def _pool_tiles(tiles, b0_ref, bp_ref, bn_ref, pw_ref, sc_ref):
    tiles_per_seq = DEC_SEQ // POOL_TM
    t = lax.broadcasted_iota(jnp.int32, (POOL_TM, POOL_GC), 0)
    sums, cnts, us = [], [], []
    for tile, u, up, un in tiles:
        in_prompt = tile < N_PROMPT // POOL_TM
        pos = (tile - N_PROMPT // POOL_TM) % tiles_per_seq
        is_start = jnp.logical_or(in_prompt, pos == 0)
        is_end = jnp.logical_or(in_prompt, pos == tiles_per_seq - 1)
        up = jnp.where(is_start, jnp.zeros_like(up), up)
        un = jnp.where(is_end, jnp.zeros_like(un), un)
        for g, win in enumerate(POOL_WINDOWS):
            cols = slice(g * POOL_GC, (g + 1) * POOL_GC)
            us.append(u[:, cols])
            sums.append(_dot(b0_ref[g], u[:, cols]) + _dot(bp_ref[g], up[:, cols]) + _dot(bn_ref[g], un[:, cols]))
            cut_lo = jnp.where(is_start, jnp.maximum(win // 2 - t, 0), 0)
            cut_hi = jnp.where(is_end, jnp.maximum(t + (win - win // 2) - POOL_TM, 0), 0)
            cnts.append((win - cut_lo - cut_hi).astype(F32))
    pooled = [(s / cnt - u.astype(F32)).astype(BF16) for s, cnt, u in zip(sums, cnts, us)]
    ys = [_dot(p, pw_ref[n % POOL_GROUPS]) for n, p in enumerate(pooled)]
    out = []
    for j in range(len(tiles)):
        out.append(jnp.concatenate(
            [(ys[j * POOL_GROUPS + g] * sc_ref[:, g * POOL_GC:(g + 1) * POOL_GC]).astype(BF16)
             for g in range(POOL_GROUPS)], axis=1))
    return out


def _rope_tables(t):
    nf = DK_RET // 4
    rows = t // GRID_W
    r = jnp.repeat(jnp.arange(rows), GRID_W).astype(F32)
    col = jnp.tile(jnp.arange(GRID_W), rows).astype(F32)
    inv = ROPE_BASE ** (-jnp.arange(nf, dtype=F32) / nf)
    ar, ac = r[:, None] * inv, col[:, None] * inv
    cos = jnp.concatenate([jnp.cos(ar), jnp.cos(ar), jnp.cos(ac), jnp.cos(ac)], axis=1)
    sin = jnp.concatenate([-jnp.sin(ar), jnp.sin(ar), -jnp.sin(ac), jnp.sin(ac)], axis=1)
    return cos, sin


def _ret_kernel(*refs, n_chunks, heads, use_rope, has_s0, emit_state, aliased):
    refs = list(refs)
    dec_ref, q_ref, k_ref, v_ref = refs[:4]
    refs = refs[4:]
    if use_rope:
        cos_ref, sin_ref = refs[:2]
        refs = refs[2:]
    if has_s0:
        s0_ref = refs[0]
        refs = refs[1:]
    if aliased:
        refs = refs[1:]
    o_ref = refs[0]
    refs = refs[1:]
    if emit_state:
        st_ref = refs[0]
        refs = refs[1:]
    (qk_ref,) = refs

    c = RET_CHUNK
    nf = DK_RET // 4
    head0 = pl.program_id(1) * heads
    inter = has_s0 or n_chunks > 1

    row = lax.broadcasted_iota(jnp.int32, (c, DK_RET), 0).astype(F32)
    ii = lax.broadcasted_iota(jnp.int32, (c, c), 0)
    jj = lax.broadcasted_iota(jnp.int32, (c, c), 1)
    dist = (ii - jj).astype(F32)
    lane = lax.broadcasted_iota(jnp.int32, (c, DK_RET), 1)
    first_half = (lane % (2 * nf)) < nf

    def rope(x, r0):
        if not use_rope:
            return x
        partner = jnp.where(first_half, pltpu.roll(x, DK_RET - nf, axis=1), pltpu.roll(x, nf, axis=1))
        return x * cos_ref[pl.ds(r0, c), :] + partner * sin_ref[pl.ds(r0, c), :]

    for hh in range(heads):
        cols = slice(hh * DK_RET, (hh + 1) * DK_RET)

        def lam(d, shape):
            return _log_sigmoid(jnp.full(shape, dec_ref[d, head0 + hh], F32))

        lam_f, lam_b = lam(0, (c, DK_RET)), lam(1, (c, DK_RET))
        dq_f = jnp.exp((row + 1.0) * lam_f)
        dk_f = jnp.exp((c - 1.0 - row) * lam_f)
        dq_b = jnp.exp((c - row) * lam_b)
        dk_b = jnp.exp(row * lam_b)
        cdec_f = jnp.exp(float(c) * lam(0, (DK_RET, DV_RET)))
        cdec_b = jnp.exp(float(c) * lam(1, (DK_RET, DV_RET)))
        dmat = jnp.where(ii > jj, jnp.exp(dist * lam(0, (c, c))),
                         jnp.where(ii < jj, jnp.exp(-dist * lam(1, (c, c))), 2.0))

        group = 2 if n_chunks % 2 == 0 else 1

        def prepare(gi, carry):
            ns = [gi * group + j for j in range(group)]
            r0s = [n * c if isinstance(n, int) else pl.multiple_of(n * c, c) for n in ns]
            q = [rope(q_ref[pl.ds(r0, c), cols].astype(F32), r0) for r0 in r0s]
            k = [rope(k_ref[pl.ds(r0, c), cols].astype(F32) * (DK_RET ** -0.5), r0) for r0 in r0s]
            a = [(_dot_nt(q[j].astype(BF16), k[j].astype(BF16)) * dmat).astype(BF16) for j in range(group)]
            for j, r0 in enumerate(r0s):
                qk_ref[0, pl.ds(r0, c), cols] = (q[j] * dq_f).astype(BF16)
                qk_ref[1, pl.ds(r0, c), cols] = (q[j] * dq_b).astype(BF16)
                qk_ref[2, pl.ds(r0, c), cols] = (k[j] * dk_f).astype(BF16)
                qk_ref[3, pl.ds(r0, c), cols] = (k[j] * dk_b).astype(BF16)
            o = [_dot(a[j], v_ref[pl.ds(r0s[j], c), cols]) for j in range(group)]
            for j, r0 in enumerate(r0s):
                o_ref[pl.ds(r0, c), cols] = o[j]
            return carry

        def advance(n, states):
            ms = (n, n_chunks - 1 - n)
            r0s = [m * c if isinstance(m, int) else pl.multiple_of(m * c, c) for m in ms]
            upd = [_dot_tn(qk_ref[2 + d, pl.ds(r0s[d], c), cols], v_ref[pl.ds(r0s[d], c), cols]) for d in (0, 1)]
            out = []
            for d, cdec in ((0, cdec_f), (1, cdec_b)):
                if inter:
                    o_ref[pl.ds(r0s[d], c), cols] += _dot(qk_ref[d, pl.ds(r0s[d], c), cols],
                                                          states[d].astype(BF16))
                out.append(cdec * states[d] + upd[d])
            return tuple(out)

        _loop(n_chunks // group, prepare, 0)
        zero = jnp.zeros((DK_RET, DV_RET), F32)
        init = (s0_ref[0, 0, 0, hh], s0_ref[0, 0, 1, hh]) if has_s0 else (zero, zero)
        s_f, s_b = _loop(n_chunks, advance, init, unroll=2)
        if emit_state:
            st_ref[0, 0, hh] = s_f
            st_ref[0, 1, hh] = s_b


def _ret_scan(proj, decay, t, batch, heads, row_block0, rope, s0, s0_layer, state_buf, state_layer):
    w = heads * DK_RET
    col0 = POOL_WIDTH // w
    per = RET_WIDTH // w
    in_specs = [
        pl.BlockSpec(memory_space=pltpu.SMEM),
        pl.BlockSpec((t, w), lambda b, h: (row_block0 + b, col0 + h)),
        pl.BlockSpec((t, w), lambda b, h: (row_block0 + b, col0 + per + h)),
        pl.BlockSpec((t, w), lambda b, h: (row_block0 + b, col0 + 2 * per + h)),
    ]
    args = [decay, proj, proj, proj]
    if rope is not None:
        assert heads == 1
        in_specs += [pl.BlockSpec((t, DK_RET), lambda b, h: (0, 0))] * 2
        args += list(rope)
    if s0 is not None:
        in_specs.append(pl.BlockSpec((1, 1, 2, heads, DK_RET, DV_RET), lambda b, h: (b, s0_layer, 0, h, 0, 0)))
        args.append(s0)
    out_specs = [pl.BlockSpec((t, w), lambda b, h: (b, h))]
    out_shape = [jax.ShapeDtypeStruct((batch * t, RET_WIDTH), F32)]
    aliases = {}
    if state_layer is not None:
        out_specs.append(pl.BlockSpec((1, None, 2, heads, DK_RET, DV_RET), lambda b, h: (b, state_layer, 0, h, 0, 0)))
        out_shape.append(jax.ShapeDtypeStruct((batch, N_EVEN, 2, H_RET, DK_RET, DV_RET), F32))
        if state_buf is not None:
            aliases = {len(args): 1}
            in_specs.append(pl.BlockSpec(memory_space=pl.ANY))
            args.append(state_buf)
    return pl.pallas_call(
        functools.partial(_ret_kernel, n_chunks=t // RET_CHUNK, heads=heads, use_rope=rope is not None,
                          has_s0=s0 is not None, emit_state=state_layer is not None, aliased=bool(aliases)),
        grid=(batch, H_RET // heads),
        in_specs=in_specs,
        out_specs=out_specs,
        out_shape=out_shape,
        input_output_aliases=aliases,
        scratch_shapes=[pltpu.VMEM((4, t, w), BF16)],
        compiler_params=_params("parallel", "parallel"),
        name="ret_scan",
    )(*args)


def _gla_kernel(*refs, n_blocks, seqs, dvb, has_s0, emit_state, aliased):
    refs = list(refs)
    q_ref, k_ref, v_ref, z_ref, g2_ref, gb_ref = refs[:6]
    refs = refs[6:]
    if has_s0:
        s0_ref = refs[0]
        refs = refs[1:]
    if aliased:
        refs = refs[1:]
    o_ref = refs[0]
    refs = refs[1:]
    if emit_state:
        st_ref = refs[0]
        refs = refs[1:]
    qd_ref, kt_ref, a_ref, dec_ref, s_ref = refs

    c = CHUNK
    blk = GLA_BLK
    cpb = blk // c
    tot_blocks = seqs * n_blocks

    @pl.when(pl.program_id(2) == 0)
    def _():
        ii = lax.broadcasted_iota(jnp.int32, (blk, blk), 0)
        jj = lax.broadcasted_iota(jnp.int32, (blk, blk), 1)
        same = (ii // c) == (jj // c)
        masks = (jnp.logical_and(same, ii >= jj), jnp.logical_and(same, ii <= jj))
        tris = tuple(jnp.tile(jnp.where(m, 1.0, 0.0).astype(BF16), (1, 2)) for m in masks)

        group = 2 if tot_blocks % 2 == 0 else 1
        chains = [(j, d) for j in range(group) for d in (0, 1)]

        def build(gi, carry):
            bis = [gi * group + j for j in range(group)]
            r0s = [bi * blk if isinstance(bi, int) else pl.multiple_of(bi * blk, blk) for bi in bis]
            q = [q_ref[pl.ds(r0, blk), :].astype(F32) * (DK_GLA ** -0.5) for r0 in r0s]
            k = [k_ref[pl.ds(r0, blk), :].astype(F32) for r0 in r0s]
            z = [_dot(z_ref[pl.ds(r0s[j], blk), :], g2_ref[0, d]) + gb_ref[0, d] for j, d in chains]
            lg = [jnp.concatenate(_split_bf16(_log_sigmoid(zz) * (LOG2E / GLA_TAU)), axis=0) for zz in z]
            g = [_dot(tris[d], lg[n]) for n, (j, d) in enumerate(chains)]
            qds, kds, cross = [], [], []
            for n, (j, d) in enumerate(chains):
                edge = c - 1 if d == 0 else 0
                gt = [g[n][ci * c + edge:ci * c + edge + 1, :] for ci in range(cpb)]
                g_tot = jnp.concatenate([jnp.broadcast_to(gt[ci], (c, DK_GLA)) for ci in range(cpb)], axis=0)
                qd_f32 = q[j] * jnp.exp2(g[n])
                qd = qd_f32.astype(BF16)
                kds.append((k[j] * jnp.exp2(-g[n])).astype(BF16))
                qds.append(qd)
                kt = k[j] * jnp.exp2(g_tot - g[n])
                scan = list(range(cpb)) if d == 0 else list(reversed(range(cpb)))
                pos = {ci: p for p, ci in enumerate(scan)}

                def span(lo, hi):
                    parts = [gt[scan[p]] for p in range(lo, hi)]
                    return functools.reduce(lambda x, y: x + y, parts) if parts else None

                def scaled(x, e):
                    return x if e is None else x * jnp.broadcast_to(jnp.exp2(e), x.shape)

                rows = lambda x, ci: x[ci * c:(ci + 1) * c, :]
                qd_ref[d, pl.ds(r0s[j], blk), :] = jnp.concatenate(
                    [scaled(rows(qd_f32, ci), span(0, pos[ci])) for ci in range(cpb)], axis=0).astype(BF16)
                kt_ref[d, pl.ds(r0s[j], blk), :] = jnp.concatenate(
                    [scaled(rows(kt, ci), span(pos[ci] + 1, cpb)) for ci in range(cpb)], axis=0).astype(BF16)
                dec_ref[d * tot_blocks + bis[j]] = jnp.exp2(jnp.broadcast_to(span(0, cpb), (8, DK_GLA)))
                zero_rows = jnp.zeros((c, DK_GLA), BF16)
                for p in range(1, cpb):
                    src = jnp.concatenate(
                        [scaled(rows(kt, cj), span(pos[cj] + 1, p)).astype(BF16) if pos[cj] < p else zero_rows
                         for cj in range(cpb)], axis=0)
                    cross.append((n, scan[p], qd[scan[p] * c:(scan[p] + 1) * c, :], src))
            a = [_dot_nt(qds[n], kds[n]) for n in range(len(chains))]
            cross = [(n, ci, _dot_nt(qrows, src)) for n, ci, qrows, src in cross]
            for j in range(group):
                total = jnp.where(masks[0], a[2 * j], 0.0) + jnp.where(masks[1], a[2 * j + 1], 0.0)
                for d in (0, 1):
                    by_chunk = {ci: sc for n, ci, sc in cross if n == 2 * j + d}
                    total = total + jnp.concatenate(
                        [by_chunk.get(ci, jnp.zeros((c, blk), F32)) for ci in range(cpb)], axis=0)
                a_ref[bis[j]] = total.astype(BF16)
            return carry

        _loop(tot_blocks // group, build, 0)

    def intra(bi, carry):
        r0 = bi * blk if isinstance(bi, int) else pl.multiple_of(bi * blk, blk)
        o_ref[pl.ds(r0, blk), :] = _dot(a_ref[bi], v_ref[pl.ds(r0, blk), :])
        return carry

    _loop(tot_blocks, intra, 0)

    runs = [(sq, d) for sq in range(seqs) for d in (0, 1)]
    for sq, d in runs:
        s_ref[2 * sq + d] = s0_ref[sq, 0, d, 0].T if has_s0 else jnp.zeros((dvb, DK_GLA), F32)

    inter = has_s0 or n_blocks > 1

    def advance(n, carry):
        ms = [sq * n_blocks + (n if d == 0 else n_blocks - 1 - n) for sq, d in runs]
        r0s = [m * blk if isinstance(m, int) else pl.multiple_of(m * blk, blk) for m in ms]
        upd = [_dot_tn(v_ref[pl.ds(r0s[n_], blk), :], kt_ref[d, pl.ds(r0s[n_], blk), :])
               for n_, (sq, d) in enumerate(runs)]
        for n_, (sq, d) in enumerate(runs):
            s = s_ref[2 * sq + d]
            if inter:
                o_ref[pl.ds(r0s[n_], blk), :] += _dot_nt(qd_ref[d, pl.ds(r0s[n_], blk), :], s.astype(BF16))
            s_ref[2 * sq + d] = s * dec_ref[d * tot_blocks + ms[n_]][0:1, :] + upd[n_]
        return carry

    _loop(n_blocks, advance, 0, unroll=2)
    if emit_state:
        for sq, d in runs:
            st_ref[sq, d, 0] = s_ref[2 * sq + d].T


def _gla_scan(proj, z1, g2, gb, t, batch, seqs, dvb, row_block0, s0, s0_layer, state_buf, state_layer):
    nd = DV_GLA // dvb
    kcol = GLA_DK_TOTAL // DK_GLA
    vcol = 2 * GLA_DK_TOTAL // dvb
    n_blocks = t // GLA_BLK
    rows = seqs * t
    in_specs = [
        pl.BlockSpec((rows, DK_GLA), lambda b, h, d: (row_block0 + b, h)),
        pl.BlockSpec((rows, DK_GLA), lambda b, h, d: (row_block0 + b, kcol + h)),
        pl.BlockSpec((rows, dvb), lambda b, h, d: (row_block0 + b, vcol + h * nd + d)),
        pl.BlockSpec((rows, 128), lambda b, h, d: (row_block0 + b, 0)),
        pl.BlockSpec((1, 2, 128, DK_GLA), lambda b, h, d: (h, 0, 0, 0)),
        pl.BlockSpec((1, 2, 1, DK_GLA), lambda b, h, d: (h, 0, 0, 0)),
    ]
    args = [proj, proj, proj, z1, g2, gb]
    if s0 is not None:
        assert seqs == 1
        in_specs.append(pl.BlockSpec((1, 1, 2, 1, DK_GLA, dvb), lambda b, h, d: (b, s0_layer, 0, h, 0, d)))
        args.append(s0)
    out_specs = [pl.BlockSpec((rows, dvb), lambda b, h, d: (b, h * nd + d))]
    out_shape = [jax.ShapeDtypeStruct((batch * t, GLA_DV_TOTAL), F32)]
    aliases = {}
    if state_layer is not None:
        out_specs.append(pl.BlockSpec((seqs, None, 2, 1, DK_GLA, dvb), lambda b, h, d: (b, state_layer, 0, h, 0, d)))
        out_shape.append(jax.ShapeDtypeStruct((batch, N_ODD, 2, H_GLA, DK_GLA, DV_GLA), F32))
        if state_buf is not None:
            aliases = {len(args): 1}
            in_specs.append(pl.BlockSpec(memory_space=pl.ANY))
            args.append(state_buf)
    return pl.pallas_call(
        functools.partial(_gla_kernel, n_blocks=n_blocks, seqs=seqs, dvb=dvb, has_s0=s0 is not None,
                          emit_state=state_layer is not None, aliased=bool(aliases)),
        grid=(batch // seqs, H_GLA, nd),
        in_specs=in_specs,
        out_specs=out_specs,
        out_shape=out_shape,
        input_output_aliases=aliases,
        scratch_shapes=[
            pltpu.VMEM((2, rows, DK_GLA), BF16),
            pltpu.VMEM((2, rows, DK_GLA), BF16),
            pltpu.VMEM((rows // GLA_BLK, GLA_BLK, GLA_BLK), BF16),
            pltpu.VMEM((2 * rows // GLA_BLK, 8, DK_GLA), F32),
            pltpu.VMEM((2 * seqs, dvb, DK_GLA), F32),
        ],
        compiler_params=_params("parallel", "parallel", "arbitrary"),
        name="gla_scan",
    )(*args)


def _group_specs(tm, width, col_block=0):
    p_tiles = N_PROMPT // tm
    return [pl.BlockSpec((tm, width), lambda *idx: (jnp.minimum(idx[-1], p_tiles - 1), col_block)),
            pl.BlockSpec((tm, width), lambda *idx: (jnp.maximum(idx[-1] - p_tiles, 0), col_block))]


def _out_proj_kernel(*refs, n_heads, width, with_pool):
    refs = list(refs)
    if with_pool:
        u_ref, up_ref, un_ref, b0_ref, bp_ref, bn_ref, pw_ref, psc_ref = refs[:8]
        refs = refs[8:]
    op_ref, os_ref, gate_ref, gn_ref, w_ref = refs[:5]
    *x_refs, g1_ref, out_ref = refs[5:]
    xp_ref, xs_ref = x_refs if len(x_refs) == 2 else (x_refs[0], x_refs[0])
    i = pl.program_id(0)
    in_prompt = i < N_PROMPT // TM

    pooled = []
    if with_pool:
        sub = TM // POOL_TM
        tiles = []
        for s in range(sub):
            r0 = s * POOL_TM
            up = up_ref[...] if s == 0 else u_ref[r0 - POOL_HALO:r0, :]
            un = un_ref[...] if s == sub - 1 else u_ref[r0 + POOL_TM:r0 + POOL_TM + POOL_HALO, :]
            tiles.append((i * sub + s, u_ref[r0:r0 + POOL_TM, :], up, un))
        pooled = [jnp.concatenate(_pool_tiles(tiles, b0_ref, bp_ref, bn_ref, pw_ref, psc_ref), axis=0)]

    def run(o_ref, x_ref):
        parts = list(pooled)
        for hd in range(n_heads):
            cols = slice(hd * width, (hd + 1) * width)
            gt = gate_ref[:, cols].astype(F32)
            parts.append((_rms(o_ref[:, cols]) * gn_ref[:, cols] * _silu(gt)).astype(BF16))
        a = jnp.concatenate(parts, axis=1)
        out_ref[...] = x_ref[...] + g1_ref[0] * _dot(a, w_ref[...])

    @pl.when(in_prompt)
    def _():
        run(op_ref, xp_ref)

    @pl.when(jnp.logical_not(in_prompt))
    def _():
        run(os_ref, xs_ref)


def _out_proj(xs, mod, layer, w, w_layer, proj, gate_col_block, norm_g, o_prompt, o_sample, n_heads, pool=None):
    width = o_prompt.shape[1]
    x_specs = [pl.BlockSpec((TM, D_MODEL), lambda i: (i, 0))] if len(xs) == 1 else _group_specs(TM, D_MODEL)
    in_specs = _group_specs(TM, width) + [
        pl.BlockSpec((TM, width), lambda i: (i, gate_col_block)),
        pl.BlockSpec((1, width), lambda i: (0, 0)),
        pl.BlockSpec((None, D_MODEL, D_MODEL), lambda i: (w_layer, 0, 0)),
    ] + x_specs + [_mod_spec(layer, 2, TM)]
    args = [o_prompt, o_sample, proj, norm_g.reshape(1, width), w, *xs, mod]
    if pool is not None:
        b0, bp, bn = _pool_bands()
        halo_blocks = TM // POOL_HALO
        last = N_TOK // POOL_HALO - 1
        full = lambda shape: pl.BlockSpec(shape, lambda i: (0,) * len(shape))
        in_specs = [
            pl.BlockSpec((TM, POOL_WIDTH), lambda i: (i, 0)),
            pl.BlockSpec((POOL_HALO, POOL_WIDTH), lambda i: (jnp.maximum(i * halo_blocks - 1, 0), 0)),
            pl.BlockSpec((POOL_HALO, POOL_WIDTH), lambda i: (jnp.minimum((i + 1) * halo_blocks, last), 0)),
            full(b0.shape), full(bp.shape), full(bn.shape),
            pl.BlockSpec((None,) + pool[0].shape[1:], lambda i: (w_layer, 0, 0, 0)),
            pl.BlockSpec((None, 1, POOL_WIDTH), lambda i: (w_layer, 0, 0)),
        ] + in_specs
        args = [proj, proj, proj, b0, bp, bn, pool[0], pool[1].reshape(N_EVEN, 1, POOL_WIDTH)] + args
    return pl.pallas_call(
        functools.partial(_out_proj_kernel, n_heads=n_heads, width=width // n_heads, with_pool=pool is not None),
        grid=(N_TOK // TM,),
        in_specs=in_specs,
        out_specs=pl.BlockSpec((TM, D_MODEL), lambda i: (i, 0)),
        out_shape=jax.ShapeDtypeStruct((N_TOK, D_MODEL), F32),
        compiler_params=_params("parallel"),
        name="out_proj",
    )(*args)


def _ffn_kernel(x_ref, xp_ref, xn_ref, gn_ref, sh_ref, sc_ref, gt_ref, wa_ref, wb_ref, cw_ref, cb_ref,
                wd_ref, fg_ref, *rest, final):
    if final:
        op_ref, o_ref, h_ref, hh_ref = rest
    else:
        o_ref, h_ref, hh_ref = rest
    i = pl.program_id(0)
    f = pl.program_id(1)
    tm = TM_FFN

    def modnorm(x):
        return (_rms(x) * gn_ref[0] * (1.0 + sc_ref[0]) + sh_ref[0]).astype(BF16)

    @pl.when(f == 0)
    def _():
        h_ref[...] = modnorm(x_ref[...])
        hh_ref[...] = modnorm(jnp.concatenate([xp_ref[...], xn_ref[...]], axis=0))
        o_ref[...] = jnp.zeros_like(o_ref)

    a = _dot(h_ref[...], wa_ref[...])
    b = _dot(h_ref[...], wb_ref[...])
    a_halo = _dot(hh_ref[...], wa_ref[...])

    seq = jnp.where(i * tm < N_PROMPT, SEQ, DEC_SEQ)
    t = lax.broadcasted_iota(jnp.int32, (tm, TF), 0)
    pos = (i * tm + t) & (seq - 1)
    a_prev = jnp.where(t == 0, a_halo[7:8, :], pltpu.roll(a, 1, axis=0))
    a_prev = jnp.where(pos == 0, 0.0, a_prev)
    a_next = jnp.where(t == tm - 1, a_halo[8:9, :], pltpu.roll(a, tm - 1, axis=0))
    a_next = jnp.where(pos == seq - 1, 0.0, a_next)
    conv = a_prev * cw_ref[0:1, :] + a * cw_ref[1:2, :] + a_next * cw_ref[2:3, :] + cb_ref[...]
    o_ref[...] += _dot((_silu(conv) * b).astype(BF16), wd_ref[...])

    @pl.when(f == pl.num_programs(1) - 1)
    def _():
        y = x_ref[...] + gt_ref[0] * o_ref[...]
        if not final:
            o_ref[...] = y
        else:
            y = _rms(y) * fg_ref[...]
            in_prompt = i < N_PROMPT // tm

            @pl.when(in_prompt)
            def _():
                op_ref[...] = y

            @pl.when(jnp.logical_not(in_prompt))
            def _():
                o_ref[...] = y


def _ffn(x, mod, layer, norm_g, w_up, conv_w, conv_b, w_down, final_g, final):
    m = x.shape[0]
    nf = D_FF // TF
    tm = TM_FFN
    halo = 8
    last = m // halo - 1
    if final:
        p_tiles = N_PROMPT // tm
        out_specs = [pl.BlockSpec((tm, D_MODEL), lambda i, f: (jnp.minimum(i, p_tiles - 1), 0),
                                  pipeline_mode=pl.Buffered(1)),
                     pl.BlockSpec((tm, D_MODEL), lambda i, f: (jnp.maximum(i - p_tiles, 0), 0),
                                  pipeline_mode=pl.Buffered(1))]
        out_shape = [jax.ShapeDtypeStruct((N_PROMPT, D_MODEL), F32),
                     jax.ShapeDtypeStruct((N_SAMPLE, D_MODEL), F32)]
    else:
        out_specs = pl.BlockSpec((tm, D_MODEL), lambda i, f: (i, 0))
        out_shape = jax.ShapeDtypeStruct((m, D_MODEL), F32)
    return pl.pallas_call(
        functools.partial(_ffn_kernel, final=final),
        grid=(m // tm, nf),
        in_specs=[
            pl.BlockSpec((tm, D_MODEL), lambda i, f: (i, 0), pipeline_mode=pl.Buffered(1)),
            pl.BlockSpec((halo, D_MODEL), lambda i, f: (jnp.maximum(i * (tm // halo) - 1, 0), 0)),
            pl.BlockSpec((halo, D_MODEL), lambda i, f: (jnp.minimum((i + 1) * (tm // halo), last), 0)),
            pl.BlockSpec((1, 1, D_MODEL), lambda i, f: (layer, 0, 0)),
            _mod_spec(layer, 3, tm),
            _mod_spec(layer, 4, tm),
            _mod_spec(layer, 5, tm),
            pl.BlockSpec((None, D_MODEL, TF), lambda i, f: (layer, 0, f)),
            pl.BlockSpec((None, D_MODEL, TF), lambda i, f: (layer, 0, nf + f)),
            pl.BlockSpec((None, 3, TF), lambda i, f: (layer, 0, f)),
            pl.BlockSpec((None, 1, TF), lambda i, f: (layer, 0, f)),
            pl.BlockSpec((None, TF, D_MODEL), lambda i, f: (layer, f, 0)),
            pl.BlockSpec((1, D_MODEL), lambda i, f: (0, 0)),
        ],
        out_specs=out_specs,
        out_shape=out_shape,
        scratch_shapes=[
            pltpu.VMEM((tm, D_MODEL), BF16),
            pltpu.VMEM((2 * halo, D_MODEL), BF16),
        ],
        compiler_params=_params("arbitrary", "arbitrary"),
        name="ffn",
    )(x, x, x, norm_g, mod, mod, mod, w_up, w_up, conv_w, conv_b.reshape(DEPTH, 1, D_FF), w_down,
      final_g.reshape(1, D_MODEL))


def _gla_gate_weights(gw2, gb):
    r = GLA_GATE_RANK
    g2 = jnp.zeros((2, 128, GLA_DK_TOTAL), F32)
    g2 = g2.at[0, 0:r].set(gw2[0]).at[1, r:2 * r].set(gw2[1])
    g2 = g2.reshape(2, 128, H_GLA, DK_GLA).transpose(2, 0, 1, 3).astype(BF16)
    gbh = gb.reshape(2, H_GLA, 1, DK_GLA).transpose(1, 0, 2, 3)
    return g2, gbh


def kernel(x_prompt, x_sample, state_ret, state_gla, c, c_ctx, ada_w, ada_b, norm1_g, norm2_g,
           even_w_in, pool_w, pool_scale, ret_decay, ret_norm_g, even_w_out, odd_w_in, gla_gw1,
           gla_gw2, gla_gb, gla_norm_g, odd_w_out, ffn_w_up, ffn_conv_w, ffn_conv_b, ffn_w_down, final_g):
    xs = (x_prompt.reshape(N_PROMPT, D_MODEL), x_sample.reshape(N_SAMPLE, D_MODEL))
    cvec = jnp.concatenate([c_ctx[None, :], c, jnp.zeros((COND_PAD - N_COND, D_MODEL), F32)], axis=0)
    mod = _ada_mod(cvec, ada_w, ada_b)
    n1 = norm1_g.reshape(DEPTH, 1, D_MODEL)
    n2 = norm2_g.reshape(DEPTH, 1, D_MODEL)
    rope = _rope_tables(DEC_SEQ)
    p_blocks = N_PROMPT // DEC_SEQ

    w_even_in, w_even_out = even_w_in.astype(BF16), even_w_out.astype(BF16)
    w_odd_in, w_odd_out = odd_w_in.astype(BF16), odd_w_out.astype(BF16)
    w_up, w_down, w_pool = ffn_w_up.astype(BF16), ffn_w_down.astype(BF16), pool_w.astype(BF16)
    w_gate1 = jnp.concatenate([gla_gw1[:, 0], gla_gw1[:, 1],
                               jnp.zeros((N_ODD, D_MODEL, 128 - 2 * GLA_GATE_RANK), F32)], axis=2).astype(BF16)

    ret_states = gla_states = None
    for l in range(DEPTH):
        if l % 2 == 0:
            i = l // 2
            proj = _in_proj(xs, n1, mod, l, w_even_in, i, IN_SPLIT)
            o_p, ret_states = _ret_scan(proj, ret_decay[i], SEQ, BATCH, H_RET, 0, None, None, 0, ret_states, i)
            (o_s,) = _ret_scan(proj, ret_decay[i], DEC_SEQ, DEC_BATCH, 1, p_blocks, rope, state_ret, i, None, None)
            x = _out_proj(xs, mod, l, w_even_out, i, proj, EVEN_IN // RET_WIDTH - 1,
                          ret_norm_g[i], o_p, o_s, H_RET, (w_pool, pool_scale))
        else:
            j = l // 2
            proj = _in_proj(xs, n1, mod, l, w_odd_in, j, IN_SPLIT)
            z1 = _in_proj(xs, n1, mod, l, w_gate1, j, 1)
            g2, gbh = _gla_gate_weights(gla_gw2[j], gla_gb[j])
            o_p, gla_states = _gla_scan(proj, z1, g2, gbh, SEQ, BATCH, GLA_PROMPT_SEQS, DV_GLA, 0, None, 0,
                                        gla_states, j)
            (o_s,) = _gla_scan(proj, z1, g2, gbh, DEC_SEQ, DEC_BATCH, 1, DV_GLA, p_blocks, state_gla, j,
                               None, None)
            x = _out_proj(xs, mod, l, w_odd_out, j, proj, ODD_IN // GLA_DV_TOTAL - 1,
                          gla_norm_g[j], o_p, o_s, H_GLA)
        xs = _ffn(x, mod, l, n2, w_up, ffn_conv_w, ffn_conv_b, w_down, final_g, l == DEPTH - 1)
        xs = tuple(xs) if l == DEPTH - 1 else (xs,)

    y_prompt = xs[0].reshape(BATCH, SEQ, D_MODEL)
    y_sample = xs[1].reshape(DEC_BATCH, DEC_SEQ, D_MODEL)
    return (y_prompt, y_sample, ret_states, gla_states)
```

```python
import functools

import numpy as np
import jax
import jax.numpy as jnp
from jax import lax
from jax.experimental import pallas as pl
from jax.experimental.pallas import tpu as pltpu

F32 = jnp.float32
BF16 = jnp.bfloat16

D_MODEL = 2048
BATCH = 16
SEQ = 256
DEPTH = 4
DEC_BATCH = 2
DEC_SEQ = 4096
GRID_W = 64
N_EVEN = (DEPTH + 1) // 2
N_ODD = DEPTH // 2
POOL_WIDTH = D_MODEL // 2
POOL_GROUPS = 4
POOL_GC = POOL_WIDTH // POOL_GROUPS
POOL_WINDOWS = (2, 4, 8, 16)
RET_WIDTH = D_MODEL // 2
H_RET = 8
DK_RET = RET_WIDTH // H_RET
DV_RET = RET_WIDTH // H_RET
ROPE_BASE = 10000.0
H_GLA = 4
GLA_DK_TOTAL = D_MODEL // 2
GLA_DV_TOTAL = D_MODEL
DK_GLA = GLA_DK_TOTAL // H_GLA
DV_GLA = GLA_DV_TOTAL // H_GLA
GLA_GATE_RANK = 16
GLA_TAU = 16.0
D_FF = 5632
CONV_W = 3
CHUNK = 64
EPS = 1e-6
EVEN_IN = POOL_WIDTH + 4 * RET_WIDTH
ODD_IN = 2 * GLA_DK_TOTAL + 2 * GLA_DV_TOTAL

N_PROMPT = BATCH * SEQ
N_SAMPLE = DEC_BATCH * DEC_SEQ
N_TOK = N_PROMPT + N_SAMPLE
N_COND = 1 + DEC_BATCH
COND_PAD = 8

VMEM_LIMIT = 56 * 1024 * 1024

TM = 512
IN_SPLIT = 1
IN_CHUNK = 512
TM_FFN = 1024
TF = 512
POOL_TM = 256
POOL_HALO = 16
ADA_TN = 1024
RET_CHUNK = 256
GLA_BLK = 256
GLA_PROMPT_SEQS = 2


def _params(*sem):
    return pltpu.CompilerParams(dimension_semantics=sem, vmem_limit_bytes=VMEM_LIMIT)


def _cond_of_tile(i, tm):
    r0 = i * tm
    return jnp.where(r0 < N_PROMPT, 0, 1 + (r0 - N_PROMPT) // DEC_SEQ)


LOG2E = 1.4426950408889634


def _log_sigmoid(x):
    return jnp.minimum(x, 0.0) - jnp.log(1.0 + jnp.exp2(jnp.abs(x) * (-LOG2E)))


def _silu(x):
    return x * jax.nn.sigmoid(x)


def _rms(x):
    return x * lax.rsqrt(jnp.mean(x * x, axis=-1, keepdims=True) + EPS)


def _dot(a, b):
    return jnp.dot(a, b, preferred_element_type=F32)


def _dot_nt(a, b):
    return lax.dot_general(a, b, (((1,), (1,)), ((), ())), preferred_element_type=F32)


def _dot_tn(a, b):
    return lax.dot_general(a, b, (((0,), (0,)), ((), ())), preferred_element_type=F32)


def _split_bf16(x):
    hi = x.astype(BF16)
    lo = (x - hi.astype(F32)).astype(BF16)
    return hi, lo


def _loop(n, body, init, unroll=1):
    if n == 1:
        return body(0, init)
    return lax.fori_loop(0, n, body, init, unroll=unroll)


def _ada_kernel(c_ref, w_ref, b_ref, o_ref):
    s = _silu(c_ref[...]).astype(BF16)
    o_ref[0] = _dot(s, w_ref[0].astype(BF16)) + b_ref[0]


def _ada_mod(cvec, ada_w, ada_b):
    n = 6 * D_MODEL
    mod = pl.pallas_call(
        _ada_kernel,
        grid=(DEPTH, n // ADA_TN),
        in_specs=[
            pl.BlockSpec((COND_PAD, D_MODEL), lambda l, j: (0, 0)),
            pl.BlockSpec((1, D_MODEL, ADA_TN), lambda l, j: (l, 0, j)),
            pl.BlockSpec((1, 1, ADA_TN), lambda l, j: (l, 0, j)),
        ],
        out_specs=pl.BlockSpec((1, COND_PAD, ADA_TN), lambda l, j: (l, 0, j)),
        out_shape=jax.ShapeDtypeStruct((DEPTH, COND_PAD, n), F32),
        compiler_params=_params("parallel", "parallel"),
        name="ada_mod",
    )(cvec, ada_w, ada_b.reshape(DEPTH, 1, n))
    mod = mod[:, :N_COND].reshape(DEPTH, N_COND, 6, D_MODEL).transpose(0, 2, 1, 3)
    return mod.reshape(DEPTH * 6 * N_COND, 1, D_MODEL)


def _mod_spec(layer, part, tm, row_axis=0):
    base = (layer * 6 + part) * N_COND
    return pl.BlockSpec((1, 1, D_MODEL), lambda *idx: (base + _cond_of_tile(idx[row_axis], tm), 0, 0))


def _in_proj_kernel(*refs, chunk):
    *x_refs, g_ref, sh_ref, sc_ref, w_ref, o_ref, h_ref = refs

    if len(x_refs) == 1:
        x = x_refs[0][...]
    else:
        x = jnp.where(pl.program_id(1) < N_PROMPT // TM, x_refs[0][...], x_refs[1][...])
    h = _rms(x) * g_ref[0]
    h_ref[...] = (h * (1.0 + sc_ref[0]) + sh_ref[0]).astype(BF16)
    for c0 in range(0, o_ref.shape[1], chunk):
        o_ref[:, c0:c0 + chunk] = _dot(h_ref[...], w_ref[:, c0:c0 + chunk]).astype(o_ref.dtype)


def _in_proj(xs, norm_g, mod, layer, w, w_layer, split):
    n = w.shape[2]
    tn = n // split
    x_specs = [pl.BlockSpec((TM, D_MODEL), lambda j, i: (i, 0))] if len(xs) == 1 else _group_specs(TM, D_MODEL)
    return pl.pallas_call(
        functools.partial(_in_proj_kernel, chunk=min(IN_CHUNK, tn)),
        grid=(split, N_TOK // TM),
        in_specs=x_specs + [
            pl.BlockSpec((1, 1, D_MODEL), lambda j, i: (layer, 0, 0)),
            _mod_spec(layer, 0, TM, 1),
            _mod_spec(layer, 1, TM, 1),
            pl.BlockSpec((None, D_MODEL, tn), lambda j, i: (w_layer, 0, j), pipeline_mode=pl.Buffered(1)),
        ],
        out_specs=pl.BlockSpec((TM, tn), lambda j, i: (i, j)),
        out_shape=jax.ShapeDtypeStruct((N_TOK, n), BF16),
        scratch_shapes=[pltpu.VMEM((TM, D_MODEL), BF16)],
        compiler_params=_params("arbitrary", "arbitrary"),
        name="in_proj",
    )(*xs, norm_g, mod, mod, w)


def _pool_bands():
    t = POOL_TM
    b0 = np.zeros((POOL_GROUPS, t, t), np.float32)
    bp = np.zeros((POOL_GROUPS, t, POOL_HALO), np.float32)
    bn = np.zeros((POOL_GROUPS, t, POOL_HALO), np.float32)
    for g, win in enumerate(POOL_WINDOWS):
        for r in range(t):
            for s in range(r - win // 2, r + win - win // 2):
                if s < 0:
                    bp[g, r, s + POOL_HALO] = 1.0
                elif s >= t:
                    bn[g, r, s - t] = 1.0
                else:
                    b0[g, r, s] = 1.0
    return jnp.asarray(b0, BF16), jnp.asarray(bp, BF16), jnp.asarray(bn, BF16)


def _pool_tiles(tiles, b0_ref, bp_ref, bn_ref, pw_ref, sc_ref):
    tiles_per_seq = DEC_SEQ // POOL_TM
    t = lax.broadcasted_iota(jnp.int32, (POOL_TM, POOL_GC), 0)
    sums, cnts, us = [], [], []
    for tile, u, up, un in tiles:
        in_prompt = tile < N_PROMPT // POOL_TM
        pos = (tile - N_PROMPT // POOL_TM) % tiles_per_seq
        is_start = jnp.logical_or(in_prompt, pos == 0)
        is_end = jnp.logical_or(in_prompt, pos == tiles_per_seq - 1)
        up = jnp.where(is_start, jnp.zeros_like(up), up)
        un = jnp.where(is_end, jnp.zeros_like(un), un)
        for g, win in enumerate(POOL_WINDOWS):
            cols = slice(g * POOL_GC, (g + 1) * POOL_GC)
            us.append(u[:, cols])
            sums.append(_dot(b0_ref[g], u[:, cols]) + _dot(bp_ref[g], up[:, cols]) + _dot(bn_ref[g], un[:, cols]))
            cut_lo = jnp.where(is_start, jnp.maximum(win // 2 - t, 0), 0)
            cut_hi = jnp.where(is_end, jnp.maximum(t + (win - win // 2) - POOL_TM, 0), 0)
            cnts.append((win - cut_lo - cut_hi).astype(F32))
    pooled = [(s / cnt - u.astype(F32)).astype(BF16) for s, cnt, u in zip(sums, cnts, us)]
    ys = [_dot(p, pw_ref[n % POOL_GROUPS]) for n, p in enumerate(pooled)]
    out = []
    for j in range(len(tiles)):
        out.append(jnp.concatenate(
            [(ys[j * POOL_GROUPS + g] * sc_ref[:, g * POOL_GC:(g + 1) * POOL_GC]).astype(BF16)
             for g in range(POOL_GROUPS)], axis=1))
    return out


def _rope_tables(t):
    nf = DK_RET // 4
    rows = t // GRID_W
    r = jnp.repeat(jnp.arange(rows), GRID_W).astype(F32)
    col = jnp.tile(jnp.arange(GRID_W), rows).astype(F32)
    inv = ROPE_BASE ** (-jnp.arange(nf, dtype=F32) / nf)
    ar, ac = r[:, None] * inv, col[:, None] * inv
    cos = jnp.concatenate([jnp.cos(ar), jnp.cos(ar), jnp.cos(ac), jnp.cos(ac)], axis=1)
    sin = jnp.concatenate([-jnp.sin(ar), jnp.sin(ar), -jnp.sin(ac), jnp.sin(ac)], axis=1)
    return cos, sin


def _ret_kernel(*refs, n_chunks, heads, use_rope, has_s0, emit_state, aliased):
    refs = list(refs)
    dec_ref, q_ref, k_ref, v_ref = refs[:4]
    refs = refs[4:]
    if use_rope:
        cos_ref, sin_ref = refs[:2]
        refs = refs[2:]
    if has_s0:
        s0_ref = refs[0]
        refs = refs[1:]
    if aliased:
        refs = refs[1:]
    o_ref = refs[0]
    refs = refs[1:]
    if emit_state:
        st_ref = refs[0]
        refs = refs[1:]
    (qk_ref,) = refs

    c = RET_CHUNK
    nf = DK_RET // 4
    head0 = pl.program_id(1) * heads
    inter = has_s0 or n_chunks > 1

    row = lax.broadcasted_iota(jnp.int32, (c, DK_RET), 0).astype(F32)
    ii = lax.broadcasted_iota(jnp.int32, (c, c), 0)
    jj = lax.broadcasted_iota(jnp.int32, (c, c), 1)
    dist = (ii - jj).astype(F32)
    lane = lax.broadcasted_iota(jnp.int32, (c, DK_RET), 1)
    first_half = (lane % (2 * nf)) < nf

    def rope(x, r0):
        if not use_rope:
            return x
        partner = jnp.where(first_half, pltpu.roll(x, DK_RET - nf, axis=1), pltpu.roll(x, nf, axis=1))
        return x * cos_ref[pl.ds(r0, c), :] + partner * sin_ref[pl.ds(r0, c), :]

    for hh in range(heads):
        cols = slice(hh * DK_RET, (hh + 1) * DK_RET)

        def lam(d, shape):
            return _log_sigmoid(jnp.full(shape, dec_ref[d, head0 + hh], F32))

        lam_f, lam_b = lam(0, (c, DK_RET)), lam(1, (c, DK_RET))
        dq_f = jnp.exp((row + 1.0) * lam_f)
        dk_f = jnp.exp((c - 1.0 - row) * lam_f)
        dq_b = jnp.exp((c - row) * lam_b)
        dk_b = jnp.exp(row * lam_b)
        cdec_f = jnp.exp(float(c) * lam(0, (DK_RET, DV_RET)))
        cdec_b = jnp.exp(float(c) * lam(1, (DK_RET, DV_RET)))
        dmat = jnp.where(ii > jj, jnp.exp(dist * lam(0, (c, c))),
                         jnp.where(ii < jj, jnp.exp(-dist * lam(1, (c, c))), 2.0))

        group = 2 if n_chunks % 2 == 0 else 1

        def prepare(gi, carry):
            ns = [gi * group + j for j in range(group)]
            r0s = [n * c if isinstance(n, int) else pl.multiple_of(n * c, c) for n in ns]
            q = [rope(q_ref[pl.ds(r0, c), cols].astype(F32), r0) for r0 in r0s]
            k = [rope(k_ref[pl.ds(r0, c), cols].astype(F32) * (DK_RET ** -0.5), r0) for r0 in r0s]
            a = [(_dot_nt(q[j].astype(BF16), k[j].astype(BF16)) * dmat).astype(BF16) for j in range(group)]
            for j, r0 in enumerate(r0s):
                qk_ref[0, pl.ds(r0, c), cols] = (q[j] * dq_f).astype(BF16)
                qk_ref[1, pl.ds(r0, c), cols] = (q[j] * dq_b).astype(BF16)
                qk_ref[2, pl.ds(r0, c), cols] = (k[j] * dk_f).astype(BF16)
                qk_ref[3, pl.ds(r0, c), cols] = (k[j] * dk_b).astype(BF16)
            o = [_dot(a[j], v_ref[pl.ds(r0s[j], c), cols]) for j in range(group)]
            for j, r0 in enumerate(r0s):
                o_ref[pl.ds(r0, c), cols] = o[j]
            return carry

        def advance(n, states):
            ms = (n, n_chunks - 1 - n)
            r0s = [m * c if isinstance(m, int) else pl.multiple_of(m * c, c) for m in ms]
            upd = [_dot_tn(qk_ref[2 + d, pl.ds(r0s[d], c), cols], v_ref[pl.ds(r0s[d], c), cols]) for d in (0, 1)]
            out = []
            for d, cdec in ((0, cdec_f), (1, cdec_b)):
                if inter:
                    o_ref[pl.ds(r0s[d], c), cols] += _dot(qk_ref[d, pl.ds(r0s[d], c), cols],
                                                          states[d].astype(BF16))
                out.append(cdec * states[d] + upd[d])
            return tuple(out)

        _loop(n_chunks // group, prepare, 0)
        zero = jnp.zeros((DK_RET, DV_RET), F32)
        init = (s0_ref[0, 0, 0, hh], s0_ref[0, 0, 1, hh]) if has_s0 else (zero, zero)
        s_f, s_b = _loop(n_chunks, advance, init, unroll=2)
        if emit_state:
            st_ref[0, 0, hh] = s_f
            st_ref[0, 1, hh] = s_b


def _ret_scan(proj, decay, t, batch, heads, row_block0, rope, s0, s0_layer, state_buf, state_layer):
    w = heads * DK_RET
    col0 = POOL_WIDTH // w
    per = RET_WIDTH // w
    in_specs = [
        pl.BlockSpec(memory_space=pltpu.SMEM),
        pl.BlockSpec((t, w), lambda b, h: (row_block0 + b, col0 + h)),
        pl.BlockSpec((t, w), lambda b, h: (row_block0 + b, col0 + per + h)),
        pl.BlockSpec((t, w), lambda b, h: (row_block0 + b, col0 + 2 * per + h)),
    ]
    args = [decay, proj, proj, proj]
    if rope is not None:
        assert heads == 1
        in_specs += [pl.BlockSpec((t, DK_RET), lambda b, h: (0, 0))] * 2
        args += list(rope)
    if s0 is not None:
        in_specs.append(pl.BlockSpec((1, 1, 2, heads, DK_RET, DV_RET), lambda b, h: (b, s0_layer, 0, h, 0, 0)))
        args.append(s0)
    out_specs = [pl.BlockSpec((t, w), lambda b, h: (b, h))]
    out_shape = [jax.ShapeDtypeStruct((batch * t, RET_WIDTH), F32)]
    aliases = {}
    if state_layer is not None:
        out_specs.append(pl.BlockSpec((1, None, 2, heads, DK_RET, DV_RET), lambda b, h: (b, state_layer, 0, h, 0, 0)))
        out_shape.append(jax.ShapeDtypeStruct((batch, N_EVEN, 2, H_RET, DK_RET, DV_RET), F32))
        if state_buf is not None:
            aliases = {len(args): 1}
            in_specs.append(pl.BlockSpec(memory_space=pl.ANY))
            args.append(state_buf)
    return pl.pallas_call(
        functools.partial(_ret_kernel, n_chunks=t // RET_CHUNK, heads=heads, use_rope=rope is not None,
                          has_s0=s0 is not None, emit_state=state_layer is not None, aliased=bool(aliases)),
        grid=(batch, H_RET // heads),
        in_specs=in_specs,
        out_specs=out_specs,
        out_shape=out_shape,
        input_output_aliases=aliases,
        scratch_shapes=[pltpu.VMEM((4, t, w), BF16)],
        compiler_params=_params("parallel", "parallel"),
        name="ret_scan",
    )(*args)


def _gla_kernel(*refs, n_blocks, seqs, dvb, has_s0, emit_state, aliased):
    refs = list(refs)
    q_ref, k_ref, v_ref, z_ref, g2_ref, gb_ref = refs[:6]
    refs = refs[6:]
    if has_s0:
        s0_ref = refs[0]
        refs = refs[1:]
    if aliased:
        refs = refs[1:]
    o_ref = refs[0]
    refs = refs[1:]
    if emit_state:
        st_ref = refs[0]
        refs = refs[1:]
    qd_ref, kt_ref, a_ref, dec_ref, s_ref = refs

    c = CHUNK
    blk = GLA_BLK
    cpb = blk // c
    tot_blocks = seqs * n_blocks

    @pl.when(pl.program_id(2) == 0)
    def _():
        ii = lax.broadcasted_iota(jnp.int32, (blk, blk), 0)
        jj = lax.broadcasted_iota(jnp.int32, (blk, blk), 1)
        same = (ii // c) == (jj // c)
        masks = (jnp.logical_and(same, ii >= jj), jnp.logical_and(same, ii <= jj))
        tris = tuple(jnp.tile(jnp.where(m, 1.0, 0.0).astype(BF16), (1, 2)) for m in masks)

        group = 2 if tot_blocks % 2 == 0 else 1
        chains = [(j, d) for j in range(group) for d in (0, 1)]

        def build(gi, carry):
            bis = [gi * group + j for j in range(group)]
            r0s = [bi * blk if isinstance(bi, int) else pl.multiple_of(bi * blk, blk) for bi in bis]
            q = [q_ref[pl.ds(r0, blk), :].astype(F32) * (DK_GLA ** -0.5) for r0 in r0s]
            k = [k_ref[pl.ds(r0, blk), :].astype(F32) for r0 in r0s]
            z = [_dot(z_ref[pl.ds(r0s[j], blk), :], g2_ref[0, d]) + gb_ref[0, d] for j, d in chains]
            lg = [jnp.concatenate(_split_bf16(_log_sigmoid(zz) * (LOG2E / GLA_TAU)), axis=0) for zz in z]
            g = [_dot(tris[d], lg[n]) for n, (j, d) in enumerate(chains)]
            qds, kds, cross = [], [], []
            for n, (j, d) in enumerate(chains):
                edge = c - 1 if d == 0 else 0
                gt = [g[n][ci * c + edge:ci * c + edge + 1, :] for ci in range(cpb)]
                g_tot = jnp.concatenate([jnp.broadcast_to(gt[ci], (c, DK_GLA)) for ci in range(cpb)], axis=0)
                qd_f32 = q[j] * jnp.exp2(g[n])
                qd = qd_f32.astype(BF16)
                kds.append((k[j] * jnp.exp2(-g[n])).astype(BF16))
                qds.append(qd)
                kt = k[j] * jnp.exp2(g_tot - g[n])
                scan = list(range(cpb)) if d == 0 else list(reversed(range(cpb)))
                pos = {ci: p for p, ci in enumerate(scan)}

                def span(lo, hi):
                    parts = [gt[scan[p]] for p in range(lo, hi)]
                    return functools.reduce(lambda x, y: x + y, parts) if parts else None

                def scaled(x, e):
                    return x if e is None else x * jnp.broadcast_to(jnp.exp2(e), x.shape)

                rows = lambda x, ci: x[ci * c:(ci + 1) * c, :]
                qd_ref[d, pl.ds(r0s[j], blk), :] = jnp.concatenate(
                    [scaled(rows(qd_f32, ci), span(0, pos[ci])) for ci in range(cpb)], axis=0).astype(BF16)
                kt_ref[d, pl.ds(r0s[j], blk), :] = jnp.concatenate(
                    [scaled(rows(kt, ci), span(pos[ci] + 1, cpb)) for ci in range(cpb)], axis=0).astype(BF16)
                dec_ref[d * tot_blocks + bis[j]] = jnp.exp2(jnp.broadcast_to(span(0, cpb), (8, DK_GLA)))
                zero_rows = jnp.zeros((c, DK_GLA), BF16)
                for p in range(1, cpb):
                    src = jnp.concatenate(
                        [scaled(rows(kt, cj), span(pos[cj] + 1, p)).astype(BF16) if pos[cj] < p else zero_rows
                         for cj in range(cpb)], axis=0)
                    cross.append((n, scan[p], qd[scan[p] * c:(scan[p] + 1) * c, :], src))
            a = [_dot_nt(qds[n], kds[n]) for n in range(len(chains))]
            cross = [(n, ci, _dot_nt(qrows, src)) for n, ci, qrows, src in cross]
            for j in range(group):
                total = jnp.where(masks[0], a[2 * j], 0.0) + jnp.where(masks[1], a[2 * j + 1], 0.0)
                for d in (0, 1):
                    by_chunk = {ci: sc for n, ci, sc in cross if n == 2 * j + d}
                    total = total + jnp.concatenate(
                        [by_chunk.get(ci, jnp.zeros((c, blk), F32)) for ci in range(cpb)], axis=0)
                a_ref[bis[j]] = total.astype(BF16)
            return carry

        _loop(tot_blocks // group, build, 0)

    def intra(bi, carry):
        r0 = bi * blk if isinstance(bi, int) else pl.multiple_of(bi * blk, blk)
        o_ref[pl.ds(r0, blk), :] = _dot(a_ref[bi], v_ref[pl.ds(r0, blk), :])
        return carry

    _loop(tot_blocks, intra, 0)

    runs = [(sq, d) for sq in range(seqs) for d in (0, 1)]
    for sq, d in runs:
        s_ref[2 * sq + d] = s0_ref[sq, 0, d, 0].T if has_s0 else jnp.zeros((dvb, DK_GLA), F32)

    inter = has_s0 or n_blocks > 1

    def advance(n, carry):
        ms = [sq * n_blocks + (n if d == 0 else n_blocks - 1 - n) for sq, d in runs]
        r0s = [m * blk if isinstance(m, int) else pl.multiple_of(m * blk, blk) for m in ms]
        upd = [_dot_tn(v_ref[pl.ds(r0s[n_], blk), :], kt_ref[d, pl.ds(r0s[n_], blk), :])
               for n_, (sq, d) in enumerate(runs)]
        for n_, (sq, d) in enumerate(runs):
            s = s_ref[2 * sq + d]
            if inter:
                o_ref[pl.ds(r0s[n_], blk), :] += _dot_nt(qd_ref[d, pl.ds(r0s[n_], blk), :], s.astype(BF16))
            s_ref[2 * sq + d] = s * dec_ref[d * tot_blocks + ms[n_]][0:1, :] + upd[n_]
        return carry

    _loop(n_blocks, advance, 0, unroll=2)
    if emit_state:
        for sq, d in runs:
            st_ref[sq, d, 0] = s_ref[2 * sq + d].T


def _gla_scan(proj, z1, g2, gb, t, batch, seqs, dvb, row_block0, s0, s0_layer, state_buf, state_layer):
    nd = DV_GLA // dvb
    kcol = GLA_DK_TOTAL // DK_GLA
    vcol = 2 * GLA_DK_TOTAL // dvb
    n_blocks = t // GLA_BLK
    rows = seqs * t
    in_specs = [
        pl.BlockSpec((rows, DK_GLA), lambda b, h, d: (row_block0 + b, h)),
        pl.BlockSpec((rows, DK_GLA), lambda b, h, d: (row_block0 + b, kcol + h)),
        pl.BlockSpec((rows, dvb), lambda b, h, d: (row_block0 + b, vcol + h * nd + d)),
        pl.BlockSpec((rows, 128), lambda b, h, d: (row_block0 + b, 0)),
        pl.BlockSpec((1, 2, 128, DK_GLA), lambda b, h, d: (h, 0, 0, 0)),
        pl.BlockSpec((1, 2, 1, DK_GLA), lambda b, h, d: (h, 0, 0, 0)),
    ]
    args = [proj, proj, proj, z1, g2, gb]
    if s0 is not None:
        assert seqs == 1
        in_specs.append(pl.BlockSpec((1, 1, 2, 1, DK_GLA, dvb), lambda b, h, d: (b, s0_layer, 0, h, 0, d)))
        args.append(s0)
    out_specs = [pl.BlockSpec((rows, dvb), lambda b, h, d: (b, h * nd + d))]
    out_shape = [jax.ShapeDtypeStruct((batch * t, GLA_DV_TOTAL), F32)]
    aliases = {}
    if state_layer is not None:
        out_specs.append(pl.BlockSpec((seqs, None, 2, 1, DK_GLA, dvb), lambda b, h, d: (b, state_layer, 0, h, 0, d)))
        out_shape.append(jax.ShapeDtypeStruct((batch, N_ODD, 2, H_GLA, DK_GLA, DV_GLA), F32))
        if state_buf is not None:
            aliases = {len(args): 1}
            in_specs.append(pl.BlockSpec(memory_space=pl.ANY))
            args.append(state_buf)
    return pl.pallas_call(
        functools.partial(_gla_kernel, n_blocks=n_blocks, seqs=seqs, dvb=dvb, has_s0=s0 is not None,
                          emit_state=state_layer is not None, aliased=bool(aliases)),
        grid=(batch // seqs, H_GLA, nd),
        in_specs=in_specs,
        out_specs=out_specs,
        out_shape=out_shape,
        input_output_aliases=aliases,
        scratch_shapes=[
            pltpu.VMEM((2, rows, DK_GLA), BF16),
            pltpu.VMEM((2, rows, DK_GLA), BF16),
            pltpu.VMEM((rows // GLA_BLK, GLA_BLK, GLA_BLK), BF16),
            pltpu.VMEM((2 * rows // GLA_BLK, 8, DK_GLA), F32),
            pltpu.VMEM((2 * seqs, dvb, DK_GLA), F32),
        ],
        compiler_params=_params("parallel", "parallel", "arbitrary"),
        name="gla_scan",
    )(*args)


def _group_specs(tm, width, col_block=0):
    p_tiles = N_PROMPT // tm
    return [pl.BlockSpec((tm, width), lambda *idx: (jnp.minimum(idx[-1], p_tiles - 1), col_block)),
            pl.BlockSpec((tm, width), lambda *idx: (jnp.maximum(idx[-1] - p_tiles, 0), col_block))]


def _out_proj_kernel(*refs, n_heads, width, with_pool):
    refs = list(refs)
    if with_pool:
        u_ref, up_ref, un_ref, b0_ref, bp_ref, bn_ref, pw_ref, psc_ref = refs[:8]
        refs = refs[8:]
    op_ref, os_ref, gate_ref, gn_ref, w_ref = refs[:5]
    *x_refs, g1_ref, out_ref = refs[5:]
    xp_ref, xs_ref = x_refs if len(x_refs) == 2 else (x_refs[0], x_refs[0])
    i = pl.program_id(0)
    in_prompt = i < N_PROMPT // TM

    pooled = []
    if with_pool:
        sub = TM // POOL_TM
        tiles = []
        for s in range(sub):
            r0 = s * POOL_TM
            up = up_ref[...] if s == 0 else u_ref[r0 - POOL_HALO:r0, :]
            un = un_ref[...] if s == sub - 1 else u_ref[r0 + POOL_TM:r0 + POOL_TM + POOL_HALO, :]
            tiles.append((i * sub + s, u_ref[r0:r0 + POOL_TM, :], up, un))
        pooled = [jnp.concatenate(_pool_tiles(tiles, b0_ref, bp_ref, bn_ref, pw_ref, psc_ref), axis=0)]

    def run(o_ref, x_ref):
        parts = list(pooled)
        for hd in range(n_heads):
            cols = slice(hd * width, (hd + 1) * width)
            gt = gate_ref[:, cols].astype(F32)
            parts.append((_rms(o_ref[:, cols]) * gn_ref[:, cols] * _silu(gt)).astype(BF16))
        a = jnp.concatenate(parts, axis=1)
        out_ref[...] = x_ref[...] + g1_ref[0] * _dot(a, w_ref[...])

    @pl.when(in_prompt)
    def _():
        run(op_ref, xp_ref)

    @pl.when(jnp.logical_not(in_prompt))
    def _():
        run(os_ref, xs_ref)


def _out_proj(xs, mod, layer, w, w_layer, proj, gate_col_block, norm_g, o_prompt, o_sample, n_heads, pool=None):
    width = o_prompt.shape[1]
    x_specs = [pl.BlockSpec((TM, D_MODEL), lambda i: (i, 0))] if len(xs) == 1 else _group_specs(TM, D_MODEL)
    in_specs = _group_specs(TM, width) + [
        pl.BlockSpec((TM, width), lambda i: (i, gate_col_block)),
        pl.BlockSpec((1, width), lambda i: (0, 0)),
        pl.BlockSpec((None, D_MODEL, D_MODEL), lambda i: (w_layer, 0, 0)),
    ] + x_specs + [_mod_spec(layer, 2, TM)]
    args = [o_prompt, o_sample, proj, norm_g.reshape(1, width), w, *xs, mod]
    if pool is not None:
        b0, bp, bn = _pool_bands()
        halo_blocks = TM // POOL_HALO
        last = N_TOK // POOL_HALO - 1
        full = lambda shape: pl.BlockSpec(shape, lambda i: (0,) * len(shape))
        in_specs = [
            pl.BlockSpec((TM, POOL_WIDTH), lambda i: (i, 0)),
            pl.BlockSpec((POOL_HALO, POOL_WIDTH), lambda i: (jnp.maximum(i * halo_blocks - 1, 0), 0)),
            pl.BlockSpec((POOL_HALO, POOL_WIDTH), lambda i: (jnp.minimum((i + 1) * halo_blocks, last), 0)),
            full(b0.shape), full(bp.shape), full(bn.shape),
            pl.BlockSpec((None,) + pool[0].shape[1:], lambda i: (w_layer, 0, 0, 0)),
            pl.BlockSpec((None, 1, POOL_WIDTH), lambda i: (w_layer, 0, 0)),
        ] + in_specs
        args = [proj, proj, proj, b0, bp, bn, pool[0], pool[1].reshape(N_EVEN, 1, POOL_WIDTH)] + args
    return pl.pallas_call(
        functools.partial(_out_proj_kernel, n_heads=n_heads, width=width // n_heads, with_pool=pool is not None),
        grid=(N_TOK // TM,),
        in_specs=in_specs,
        out_specs=pl.BlockSpec((TM, D_MODEL), lambda i: (i, 0)),
        out_shape=jax.ShapeDtypeStruct((N_TOK, D_MODEL), F32),
        compiler_params=_params("parallel"),
        name="out_proj",
    )(*args)


def _ffn_kernel(x_ref, xp_ref, xn_ref, gn_ref, sh_ref, sc_ref, gt_ref, wa_ref, wb_ref, cv_ref,
                wd_ref, fg_ref, *rest, final):
    if final:
        op_ref, o_ref, h_ref, hh_ref = rest
    else:
        o_ref, h_ref, hh_ref = rest
    i = pl.program_id(0)
    f = pl.program_id(1)
    tm = TM_FFN

    def modnorm(x):
        return (_rms(x) * gn_ref[0] * (1.0 + sc_ref[0]) + sh_ref[0]).astype(BF16)

    @pl.when(f == 0)
    def _():
        h_ref[...] = modnorm(x_ref[...])
        hh_ref[...] = modnorm(jnp.concatenate([xp_ref[...], xn_ref[...]], axis=0))
        o_ref[...] = jnp.zeros_like(o_ref)

    a = _dot(h_ref[...], wa_ref[...])
    b = _dot(h_ref[...], wb_ref[...])
    a_halo = _dot(hh_ref[...], wa_ref[...])

    seq = jnp.where(i * tm < N_PROMPT, SEQ, DEC_SEQ)
    t = lax.broadcasted_iota(jnp.int32, (tm, TF), 0)
    pos = (i * tm + t) & (seq - 1)
    a_prev = jnp.where(t == 0, a_halo[7:8, :], pltpu.roll(a, 1, axis=0))
    a_prev = jnp.where(pos == 0, 0.0, a_prev)
    a_next = jnp.where(t == tm - 1, a_halo[8:9, :], pltpu.roll(a, tm - 1, axis=0))
    a_next = jnp.where(pos == seq - 1, 0.0, a_next)
    cv = cv_ref[:, pl.ds(pl.multiple_of(f * TF, TF), TF)]
    conv = a_prev * cv[0:1, :] + a * cv[1:2, :] + a_next * cv[2:3, :] + cv[3:4, :]
    o_ref[...] += _dot((_silu(conv) * b).astype(BF16), wd_ref[...])

    @pl.when(f == pl.num_programs(1) - 1)
    def _():
        y = x_ref[...] + gt_ref[0] * o_ref[...]
        if not final:
            o_ref[...] = y
        else:
            y = _rms(y) * fg_ref[...]
            in_prompt = i < N_PROMPT // tm

            @pl.when(in_prompt)
            def _():
                op_ref[...] = y

            @pl.when(jnp.logical_not(in_prompt))
            def _():
                o_ref[...] = y


def _ffn(x, mod, layer, norm_g, w_up, conv_w, conv_b, w_down, final_g, final):
    m = x.shape[0]
    nf = D_FF // TF
    tm = TM_FFN
    halo = 8
    last = m // halo - 1
    if final:
        p_tiles = N_PROMPT // tm
        out_specs = [pl.BlockSpec((tm, D_MODEL), lambda i, f: (jnp.minimum(i, p_tiles - 1), 0),
                                  pipeline_mode=pl.Buffered(1)),
                     pl.BlockSpec((tm, D_MODEL), lambda i, f: (jnp.maximum(i - p_tiles, 0), 0),
                                  pipeline_mode=pl.Buffered(1))]
        out_shape = [jax.ShapeDtypeStruct((N_PROMPT, D_MODEL), F32),
                     jax.ShapeDtypeStruct((N_SAMPLE, D_MODEL), F32)]
    else:
        out_specs = pl.BlockSpec((tm, D_MODEL), lambda i, f: (i, 0))
        out_shape = jax.ShapeDtypeStruct((m, D_MODEL), F32)
    return pl.pallas_call(
        functools.partial(_ffn_kernel, final=final),
        grid=(m // tm, nf),
        in_specs=[
            pl.BlockSpec((tm, D_MODEL), lambda i, f: (i, 0), pipeline_mode=pl.Buffered(1)),
            pl.BlockSpec((halo, D_MODEL), lambda i, f: (jnp.maximum(i * (tm // halo) - 1, 0), 0)),
            pl.BlockSpec((halo, D_MODEL), lambda i, f: (jnp.minimum((i + 1) * (tm // halo), last), 0)),
            pl.BlockSpec((1, 1, D_MODEL), lambda i, f: (layer, 0, 0)),
            _mod_spec(layer, 3, tm),
            _mod_spec(layer, 4, tm),
            _mod_spec(layer, 5, tm),
            pl.BlockSpec((None, D_MODEL, TF), lambda i, f: (layer, 0, f)),
            pl.BlockSpec((None, D_MODEL, TF), lambda i, f: (layer, 0, nf + f)),
            pl.BlockSpec((None, CONV_W + 1, D_FF), lambda i, f: (layer, 0, 0)),
            pl.BlockSpec((None, TF, D_MODEL), lambda i, f: (layer, f, 0)),
            pl.BlockSpec((1, D_MODEL), lambda i, f: (0, 0)),
        ],
        out_specs=out_specs,
        out_shape=out_shape,
        scratch_shapes=[
            pltpu.VMEM((tm, D_MODEL), BF16),
            pltpu.VMEM((2 * halo, D_MODEL), BF16),
        ],
        compiler_params=_params("arbitrary", "arbitrary"),
        name="ffn",
    )(x, x, x, norm_g, mod, mod, mod, w_up, w_up,
      jnp.concatenate([conv_w, conv_b.reshape(DEPTH, 1, D_FF)], axis=1), w_down, final_g.reshape(1, D_MODEL))


def _gla_gate_weights(gw2, gb):
    r = GLA_GATE_RANK
    g2 = jnp.zeros((2, 128, GLA_DK_TOTAL), F32)
    g2 = g2.at[0, 0:r].set(gw2[0]).at[1, r:2 * r].set(gw2[1])
    g2 = g2.reshape(2, 128, H_GLA, DK_GLA).transpose(2, 0, 1, 3).astype(BF16)
    gbh = gb.reshape(2, H_GLA, 1, DK_GLA).transpose(1, 0, 2, 3)
    return g2, gbh


def kernel(x_prompt, x_sample, state_ret, state_gla, c, c_ctx, ada_w, ada_b, norm1_g, norm2_g,
           even_w_in, pool_w, pool_scale, ret_decay, ret_norm_g, even_w_out, odd_w_in, gla_gw1,
           gla_gw2, gla_gb, gla_norm_g, odd_w_out, ffn_w_up, ffn_conv_w, ffn_conv_b, ffn_w_down, final_g):
    xs = (x_prompt.reshape(N_PROMPT, D_MODEL), x_sample.reshape(N_SAMPLE, D_MODEL))
    cvec = jnp.concatenate([c_ctx[None, :], c, jnp.zeros((COND_PAD - N_COND, D_MODEL), F32)], axis=0)
    mod = _ada_mod(cvec, ada_w, ada_b)
    n1 = norm1_g.reshape(DEPTH, 1, D_MODEL)
    n2 = norm2_g.reshape(DEPTH, 1, D_MODEL)
    rope = _rope_tables(DEC_SEQ)
    p_blocks = N_PROMPT // DEC_SEQ

    w_even_in, w_even_out = even_w_in.astype(BF16), even_w_out.astype(BF16)
    w_odd_in, w_odd_out = odd_w_in.astype(BF16), odd_w_out.astype(BF16)
    w_up, w_down, w_pool = ffn_w_up.astype(BF16), ffn_w_down.astype(BF16), pool_w.astype(BF16)
    w_gate1 = jnp.concatenate([gla_gw1[:, 0], gla_gw1[:, 1],
                               jnp.zeros((N_ODD, D_MODEL, 128 - 2 * GLA_GATE_RANK), F32)], axis=2).astype(BF16)

    ret_states = gla_states = None
    for l in range(DEPTH):
        if l % 2 == 0:
            i = l // 2
            proj = _in_proj(xs, n1, mod, l, w_even_in, i, IN_SPLIT)
            o_p, ret_states = _ret_scan(proj, ret_decay[i], SEQ, BATCH, H_RET, 0, None, None, 0, ret_states, i)
            (o_s,) = _ret_scan(proj, ret_decay[i], DEC_SEQ, DEC_BATCH, 1, p_blocks, rope, state_ret, i, None, None)
            x = _out_proj(xs, mod, l, w_even_out, i, proj, EVEN_IN // RET_WIDTH - 1,
                          ret_norm_g[i], o_p, o_s, H_RET, (w_pool, pool_scale))
        else:
            j = l // 2
            proj = _in_proj(xs, n1, mod, l, w_odd_in, j, IN_SPLIT)
            z1 = _in_proj(xs, n1, mod, l, w_gate1, j, 1)
            g2, gbh = _gla_gate_weights(gla_gw2[j], gla_gb[j])
            o_p, gla_states = _gla_scan(proj, z1, g2, gbh, SEQ, BATCH, GLA_PROMPT_SEQS, DV_GLA, 0, None, 0,
                                        gla_states, j)
            (o_s,) = _gla_scan(proj, z1, g2, gbh, DEC_SEQ, DEC_BATCH, 1, DV_GLA, p_blocks, state_gla, j,
                               None, None)
            x = _out_proj(xs, mod, l, w_odd_out, j, proj, ODD_IN // GLA_DV_TOTAL - 1,
                          gla_norm_g[j], o_p, o_s, H_GLA)
        xs = _ffn(x, mod, l, n2, w_up, ffn_conv_w, ffn_conv_b, w_down, final_g, l == DEPTH - 1)
        xs = tuple(xs) if l == DEPTH - 1 else (xs,)

    y_prompt = xs[0].reshape(BATCH, SEQ, D_MODEL)
    y_sample = xs[1].reshape(DEC_BATCH, DEC_SEQ, D_MODEL)
    return (y_prompt, y_sample, ret_states, gla_states)
```

```python
import functools

import numpy as np
import jax
import jax.numpy as jnp
from jax import lax
from jax.experimental import pallas as pl
from jax.experimental.pallas import tpu as pltpu

F32 = jnp.float32
BF16 = jnp.bfloat16

D_MODEL = 2048
BATCH = 16
SEQ = 256
DEPTH = 4
DEC_BATCH = 2
DEC_SEQ = 4096
GRID_W = 64
N_EVEN = (DEPTH + 1) // 2
N_ODD = DEPTH // 2
POOL_WIDTH = D_MODEL // 2
POOL_GROUPS = 4
POOL_GC = POOL_WIDTH // POOL_GROUPS
POOL_WINDOWS = (2, 4, 8, 16)
RET_WIDTH = D_MODEL // 2
H_RET = 8
DK_RET = RET_WIDTH // H_RET
DV_RET = RET_WIDTH // H_RET
ROPE_BASE = 10000.0
H_GLA = 4
GLA_DK_TOTAL = D_MODEL // 2
GLA_DV_TOTAL = D_MODEL
DK_GLA = GLA_DK_TOTAL // H_GLA
DV_GLA = GLA_DV_TOTAL // H_GLA
GLA_GATE_RANK = 16
GLA_TAU = 16.0
D_FF = 5632
CONV_W = 3
CHUNK = 64
EPS = 1e-6
EVEN_IN = POOL_WIDTH + 4 * RET_WIDTH
ODD_IN = 2 * GLA_DK_TOTAL + 2 * GLA_DV_TOTAL

N_PROMPT = BATCH * SEQ
N_SAMPLE = DEC_BATCH * DEC_SEQ
N_TOK = N_PROMPT + N_SAMPLE
N_COND = 1 + DEC_BATCH
COND_PAD = 8

VMEM_LIMIT = 56 * 1024 * 1024

TM = 512
IN_CHUNK = 512
TM_FFN = 1024
TF = 512
POOL_TM = 256
POOL_HALO = 16
ADA_TN = 1024
RET_CHUNK = 256
GLA_BLK = 256
GLA_PROMPT_SEQS = 2


def _params(*sem):
    return pltpu.CompilerParams(dimension_semantics=sem, vmem_limit_bytes=VMEM_LIMIT)


def _cond_of_tile(i, tm):
    r0 = i * tm
    return jnp.where(r0 < N_PROMPT, 0, 1 + (r0 - N_PROMPT) // DEC_SEQ)


LOG2E = 1.4426950408889634


def _log_sigmoid(x):
    return jnp.minimum(x, 0.0) - jnp.log(1.0 + jnp.exp2(jnp.abs(x) * (-LOG2E)))


def _silu(x):
    return x * jax.nn.sigmoid(x)


def _rms(x):
    return x * lax.rsqrt(jnp.mean(x * x, axis=-1, keepdims=True) + EPS)


def _dot(a, b):
    return jnp.dot(a, b, preferred_element_type=F32)


def _dot_nt(a, b):
    return lax.dot_general(a, b, (((1,), (1,)), ((), ())), preferred_element_type=F32)


def _dot_tn(a, b):
    return lax.dot_general(a, b, (((0,), (0,)), ((), ())), preferred_element_type=F32)


def _split_bf16(x):
    hi = x.astype(BF16)
    lo = (x - hi.astype(F32)).astype(BF16)
    return hi, lo


def _loop(n, body, init, unroll=1):
    if n == 1:
        return body(0, init)
    return lax.fori_loop(0, n, body, init, unroll=unroll)


def _ada_kernel(c_ref, w_ref, b_ref, o_ref):
    s = _silu(c_ref[...]).astype(BF16)
    o_ref[0] = _dot(s, w_ref[0].astype(BF16)) + b_ref[0]


def _ada_mod(cvec, ada_w, ada_b):
    n = 6 * D_MODEL
    mod = pl.pallas_call(
        _ada_kernel,
        grid=(DEPTH, n // ADA_TN),
        in_specs=[
            pl.BlockSpec((COND_PAD, D_MODEL), lambda l, j: (0, 0)),
            pl.BlockSpec((1, D_MODEL, ADA_TN), lambda l, j: (l, 0, j)),
            pl.BlockSpec((1, 1, ADA_TN), lambda l, j: (l, 0, j)),
        ],
        out_specs=pl.BlockSpec((1, COND_PAD, ADA_TN), lambda l, j: (l, 0, j)),
        out_shape=jax.ShapeDtypeStruct((DEPTH, COND_PAD, n), F32),
        compiler_params=_params("parallel", "parallel"),
        name="ada_mod",
    )(cvec, ada_w, ada_b.reshape(DEPTH, 1, n))
    mod = mod[:, :N_COND].reshape(DEPTH, N_COND, 6, D_MODEL).transpose(0, 2, 1, 3)
    return mod.reshape(DEPTH * 6 * N_COND, 1, D_MODEL)


def _mod_spec(layer, part, tm, row_axis=0):
    base = (layer * 6 + part) * N_COND
    return pl.BlockSpec((1, 1, D_MODEL), lambda *idx: (base + _cond_of_tile(idx[row_axis], tm), 0, 0))


def _in_proj_kernel(*refs, n_x, n_w):
    x_refs, (g_ref, sh_ref, sc_ref), refs = refs[:n_x], refs[n_x:n_x + 3], refs[n_x + 3:]
    w_refs, (o_ref, h_ref) = refs[:n_w], refs[n_w:]

    if len(x_refs) == 1:
        x = x_refs[0][...]
    else:
        x = jnp.where(pl.program_id(0) < N_PROMPT // TM, x_refs[0][...], x_refs[1][...])
    h = _rms(x) * g_ref[0]
    h_ref[...] = (h * (1.0 + sc_ref[0]) + sh_ref[0]).astype(BF16)
    col = 0
    for w_ref in w_refs:
        for c0 in range(0, w_ref.shape[1], IN_CHUNK):
            c1 = min(c0 + IN_CHUNK, w_ref.shape[1])
            o_ref[:, col + c0:col + c1] = _dot(h_ref[...], w_ref[:, c0:c1]).astype(o_ref.dtype)
        col += w_ref.shape[1]


def _in_proj(xs, norm_g, mod, layer, ws, w_layer):
    n = sum(w.shape[2] for w in ws)
    x_specs = [pl.BlockSpec((TM, D_MODEL), lambda i: (i, 0))] if len(xs) == 1 else _group_specs(TM, D_MODEL)
    return pl.pallas_call(
        functools.partial(_in_proj_kernel, n_x=len(xs), n_w=len(ws)),
        grid=(N_TOK // TM,),
        in_specs=x_specs + [
            pl.BlockSpec((1, 1, D_MODEL), lambda i: (layer, 0, 0)),
            _mod_spec(layer, 0, TM),
            _mod_spec(layer, 1, TM),
        ] + [pl.BlockSpec((None, D_MODEL, w.shape[2]), lambda i: (w_layer, 0, 0), pipeline_mode=pl.Buffered(1))
             for w in ws],
        out_specs=pl.BlockSpec((TM, n), lambda i: (i, 0)),
        out_shape=jax.ShapeDtypeStruct((N_TOK, n), BF16),
        scratch_shapes=[pltpu.VMEM((TM, D_MODEL), BF16)],
        compiler_params=_params("arbitrary"),
        name="in_proj",
    )(*xs, norm_g, mod, mod, *ws)


def _pool_bands():
    t = POOL_TM
    b0 = np.zeros((POOL_GROUPS, t, t), np.float32)
    bp = np.zeros((POOL_GROUPS, t, POOL_HALO), np.float32)
    bn = np.zeros((POOL_GROUPS, t, POOL_HALO), np.float32)
    for g, win in enumerate(POOL_WINDOWS):
        for r in range(t):
            for s in range(r - win // 2, r + win - win // 2):
                if s < 0:
                    bp[g, r, s + POOL_HALO] = 1.0
                elif s >= t:
                    bn[g, r, s - t] = 1.0
                else:
                    b0[g, r, s] = 1.0
    return jnp.asarray(b0, BF16), jnp.asarray(bp, BF16), jnp.asarray(bn, BF16)


def _pool_tiles(tiles, b0_ref, bp_ref, bn_ref, pw_ref, sc_ref):
    tiles_per_seq = DEC_SEQ // POOL_TM
    t = lax.broadcasted_iota(jnp.int32, (POOL_TM, POOL_GC), 0)
    sums, cnts, us = [], [], []
    for tile, u, up, un in tiles:
        in_prompt = tile < N_PROMPT // POOL_TM
        pos = (tile - N_PROMPT // POOL_TM) % tiles_per_seq
        is_start = jnp.logical_or(in_prompt, pos == 0)
        is_end = jnp.logical_or(in_prompt, pos == tiles_per_seq - 1)
        up = jnp.where(is_start, jnp.zeros_like(up), up)
        un = jnp.where(is_end, jnp.zeros_like(un), un)
        for g, win in enumerate(POOL_WINDOWS):
            cols = slice(g * POOL_GC, (g + 1) * POOL_GC)
            us.append(u[:, cols])
            sums.append(_dot(b0_ref[g], u[:, cols]) + _dot(bp_ref[g], up[:, cols]) + _dot(bn_ref[g], un[:, cols]))
            cut_lo = jnp.where(is_start, jnp.maximum(win // 2 - t, 0), 0)
            cut_hi = jnp.where(is_end, jnp.maximum(t + (win - win // 2) - POOL_TM, 0), 0)
            cnts.append((win - cut_lo - cut_hi).astype(F32))
    pooled = [(s / cnt - u.astype(F32)).astype(BF16) for s, cnt, u in zip(sums, cnts, us)]
    ys = [_dot(p, pw_ref[n % POOL_GROUPS]) for n, p in enumerate(pooled)]
    out = []
    for j in range(len(tiles)):
        out.append(jnp.concatenate(
            [(ys[j * POOL_GROUPS + g] * sc_ref[:, g * POOL_GC:(g + 1) * POOL_GC]).astype(BF16)
             for g in range(POOL_GROUPS)], axis=1))
    return out


def _rope_tables(t):
    nf = DK_RET // 4
    rows = t // GRID_W
    r = jnp.repeat(jnp.arange(rows), GRID_W).astype(F32)
    col = jnp.tile(jnp.arange(GRID_W), rows).astype(F32)
    inv = ROPE_BASE ** (-jnp.arange(nf, dtype=F32) / nf)
    ar, ac = r[:, None] * inv, col[:, None] * inv
    cos = jnp.concatenate([jnp.cos(ar), jnp.cos(ar), jnp.cos(ac), jnp.cos(ac)], axis=1)
    sin = jnp.concatenate([-jnp.sin(ar), jnp.sin(ar), -jnp.sin(ac), jnp.sin(ac)], axis=1)
    return cos, sin


def _ret_kernel(*refs, n_chunks, heads, use_rope, has_s0, emit_state, aliased):
    refs = list(refs)
    dec_ref, q_ref, k_ref, v_ref = refs[:4]
    refs = refs[4:]
    if use_rope:
        cos_ref, sin_ref = refs[:2]
        refs = refs[2:]
    if has_s0:
        s0_ref = refs[0]
        refs = refs[1:]
    if aliased:
        refs = refs[1:]
    o_ref = refs[0]
    refs = refs[1:]
    if emit_state:
        st_ref = refs[0]
        refs = refs[1:]
    (qk_ref,) = refs

    c = RET_CHUNK
    nf = DK_RET // 4
    head0 = pl.program_id(1) * heads
    inter = has_s0 or n_chunks > 1

    row = lax.broadcasted_iota(jnp.int32, (c, DK_RET), 0).astype(F32)
    ii = lax.broadcasted_iota(jnp.int32, (c, c), 0)
    jj = lax.broadcasted_iota(jnp.int32, (c, c), 1)
    dist = (ii - jj).astype(F32)
    lane = lax.broadcasted_iota(jnp.int32, (c, DK_RET), 1)
    first_half = (lane % (2 * nf)) < nf

    def rope(x, r0):
        if not use_rope:
            return x
        partner = jnp.where(first_half, pltpu.roll(x, DK_RET - nf, axis=1), pltpu.roll(x, nf, axis=1))
        return x * cos_ref[pl.ds(r0, c), :] + partner * sin_ref[pl.ds(r0, c), :]

    for hh in range(heads):
        cols = slice(hh * DK_RET, (hh + 1) * DK_RET)

        def lam(d, shape):
            return _log_sigmoid(jnp.full(shape, dec_ref[d, head0 + hh], F32))

        lam_f, lam_b = lam(0, (c, DK_RET)), lam(1, (c, DK_RET))
        dq_f = jnp.exp((row + 1.0) * lam_f)
        dk_f = jnp.exp((c - 1.0 - row) * lam_f)
        dq_b = jnp.exp((c - row) * lam_b)
        dk_b = jnp.exp(row * lam_b)
        cdec_f = jnp.exp(float(c) * lam(0, (DK_RET, DV_RET)))
        cdec_b = jnp.exp(float(c) * lam(1, (DK_RET, DV_RET)))
        dmat = jnp.where(ii > jj, jnp.exp(dist * lam(0, (c, c))),
                         jnp.where(ii < jj, jnp.exp(-dist * lam(1, (c, c))), 2.0))

        group = 2 if n_chunks % 2 == 0 else 1

        def prepare(gi, carry):
            ns = [gi * group + j for j in range(group)]
            r0s = [n * c if isinstance(n, int) else pl.multiple_of(n * c, c) for n in ns]
            q = [rope(q_ref[pl.ds(r0, c), cols].astype(F32), r0) for r0 in r0s]
            k = [rope(k_ref[pl.ds(r0, c), cols].astype(F32) * (DK_RET ** -0.5), r0) for r0 in r0s]
            a = [(_dot_nt(q[j].astype(BF16), k[j].astype(BF16)) * dmat).astype(BF16) for j in range(group)]
            for j, r0 in enumerate(r0s):
                qk_ref[0, pl.ds(r0, c), cols] = (q[j] * dq_f).astype(BF16)
                qk_ref[1, pl.ds(r0, c), cols] = (q[j] * dq_b).astype(BF16)
                qk_ref[2, pl.ds(r0, c), cols] = (k[j] * dk_f).astype(BF16)
                qk_ref[3, pl.ds(r0, c), cols] = (k[j] * dk_b).astype(BF16)
            o = [_dot(a[j], v_ref[pl.ds(r0s[j], c), cols]) for j in range(group)]
            for j, r0 in enumerate(r0s):
                o_ref[pl.ds(r0, c), cols] = o[j]
            return carry

        def advance(n, states):
            ms = (n, n_chunks - 1 - n)
            r0s = [m * c if isinstance(m, int) else pl.multiple_of(m * c, c) for m in ms]
            upd = [_dot_tn(qk_ref[2 + d, pl.ds(r0s[d], c), cols], v_ref[pl.ds(r0s[d], c), cols]) for d in (0, 1)]
            out = []
            for d, cdec in ((0, cdec_f), (1, cdec_b)):
                if inter:
                    o_ref[pl.ds(r0s[d], c), cols] += _dot(qk_ref[d, pl.ds(r0s[d], c), cols],
                                                          states[d].astype(BF16))
                out.append(cdec * states[d] + upd[d])
            return tuple(out)

        _loop(n_chunks // group, prepare, 0)
        zero = jnp.zeros((DK_RET, DV_RET), F32)
        init = (s0_ref[0, 0, 0, hh], s0_ref[0, 0, 1, hh]) if has_s0 else (zero, zero)
        s_f, s_b = _loop(n_chunks, advance, init, unroll=2)
        if emit_state:
            st_ref[0, 0, hh] = s_f
            st_ref[0, 1, hh] = s_b


def _ret_scan(proj, decay, t, batch, heads, row_block0, rope, s0, s0_layer, state_buf, state_layer):
    w = heads * DK_RET
    col0 = POOL_WIDTH // w
    per = RET_WIDTH // w
    in_specs = [
        pl.BlockSpec(memory_space=pltpu.SMEM),
        pl.BlockSpec((t, w), lambda b, h: (row_block0 + b, col0 + h)),
        pl.BlockSpec((t, w), lambda b, h: (row_block0 + b, col0 + per + h)),
        pl.BlockSpec((t, w), lambda b, h: (row_block0 + b, col0 + 2 * per + h)),
    ]
    args = [decay, proj, proj, proj]
    if rope is not None:
        assert heads == 1
        in_specs += [pl.BlockSpec((t, DK_RET), lambda b, h: (0, 0))] * 2
        args += list(rope)
    if s0 is not None:
        in_specs.append(pl.BlockSpec((1, 1, 2, heads, DK_RET, DV_RET), lambda b, h: (b, s0_layer, 0, h, 0, 0)))
        args.append(s0)
    out_specs = [pl.BlockSpec((t, w), lambda b, h: (b, h))]
    out_shape = [jax.ShapeDtypeStruct((batch * t, RET_WIDTH), F32)]
    aliases = {}
    if state_layer is not None:
        out_specs.append(pl.BlockSpec((1, None, 2, heads, DK_RET, DV_RET), lambda b, h: (b, state_layer, 0, h, 0, 0)))
        out_shape.append(jax.ShapeDtypeStruct((batch, N_EVEN, 2, H_RET, DK_RET, DV_RET), F32))
        if state_buf is not None:
            aliases = {len(args): 1}
            in_specs.append(pl.BlockSpec(memory_space=pl.ANY))
            args.append(state_buf)
    return pl.pallas_call(
        functools.partial(_ret_kernel, n_chunks=t // RET_CHUNK, heads=heads, use_rope=rope is not None,
                          has_s0=s0 is not None, emit_state=state_layer is not None, aliased=bool(aliases)),
        grid=(batch, H_RET // heads),
        in_specs=in_specs,
        out_specs=out_specs,
        out_shape=out_shape,
        input_output_aliases=aliases,
        scratch_shapes=[pltpu.VMEM((4, t, w), BF16)],
        compiler_params=_params("parallel", "parallel"),
        name="ret_scan",
    )(*args)


def _gla_kernel(*refs, n_blocks, seqs, dvb, has_s0, emit_state, aliased):
    refs = list(refs)
    q_ref, k_ref, v_ref, z_ref, g2_ref, gb_ref = refs[:6]
    refs = refs[6:]
    if has_s0:
        s0_ref = refs[0]
        refs = refs[1:]
    if aliased:
        refs = refs[1:]
    o_ref = refs[0]
    refs = refs[1:]
    if emit_state:
        st_ref = refs[0]
        refs = refs[1:]
    qd_ref, kt_ref, a_ref, dec_ref, s_ref = refs

    c = CHUNK
    blk = GLA_BLK
    cpb = blk // c
    tot_blocks = seqs * n_blocks

    @pl.when(pl.program_id(2) == 0)
    def _():
        ii = lax.broadcasted_iota(jnp.int32, (blk, blk), 0)
        jj = lax.broadcasted_iota(jnp.int32, (blk, blk), 1)
        same = (ii // c) == (jj // c)
        masks = (jnp.logical_and(same, ii >= jj), jnp.logical_and(same, ii <= jj))
        tris = tuple(jnp.tile(jnp.where(m, 1.0, 0.0).astype(BF16), (1, 2)) for m in masks)

        group = 2 if tot_blocks % 2 == 0 else 1
        chains = [(j, d) for j in range(group) for d in (0, 1)]

        def build(gi, carry):
            bis = [gi * group + j for j in range(group)]
            r0s = [bi * blk if isinstance(bi, int) else pl.multiple_of(bi * blk, blk) for bi in bis]
            q = [q_ref[pl.ds(r0, blk), :].astype(F32) * (DK_GLA ** -0.5) for r0 in r0s]
            k = [k_ref[pl.ds(r0, blk), :].astype(F32) for r0 in r0s]
            z = [_dot(z_ref[pl.ds(r0s[j], blk), :], g2_ref[0, d]) + gb_ref[0, d] for j, d in chains]
            lg = [jnp.concatenate(_split_bf16(_log_sigmoid(zz) * (LOG2E / GLA_TAU)), axis=0) for zz in z]
            g = [_dot(tris[d], lg[n]) for n, (j, d) in enumerate(chains)]
            qds, kds, cross = [], [], []
            for n, (j, d) in enumerate(chains):
                edge = c - 1 if d == 0 else 0
                gt = [g[n][ci * c + edge:ci * c + edge + 1, :] for ci in range(cpb)]
                g_tot = jnp.concatenate([jnp.broadcast_to(gt[ci], (c, DK_GLA)) for ci in range(cpb)], axis=0)
                qd_f32 = q[j] * jnp.exp2(g[n])
                qd = qd_f32.astype(BF16)
                kds.append((k[j] * jnp.exp2(-g[n])).astype(BF16))
                qds.append(qd)
                kt = k[j] * jnp.exp2(g_tot - g[n])
                scan = list(range(cpb)) if d == 0 else list(reversed(range(cpb)))
                pos = {ci: p for p, ci in enumerate(scan)}

                def span(lo, hi):
                    parts = [gt[scan[p]] for p in range(lo, hi)]
                    return functools.reduce(lambda x, y: x + y, parts) if parts else None

                def scaled(x, e):
                    return x if e is None else x * jnp.broadcast_to(jnp.exp2(e), x.shape)

                rows = lambda x, ci: x[ci * c:(ci + 1) * c, :]
                qd_ref[d, pl.ds(r0s[j], blk), :] = jnp.concatenate(
                    [scaled(rows(qd_f32, ci), span(0, pos[ci])) for ci in range(cpb)], axis=0).astype(BF16)
                kt_ref[d, pl.ds(r0s[j], blk), :] = jnp.concatenate(
                    [scaled(rows(kt, ci), span(pos[ci] + 1, cpb)) for ci in range(cpb)], axis=0).astype(BF16)
                dec_ref[d * tot_blocks + bis[j]] = jnp.exp2(jnp.broadcast_to(span(0, cpb), (8, DK_GLA)))
                zero_rows = jnp.zeros((c, DK_GLA), BF16)
                for p in range(1, cpb):
                    src = jnp.concatenate(
                        [scaled(rows(kt, cj), span(pos[cj] + 1, p)).astype(BF16) if pos[cj] < p else zero_rows
                         for cj in range(cpb)], axis=0)
                    cross.append((n, scan[p], qd[scan[p] * c:(scan[p] + 1) * c, :], src))
            a = [_dot_nt(qds[n], kds[n]) for n in range(len(chains))]
            cross = [(n, ci, _dot_nt(qrows, src)) for n, ci, qrows, src in cross]
            for j in range(group):
                total = jnp.where(masks[0], a[2 * j], 0.0) + jnp.where(masks[1], a[2 * j + 1], 0.0)
                for d in (0, 1):
                    by_chunk = {ci: sc for n, ci, sc in cross if n == 2 * j + d}
                    total = total + jnp.concatenate(
                        [by_chunk.get(ci, jnp.zeros((c, blk), F32)) for ci in range(cpb)], axis=0)
                a_ref[bis[j]] = total.astype(BF16)
            return carry

        _loop(tot_blocks // group, build, 0)

    def intra(bi, carry):
        r0 = bi * blk if isinstance(bi, int) else pl.multiple_of(bi * blk, blk)
        o_ref[pl.ds(r0, blk), :] = _dot(a_ref[bi], v_ref[pl.ds(r0, blk), :])
        return carry

    _loop(tot_blocks, intra, 0)

    runs = [(sq, d) for sq in range(seqs) for d in (0, 1)]
    for sq, d in runs:
        s_ref[2 * sq + d] = s0_ref[sq, 0, d, 0].T if has_s0 else jnp.zeros((dvb, DK_GLA), F32)

    inter = has_s0 or n_blocks > 1

    def advance(n, carry):
        ms = [sq * n_blocks + (n if d == 0 else n_blocks - 1 - n) for sq, d in runs]
        r0s = [m * blk if isinstance(m, int) else pl.multiple_of(m * blk, blk) for m in ms]
        upd = [_dot_tn(v_ref[pl.ds(r0s[n_], blk), :], kt_ref[d, pl.ds(r0s[n_], blk), :])
               for n_, (sq, d) in enumerate(runs)]
        for n_, (sq, d) in enumerate(runs):
            s = s_ref[2 * sq + d]
            if inter:
                o_ref[pl.ds(r0s[n_], blk), :] += _dot_nt(qd_ref[d, pl.ds(r0s[n_], blk), :], s.astype(BF16))
            s_ref[2 * sq + d] = s * dec_ref[d * tot_blocks + ms[n_]][0:1, :] + upd[n_]
        return carry

    _loop(n_blocks, advance, 0, unroll=2)
    if emit_state:
        for sq, d in runs:
            st_ref[sq, d, 0] = s_ref[2 * sq + d].T


def _gla_scan(proj, g2, gb, t, batch, seqs, dvb, row_block0, s0, s0_layer, state_buf, state_layer):
    nd = DV_GLA // dvb
    kcol = GLA_DK_TOTAL // DK_GLA
    vcol = 2 * GLA_DK_TOTAL // dvb
    n_blocks = t // GLA_BLK
    rows = seqs * t
    in_specs = [
        pl.BlockSpec((rows, DK_GLA), lambda b, h, d: (row_block0 + b, h)),
        pl.BlockSpec((rows, DK_GLA), lambda b, h, d: (row_block0 + b, kcol + h)),
        pl.BlockSpec((rows, dvb), lambda b, h, d: (row_block0 + b, vcol + h * nd + d)),
        pl.BlockSpec((rows, 128), lambda b, h, d: (row_block0 + b, ODD_IN // 128)),
        pl.BlockSpec((1, 2, 128, DK_GLA), lambda b, h, d: (h, 0, 0, 0)),
        pl.BlockSpec((1, 2, 1, DK_GLA), lambda b, h, d: (h, 0, 0, 0)),
    ]
    args = [proj, proj, proj, proj, g2, gb]
    if s0 is not None:
        assert seqs == 1
        in_specs.append(pl.BlockSpec((1, 1, 2, 1, DK_GLA, dvb), lambda b, h, d: (b, s0_layer, 0, h, 0, d)))
        args.append(s0)
    out_specs = [pl.BlockSpec((rows, dvb), lambda b, h, d: (b, h * nd + d))]
    out_shape = [jax.ShapeDtypeStruct((batch * t, GLA_DV_TOTAL), F32)]
    aliases = {}
    if state_layer is not None:
        out_specs.append(pl.BlockSpec((seqs, None, 2, 1, DK_GLA, dvb), lambda b, h, d: (b, state_layer, 0, h, 0, d)))
        out_shape.append(jax.ShapeDtypeStruct((batch, N_ODD, 2, H_GLA, DK_GLA, DV_GLA), F32))
        if state_buf is not None:
            aliases = {len(args): 1}
            in_specs.append(pl.BlockSpec(memory_space=pl.ANY))
            args.append(state_buf)
    return pl.pallas_call(
        functools.partial(_gla_kernel, n_blocks=n_blocks, seqs=seqs, dvb=dvb, has_s0=s0 is not None,
                          emit_state=state_layer is not None, aliased=bool(aliases)),
        grid=(batch // seqs, H_GLA, nd),
        in_specs=in_specs,
        out_specs=out_specs,
        out_shape=out_shape,
        input_output_aliases=aliases,
        scratch_shapes=[
            pltpu.VMEM((2, rows, DK_GLA), BF16),
            pltpu.VMEM((2, rows, DK_GLA), BF16),
            pltpu.VMEM((rows // GLA_BLK, GLA_BLK, GLA_BLK), BF16),
            pltpu.VMEM((2 * rows // GLA_BLK, 8, DK_GLA), F32),
            pltpu.VMEM((2 * seqs, dvb, DK_GLA), F32),
        ],
        compiler_params=_params("parallel", "parallel", "arbitrary"),
        name="gla_scan",
    )(*args)


def _group_specs(tm, width, col_block=0):
    p_tiles = N_PROMPT // tm
    return [pl.BlockSpec((tm, width), lambda *idx: (jnp.minimum(idx[-1], p_tiles - 1), col_block)),
            pl.BlockSpec((tm, width), lambda *idx: (jnp.maximum(idx[-1] - p_tiles, 0), col_block))]


def _out_proj_kernel(*refs, n_heads, width, with_pool):
    refs = list(refs)
    if with_pool:
        u_ref, up_ref, un_ref, b0_ref, bp_ref, bn_ref, pw_ref, psc_ref = refs[:8]
        refs = refs[8:]
    op_ref, os_ref, gate_ref, gn_ref, w_ref = refs[:5]
    *x_refs, g1_ref, out_ref = refs[5:]
    xp_ref, xs_ref = x_refs if len(x_refs) == 2 else (x_refs[0], x_refs[0])
    i = pl.program_id(0)
    in_prompt = i < N_PROMPT // TM

    pooled = []
    if with_pool:
        sub = TM // POOL_TM
        tiles = []
        for s in range(sub):
            r0 = s * POOL_TM
            up = up_ref[...] if s == 0 else u_ref[r0 - POOL_HALO:r0, :]
            un = un_ref[...] if s == sub - 1 else u_ref[r0 + POOL_TM:r0 + POOL_TM + POOL_HALO, :]
            tiles.append((i * sub + s, u_ref[r0:r0 + POOL_TM, :], up, un))
        pooled = [jnp.concatenate(_pool_tiles(tiles, b0_ref, bp_ref, bn_ref, pw_ref, psc_ref), axis=0)]

    def run(o_ref, x_ref):
        parts = list(pooled)
        for hd in range(n_heads):
            cols = slice(hd * width, (hd + 1) * width)
            gt = gate_ref[:, cols].astype(F32)
            parts.append((_rms(o_ref[:, cols]) * gn_ref[:, cols] * _silu(gt)).astype(BF16))
        a = jnp.concatenate(parts, axis=1)
        out_ref[...] = x_ref[...] + g1_ref[0] * _dot(a, w_ref[...])

    @pl.when(in_prompt)
    def _():
        run(op_ref, xp_ref)

    @pl.when(jnp.logical_not(in_prompt))
    def _():
        run(os_ref, xs_ref)


def _out_proj(xs, mod, layer, w, w_layer, proj, gate_col_block, norm_g, o_prompt, o_sample, n_heads, pool=None):
    width = o_prompt.shape[1]
    x_specs = [pl.BlockSpec((TM, D_MODEL), lambda i: (i, 0))] if len(xs) == 1 else _group_specs(TM, D_MODEL)
    in_specs = _group_specs(TM, width) + [
        pl.BlockSpec((TM, width), lambda i: (i, gate_col_block)),
        pl.BlockSpec((1, width), lambda i: (0, 0)),
        pl.BlockSpec((None, D_MODEL, D_MODEL), lambda i: (w_layer, 0, 0)),
    ] + x_specs + [_mod_spec(layer, 2, TM)]
    args = [o_prompt, o_sample, proj, norm_g.reshape(1, width), w, *xs, mod]
    if pool is not None:
        b0, bp, bn = _pool_bands()
        halo_blocks = TM // POOL_HALO
        last = N_TOK // POOL_HALO - 1
        full = lambda shape: pl.BlockSpec(shape, lambda i: (0,) * len(shape))
        in_specs = [
            pl.BlockSpec((TM, POOL_WIDTH), lambda i: (i, 0)),
            pl.BlockSpec((POOL_HALO, POOL_WIDTH), lambda i: (jnp.maximum(i * halo_blocks - 1, 0), 0)),
            pl.BlockSpec((POOL_HALO, POOL_WIDTH), lambda i: (jnp.minimum((i + 1) * halo_blocks, last), 0)),
            full(b0.shape), full(bp.shape), full(bn.shape),
            pl.BlockSpec((None,) + pool[0].shape[1:], lambda i: (w_layer, 0, 0, 0)),
            pl.BlockSpec((None, 1, POOL_WIDTH), lambda i: (w_layer, 0, 0)),
        ] + in_specs
        args = [proj, proj, proj, b0, bp, bn, pool[0], pool[1].reshape(N_EVEN, 1, POOL_WIDTH)] + args
    return pl.pallas_call(
        functools.partial(_out_proj_kernel, n_heads=n_heads, width=width // n_heads, with_pool=pool is not None),
        grid=(N_TOK // TM,),
        in_specs=in_specs,
        out_specs=pl.BlockSpec((TM, D_MODEL), lambda i: (i, 0)),
        out_shape=jax.ShapeDtypeStruct((N_TOK, D_MODEL), F32),
        compiler_params=_params("parallel"),
        name="out_proj",
    )(*args)


def _ffn_kernel(x_ref, xp_ref, xn_ref, gn_ref, sh_ref, sc_ref, gt_ref, wa_ref, wb_ref, cv_ref,
                wd_ref, fg_ref, *rest, final):
    if final:
        op_ref, o_ref, h_ref, hh_ref = rest
    else:
        o_ref, h_ref, hh_ref = rest
    i = pl.program_id(0)
    f = pl.program_id(1)
    tm = TM_FFN

    def modnorm(x):
        return (_rms(x) * gn_ref[0] * (1.0 + sc_ref[0]) + sh_ref[0]).astype(BF16)

    @pl.when(f == 0)
    def _():
        h_ref[...] = modnorm(x_ref[...])
        hh_ref[...] = modnorm(jnp.concatenate([xp_ref[...], xn_ref[...]], axis=0))
        o_ref[...] = jnp.zeros_like(o_ref)

    a = _dot(h_ref[...], wa_ref[...])
    b = _dot(h_ref[...], wb_ref[...])
    a_halo = _dot(hh_ref[...], wa_ref[...])

    seq = jnp.where(i * tm < N_PROMPT, SEQ, DEC_SEQ)
    t = lax.broadcasted_iota(jnp.int32, (tm, TF), 0)
    pos = (i * tm + t) & (seq - 1)
    a_prev = jnp.where(t == 0, a_halo[7:8, :], pltpu.roll(a, 1, axis=0))
    a_prev = jnp.where(pos == 0, 0.0, a_prev)
    a_next = jnp.where(t == tm - 1, a_halo[8:9, :], pltpu.roll(a, tm - 1, axis=0))
    a_next = jnp.where(pos == seq - 1, 0.0, a_next)
    cv = cv_ref[:, pl.ds(pl.multiple_of(f * TF, TF), TF)]
    conv = a_prev * cv[0:1, :] + a * cv[1:2, :] + a_next * cv[2:3, :] + cv[3:4, :]
    o_ref[...] += _dot((_silu(conv) * b).astype(BF16), wd_ref[...])

    @pl.when(f == pl.num_programs(1) - 1)
    def _():
        y = x_ref[...] + gt_ref[0] * o_ref[...]
        if not final:
            o_ref[...] = y
        else:
            y = _rms(y) * fg_ref[...]
            in_prompt = i < N_PROMPT // tm

            @pl.when(in_prompt)
            def _():
                op_ref[...] = y

            @pl.when(jnp.logical_not(in_prompt))
            def _():
                o_ref[...] = y


def _ffn(x, mod, layer, norm_g, w_up, conv_w, conv_b, w_down, final_g, final):
    m = x.shape[0]
    nf = D_FF // TF
    tm = TM_FFN
    halo = 8
    last = m // halo - 1
    if final:
        p_tiles = N_PROMPT // tm
        out_specs = [pl.BlockSpec((tm, D_MODEL), lambda i, f: (jnp.minimum(i, p_tiles - 1), 0),
                                  pipeline_mode=pl.Buffered(1)),
                     pl.BlockSpec((tm, D_MODEL), lambda i, f: (jnp.maximum(i - p_tiles, 0), 0),
                                  pipeline_mode=pl.Buffered(1))]
        out_shape = [jax.ShapeDtypeStruct((N_PROMPT, D_MODEL), F32),
                     jax.ShapeDtypeStruct((N_SAMPLE, D_MODEL), F32)]
    else:
        out_specs = pl.BlockSpec((tm, D_MODEL), lambda i, f: (i, 0))
        out_shape = jax.ShapeDtypeStruct((m, D_MODEL), F32)
    return pl.pallas_call(
        functools.partial(_ffn_kernel, final=final),
        grid=(m // tm, nf),
        in_specs=[
            pl.BlockSpec((tm, D_MODEL), lambda i, f: (i, 0), pipeline_mode=pl.Buffered(1)),
            pl.BlockSpec((halo, D_MODEL), lambda i, f: (jnp.maximum(i * (tm // halo) - 1, 0), 0)),
            pl.BlockSpec((halo, D_MODEL), lambda i, f: (jnp.minimum((i + 1) * (tm // halo), last), 0)),
            pl.BlockSpec((1, 1, D_MODEL), lambda i, f: (layer, 0, 0)),
            _mod_spec(layer, 3, tm),
            _mod_spec(layer, 4, tm),
            _mod_spec(layer, 5, tm),
            pl.BlockSpec((None, D_MODEL, TF), lambda i, f: (layer, 0, f)),
            pl.BlockSpec((None, D_MODEL, TF), lambda i, f: (layer, 0, nf + f)),
            pl.BlockSpec((None, CONV_W + 1, D_FF), lambda i, f: (layer, 0, 0)),
            pl.BlockSpec((None, TF, D_MODEL), lambda i, f: (layer, f, 0)),
            pl.BlockSpec((1, D_MODEL), lambda i, f: (0, 0)),
        ],
        out_specs=out_specs,
        out_shape=out_shape,
        scratch_shapes=[
            pltpu.VMEM((tm, D_MODEL), BF16),
            pltpu.VMEM((2 * halo, D_MODEL), BF16),
        ],
        compiler_params=_params("arbitrary", "arbitrary"),
        name="ffn",
    )(x, x, x, norm_g, mod, mod, mod, w_up, w_up,
      jnp.concatenate([conv_w, conv_b.reshape(DEPTH, 1, D_FF)], axis=1), w_down, final_g.reshape(1, D_MODEL))


def _gla_gate_weights(gw2, gb):
    r = GLA_GATE_RANK
    g2 = jnp.zeros((2, 128, GLA_DK_TOTAL), F32)
    g2 = g2.at[0, 0:r].set(gw2[0]).at[1, r:2 * r].set(gw2[1])
    g2 = g2.reshape(2, 128, H_GLA, DK_GLA).transpose(2, 0, 1, 3).astype(BF16)
    gbh = gb.reshape(2, H_GLA, 1, DK_GLA).transpose(1, 0, 2, 3)
    return g2, gbh


def kernel(x_prompt, x_sample, state_ret, state_gla, c, c_ctx, ada_w, ada_b, norm1_g, norm2_g,
           even_w_in, pool_w, pool_scale, ret_decay, ret_norm_g, even_w_out, odd_w_in, gla_gw1,
           gla_gw2, gla_gb, gla_norm_g, odd_w_out, ffn_w_up, ffn_conv_w, ffn_conv_b, ffn_w_down, final_g):
    xs = (x_prompt.reshape(N_PROMPT, D_MODEL), x_sample.reshape(N_SAMPLE, D_MODEL))
    cvec = jnp.concatenate([c_ctx[None, :], c, jnp.zeros((COND_PAD - N_COND, D_MODEL), F32)], axis=0)
    mod = _ada_mod(cvec, ada_w, ada_b)
    n1 = norm1_g.reshape(DEPTH, 1, D_MODEL)
    n2 = norm2_g.reshape(DEPTH, 1, D_MODEL)
    rope = _rope_tables(DEC_SEQ)
    p_blocks = N_PROMPT // DEC_SEQ

    w_even_in, w_even_out = even_w_in.astype(BF16), even_w_out.astype(BF16)
    w_odd_in, w_odd_out = odd_w_in.astype(BF16), odd_w_out.astype(BF16)
    w_up, w_down, w_pool = ffn_w_up.astype(BF16), ffn_w_down.astype(BF16), pool_w.astype(BF16)
    w_gate1 = jnp.concatenate([gla_gw1[:, 0], gla_gw1[:, 1],
                               jnp.zeros((N_ODD, D_MODEL, 128 - 2 * GLA_GATE_RANK), F32)], axis=2).astype(BF16)

    ret_states = gla_states = None
    for l in range(DEPTH):
        if l % 2 == 0:
            i = l // 2
            proj = _in_proj(xs, n1, mod, l, (w_even_in,), i)
            o_p, ret_states = _ret_scan(proj, ret_decay[i], SEQ, BATCH, H_RET, 0, None, None, 0, ret_states, i)
            (o_s,) = _ret_scan(proj, ret_decay[i], DEC_SEQ, DEC_BATCH, 1, p_blocks, rope, state_ret, i, None, None)
            x = _out_proj(xs, mod, l, w_even_out, i, proj, EVEN_IN // RET_WIDTH - 1,
                          ret_norm_g[i], o_p, o_s, H_RET, (w_pool, pool_scale))
        else:
            j = l // 2
            proj = _in_proj(xs, n1, mod, l, (w_odd_in, w_gate1), j)
            g2, gbh = _gla_gate_weights(gla_gw2[j], gla_gb[j])
            o_p, gla_states = _gla_scan(proj, g2, gbh, SEQ, BATCH, GLA_PROMPT_SEQS, DV_GLA, 0, None, 0,
                                        gla_states, j)
            (o_s,) = _gla_scan(proj, g2, gbh, DEC_SEQ, DEC_BATCH, 1, DV_GLA, p_blocks, state_gla, j,
                               None, None)
            x = _out_proj(xs, mod, l, w_odd_out, j, proj, ODD_IN // GLA_DV_TOTAL - 1,
                          gla_norm_g[j], o_p, o_s, H_GLA)
        xs = _ffn(x, mod, l, n2, w_up, ffn_conv_w, ffn_conv_b, w_down, final_g, l == DEPTH - 1)
        xs = tuple(xs) if l == DEPTH - 1 else (xs,)

    y_prompt = xs[0].reshape(BATCH, SEQ, D_MODEL)
    y_sample = xs[1].reshape(DEC_BATCH, DEC_SEQ, D_MODEL)
    return (y_prompt, y_sample, ret_states, gla_states)
```

```python
import functools

import numpy as np
import jax
import jax.numpy as jnp
from jax import lax
from jax.experimental import pallas as pl
from jax.experimental.pallas import tpu as pltpu

F32 = jnp.float32
BF16 = jnp.bfloat16

D_MODEL = 2048
BATCH = 16
SEQ = 256
DEPTH = 4
DEC_BATCH = 2
DEC_SEQ = 4096
GRID_W = 64
N_EVEN = (DEPTH + 1) // 2
N_ODD = DEPTH // 2
POOL_WIDTH = D_MODEL // 2
POOL_GROUPS = 4
POOL_GC = POOL_WIDTH // POOL_GROUPS
POOL_WINDOWS = (2, 4, 8, 16)
RET_WIDTH = D_MODEL // 2
H_RET = 8
DK_RET = RET_WIDTH // H_RET
DV_RET = RET_WIDTH // H_RET
ROPE_BASE = 10000.0
H_GLA = 4
GLA_DK_TOTAL = D_MODEL // 2
GLA_DV_TOTAL = D_MODEL
DK_GLA = GLA_DK_TOTAL // H_GLA
DV_GLA = GLA_DV_TOTAL // H_GLA
GLA_GATE_RANK = 16
GLA_TAU = 16.0
D_FF = 5632
CONV_W = 3
CHUNK = 64
EPS = 1e-6
EVEN_IN = POOL_WIDTH + 4 * RET_WIDTH
ODD_IN = 2 * GLA_DK_TOTAL + 2 * GLA_DV_TOTAL

N_PROMPT = BATCH * SEQ
N_SAMPLE = DEC_BATCH * DEC_SEQ
N_TOK = N_PROMPT + N_SAMPLE
N_COND = 1 + DEC_BATCH
COND_PAD = 8

VMEM_LIMIT = 56 * 1024 * 1024

TM = 512
IN_CHUNK = 512
TM_FFN = 1024
TF = 512
POOL_TM = 256
POOL_HALO = 16
ADA_TN = 1024
RET_CHUNK = 256
GLA_BLK = 256
GLA_PROMPT_SEQS = 2


def _params(*sem):
    return pltpu.CompilerParams(dimension_semantics=sem, vmem_limit_bytes=VMEM_LIMIT)


def _cond_of_tile(i, tm):
    r0 = i * tm
    return jnp.where(r0 < N_PROMPT, 0, 1 + (r0 - N_PROMPT) // DEC_SEQ)


LOG2E = 1.4426950408889634


def _log_sigmoid(x):
    return jnp.minimum(x, 0.0) - jnp.log(1.0 + jnp.exp2(jnp.abs(x) * (-LOG2E)))


def _silu(x):
    return x * jax.nn.sigmoid(x)


def _rms(x):
    return x * lax.rsqrt(jnp.mean(x * x, axis=-1, keepdims=True) + EPS)


def _dot(a, b):
    return jnp.dot(a, b, preferred_element_type=F32)


def _dot_nt(a, b):
    return lax.dot_general(a, b, (((1,), (1,)), ((), ())), preferred_element_type=F32)


def _dot_tn(a, b):
    return lax.dot_general(a, b, (((0,), (0,)), ((), ())), preferred_element_type=F32)


def _split_bf16(x):
    hi = x.astype(BF16)
    lo = (x - hi.astype(F32)).astype(BF16)
    return hi, lo


def _loop(n, body, init, unroll=1):
    if n == 1:
        return body(0, init)
    return lax.fori_loop(0, n, body, init, unroll=unroll)


def _ada_kernel(c_ref, w_ref, b_ref, o_ref):
    s = _silu(c_ref[...]).astype(BF16)
    o_ref[0] = _dot(s, w_ref[0].astype(BF16)) + b_ref[0]


def _ada_mod(cvec, ada_w, ada_b):
    n = 6 * D_MODEL
    mod = pl.pallas_call(
        _ada_kernel,
        grid=(DEPTH, n // ADA_TN),
        in_specs=[
            pl.BlockSpec((COND_PAD, D_MODEL), lambda l, j: (0, 0)),
            pl.BlockSpec((1, D_MODEL, ADA_TN), lambda l, j: (l, 0, j)),
            pl.BlockSpec((1, 1, ADA_TN), lambda l, j: (l, 0, j)),
        ],
        out_specs=pl.BlockSpec((1, COND_PAD, ADA_TN), lambda l, j: (l, 0, j)),
        out_shape=jax.ShapeDtypeStruct((DEPTH, COND_PAD, n), F32),
        compiler_params=_params("parallel", "parallel"),
        name="ada_mod",
    )(cvec, ada_w, ada_b.reshape(DEPTH, 1, n))
    mod = mod[:, :N_COND].reshape(DEPTH, N_COND, 6, D_MODEL).transpose(0, 2, 1, 3)
    return mod.reshape(DEPTH * 6 * N_COND, 1, D_MODEL)


def _mod_spec(layer, part, tm):
    base = (layer * 6 + part) * N_COND
    return pl.BlockSpec((1, 1, D_MODEL), lambda *idx: (base + _cond_of_tile(idx[0], tm), 0, 0))


def _group_specs(tm, width):
    p_tiles = N_PROMPT // tm
    return [pl.BlockSpec((tm, width), lambda *idx: (jnp.minimum(idx[0], p_tiles - 1), 0)),
            pl.BlockSpec((tm, width), lambda *idx: (jnp.maximum(idx[0] - p_tiles, 0), 0))]


def _in_proj_kernel(*refs, n_x, n_w):
    x_refs, (g_ref, sh_ref, sc_ref), refs = refs[:n_x], refs[n_x:n_x + 3], refs[n_x + 3:]
    w_refs, (o_ref, h_ref) = refs[:n_w], refs[n_w:]

    if len(x_refs) == 1:
        x = x_refs[0][...]
    else:
        x = jnp.where(pl.program_id(0) < N_PROMPT // TM, x_refs[0][...], x_refs[1][...])
    h = _rms(x) * g_ref[0]
    h_ref[...] = (h * (1.0 + sc_ref[0]) + sh_ref[0]).astype(BF16)
    col = 0
    for w_ref in w_refs:
        for c0 in range(0, w_ref.shape[1], IN_CHUNK):
            c1 = min(c0 + IN_CHUNK, w_ref.shape[1])
            o_ref[:, col + c0:col + c1] = _dot(h_ref[...], w_ref[:, c0:c1]).astype(o_ref.dtype)
        col += w_ref.shape[1]


def _in_proj(xs, norm_g, mod, layer, ws, w_layer):
    n = sum(w.shape[2] for w in ws)
    x_specs = [pl.BlockSpec((TM, D_MODEL), lambda i: (i, 0))] if len(xs) == 1 else _group_specs(TM, D_MODEL)
    return pl.pallas_call(
        functools.partial(_in_proj_kernel, n_x=len(xs), n_w=len(ws)),
        grid=(N_TOK // TM,),
        in_specs=x_specs + [
            pl.BlockSpec((1, 1, D_MODEL), lambda i: (layer, 0, 0)),
            _mod_spec(layer, 0, TM),
            _mod_spec(layer, 1, TM),
        ] + [pl.BlockSpec((None, D_MODEL, w.shape[2]), lambda i: (w_layer, 0, 0), pipeline_mode=pl.Buffered(1))
             for w in ws],
        out_specs=pl.BlockSpec((TM, n), lambda i: (i, 0)),
        out_shape=jax.ShapeDtypeStruct((N_TOK, n), BF16),
        scratch_shapes=[pltpu.VMEM((TM, D_MODEL), BF16)],
        compiler_params=_params("arbitrary"),
        name="in_proj",
    )(*xs, norm_g, mod, mod, *ws)


def _pool_bands():
    t = POOL_TM
    b0 = np.zeros((POOL_GROUPS, t, t), np.float32)
    bp = np.zeros((POOL_GROUPS, t, POOL_HALO), np.float32)
    bn = np.zeros((POOL_GROUPS, t, POOL_HALO), np.float32)
    for g, win in enumerate(POOL_WINDOWS):
        for r in range(t):
            for s in range(r - win // 2, r + win - win // 2):
                if s < 0:
                    bp[g, r, s + POOL_HALO] = 1.0
                elif s >= t:
                    bn[g, r, s - t] = 1.0
                else:
                    b0[g, r, s] = 1.0
    return jnp.asarray(b0, BF16), jnp.asarray(bp, BF16), jnp.asarray(bn, BF16)


def _pool_tiles(tiles, b0_ref, bp_ref, bn_ref, pw_ref, sc_ref):
    tiles_per_seq = DEC_SEQ // POOL_TM
    t = lax.broadcasted_iota(jnp.int32, (POOL_TM, POOL_GC), 0)
    sums, cnts, us = [], [], []
    for tile, u, up, un in tiles:
        in_prompt = tile < N_PROMPT // POOL_TM
        pos = (tile - N_PROMPT // POOL_TM) % tiles_per_seq
        is_start = jnp.logical_or(in_prompt, pos == 0)
        is_end = jnp.logical_or(in_prompt, pos == tiles_per_seq - 1)
        up = jnp.where(is_start, jnp.zeros_like(up), up)
        un = jnp.where(is_end, jnp.zeros_like(un), un)
        for g, win in enumerate(POOL_WINDOWS):
            cols = slice(g * POOL_GC, (g + 1) * POOL_GC)
            us.append(u[:, cols])
            sums.append(_dot(b0_ref[g], u[:, cols]) + _dot(bp_ref[g], up[:, cols]) + _dot(bn_ref[g], un[:, cols]))
            cut_lo = jnp.where(is_start, jnp.maximum(win // 2 - t, 0), 0)
            cut_hi = jnp.where(is_end, jnp.maximum(t + (win - win // 2) - POOL_TM, 0), 0)
            cnts.append((win - cut_lo - cut_hi).astype(F32))
    pooled = [(s / cnt - u.astype(F32)).astype(BF16) for s, cnt, u in zip(sums, cnts, us)]
    ys = [_dot(p, pw_ref[n % POOL_GROUPS]) for n, p in enumerate(pooled)]
    out = []
    for j in range(len(tiles)):
        out.append(jnp.concatenate(
            [(ys[j * POOL_GROUPS + g] * sc_ref[:, g * POOL_GC:(g + 1) * POOL_GC]).astype(BF16)
             for g in range(POOL_GROUPS)], axis=1))
    return out


def _rope_tables(t):
    nf = DK_RET // 4
    rows = t // GRID_W
    r = jnp.repeat(jnp.arange(rows), GRID_W).astype(F32)
    col = jnp.tile(jnp.arange(GRID_W), rows).astype(F32)
    inv = ROPE_BASE ** (-jnp.arange(nf, dtype=F32) / nf)
    ar, ac = r[:, None] * inv, col[:, None] * inv
    cos = jnp.concatenate([jnp.cos(ar), jnp.cos(ar), jnp.cos(ac), jnp.cos(ac)], axis=1)
    sin = jnp.concatenate([-jnp.sin(ar), jnp.sin(ar), -jnp.sin(ac), jnp.sin(ac)], axis=1)
    return cos, sin


def _ret_kernel(*refs, n_chunks, heads, use_rope, has_s0, emit_state, aliased):
    refs = list(refs)
    dec_ref, q_ref, k_ref, v_ref = refs[:4]
    refs = refs[4:]
    if use_rope:
        cos_ref, sin_ref = refs[:2]
        refs = refs[2:]
    if has_s0:
        s0_ref = refs[0]
        refs = refs[1:]
    if aliased:
        refs = refs[1:]
    o_ref = refs[0]
    refs = refs[1:]
    if emit_state:
        st_ref = refs[0]
        refs = refs[1:]
    (qk_ref,) = refs

    c = RET_CHUNK
    nf = DK_RET // 4
    head0 = pl.program_id(1) * heads
    inter = has_s0 or n_chunks > 1

    row = lax.broadcasted_iota(jnp.int32, (c, DK_RET), 0).astype(F32)
    ii = lax.broadcasted_iota(jnp.int32, (c, c), 0)
    jj = lax.broadcasted_iota(jnp.int32, (c, c), 1)
    dist = (ii - jj).astype(F32)
    lane = lax.broadcasted_iota(jnp.int32, (c, DK_RET), 1)
    first_half = (lane % (2 * nf)) < nf

    def rope(x, r0):
        if not use_rope:
            return x
        partner = jnp.where(first_half, pltpu.roll(x, DK_RET - nf, axis=1), pltpu.roll(x, nf, axis=1))
        return x * cos_ref[pl.ds(r0, c), :] + partner * sin_ref[pl.ds(r0, c), :]

    for hh in range(heads):
        cols = slice(hh * DK_RET, (hh + 1) * DK_RET)

        def lam(d, shape):
            return _log_sigmoid(jnp.full(shape, dec_ref[d, head0 + hh], F32))

        lam_f, lam_b = lam(0, (c, DK_RET)), lam(1, (c, DK_RET))
        dq_f = jnp.exp((row + 1.0) * lam_f)
        dk_f = jnp.exp((c - 1.0 - row) * lam_f)
        dq_b = jnp.exp((c - row) * lam_b)
        dk_b = jnp.exp(row * lam_b)
        cdec_f = jnp.exp(float(c) * lam(0, (DK_RET, DV_RET)))
        cdec_b = jnp.exp(float(c) * lam(1, (DK_RET, DV_RET)))
        dmat = jnp.where(ii > jj, jnp.exp(dist * lam(0, (c, c))),
                         jnp.where(ii < jj, jnp.exp(-dist * lam(1, (c, c))), 2.0))

        group = 2 if n_chunks % 2 == 0 else 1

        def prepare(gi, carry):
            ns = [gi * group + j for j in range(group)]
            r0s = [n * c if isinstance(n, int) else pl.multiple_of(n * c, c) for n in ns]
            q = [rope(q_ref[pl.ds(r0, c), cols].astype(F32), r0) for r0 in r0s]
            k = [rope(k_ref[pl.ds(r0, c), cols].astype(F32) * (DK_RET ** -0.5), r0) for r0 in r0s]
            a = [(_dot_nt(q[j].astype(BF16), k[j].astype(BF16)) * dmat).astype(BF16) for j in range(group)]
            for j, r0 in enumerate(r0s):
                qk_ref[0, pl.ds(r0, c), cols] = (q[j] * dq_f).astype(BF16)
                qk_ref[1, pl.ds(r0, c), cols] = (q[j] * dq_b).astype(BF16)
                qk_ref[2, pl.ds(r0, c), cols] = (k[j] * dk_f).astype(BF16)
                qk_ref[3, pl.ds(r0, c), cols] = (k[j] * dk_b).astype(BF16)
            o = [_dot(a[j], v_ref[pl.ds(r0s[j], c), cols]) for j in range(group)]
            for j, r0 in enumerate(r0s):
                o_ref[pl.ds(r0, c), cols] = o[j]
            return carry

        def advance(n, states):
            ms = (n, n_chunks - 1 - n)
            r0s = [m * c if isinstance(m, int) else pl.multiple_of(m * c, c) for m in ms]
            upd = [_dot_tn(qk_ref[2 + d, pl.ds(r0s[d], c), cols], v_ref[pl.ds(r0s[d], c), cols]) for d in (0, 1)]
            out = []
            for d, cdec in ((0, cdec_f), (1, cdec_b)):
                if inter:
                    o_ref[pl.ds(r0s[d], c), cols] += _dot(qk_ref[d, pl.ds(r0s[d], c), cols],
                                                          states[d].astype(BF16))
                out.append(cdec * states[d] + upd[d])
            return tuple(out)

        _loop(n_chunks // group, prepare, 0)
        zero = jnp.zeros((DK_RET, DV_RET), F32)
        init = (s0_ref[0, 0, 0, hh], s0_ref[0, 0, 1, hh]) if has_s0 else (zero, zero)
        s_f, s_b = _loop(n_chunks, advance, init, unroll=2)
        if emit_state:
            st_ref[0, 0, hh] = s_f
            st_ref[0, 1, hh] = s_b


def _ret_scan(proj, decay, t, batch, heads, row_block0, rope, s0, s0_layer, state_buf, state_layer):
    w = heads * DK_RET
    col0 = POOL_WIDTH // w
    per = RET_WIDTH // w
    in_specs = [
        pl.BlockSpec(memory_space=pltpu.SMEM),
        pl.BlockSpec((t, w), lambda b, h: (row_block0 + b, col0 + h)),
        pl.BlockSpec((t, w), lambda b, h: (row_block0 + b, col0 + per + h)),
        pl.BlockSpec((t, w), lambda b, h: (row_block0 + b, col0 + 2 * per + h)),
    ]
    args = [decay, proj, proj, proj]
    if rope is not None:
        assert heads == 1
        in_specs += [pl.BlockSpec((t, DK_RET), lambda b, h: (0, 0))] * 2
        args += list(rope)
    if s0 is not None:
        in_specs.append(pl.BlockSpec((1, 1, 2, heads, DK_RET, DV_RET), lambda b, h: (b, s0_layer, 0, h, 0, 0)))
        args.append(s0)
    out_specs = [pl.BlockSpec((t, w), lambda b, h: (b, h))]
    out_shape = [jax.ShapeDtypeStruct((batch * t, RET_WIDTH), F32)]
    aliases = {}
    if state_layer is not None:
        out_specs.append(pl.BlockSpec((1, None, 2, heads, DK_RET, DV_RET), lambda b, h: (b, state_layer, 0, h, 0, 0)))
        out_shape.append(jax.ShapeDtypeStruct((batch, N_EVEN, 2, H_RET, DK_RET, DV_RET), F32))
        if state_buf is not None:
            aliases = {len(args): 1}
            in_specs.append(pl.BlockSpec(memory_space=pl.ANY))
            args.append(state_buf)
    return pl.pallas_call(
        functools.partial(_ret_kernel, n_chunks=t // RET_CHUNK, heads=heads, use_rope=rope is not None,
                          has_s0=s0 is not None, emit_state=state_layer is not None, aliased=bool(aliases)),
        grid=(batch, H_RET // heads),
        in_specs=in_specs,
        out_specs=out_specs,
        out_shape=out_shape,
        input_output_aliases=aliases,
        scratch_shapes=[pltpu.VMEM((4, t, w), BF16)],
        compiler_params=_params("parallel", "parallel"),
        name="ret_scan",
    )(*args)


def _gla_kernel(*refs, n_blocks, seqs, dvb, has_s0, emit_state, aliased):
    refs = list(refs)
    q_ref, k_ref, v_ref, z_ref, g2_ref, gb_ref = refs[:6]
    refs = refs[6:]
    if has_s0:
        s0_ref = refs[0]
        refs = refs[1:]
    if aliased:
        refs = refs[1:]
    o_ref = refs[0]
    refs = refs[1:]
    if emit_state:
        st_ref = refs[0]
        refs = refs[1:]
    qd_ref, kt_ref, a_ref, dec_ref, s_ref = refs

    c = CHUNK
    blk = GLA_BLK
    cpb = blk // c
    tot_blocks = seqs * n_blocks

    @pl.when(pl.program_id(2) == 0)
    def _():
        ii = lax.broadcasted_iota(jnp.int32, (blk, blk), 0)
        jj = lax.broadcasted_iota(jnp.int32, (blk, blk), 1)
        same = (ii // c) == (jj // c)
        masks = (jnp.logical_and(same, ii >= jj), jnp.logical_and(same, ii <= jj))
        tris = tuple(jnp.tile(jnp.where(m, 1.0, 0.0).astype(BF16), (1, 2)) for m in masks)

        group = 2 if tot_blocks % 2 == 0 else 1
        chains = [(j, d) for j in range(group) for d in (0, 1)]

        def build(gi, carry):
            bis = [gi * group + j for j in range(group)]
            r0s = [bi * blk if isinstance(bi, int) else pl.multiple_of(bi * blk, blk) for bi in bis]
            q = [q_ref[pl.ds(r0, blk), :].astype(F32) * (DK_GLA ** -0.5) for r0 in r0s]
            k = [k_ref[pl.ds(r0, blk), :].astype(F32) for r0 in r0s]
            z = [_dot(z_ref[pl.ds(r0s[j], blk), :], g2_ref[0, d]) + gb_ref[0, d] for j, d in chains]
            lg = [jnp.concatenate(_split_bf16(_log_sigmoid(zz) * (LOG2E / GLA_TAU)), axis=0) for zz in z]
            g = [_dot(tris[d], lg[n]) for n, (j, d) in enumerate(chains)]
            qds, kds, cross = [], [], []
            for n, (j, d) in enumerate(chains):
                edge = c - 1 if d == 0 else 0
                gt = [g[n][ci * c + edge:ci * c + edge + 1, :] for ci in range(cpb)]
                g_tot = jnp.concatenate([jnp.broadcast_to(gt[ci], (c, DK_GLA)) for ci in range(cpb)], axis=0)
                qd_f32 = q[j] * jnp.exp2(g[n])
                qd = qd_f32.astype(BF16)
                kds.append((k[j] * jnp.exp2(-g[n])).astype(BF16))
                qds.append(qd)
                kt = k[j] * jnp.exp2(g_tot - g[n])
                scan = list(range(cpb)) if d == 0 else list(reversed(range(cpb)))
                pos = {ci: p for p, ci in enumerate(scan)}

                def span(lo, hi):
                    parts = [gt[scan[p]] for p in range(lo, hi)]
                    return functools.reduce(lambda x, y: x + y, parts) if parts else None

                def scaled(x, e):
                    return x if e is None else x * jnp.broadcast_to(jnp.exp2(e), x.shape)

                rows = lambda x, ci: x[ci * c:(ci + 1) * c, :]
                qd_ref[d, pl.ds(r0s[j], blk), :] = jnp.concatenate(
                    [scaled(rows(qd_f32, ci), span(0, pos[ci])) for ci in range(cpb)], axis=0).astype(BF16)
                kt_ref[d, pl.ds(r0s[j], blk), :] = jnp.concatenate(
                    [scaled(rows(kt, ci), span(pos[ci] + 1, cpb)) for ci in range(cpb)], axis=0).astype(BF16)
                dec_ref[d * tot_blocks + bis[j]] = jnp.exp2(jnp.broadcast_to(span(0, cpb), (8, DK_GLA)))
                zero_rows = jnp.zeros((c, DK_GLA), BF16)
                for p in range(1, cpb):
                    src = jnp.concatenate(
                        [scaled(rows(kt, cj), span(pos[cj] + 1, p)).astype(BF16) if pos[cj] < p else zero_rows
                         for cj in range(cpb)], axis=0)
                    cross.append((n, scan[p], qd[scan[p] * c:(scan[p] + 1) * c, :], src))
            a = [_dot_nt(qds[n], kds[n]) for n in range(len(chains))]
            cross = [(n, ci, _dot_nt(qrows, src)) for n, ci, qrows, src in cross]
            for j in range(group):
                total = jnp.where(masks[0], a[2 * j], 0.0) + jnp.where(masks[1], a[2 * j + 1], 0.0)
                for d in (0, 1):
                    by_chunk = {ci: sc for n, ci, sc in cross if n == 2 * j + d}
                    total = total + jnp.concatenate(
                        [by_chunk.get(ci, jnp.zeros((c, blk), F32)) for ci in range(cpb)], axis=0)
                a_ref[bis[j]] = total.astype(BF16)
            return carry

        _loop(tot_blocks // group, build, 0)

    def intra(bi, carry):
        r0 = bi * blk if isinstance(bi, int) else pl.multiple_of(bi * blk, blk)
        o_ref[pl.ds(r0, blk), :] = _dot(a_ref[bi], v_ref[pl.ds(r0, blk), :])
        return carry

    _loop(tot_blocks, intra, 0, unroll=2)

    runs = [(sq, d) for sq in range(seqs) for d in (0, 1)]
    for sq, d in runs:
        s_ref[2 * sq + d] = s0_ref[sq, 0, d, 0].T if has_s0 else jnp.zeros((dvb, DK_GLA), F32)

    inter = has_s0 or n_blocks > 1

    def advance(n, carry):
        ms = [sq * n_blocks + (n if d == 0 else n_blocks - 1 - n) for sq, d in runs]
        r0s = [m * blk if isinstance(m, int) else pl.multiple_of(m * blk, blk) for m in ms]
        upd = [_dot_tn(v_ref[pl.ds(r0s[n_], blk), :], kt_ref[d, pl.ds(r0s[n_], blk), :])
               for n_, (sq, d) in enumerate(runs)]
        for n_, (sq, d) in enumerate(runs):
            s = s_ref[2 * sq + d]
            if inter:
                o_ref[pl.ds(r0s[n_], blk), :] += _dot_nt(qd_ref[d, pl.ds(r0s[n_], blk), :], s.astype(BF16))
            s_ref[2 * sq + d] = s * dec_ref[d * tot_blocks + ms[n_]][0:1, :] + upd[n_]
        return carry

    _loop(n_blocks, advance, 0, unroll=2)
    if emit_state:
        for sq, d in runs:
            st_ref[sq, d, 0] = s_ref[2 * sq + d].T


def _gla_scan(proj, g2, gb, t, batch, seqs, dvb, row_block0, s0, s0_layer, state_buf, state_layer):
    nd = DV_GLA // dvb
    kcol = GLA_DK_TOTAL // DK_GLA
    vcol = 2 * GLA_DK_TOTAL // dvb
    n_blocks = t // GLA_BLK
    rows = seqs * t
    in_specs = [
        pl.BlockSpec((rows, DK_GLA), lambda b, h, d: (row_block0 + b, h)),
        pl.BlockSpec((rows, DK_GLA), lambda b, h, d: (row_block0 + b, kcol + h)),
        pl.BlockSpec((rows, dvb), lambda b, h, d: (row_block0 + b, vcol + h * nd + d)),
        pl.BlockSpec((rows, 128), lambda b, h, d: (row_block0 + b, ODD_IN // 128)),
        pl.BlockSpec((1, 2, 128, DK_GLA), lambda b, h, d: (h, 0, 0, 0)),
        pl.BlockSpec((1, 2, 1, DK_GLA), lambda b, h, d: (h, 0, 0, 0)),
    ]
    args = [proj, proj, proj, proj, g2, gb]
    if s0 is not None:
        assert seqs == 1
        in_specs.append(pl.BlockSpec((1, 1, 2, 1, DK_GLA, dvb), lambda b, h, d: (b, s0_layer, 0, h, 0, d)))
        args.append(s0)
    out_specs = [pl.BlockSpec((rows, dvb), lambda b, h, d: (b, h * nd + d))]
    out_shape = [jax.ShapeDtypeStruct((batch * t, GLA_DV_TOTAL), F32)]
    aliases = {}
    if state_layer is not None:
        out_specs.append(pl.BlockSpec((seqs, None, 2, 1, DK_GLA, dvb), lambda b, h, d: (b, state_layer, 0, h, 0, d)))
        out_shape.append(jax.ShapeDtypeStruct((batch, N_ODD, 2, H_GLA, DK_GLA, DV_GLA), F32))
        if state_buf is not None:
            aliases = {len(args): 1}
            in_specs.append(pl.BlockSpec(memory_space=pl.ANY))
            args.append(state_buf)
    return pl.pallas_call(
        functools.partial(_gla_kernel, n_blocks=n_blocks, seqs=seqs, dvb=dvb, has_s0=s0 is not None,
                          emit_state=state_layer is not None, aliased=bool(aliases)),
        grid=(batch // seqs, H_GLA, nd),
        in_specs=in_specs,
        out_specs=out_specs,
        out_shape=out_shape,
        input_output_aliases=aliases,
        scratch_shapes=[
            pltpu.VMEM((2, rows, DK_GLA), BF16),
            pltpu.VMEM((2, rows, DK_GLA), BF16),
            pltpu.VMEM((rows // GLA_BLK, GLA_BLK, GLA_BLK), BF16),
            pltpu.VMEM((2 * rows // GLA_BLK, 8, DK_GLA), F32),
            pltpu.VMEM((2 * seqs, dvb, DK_GLA), F32),
        ],
        compiler_params=_params("parallel", "parallel", "arbitrary"),
        name="gla_scan",
    )(*args)


def _out_proj_kernel(*refs, n_heads, width, with_pool):
    refs = list(refs)
    if with_pool:
        u_ref, up_ref, un_ref, b0_ref, bp_ref, bn_ref, pw_ref, psc_ref = refs[:8]
        refs = refs[8:]
    op_ref, os_ref, gate_ref, gn_ref, w_ref = refs[:5]
    *x_refs, g1_ref, out_ref = refs[5:]
    xp_ref, xs_ref = x_refs if len(x_refs) == 2 else (x_refs[0], x_refs[0])
    i = pl.program_id(0)
    in_prompt = i < N_PROMPT // TM

    pooled = []
    if with_pool:
        sub = TM // POOL_TM
        tiles = []
        for s in range(sub):
            r0 = s * POOL_TM
            up = up_ref[...] if s == 0 else u_ref[r0 - POOL_HALO:r0, :]
            un = un_ref[...] if s == sub - 1 else u_ref[r0 + POOL_TM:r0 + POOL_TM + POOL_HALO, :]
            tiles.append((i * sub + s, u_ref[r0:r0 + POOL_TM, :], up, un))
        pooled = [jnp.concatenate(_pool_tiles(tiles, b0_ref, bp_ref, bn_ref, pw_ref, psc_ref), axis=0)]

    def run(o_ref, x_ref):
        parts = list(pooled)
        for hd in range(n_heads):
            cols = slice(hd * width, (hd + 1) * width)
            gt = gate_ref[:, cols].astype(F32)
            parts.append((_rms(o_ref[:, cols]) * gn_ref[:, cols] * _silu(gt)).astype(BF16))
        a = jnp.concatenate(parts, axis=1)
        out_ref[...] = x_ref[...] + g1_ref[0] * _dot(a, w_ref[...])

    @pl.when(in_prompt)
    def _():
        run(op_ref, xp_ref)

    @pl.when(jnp.logical_not(in_prompt))
    def _():
        run(os_ref, xs_ref)


def _out_proj(xs, mod, layer, w, w_layer, proj, gate_col_block, norm_g, o_prompt, o_sample, n_heads, pool=None):
    width = o_prompt.shape[1]
    x_specs = [pl.BlockSpec((TM, D_MODEL), lambda i: (i, 0))] if len(xs) == 1 else _group_specs(TM, D_MODEL)
    in_specs = _group_specs(TM, width) + [
        pl.BlockSpec((TM, width), lambda i: (i, gate_col_block)),
        pl.BlockSpec((1, width), lambda i: (0, 0)),
        pl.BlockSpec((None, D_MODEL, D_MODEL), lambda i: (w_layer, 0, 0)),
    ] + x_specs + [_mod_spec(layer, 2, TM)]
    args = [o_prompt, o_sample, proj, norm_g.reshape(1, width), w, *xs, mod]
    if pool is not None:
        b0, bp, bn = _pool_bands()
        halo_blocks = TM // POOL_HALO
        last = N_TOK // POOL_HALO - 1
        full = lambda shape: pl.BlockSpec(shape, lambda i: (0,) * len(shape))
        in_specs = [
            pl.BlockSpec((TM, POOL_WIDTH), lambda i: (i, 0)),
            pl.BlockSpec((POOL_HALO, POOL_WIDTH), lambda i: (jnp.maximum(i * halo_blocks - 1, 0), 0)),
            pl.BlockSpec((POOL_HALO, POOL_WIDTH), lambda i: (jnp.minimum((i + 1) * halo_blocks, last), 0)),
            full(b0.shape), full(bp.shape), full(bn.shape),
            pl.BlockSpec((None,) + pool[0].shape[1:], lambda i: (w_layer, 0, 0, 0)),
            pl.BlockSpec((None, 1, POOL_WIDTH), lambda i: (w_layer, 0, 0)),
        ] + in_specs
        args = [proj, proj, proj, b0, bp, bn, pool[0], pool[1].reshape(N_EVEN, 1, POOL_WIDTH)] + args
    return pl.pallas_call(
        functools.partial(_out_proj_kernel, n_heads=n_heads, width=width // n_heads, with_pool=pool is not None),
        grid=(N_TOK // TM,),
        in_specs=in_specs,
        out_specs=pl.BlockSpec((TM, D_MODEL), lambda i: (i, 0)),
        out_shape=jax.ShapeDtypeStruct((N_TOK, D_MODEL), F32),
        compiler_params=_params("parallel"),
        name="out_proj",
    )(*args)


def _ffn_kernel(x_ref, xp_ref, xn_ref, gn_ref, sh_ref, sc_ref, gt_ref, wa_ref, wb_ref, cv_ref,
                wd_ref, fg_ref, *rest, final):
    if final:
        op_ref, o_ref, h_ref, hh_ref = rest
    else:
        o_ref, h_ref, hh_ref = rest
    i = pl.program_id(0)
    f = pl.program_id(1)
    tm = TM_FFN

    def modnorm(x):
        return (_rms(x) * gn_ref[0] * (1.0 + sc_ref[0]) + sh_ref[0]).astype(BF16)

    @pl.when(f == 0)
    def _():
        h_ref[...] = modnorm(x_ref[...])
        hh_ref[...] = modnorm(jnp.concatenate([xp_ref[...], xn_ref[...]], axis=0))
        o_ref[...] = jnp.zeros_like(o_ref)

    a = _dot(h_ref[...], wa_ref[...])
    b = _dot(h_ref[...], wb_ref[...])
    a_halo = _dot(hh_ref[...], wa_ref[...])

    seq = jnp.where(i * tm < N_PROMPT, SEQ, DEC_SEQ)
    t = lax.broadcasted_iota(jnp.int32, (tm, TF), 0)
    pos = (i * tm + t) & (seq - 1)
    a_prev = jnp.where(t == 0, a_halo[7:8, :], pltpu.roll(a, 1, axis=0))
    a_prev = jnp.where(pos == 0, 0.0, a_prev)
    a_next = jnp.where(t == tm - 1, a_halo[8:9, :], pltpu.roll(a, tm - 1, axis=0))
    a_next = jnp.where(pos == seq - 1, 0.0, a_next)
    cv = cv_ref[:, pl.ds(pl.multiple_of(f * TF, TF), TF)]
    conv = a_prev * cv[0:1, :] + a * cv[1:2, :] + a_next * cv[2:3, :] + cv[3:4, :]
    o_ref[...] += _dot((_silu(conv) * b).astype(BF16), wd_ref[...])

    @pl.when(f == pl.num_programs(1) - 1)
    def _():
        y = x_ref[...] + gt_ref[0] * o_ref[...]
        if not final:
            o_ref[...] = y
        else:
            y = _rms(y) * fg_ref[...]
            in_prompt = i < N_PROMPT // tm

            @pl.when(in_prompt)
            def _():
                op_ref[...] = y

            @pl.when(jnp.logical_not(in_prompt))
            def _():
                o_ref[...] = y


def _ffn(x, mod, layer, norm_g, w_up, conv_w, conv_b, w_down, final_g, final):
    m = x.shape[0]
    nf = D_FF // TF
    tm = TM_FFN
    halo = 8
    last = m // halo - 1
    if final:
        p_tiles = N_PROMPT // tm
        out_specs = [pl.BlockSpec((tm, D_MODEL), lambda i, f: (jnp.minimum(i, p_tiles - 1), 0),
                                  pipeline_mode=pl.Buffered(1)),
                     pl.BlockSpec((tm, D_MODEL), lambda i, f: (jnp.maximum(i - p_tiles, 0), 0),
                                  pipeline_mode=pl.Buffered(1))]
        out_shape = [jax.ShapeDtypeStruct((N_PROMPT, D_MODEL), F32),
                     jax.ShapeDtypeStruct((N_SAMPLE, D_MODEL), F32)]
    else:
        out_specs = pl.BlockSpec((tm, D_MODEL), lambda i, f: (i, 0))
        out_shape = jax.ShapeDtypeStruct((m, D_MODEL), F32)
    return pl.pallas_call(
        functools.partial(_ffn_kernel, final=final),
        grid=(m // tm, nf),
        in_specs=[
            pl.BlockSpec((tm, D_MODEL), lambda i, f: (i, 0), pipeline_mode=pl.Buffered(1)),
            pl.BlockSpec((halo, D_MODEL), lambda i, f: (jnp.maximum(i * (tm // halo) - 1, 0), 0)),
            pl.BlockSpec((halo, D_MODEL), lambda i, f: (jnp.minimum((i + 1) * (tm // halo), last), 0)),
            pl.BlockSpec((1, 1, D_MODEL), lambda i, f: (layer, 0, 0)),
            _mod_spec(layer, 3, tm),
            _mod_spec(layer, 4, tm),
            _mod_spec(layer, 5, tm),
            pl.BlockSpec((None, D_MODEL, TF), lambda i, f: (layer, 0, f)),
            pl.BlockSpec((None, D_MODEL, TF), lambda i, f: (layer, 0, nf + f)),
            pl.BlockSpec((None, CONV_W + 1, D_FF), lambda i, f: (layer, 0, 0)),
            pl.BlockSpec((None, TF, D_MODEL), lambda i, f: (layer, f, 0)),
            pl.BlockSpec((1, D_MODEL), lambda i, f: (0, 0)),
        ],
        out_specs=out_specs,
        out_shape=out_shape,
        scratch_shapes=[
            pltpu.VMEM((tm, D_MODEL), BF16),
            pltpu.VMEM((2 * halo, D_MODEL), BF16),
        ],
        compiler_params=_params("arbitrary", "arbitrary"),
        name="ffn",
    )(x, x, x, norm_g, mod, mod, mod, w_up, w_up,
      jnp.concatenate([conv_w, conv_b.reshape(DEPTH, 1, D_FF)], axis=1), w_down, final_g.reshape(1, D_MODEL))


def _gla_gate_weights(gw2, gb):
    r = GLA_GATE_RANK
    g2 = jnp.zeros((2, 128, GLA_DK_TOTAL), F32)
    g2 = g2.at[0, 0:r].set(gw2[0]).at[1, r:2 * r].set(gw2[1])
    g2 = g2.reshape(2, 128, H_GLA, DK_GLA).transpose(2, 0, 1, 3).astype(BF16)
    gbh = gb.reshape(2, H_GLA, 1, DK_GLA).transpose(1, 0, 2, 3)
    return g2, gbh


def kernel(x_prompt, x_sample, state_ret, state_gla, c, c_ctx, ada_w, ada_b, norm1_g, norm2_g,
           even_w_in, pool_w, pool_scale, ret_decay, ret_norm_g, even_w_out, odd_w_in, gla_gw1,
           gla_gw2, gla_gb, gla_norm_g, odd_w_out, ffn_w_up, ffn_conv_w, ffn_conv_b, ffn_w_down, final_g):
    xs = (x_prompt.reshape(N_PROMPT, D_MODEL), x_sample.reshape(N_SAMPLE, D_MODEL))
    cvec = jnp.concatenate([c_ctx[None, :], c, jnp.zeros((COND_PAD - N_COND, D_MODEL), F32)], axis=0)
    mod = _ada_mod(cvec, ada_w, ada_b)
    n1 = norm1_g.reshape(DEPTH, 1, D_MODEL)
    n2 = norm2_g.reshape(DEPTH, 1, D_MODEL)
    rope = _rope_tables(DEC_SEQ)
    p_blocks = N_PROMPT // DEC_SEQ

    w_even_in, w_even_out = even_w_in.astype(BF16), even_w_out.astype(BF16)
    w_odd_in, w_odd_out = odd_w_in.astype(BF16), odd_w_out.astype(BF16)
    w_up, w_down, w_pool = ffn_w_up.astype(BF16), ffn_w_down.astype(BF16), pool_w.astype(BF16)
    w_gate1 = jnp.concatenate([gla_gw1[:, 0], gla_gw1[:, 1],
                               jnp.zeros((N_ODD, D_MODEL, 128 - 2 * GLA_GATE_RANK), F32)], axis=2).astype(BF16)

    ret_states = gla_states = None
    for l in range(DEPTH):
        if l % 2 == 0:
            i = l // 2
            proj = _in_proj(xs, n1, mod, l, (w_even_in,), i)
            o_p, ret_states = _ret_scan(proj, ret_decay[i], SEQ, BATCH, H_RET, 0, None, None, 0, ret_states, i)
            (o_s,) = _ret_scan(proj, ret_decay[i], DEC_SEQ, DEC_BATCH, 1, p_blocks, rope, state_ret, i, None, None)
            x = _out_proj(xs, mod, l, w_even_out, i, proj, EVEN_IN // RET_WIDTH - 1,
                          ret_norm_g[i], o_p, o_s, H_RET, (w_pool, pool_scale))
        else:
            j = l // 2
            proj = _in_proj(xs, n1, mod, l, (w_odd_in, w_gate1), j)
            g2, gbh = _gla_gate_weights(gla_gw2[j], gla_gb[j])
            o_p, gla_states = _gla_scan(proj, g2, gbh, SEQ, BATCH, GLA_PROMPT_SEQS, DV_GLA, 0, None, 0,
                                        gla_states, j)
            (o_s,) = _gla_scan(proj, g2, gbh, DEC_SEQ, DEC_BATCH, 1, DV_GLA, p_blocks, state_gla, j,
                               None, None)
            x = _out_proj(xs, mod, l, w_odd_out, j, proj, ODD_IN // GLA_DV_TOTAL - 1,
                          gla_norm_g[j], o_p, o_s, H_GLA)
        xs = _ffn(x, mod, l, n2, w_up, ffn_conv_w, ffn_conv_b, w_down, final_g, l == DEPTH - 1)
        xs = tuple(xs) if l == DEPTH - 1 else (xs,)

    y_prompt = xs[0].reshape(BATCH, SEQ, D_MODEL)
    y_sample = xs[1].reshape(DEC_BATCH, DEC_SEQ, D_MODEL)
    return (y_prompt, y_sample, ret_states, gla_states)
```

```python
import functools

import numpy as np
import jax
import jax.numpy as jnp
from jax import lax
from jax.experimental import pallas as pl
from jax.experimental.pallas import tpu as pltpu

F32 = jnp.float32
BF16 = jnp.bfloat16

D_MODEL = 2048
BATCH = 16
SEQ = 256
DEPTH = 4
DEC_BATCH = 2
DEC_SEQ = 4096
GRID_W = 64
N_EVEN = (DEPTH + 1) // 2
N_ODD = DEPTH // 2
POOL_WIDTH = D_MODEL // 2
POOL_GROUPS = 4
POOL_GC = POOL_WIDTH // POOL_GROUPS
POOL_WINDOWS = (2, 4, 8, 16)
RET_WIDTH = D_MODEL // 2
H_RET = 8
DK_RET = RET_WIDTH // H_RET
DV_RET = RET_WIDTH // H_RET
ROPE_BASE = 10000.0
H_GLA = 4
GLA_DK_TOTAL = D_MODEL // 2
GLA_DV_TOTAL = D_MODEL
DK_GLA = GLA_DK_TOTAL // H_GLA
DV_GLA = GLA_DV_TOTAL // H_GLA
GLA_GATE_RANK = 16
GLA_TAU = 16.0
D_FF = 5632
CONV_W = 3
CHUNK = 64
EPS = 1e-6
EVEN_IN = POOL_WIDTH + 4 * RET_WIDTH
ODD_IN = 2 * GLA_DK_TOTAL + 2 * GLA_DV_TOTAL

N_PROMPT = BATCH * SEQ
N_SAMPLE = DEC_BATCH * DEC_SEQ
N_TOK = N_PROMPT + N_SAMPLE
N_COND = 1 + DEC_BATCH
COND_PAD = 8

VMEM_LIMIT = 56 * 1024 * 1024

TM = 512
IN_CHUNK = 512
TM_FFN = 1024
TF = 512
POOL_TM = 256
POOL_HALO = 16
ADA_TN = 1024
RET_CHUNK = 256
GLA_BLK = 256
GLA_PROMPT_SEQS = 2


def _params(*sem):
    return pltpu.CompilerParams(dimension_semantics=sem, vmem_limit_bytes=VMEM_LIMIT)


def _cond_of_tile(i, tm):
    r0 = i * tm
    return jnp.where(r0 < N_PROMPT, 0, 1 + (r0 - N_PROMPT) // DEC_SEQ)


LOG2E = 1.4426950408889634


def _log_sigmoid(x):
    return jnp.minimum(x, 0.0) - jnp.log(1.0 + jnp.exp2(jnp.abs(x) * (-LOG2E)))


def _silu(x):
    return x * jax.nn.sigmoid(x)


def _rms(x):
    return x * lax.rsqrt(jnp.mean(x * x, axis=-1, keepdims=True) + EPS)


def _dot(a, b):
    return jnp.dot(a, b, preferred_element_type=F32)


def _dot_nt(a, b):
    return lax.dot_general(a, b, (((1,), (1,)), ((), ())), preferred_element_type=F32)


def _dot_tn(a, b):
    return lax.dot_general(a, b, (((0,), (0,)), ((), ())), preferred_element_type=F32)


def _split_bf16(x):
    hi = x.astype(BF16)
    lo = (x - hi.astype(F32)).astype(BF16)
    return hi, lo


def _loop(n, body, init, unroll=1):
    if n == 1:
        return body(0, init)
    return lax.fori_loop(0, n, body, init, unroll=unroll)


def _ada_kernel(c_ref, w_ref, b_ref, o_ref):
    s = _silu(c_ref[...]).astype(BF16)
    o_ref[0] = _dot(s, w_ref[0].astype(BF16)) + b_ref[0]


def _ada_mod(cvec, ada_w, ada_b):
    n = 6 * D_MODEL
    mod = pl.pallas_call(
        _ada_kernel,
        grid=(DEPTH, n // ADA_TN),
        in_specs=[
            pl.BlockSpec((COND_PAD, D_MODEL), lambda l, j: (0, 0)),
            pl.BlockSpec((1, D_MODEL, ADA_TN), lambda l, j: (l, 0, j)),
            pl.BlockSpec((1, 1, ADA_TN), lambda l, j: (l, 0, j)),
        ],
        out_specs=pl.BlockSpec((1, COND_PAD, ADA_TN), lambda l, j: (l, 0, j)),
        out_shape=jax.ShapeDtypeStruct((DEPTH, COND_PAD, n), F32),
        compiler_params=_params("parallel", "parallel"),
        name="ada_mod",
    )(cvec, ada_w, ada_b.reshape(DEPTH, 1, n))
    mod = mod[:, :N_COND].reshape(DEPTH, N_COND, 6, D_MODEL).transpose(0, 2, 1, 3)
    return mod.reshape(DEPTH * 6 * N_COND, 1, D_MODEL)


def _mod_spec(layer, part, tm):
    base = (layer * 6 + part) * N_COND
    return pl.BlockSpec((1, 1, D_MODEL), lambda *idx: (base + _cond_of_tile(idx[0], tm), 0, 0))


def _group_specs(tm, width):
    p_tiles = N_PROMPT // tm
    return [pl.BlockSpec((tm, width), lambda *idx: (jnp.minimum(idx[0], p_tiles - 1), 0)),
            pl.BlockSpec((tm, width), lambda *idx: (jnp.maximum(idx[0] - p_tiles, 0), 0))]


def _in_proj_kernel(*refs, n_x, n_w):
    x_refs, (g_ref, sh_ref, sc_ref), refs = refs[:n_x], refs[n_x:n_x + 3], refs[n_x + 3:]
    w_refs, (o_ref, h_ref) = refs[:n_w], refs[n_w:]

    if len(x_refs) == 1:
        x = x_refs[0][...]
    else:
        x = jnp.where(pl.program_id(0) < N_PROMPT // TM, x_refs[0][...], x_refs[1][...])
    h = _rms(x) * g_ref[0]
    h_ref[...] = (h * (1.0 + sc_ref[0]) + sh_ref[0]).astype(BF16)
    col = 0
    for w_ref in w_refs:
        for c0 in range(0, w_ref.shape[1], IN_CHUNK):
            c1 = min(c0 + IN_CHUNK, w_ref.shape[1])
            o_ref[:, col + c0:col + c1] = _dot(h_ref[...], w_ref[:, c0:c1]).astype(o_ref.dtype)
        col += w_ref.shape[1]


def _in_proj(xs, norm_g, mod, layer, ws, w_layer):
    n = sum(w.shape[2] for w in ws)
    x_specs = [pl.BlockSpec((TM, D_MODEL), lambda i: (i, 0))] if len(xs) == 1 else _group_specs(TM, D_MODEL)
    return pl.pallas_call(
        functools.partial(_in_proj_kernel, n_x=len(xs), n_w=len(ws)),
        grid=(N_TOK // TM,),
        in_specs=x_specs + [
            pl.BlockSpec((1, 1, D_MODEL), lambda i: (layer, 0, 0)),
            _mod_spec(layer, 0, TM),
            _mod_spec(layer, 1, TM),
        ] + [pl.BlockSpec((None, D_MODEL, w.shape[2]), lambda i: (w_layer, 0, 0), pipeline_mode=pl.Buffered(1))
             for w in ws],
        out_specs=pl.BlockSpec((TM, n), lambda i: (i, 0)),
        out_shape=jax.ShapeDtypeStruct((N_TOK, n), BF16),
        scratch_shapes=[pltpu.VMEM((TM, D_MODEL), BF16)],
        compiler_params=_params("arbitrary"),
        name="in_proj",
    )(*xs, norm_g, mod, mod, *ws)


def _pool_bands():
    t = POOL_TM
    b0 = np.zeros((POOL_GROUPS, t, t), np.float32)
    bp = np.zeros((POOL_GROUPS, t, POOL_HALO), np.float32)
    bn = np.zeros((POOL_GROUPS, t, POOL_HALO), np.float32)
    for g, win in enumerate(POOL_WINDOWS):
        for r in range(t):
            for s in range(r - win // 2, r + win - win // 2):
                if s < 0:
                    bp[g, r, s + POOL_HALO] = 1.0
                elif s >= t:
                    bn[g, r, s - t] = 1.0
                else:
                    b0[g, r, s] = 1.0
    return jnp.asarray(b0, BF16), jnp.asarray(bp, BF16), jnp.asarray(bn, BF16)


def _pool_tiles(tiles, b0_ref, bp_ref, bn_ref, pw_ref, sc_ref):
    tiles_per_seq = DEC_SEQ // POOL_TM
    t = lax.broadcasted_iota(jnp.int32, (POOL_TM, POOL_GC), 0)
    sums, cnts, us = [], [], []
    for tile, u, up, un in tiles:
        in_prompt = tile < N_PROMPT // POOL_TM
        pos = (tile - N_PROMPT // POOL_TM) % tiles_per_seq
        is_start = jnp.logical_or(in_prompt, pos == 0)
        is_end = jnp.logical_or(in_prompt, pos == tiles_per_seq - 1)
        up = jnp.where(is_start, jnp.zeros_like(up), up)
        un = jnp.where(is_end, jnp.zeros_like(un), un)
        for g, win in enumerate(POOL_WINDOWS):
            cols = slice(g * POOL_GC, (g + 1) * POOL_GC)
            us.append(u[:, cols])
            sums.append(_dot(b0_ref[g], u[:, cols]) + _dot(bp_ref[g], up[:, cols]) + _dot(bn_ref[g], un[:, cols]))
            cut_lo = jnp.where(is_start, jnp.maximum(win // 2 - t, 0), 0)
            cut_hi = jnp.where(is_end, jnp.maximum(t + (win - win // 2) - POOL_TM, 0), 0)
            cnts.append((win - cut_lo - cut_hi).astype(F32))
    pooled = [(s / cnt - u.astype(F32)).astype(BF16) for s, cnt, u in zip(sums, cnts, us)]
    ys = [_dot(p, pw_ref[n % POOL_GROUPS]) for n, p in enumerate(pooled)]
    out = []
    for j in range(len(tiles)):
        out.append(jnp.concatenate(
            [(ys[j * POOL_GROUPS + g] * sc_ref[:, g * POOL_GC:(g + 1) * POOL_GC]).astype(BF16)
             for g in range(POOL_GROUPS)], axis=1))
    return out


def _rope_tables(t):
    nf = DK_RET // 4
    rows = t // GRID_W
    r = jnp.repeat(jnp.arange(rows), GRID_W).astype(F32)
    col = jnp.tile(jnp.arange(GRID_W), rows).astype(F32)
    inv = ROPE_BASE ** (-jnp.arange(nf, dtype=F32) / nf)
    ar, ac = r[:, None] * inv, col[:, None] * inv
    cos = jnp.concatenate([jnp.cos(ar), jnp.cos(ar), jnp.cos(ac), jnp.cos(ac)], axis=1)
    sin = jnp.concatenate([-jnp.sin(ar), jnp.sin(ar), -jnp.sin(ac), jnp.sin(ac)], axis=1)
    return cos, sin


def _ret_kernel(*refs, n_chunks, heads, use_rope, has_s0, state_slots, aliased):
    refs = list(refs)
    dec_ref, q_ref, k_ref, v_ref = refs[:4]
    refs = refs[4:]
    if use_rope:
        cos_ref, sin_ref = refs[:2]
        refs = refs[2:]
    if has_s0:
        s0_ref = refs[0]
        refs = refs[1:]
    if aliased:
        refs = refs[1:]
    o_ref = refs[0]
    refs = refs[1:]
    if state_slots is not None:
        st_ref = refs[0]
        refs = refs[1:]
    (qk_ref,) = refs

    c = RET_CHUNK
    nf = DK_RET // 4
    head0 = pl.program_id(1) * heads
    inter = has_s0 or n_chunks > 1

    row = lax.broadcasted_iota(jnp.int32, (c, DK_RET), 0).astype(F32)
    ii = lax.broadcasted_iota(jnp.int32, (c, c), 0)
    jj = lax.broadcasted_iota(jnp.int32, (c, c), 1)
    dist = (ii - jj).astype(F32)
    lane = lax.broadcasted_iota(jnp.int32, (c, DK_RET), 1)
    first_half = (lane % (2 * nf)) < nf

    def rope(x, r0):
        if not use_rope:
            return x
        partner = jnp.where(first_half, pltpu.roll(x, DK_RET - nf, axis=1), pltpu.roll(x, nf, axis=1))
        return x * cos_ref[pl.ds(r0, c), :] + partner * sin_ref[pl.ds(r0, c), :]

    for hh in range(heads):
        cols = slice(hh * DK_RET, (hh + 1) * DK_RET)

        def lam(d, shape):
            return _log_sigmoid(jnp.full(shape, dec_ref[d, head0 + hh], F32))

        lam_f, lam_b = lam(0, (c, DK_RET)), lam(1, (c, DK_RET))
        dq_f = jnp.exp((row + 1.0) * lam_f)
        dk_f = jnp.exp((c - 1.0 - row) * lam_f)
        dq_b = jnp.exp((c - row) * lam_b)
        dk_b = jnp.exp(row * lam_b)
        cdec_f = jnp.exp(float(c) * lam(0, (DK_RET, DV_RET)))
        cdec_b = jnp.exp(float(c) * lam(1, (DK_RET, DV_RET)))
        dmat = jnp.where(ii > jj, jnp.exp(dist * lam(0, (c, c))),
                         jnp.where(ii < jj, jnp.exp(-dist * lam(1, (c, c))), 2.0))

        group = 2 if n_chunks % 2 == 0 else 1

        def prepare(gi, carry):
            ns = [gi * group + j for j in range(group)]
            r0s = [n * c if isinstance(n, int) else pl.multiple_of(n * c, c) for n in ns]
            q = [rope(q_ref[pl.ds(r0, c), cols].astype(F32), r0) for r0 in r0s]
            k = [rope(k_ref[pl.ds(r0, c), cols].astype(F32) * (DK_RET ** -0.5), r0) for r0 in r0s]
            a = [(_dot_nt(q[j].astype(BF16), k[j].astype(BF16)) * dmat).astype(BF16) for j in range(group)]
            for j, r0 in enumerate(r0s):
                qk_ref[0, pl.ds(r0, c), cols] = (q[j] * dq_f).astype(BF16)
                qk_ref[1, pl.ds(r0, c), cols] = (q[j] * dq_b).astype(BF16)
                qk_ref[2, pl.ds(r0, c), cols] = (k[j] * dk_f).astype(BF16)
                qk_ref[3, pl.ds(r0, c), cols] = (k[j] * dk_b).astype(BF16)
            o = [_dot(a[j], v_ref[pl.ds(r0s[j], c), cols]) for j in range(group)]
            for j, r0 in enumerate(r0s):
                o_ref[pl.ds(r0, c), cols] = o[j]
            return carry

        def advance(n, states):
            ms = (n, n_chunks - 1 - n)
            r0s = [m * c if isinstance(m, int) else pl.multiple_of(m * c, c) for m in ms]
            upd = [_dot_tn(qk_ref[2 + d, pl.ds(r0s[d], c), cols], v_ref[pl.ds(r0s[d], c), cols]) for d in (0, 1)]
            out = []
            for d, cdec in ((0, cdec_f), (1, cdec_b)):
                if inter:
                    o_ref[pl.ds(r0s[d], c), cols] += _dot(qk_ref[d, pl.ds(r0s[d], c), cols],
                                                          states[d].astype(BF16))
                out.append(cdec * states[d] + upd[d])
            return tuple(out)

        _loop(n_chunks // group, prepare, 0)
        zero = jnp.zeros((DK_RET, DV_RET), F32)
        init = (s0_ref[0, 0, 0, hh], s0_ref[0, 0, 1, hh]) if has_s0 else (zero, zero)
        s_f, s_b = _loop(n_chunks, advance, init, unroll=2)
        if state_slots is not None:
            slot, n_slots = state_slots
            st_ref[0, slot, 0, hh] = s_f
            st_ref[0, slot, 1, hh] = s_b
            for other in range(n_slots):
                if other != slot:
                    st_ref[0, other, :, hh] = jnp.zeros((2, DK_RET, DV_RET), F32)


def _ret_scan(proj, decay, t, batch, heads, row_block0, rope, s0, s0_layer, state_buf, state_layer):
    w = heads * DK_RET
    col0 = POOL_WIDTH // w
    per = RET_WIDTH // w
    in_specs = [
        pl.BlockSpec(memory_space=pltpu.SMEM),
        pl.BlockSpec((t, w), lambda b, h: (row_block0 + b, col0 + h)),
        pl.BlockSpec((t, w), lambda b, h: (row_block0 + b, col0 + per + h)),
        pl.BlockSpec((t, w), lambda b, h: (row_block0 + b, col0 + 2 * per + h)),
    ]
    args = [decay, proj, proj, proj]
    if rope is not None:
        assert heads == 1
        in_specs += [pl.BlockSpec((t, DK_RET), lambda b, h: (0, 0))] * 2
        args += list(rope)
    if s0 is not None:
        in_specs.append(pl.BlockSpec((1, 1, 2, heads, DK_RET, DV_RET), lambda b, h: (b, s0_layer, 0, h, 0, 0)))
        args.append(s0)
    out_specs = [pl.BlockSpec((t, w), lambda b, h: (b, h))]
    out_shape = [jax.ShapeDtypeStruct((batch * t, RET_WIDTH), F32)]
    aliases, state_slots = {}, None
    if state_layer is not None:
        state_slots = (state_layer, N_EVEN) if state_buf is None else (0, 1)
        slot0 = 0 if state_buf is None else state_layer
        out_specs.append(pl.BlockSpec((1, state_slots[1], 2, heads, DK_RET, DV_RET),
                                      lambda b, h: (b, slot0, 0, h, 0, 0)))
        out_shape.append(jax.ShapeDtypeStruct((batch, N_EVEN, 2, H_RET, DK_RET, DV_RET), F32))
        if state_buf is not None:
            aliases = {len(args): 1}
            in_specs.append(pl.BlockSpec(memory_space=pl.ANY))
            args.append(state_buf)
    return pl.pallas_call(
        functools.partial(_ret_kernel, n_chunks=t // RET_CHUNK, heads=heads, use_rope=rope is not None,
                          has_s0=s0 is not None, state_slots=state_slots, aliased=bool(aliases)),
        grid=(batch, H_RET // heads),
        in_specs=in_specs,
        out_specs=out_specs,
        out_shape=out_shape,
        input_output_aliases=aliases,
        scratch_shapes=[pltpu.VMEM((4, t, w), BF16)],
        compiler_params=_params("parallel", "parallel"),
        name="ret_scan",
    )(*args)


def _gla_kernel(*refs, n_blocks, seqs, dvb, has_s0, state_slots, aliased):
    refs = list(refs)
    q_ref, k_ref, v_ref, z_ref, g2_ref, gb_ref = refs[:6]
    refs = refs[6:]
    if has_s0:
        s0_ref = refs[0]
        refs = refs[1:]
    if aliased:
        refs = refs[1:]
    o_ref = refs[0]
    refs = refs[1:]
    if state_slots is not None:
        st_ref = refs[0]
        refs = refs[1:]
    qd_ref, kt_ref, a_ref, dec_ref, s_ref = refs

    c = CHUNK
    blk = GLA_BLK
    cpb = blk // c
    tot_blocks = seqs * n_blocks

    @pl.when(pl.program_id(2) == 0)
    def _():
        ii = lax.broadcasted_iota(jnp.int32, (blk, blk), 0)
        jj = lax.broadcasted_iota(jnp.int32, (blk, blk), 1)
        same = (ii // c) == (jj // c)
        masks = (jnp.logical_and(same, ii >= jj), jnp.logical_and(same, ii <= jj))
        tris = tuple(jnp.tile(jnp.where(m, 1.0, 0.0).astype(BF16), (1, 2)) for m in masks)

        group = 2 if tot_blocks % 2 == 0 else 1
        chains = [(j, d) for j in range(group) for d in (0, 1)]

        def build(gi, carry):
            bis = [gi * group + j for j in range(group)]
            r0s = [bi * blk if isinstance(bi, int) else pl.multiple_of(bi * blk, blk) for bi in bis]
            q = [q_ref[pl.ds(r0, blk), :].astype(F32) * (DK_GLA ** -0.5) for r0 in r0s]
            k = [k_ref[pl.ds(r0, blk), :].astype(F32) for r0 in r0s]
            z = [_dot(z_ref[pl.ds(r0s[j], blk), :], g2_ref[0, d]) + gb_ref[0, d] for j, d in chains]
            lg = [jnp.concatenate(_split_bf16(_log_sigmoid(zz) * (LOG2E / GLA_TAU)), axis=0) for zz in z]
            g = [_dot(tris[d], lg[n]) for n, (j, d) in enumerate(chains)]
            qds, kds, cross = [], [], []
            for n, (j, d) in enumerate(chains):
                edge = c - 1 if d == 0 else 0
                gt = [g[n][ci * c + edge:ci * c + edge + 1, :] for ci in range(cpb)]
                g_tot = jnp.concatenate([jnp.broadcast_to(gt[ci], (c, DK_GLA)) for ci in range(cpb)], axis=0)
                qd_f32 = q[j] * jnp.exp2(g[n])
                qd = qd_f32.astype(BF16)
                kds.append((k[j] * jnp.exp2(-g[n])).astype(BF16))
                qds.append(qd)
                kt = k[j] * jnp.exp2(g_tot - g[n])
                scan = list(range(cpb)) if d == 0 else list(reversed(range(cpb)))
                pos = {ci: p for p, ci in enumerate(scan)}

                def span(lo, hi):
                    parts = [gt[scan[p]] for p in range(lo, hi)]
                    return functools.reduce(lambda x, y: x + y, parts) if parts else None

                def scaled(x, e):
                    return x if e is None else x * jnp.broadcast_to(jnp.exp2(e), x.shape)

                rows = lambda x, ci: x[ci * c:(ci + 1) * c, :]
                qd_ref[d, pl.ds(r0s[j], blk), :] = jnp.concatenate(
                    [scaled(rows(qd_f32, ci), span(0, pos[ci])) for ci in range(cpb)], axis=0).astype(BF16)
                kt_ref[d, pl.ds(r0s[j], blk), :] = jnp.concatenate(
                    [scaled(rows(kt, ci), span(pos[ci] + 1, cpb)) for ci in range(cpb)], axis=0).astype(BF16)
                dec_ref[d * tot_blocks + bis[j]] = jnp.exp2(jnp.broadcast_to(span(0, cpb), (8, DK_GLA)))
                zero_rows = jnp.zeros((c, DK_GLA), BF16)
                for p in range(1, cpb):
                    src = jnp.concatenate(
                        [scaled(rows(kt, cj), span(pos[cj] + 1, p)).astype(BF16) if pos[cj] < p else zero_rows
                         for cj in range(cpb)], axis=0)
                    cross.append((n, scan[p], qd[scan[p] * c:(scan[p] + 1) * c, :], src))
            a = [_dot_nt(qds[n], kds[n]) for n in range(len(chains))]
            cross = [(n, ci, _dot_nt(qrows, src)) for n, ci, qrows, src in cross]
            for j in range(group):
                total = jnp.where(masks[0], a[2 * j], 0.0) + jnp.where(masks[1], a[2 * j + 1], 0.0)
                for d in (0, 1):
                    by_chunk = {ci: sc for n, ci, sc in cross if n == 2 * j + d}
                    total = total + jnp.concatenate(
                        [by_chunk.get(ci, jnp.zeros((c, blk), F32)) for ci in range(cpb)], axis=0)
                a_ref[bis[j]] = total.astype(BF16)
            return carry

        _loop(tot_blocks // group, build, 0)

    def intra(bi, carry):
        r0 = bi * blk if isinstance(bi, int) else pl.multiple_of(bi * blk, blk)
        o_ref[pl.ds(r0, blk), :] = _dot(a_ref[bi], v_ref[pl.ds(r0, blk), :])
        return carry

    _loop(tot_blocks, intra, 0, unroll=2)

    runs = [(sq, d) for sq in range(seqs) for d in (0, 1)]
    for sq, d in runs:
        s_ref[2 * sq + d] = s0_ref[sq, 0, d, 0].T if has_s0 else jnp.zeros((dvb, DK_GLA), F32)

    inter = has_s0 or n_blocks > 1

    def advance(n, carry):
        ms = [sq * n_blocks + (n if d == 0 else n_blocks - 1 - n) for sq, d in runs]
        r0s = [m * blk if isinstance(m, int) else pl.multiple_of(m * blk, blk) for m in ms]
        upd = [_dot_tn(v_ref[pl.ds(r0s[n_], blk), :], kt_ref[d, pl.ds(r0s[n_], blk), :])
               for n_, (sq, d) in enumerate(runs)]
        for n_, (sq, d) in enumerate(runs):
            s = s_ref[2 * sq + d]
            if inter:
                o_ref[pl.ds(r0s[n_], blk), :] += _dot_nt(qd_ref[d, pl.ds(r0s[n_], blk), :], s.astype(BF16))
            s_ref[2 * sq + d] = s * dec_ref[d * tot_blocks + ms[n_]][0:1, :] + upd[n_]
        return carry

    _loop(n_blocks, advance, 0, unroll=2)
    if state_slots is not None:
        slot, n_slots = state_slots
        for sq, d in runs:
            st_ref[sq, slot, d, 0] = s_ref[2 * sq + d].T
            for other in range(n_slots):
                if other != slot:
                    st_ref[sq, other, d, 0] = jnp.zeros((DK_GLA, dvb), F32)


def _gla_scan(proj, g2, gb, t, batch, seqs, dvb, row_block0, s0, s0_layer, state_buf, state_layer):
    nd = DV_GLA // dvb
    kcol = GLA_DK_TOTAL // DK_GLA
    vcol = 2 * GLA_DK_TOTAL // dvb
    n_blocks = t // GLA_BLK
    rows = seqs * t
    in_specs = [
        pl.BlockSpec((rows, DK_GLA), lambda b, h, d: (row_block0 + b, h)),
        pl.BlockSpec((rows, DK_GLA), lambda b, h, d: (row_block0 + b, kcol + h)),
        pl.BlockSpec((rows, dvb), lambda b, h, d: (row_block0 + b, vcol + h * nd + d)),
        pl.BlockSpec((rows, 128), lambda b, h, d: (row_block0 + b, ODD_IN // 128)),
        pl.BlockSpec((1, 2, 128, DK_GLA), lambda b, h, d: (h, 0, 0, 0)),
        pl.BlockSpec((1, 2, 1, DK_GLA), lambda b, h, d: (h, 0, 0, 0)),
    ]
    args = [proj, proj, proj, proj, g2, gb]
    if s0 is not None:
        assert seqs == 1
        in_specs.append(pl.BlockSpec((1, 1, 2, 1, DK_GLA, dvb), lambda b, h, d: (b, s0_layer, 0, h, 0, d)))
        args.append(s0)
    out_specs = [pl.BlockSpec((rows, dvb), lambda b, h, d: (b, h * nd + d))]
    out_shape = [jax.ShapeDtypeStruct((batch * t, GLA_DV_TOTAL), F32)]
    aliases, state_slots = {}, None
    if state_layer is not None:
        state_slots = (state_layer, N_ODD) if state_buf is None else (0, 1)
        slot0 = 0 if state_buf is None else state_layer
        out_specs.append(pl.BlockSpec((seqs, state_slots[1], 2, 1, DK_GLA, dvb),
                                      lambda b, h, d: (b, slot0, 0, h, 0, d)))
        out_shape.append(jax.ShapeDtypeStruct((batch, N_ODD, 2, H_GLA, DK_GLA, DV_GLA), F32))
        if state_buf is not None:
            aliases = {len(args): 1}
            in_specs.append(pl.BlockSpec(memory_space=pl.ANY))
            args.append(state_buf)
    return pl.pallas_call(
        functools.partial(_gla_kernel, n_blocks=n_blocks, seqs=seqs, dvb=dvb, has_s0=s0 is not None,
                          state_slots=state_slots, aliased=bool(aliases)),
        grid=(batch // seqs, H_GLA, nd),
        in_specs=in_specs,
        out_specs=out_specs,
        out_shape=out_shape,
        input_output_aliases=aliases,
        scratch_shapes=[
            pltpu.VMEM((2, rows, DK_GLA), BF16),
            pltpu.VMEM((2, rows, DK_GLA), BF16),
            pltpu.VMEM((rows // GLA_BLK, GLA_BLK, GLA_BLK), BF16),
            pltpu.VMEM((2 * rows // GLA_BLK, 8, DK_GLA), F32),
            pltpu.VMEM((2 * seqs, dvb, DK_GLA), F32),
        ],
        compiler_params=_params("parallel", "parallel", "arbitrary"),
        name="gla_scan",
    )(*args)


def _out_proj_kernel(*refs, n_heads, width, with_pool):
    refs = list(refs)
    if with_pool:
        u_ref, up_ref, un_ref, b0_ref, bp_ref, bn_ref, pw_ref, psc_ref = refs[:8]
        refs = refs[8:]
    op_ref, os_ref, gate_ref, gn_ref, w_ref = refs[:5]
    *x_refs, g1_ref, out_ref = refs[5:]
    xp_ref, xs_ref = x_refs if len(x_refs) == 2 else (x_refs[0], x_refs[0])
    i = pl.program_id(0)
    in_prompt = i < N_PROMPT // TM

    pooled = []
    if with_pool:
        sub = TM // POOL_TM
        tiles = []
        for s in range(sub):
            r0 = s * POOL_TM
            up = up_ref[...] if s == 0 else u_ref[r0 - POOL_HALO:r0, :]
            un = un_ref[...] if s == sub - 1 else u_ref[r0 + POOL_TM:r0 + POOL_TM + POOL_HALO, :]
            tiles.append((i * sub + s, u_ref[r0:r0 + POOL_TM, :], up, un))
        pooled = [jnp.concatenate(_pool_tiles(tiles, b0_ref, bp_ref, bn_ref, pw_ref, psc_ref), axis=0)]

    def run(o_ref, x_ref):
        parts = list(pooled)
        for hd in range(n_heads):
            cols = slice(hd * width, (hd + 1) * width)
            gt = gate_ref[:, cols].astype(F32)
            parts.append((_rms(o_ref[:, cols]) * gn_ref[:, cols] * _silu(gt)).astype(BF16))
        a = jnp.concatenate(parts, axis=1)
        out_ref[...] = x_ref[...] + g1_ref[0] * _dot(a, w_ref[...])

    @pl.when(in_prompt)
    def _():
        run(op_ref, xp_ref)

    @pl.when(jnp.logical_not(in_prompt))
    def _():
        run(os_ref, xs_ref)


def _out_proj(xs, mod, layer, w, w_layer, proj, gate_col_block, norm_g, o_prompt, o_sample, n_heads, pool=None):
    width = o_prompt.shape[1]
    x_specs = [pl.BlockSpec((TM, D_MODEL), lambda i: (i, 0))] if len(xs) == 1 else _group_specs(TM, D_MODEL)
    in_specs = _group_specs(TM, width) + [
        pl.BlockSpec((TM, width), lambda i: (i, gate_col_block)),
        pl.BlockSpec((1, width), lambda i: (0, 0)),
        pl.BlockSpec((None, D_MODEL, D_MODEL), lambda i: (w_layer, 0, 0)),
    ] + x_specs + [_mod_spec(layer, 2, TM)]
    args = [o_prompt, o_sample, proj, norm_g.reshape(1, width), w, *xs, mod]
    if pool is not None:
        b0, bp, bn = _pool_bands()
        halo_blocks = TM // POOL_HALO
        last = N_TOK // POOL_HALO - 1
        full = lambda shape: pl.BlockSpec(shape, lambda i: (0,) * len(shape))
        in_specs = [
            pl.BlockSpec((TM, POOL_WIDTH), lambda i: (i, 0)),
            pl.BlockSpec((POOL_HALO, POOL_WIDTH), lambda i: (jnp.maximum(i * halo_blocks - 1, 0), 0)),
            pl.BlockSpec((POOL_HALO, POOL_WIDTH), lambda i: (jnp.minimum((i + 1) * halo_blocks, last), 0)),
            full(b0.shape), full(bp.shape), full(bn.shape),
            pl.BlockSpec((None,) + pool[0].shape[1:], lambda i: (w_layer, 0, 0, 0)),
            pl.BlockSpec((None, 1, POOL_WIDTH), lambda i: (w_layer, 0, 0)),
        ] + in_specs
        args = [proj, proj, proj, b0, bp, bn, pool[0], pool[1].reshape(N_EVEN, 1, POOL_WIDTH)] + args
    return pl.pallas_call(
        functools.partial(_out_proj_kernel, n_heads=n_heads, width=width // n_heads, with_pool=pool is not None),
        grid=(N_TOK // TM,),
        in_specs=in_specs,
        out_specs=pl.BlockSpec((TM, D_MODEL), lambda i: (i, 0)),
        out_shape=jax.ShapeDtypeStruct((N_TOK, D_MODEL), F32),
        compiler_params=_params("parallel"),
        name="out_proj",
    )(*args)


def _ffn_kernel(x_ref, xp_ref, xn_ref, gn_ref, sh_ref, sc_ref, gt_ref, wa_ref, wb_ref, cv_ref,
                wd_ref, fg_ref, *rest, final):
    if final:
        op_ref, o_ref, h_ref, hh_ref = rest
    else:
        o_ref, h_ref, hh_ref = rest
    i = pl.program_id(0)
    f = pl.program_id(1)
    tm = TM_FFN

    def modnorm(x):
        return (_rms(x) * gn_ref[0] * (1.0 + sc_ref[0]) + sh_ref[0]).astype(BF16)

    @pl.when(f == 0)
    def _():
        h_ref[...] = modnorm(x_ref[...])
        hh_ref[...] = modnorm(jnp.concatenate([xp_ref[...], xn_ref[...]], axis=0))
        o_ref[...] = jnp.zeros_like(o_ref)

    a = _dot(h_ref[...], wa_ref[...])
    b = _dot(h_ref[...], wb_ref[...])
    a_halo = _dot(hh_ref[...], wa_ref[...])

    seq = jnp.where(i * tm < N_PROMPT, SEQ, DEC_SEQ)
    t = lax.broadcasted_iota(jnp.int32, (tm, TF), 0)
    pos = (i * tm + t) & (seq - 1)
    a_prev = jnp.where(t == 0, a_halo[7:8, :], pltpu.roll(a, 1, axis=0))
    a_prev = jnp.where(pos == 0, 0.0, a_prev)
    a_next = jnp.where(t == tm - 1, a_halo[8:9, :], pltpu.roll(a, tm - 1, axis=0))
    a_next = jnp.where(pos == seq - 1, 0.0, a_next)
    cv = cv_ref[:, pl.ds(pl.multiple_of(f * TF, TF), TF)]
    conv = a_prev * cv[0:1, :] + a * cv[1:2, :] + a_next * cv[2:3, :] + cv[3:4, :]
    o_ref[...] += _dot((_silu(conv) * b).astype(BF16), wd_ref[...])

    @pl.when(f == pl.num_programs(1) - 1)
    def _():
        y = x_ref[...] + gt_ref[0] * o_ref[...]
        if not final:
            o_ref[...] = y
        else:
            y = _rms(y) * fg_ref[...]
            in_prompt = i < N_PROMPT // tm

            @pl.when(in_prompt)
            def _():
                op_ref[...] = y

            @pl.when(jnp.logical_not(in_prompt))
            def _():
                o_ref[...] = y


def _ffn(x, mod, layer, norm_g, w_up, conv_w, conv_b, w_down, final_g, final):
    m = x.shape[0]
    nf = D_FF // TF
    tm = TM_FFN
    halo = 8
    last = m // halo - 1
    if final:
        p_tiles = N_PROMPT // tm
        out_specs = [pl.BlockSpec((tm, D_MODEL), lambda i, f: (jnp.minimum(i, p_tiles - 1), 0),
                                  pipeline_mode=pl.Buffered(1)),
                     pl.BlockSpec((tm, D_MODEL), lambda i, f: (jnp.maximum(i - p_tiles, 0), 0),
                                  pipeline_mode=pl.Buffered(1))]
        out_shape = [jax.ShapeDtypeStruct((N_PROMPT, D_MODEL), F32),
                     jax.ShapeDtypeStruct((N_SAMPLE, D_MODEL), F32)]
    else:
        out_specs = pl.BlockSpec((tm, D_MODEL), lambda i, f: (i, 0))
        out_shape = jax.ShapeDtypeStruct((m, D_MODEL), F32)
    return pl.pallas_call(
        functools.partial(_ffn_kernel, final=final),
        grid=(m // tm, nf),
        in_specs=[
            pl.BlockSpec((tm, D_MODEL), lambda i, f: (i, 0), pipeline_mode=pl.Buffered(1)),
            pl.BlockSpec((halo, D_MODEL), lambda i, f: (jnp.maximum(i * (tm // halo) - 1, 0), 0)),
            pl.BlockSpec((halo, D_MODEL), lambda i, f: (jnp.minimum((i + 1) * (tm // halo), last), 0)),
            pl.BlockSpec((1, 1, D_MODEL), lambda i, f: (layer, 0, 0)),
            _mod_spec(layer, 3, tm),
            _mod_spec(layer, 4, tm),
            _mod_spec(layer, 5, tm),
            pl.BlockSpec((None, D_MODEL, TF), lambda i, f: (layer, 0, f)),
            pl.BlockSpec((None, D_MODEL, TF), lambda i, f: (layer, 0, nf + f)),
            pl.BlockSpec((None, CONV_W + 1, D_FF), lambda i, f: (layer, 0, 0)),
            pl.BlockSpec((None, TF, D_MODEL), lambda i, f: (layer, f, 0)),
            pl.BlockSpec((1, D_MODEL), lambda i, f: (0, 0)),
        ],
        out_specs=out_specs,
        out_shape=out_shape,
        scratch_shapes=[
            pltpu.VMEM((tm, D_MODEL), BF16),
            pltpu.VMEM((2 * halo, D_MODEL), BF16),
        ],
        compiler_params=_params("arbitrary", "arbitrary"),
        name="ffn",
    )(x, x, x, norm_g, mod, mod, mod, w_up, w_up,
      jnp.concatenate([conv_w, conv_b.reshape(DEPTH, 1, D_FF)], axis=1), w_down, final_g.reshape(1, D_MODEL))


def _gla_gate_weights(gw2, gb):
    r = GLA_GATE_RANK
    g2 = jnp.zeros((2, 128, GLA_DK_TOTAL), F32)
    g2 = g2.at[0, 0:r].set(gw2[0]).at[1, r:2 * r].set(gw2[1])
    g2 = g2.reshape(2, 128, H_GLA, DK_GLA).transpose(2, 0, 1, 3).astype(BF16)
    gbh = gb.reshape(2, H_GLA, 1, DK_GLA).transpose(1, 0, 2, 3)
    return g2, gbh


def kernel(x_prompt, x_sample, state_ret, state_gla, c, c_ctx, ada_w, ada_b, norm1_g, norm2_g,
           even_w_in, pool_w, pool_scale, ret_decay, ret_norm_g, even_w_out, odd_w_in, gla_gw1,
           gla_gw2, gla_gb, gla_norm_g, odd_w_out, ffn_w_up, ffn_conv_w, ffn_conv_b, ffn_w_down, final_g):
    xs = (x_prompt.reshape(N_PROMPT, D_MODEL), x_sample.reshape(N_SAMPLE, D_MODEL))
    cvec = jnp.concatenate([c_ctx[None, :], c, jnp.zeros((COND_PAD - N_COND, D_MODEL), F32)], axis=0)
    mod = _ada_mod(cvec, ada_w, ada_b)
    n1 = norm1_g.reshape(DEPTH, 1, D_MODEL)
    n2 = norm2_g.reshape(DEPTH, 1, D_MODEL)
    rope = _rope_tables(DEC_SEQ)
    p_blocks = N_PROMPT // DEC_SEQ

    w_even_in, w_even_out = even_w_in.astype(BF16), even_w_out.astype(BF16)
    w_odd_in, w_odd_out = odd_w_in.astype(BF16), odd_w_out.astype(BF16)
    w_up, w_down, w_pool = ffn_w_up.astype(BF16), ffn_w_down.astype(BF16), pool_w.astype(BF16)
    w_gate1 = jnp.concatenate([gla_gw1[:, 0], gla_gw1[:, 1],
                               jnp.zeros((N_ODD, D_MODEL, 128 - 2 * GLA_GATE_RANK), F32)], axis=2).astype(BF16)

    ret_states = gla_states = None
    for l in range(DEPTH):
        if l % 2 == 0:
            i = l // 2
            proj = _in_proj(xs, n1, mod, l, (w_even_in,), i)
            o_p, ret_states = _ret_scan(proj, ret_decay[i], SEQ, BATCH, H_RET, 0, None, None, 0, ret_states, i)
            (o_s,) = _ret_scan(proj, ret_decay[i], DEC_SEQ, DEC_BATCH, 1, p_blocks, rope, state_ret, i, None, None)
            x = _out_proj(xs, mod, l, w_even_out, i, proj, EVEN_IN // RET_WIDTH - 1,
                          ret_norm_g[i], o_p, o_s, H_RET, (w_pool, pool_scale))
        else:
            j = l // 2
            proj = _in_proj(xs, n1, mod, l, (w_odd_in, w_gate1), j)
            g2, gbh = _gla_gate_weights(gla_gw2[j], gla_gb[j])
            o_p, gla_states = _gla_scan(proj, g2, gbh, SEQ, BATCH, GLA_PROMPT_SEQS, DV_GLA, 0, None, 0,
                                        gla_states, j)
            (o_s,) = _gla_scan(proj, g2, gbh, DEC_SEQ, DEC_BATCH, 1, DV_GLA, p_blocks, state_gla, j,
                               None, None)
            x = _out_proj(xs, mod, l, w_odd_out, j, proj, ODD_IN // GLA_DV_TOTAL - 1,
                          gla_norm_g[j], o_p, o_s, H_GLA)
        xs = _ffn(x, mod, l, n2, w_up, ffn_conv_w, ffn_conv_b, w_down, final_g, l == DEPTH - 1)
        xs = tuple(xs) if l == DEPTH - 1 else (xs,)

    y_prompt = xs[0].reshape(BATCH, SEQ, D_MODEL)
    y_sample = xs[1].reshape(DEC_BATCH, DEC_SEQ, D_MODEL)
    return (y_prompt, y_sample, ret_states, gla_states)
```

```python
import functools

import numpy as np
import jax
import jax.numpy as jnp
from jax import lax
from jax.experimental import pallas as pl
from jax.experimental.pallas import tpu as pltpu

F32 = jnp.float32
BF16 = jnp.bfloat16

D_MODEL = 2048
BATCH = 16
SEQ = 256
DEPTH = 4
DEC_BATCH = 2
DEC_SEQ = 4096
GRID_W = 64
N_EVEN = (DEPTH + 1) // 2
N_ODD = DEPTH // 2
POOL_WIDTH = D_MODEL // 2
POOL_GROUPS = 4
POOL_GC = POOL_WIDTH // POOL_GROUPS
POOL_WINDOWS = (2, 4, 8, 16)
RET_WIDTH = D_MODEL // 2
H_RET = 8
DK_RET = RET_WIDTH // H_RET
DV_RET = RET_WIDTH // H_RET
ROPE_BASE = 10000.0
H_GLA = 4
GLA_DK_TOTAL = D_MODEL // 2
GLA_DV_TOTAL = D_MODEL
DK_GLA = GLA_DK_TOTAL // H_GLA
DV_GLA = GLA_DV_TOTAL // H_GLA
GLA_GATE_RANK = 16
GLA_TAU = 16.0
D_FF = 5632
CONV_W = 3
CHUNK = 64
EPS = 1e-6
EVEN_IN = POOL_WIDTH + 4 * RET_WIDTH
ODD_IN = 2 * GLA_DK_TOTAL + 2 * GLA_DV_TOTAL

N_PROMPT = BATCH * SEQ
N_SAMPLE = DEC_BATCH * DEC_SEQ
N_TOK = N_PROMPT + N_SAMPLE
N_COND = 1 + DEC_BATCH
COND_PAD = 8

VMEM_LIMIT = 56 * 1024 * 1024

TM = 512
IN_CHUNK = 512
TM_FFN = 1024
TF = 512
POOL_TM = 256
POOL_HALO = 16
ADA_TN = 1024
RET_CHUNK = 256
GLA_BLK = 256
GLA_PROMPT_SEQS = 2


def _params(*sem):
    return pltpu.CompilerParams(dimension_semantics=sem, vmem_limit_bytes=VMEM_LIMIT)


def _cond_of_tile(i, tm):
    r0 = i * tm
    return jnp.where(r0 < N_PROMPT, 0, 1 + (r0 - N_PROMPT) // DEC_SEQ)


LOG2E = 1.4426950408889634


def _log_sigmoid(x):
    return jnp.minimum(x, 0.0) - jnp.log(1.0 + jnp.exp2(jnp.abs(x) * (-LOG2E)))


def _silu(x):
    return x * jax.nn.sigmoid(x)


def _rms(x):
    return x * lax.rsqrt(jnp.mean(x * x, axis=-1, keepdims=True) + EPS)


def _dot(a, b):
    return jnp.dot(a, b, preferred_element_type=F32)


def _dot_nt(a, b):
    return lax.dot_general(a, b, (((1,), (1,)), ((), ())), preferred_element_type=F32)


def _dot_tn(a, b):
    return lax.dot_general(a, b, (((0,), (0,)), ((), ())), preferred_element_type=F32)


def _split_bf16(x):
    hi = x.astype(BF16)
    lo = (x - hi.astype(F32)).astype(BF16)
    return hi, lo


def _loop(n, body, init, unroll=1):
    if n == 1:
        return body(0, init)
    return lax.fori_loop(0, n, body, init, unroll=unroll)


def _ada_kernel(c_ref, w_ref, b_ref, o_ref):
    s = _silu(c_ref[...]).astype(BF16)
    o_ref[0] = _dot(s, w_ref[0].astype(BF16)) + b_ref[0]


def _ada_mod(cvec, ada_w, ada_b):
    n = 6 * D_MODEL
    mod = pl.pallas_call(
        _ada_kernel,
        grid=(DEPTH, n // ADA_TN),
        in_specs=[
            pl.BlockSpec((COND_PAD, D_MODEL), lambda l, j: (0, 0)),
            pl.BlockSpec((1, D_MODEL, ADA_TN), lambda l, j: (l, 0, j)),
            pl.BlockSpec((1, 1, ADA_TN), lambda l, j: (l, 0, j)),
        ],
        out_specs=pl.BlockSpec((1, COND_PAD, ADA_TN), lambda l, j: (l, 0, j)),
        out_shape=jax.ShapeDtypeStruct((DEPTH, COND_PAD, n), F32),
        compiler_params=_params("parallel", "parallel"),
        name="ada_mod",
    )(cvec, ada_w, ada_b.reshape(DEPTH, 1, n))
    mod = mod[:, :N_COND].reshape(DEPTH, N_COND, 6, D_MODEL).transpose(0, 2, 1, 3)
    return mod.reshape(DEPTH * 6 * N_COND, 1, D_MODEL)


def _mod_spec(layer, part, tm):
    base = (layer * 6 + part) * N_COND
    return pl.BlockSpec((1, 1, D_MODEL), lambda *idx: (base + _cond_of_tile(idx[0], tm), 0, 0))


def _group_specs(tm, width):
    p_tiles = N_PROMPT // tm
    return [pl.BlockSpec((tm, width), lambda *idx: (jnp.minimum(idx[0], p_tiles - 1), 0)),
            pl.BlockSpec((tm, width), lambda *idx: (jnp.maximum(idx[0] - p_tiles, 0), 0))]


def _in_proj_kernel(*refs, n_x, n_w):
    x_refs, (g_ref, sh_ref, sc_ref), refs = refs[:n_x], refs[n_x:n_x + 3], refs[n_x + 3:]
    w_refs, (o_ref, h_ref) = refs[:n_w], refs[n_w:]

    if len(x_refs) == 1:
        x = x_refs[0][...]
    else:
        x = jnp.where(pl.program_id(0) < N_PROMPT // TM, x_refs[0][...], x_refs[1][...])
    h = _rms(x) * g_ref[0]
    h_ref[...] = (h * (1.0 + sc_ref[0]) + sh_ref[0]).astype(BF16)
    col = 0
    for w_ref in w_refs:
        for c0 in range(0, w_ref.shape[1], IN_CHUNK):
            c1 = min(c0 + IN_CHUNK, w_ref.shape[1])
            o_ref[:, col + c0:col + c1] = _dot(h_ref[...], w_ref[:, c0:c1]).astype(o_ref.dtype)
        col += w_ref.shape[1]


def _in_proj(xs, norm_g, mod, layer, ws, w_layer):
    n = sum(w.shape[2] for w in ws)
    x_specs = [pl.BlockSpec((TM, D_MODEL), lambda i: (i, 0))] if len(xs) == 1 else _group_specs(TM, D_MODEL)
    return pl.pallas_call(
        functools.partial(_in_proj_kernel, n_x=len(xs), n_w=len(ws)),
        grid=(N_TOK // TM,),
        in_specs=x_specs + [
            pl.BlockSpec((1, 1, D_MODEL), lambda i: (layer, 0, 0)),
            _mod_spec(layer, 0, TM),
            _mod_spec(layer, 1, TM),
        ] + [pl.BlockSpec((None, D_MODEL, w.shape[2]), lambda i: (w_layer, 0, 0), pipeline_mode=pl.Buffered(1))
             for w in ws],
        out_specs=pl.BlockSpec((TM, n), lambda i: (i, 0)),
        out_shape=jax.ShapeDtypeStruct((N_TOK, n), BF16),
        scratch_shapes=[pltpu.VMEM((TM, D_MODEL), BF16)],
        compiler_params=_params("arbitrary"),
        name="in_proj",
    )(*xs, norm_g, mod, mod, *ws)


def _pool_bands():
    t = POOL_TM
    b0 = np.zeros((POOL_GROUPS, t, t), np.float32)
    bp = np.zeros((POOL_GROUPS, t, POOL_HALO), np.float32)
    bn = np.zeros((POOL_GROUPS, t, POOL_HALO), np.float32)
    for g, win in enumerate(POOL_WINDOWS):
        for r in range(t):
            for s in range(r - win // 2, r + win - win // 2):
                if s < 0:
                    bp[g, r, s + POOL_HALO] = 1.0
                elif s >= t:
                    bn[g, r, s - t] = 1.0
                else:
                    b0[g, r, s] = 1.0
    return jnp.asarray(b0, BF16), jnp.asarray(bp, BF16), jnp.asarray(bn, BF16)


def _pool_tiles(tiles, b0_ref, bp_ref, bn_ref, pw_ref, sc_ref):
    tiles_per_seq = DEC_SEQ // POOL_TM
    t = lax.broadcasted_iota(jnp.int32, (POOL_TM, POOL_GC), 0)
    sums, cnts, us = [], [], []
    for tile, u, up, un in tiles:
        in_prompt = tile < N_PROMPT // POOL_TM
        pos = (tile - N_PROMPT // POOL_TM) % tiles_per_seq
        is_start = jnp.logical_or(in_prompt, pos == 0)
        is_end = jnp.logical_or(in_prompt, pos == tiles_per_seq - 1)
        up = jnp.where(is_start, jnp.zeros_like(up), up)
        un = jnp.where(is_end, jnp.zeros_like(un), un)
        for g, win in enumerate(POOL_WINDOWS):
            cols = slice(g * POOL_GC, (g + 1) * POOL_GC)
            us.append(u[:, cols])
            sums.append(_dot(b0_ref[g], u[:, cols]) + _dot(bp_ref[g], up[:, cols]) + _dot(bn_ref[g], un[:, cols]))
            cut_lo = jnp.where(is_start, jnp.maximum(win // 2 - t, 0), 0)
            cut_hi = jnp.where(is_end, jnp.maximum(t + (win - win // 2) - POOL_TM, 0), 0)
            cnts.append((win - cut_lo - cut_hi).astype(F32))
    pooled = [(s / cnt - u.astype(F32)).astype(BF16) for s, cnt, u in zip(sums, cnts, us)]
    ys = [_dot(p, pw_ref[n % POOL_GROUPS]) for n, p in enumerate(pooled)]
    out = []
    for j in range(len(tiles)):
        out.append(jnp.concatenate(
            [(ys[j * POOL_GROUPS + g] * sc_ref[:, g * POOL_GC:(g + 1) * POOL_GC]).astype(BF16)
             for g in range(POOL_GROUPS)], axis=1))
    return out


def _rope_tables(t):
    nf = DK_RET // 4
    rows = t // GRID_W
    r = jnp.repeat(jnp.arange(rows), GRID_W).astype(F32)
    col = jnp.tile(jnp.arange(GRID_W), rows).astype(F32)
    inv = ROPE_BASE ** (-jnp.arange(nf, dtype=F32) / nf)
    ar, ac = r[:, None] * inv, col[:, None] * inv
    cos = jnp.concatenate([jnp.cos(ar), jnp.cos(ar), jnp.cos(ac), jnp.cos(ac)], axis=1)
    sin = jnp.concatenate([-jnp.sin(ar), jnp.sin(ar), -jnp.sin(ac), jnp.sin(ac)], axis=1)
    return cos, sin


def _ret_kernel(*refs, n_chunks, heads, use_rope, has_s0, state_slots, aliased):
    refs = list(refs)
    dec_ref, q_ref, k_ref, v_ref = refs[:4]
    refs = refs[4:]
    if use_rope:
        cos_ref, sin_ref = refs[:2]
        refs = refs[2:]
    if has_s0:
        s0_ref = refs[0]
        refs = refs[1:]
    if aliased:
        refs = refs[1:]
    o_ref = refs[0]
    refs = refs[1:]
    if state_slots is not None:
        st_ref = refs[0]
        refs = refs[1:]
    (qk_ref,) = refs

    c = RET_CHUNK
    nf = DK_RET // 4
    head0 = pl.program_id(1) * heads
    inter = has_s0 or n_chunks > 1

    row = lax.broadcasted_iota(jnp.int32, (c, DK_RET), 0).astype(F32)
    ii = lax.broadcasted_iota(jnp.int32, (c, c), 0)
    jj = lax.broadcasted_iota(jnp.int32, (c, c), 1)
    dist = (ii - jj).astype(F32)
    lane = lax.broadcasted_iota(jnp.int32, (c, DK_RET), 1)
    first_half = (lane % (2 * nf)) < nf

    def rope(x, r0):
        if not use_rope:
            return x
        partner = jnp.where(first_half, pltpu.roll(x, DK_RET - nf, axis=1), pltpu.roll(x, nf, axis=1))
        return x * cos_ref[pl.ds(r0, c), :] + partner * sin_ref[pl.ds(r0, c), :]

    for hh in range(heads):
        cols = slice(hh * DK_RET, (hh + 1) * DK_RET)

        def lam(d, shape):
            return _log_sigmoid(jnp.full(shape, dec_ref[d, head0 + hh], F32))

        lam_f, lam_b = lam(0, (c, DK_RET)), lam(1, (c, DK_RET))
        dq_f = jnp.exp((row + 1.0) * lam_f)
        dk_f = jnp.exp((c - 1.0 - row) * lam_f)
        dq_b = jnp.exp((c - row) * lam_b)
        dk_b = jnp.exp(row * lam_b)
        cdec_f = jnp.exp(float(c) * lam(0, (DK_RET, DV_RET)))
        cdec_b = jnp.exp(float(c) * lam(1, (DK_RET, DV_RET)))
        dmat = jnp.where(ii > jj, jnp.exp(dist * lam(0, (c, c))),
                         jnp.where(ii < jj, jnp.exp(-dist * lam(1, (c, c))), 2.0))

        group = 4 if n_chunks % 4 == 0 else 1

        def prepare(gi, carry):
            ns = [gi * group + j for j in range(group)]
            r0s = [n * c if isinstance(n, int) else pl.multiple_of(n * c, c) for n in ns]
            q = [rope(q_ref[pl.ds(r0, c), cols].astype(F32), r0) for r0 in r0s]
            k = [rope(k_ref[pl.ds(r0, c), cols].astype(F32) * (DK_RET ** -0.5), r0) for r0 in r0s]
            a = [(_dot_nt(q[j].astype(BF16), k[j].astype(BF16)) * dmat).astype(BF16) for j in range(group)]
            for j, r0 in enumerate(r0s):
                qk_ref[0, pl.ds(r0, c), cols] = (q[j] * dq_f).astype(BF16)
                qk_ref[1, pl.ds(r0, c), cols] = (q[j] * dq_b).astype(BF16)
                qk_ref[2, pl.ds(r0, c), cols] = (k[j] * dk_f).astype(BF16)
                qk_ref[3, pl.ds(r0, c), cols] = (k[j] * dk_b).astype(BF16)
            o = [_dot(a[j], v_ref[pl.ds(r0s[j], c), cols]) for j in range(group)]
            for j, r0 in enumerate(r0s):
                o_ref[pl.ds(r0, c), cols] = o[j]
            return carry

        def advance(n, states):
            ms = (n, n_chunks - 1 - n)
            r0s = [m * c if isinstance(m, int) else pl.multiple_of(m * c, c) for m in ms]
            upd = [_dot_tn(qk_ref[2 + d, pl.ds(r0s[d], c), cols], v_ref[pl.ds(r0s[d], c), cols]) for d in (0, 1)]
            out = []
            for d, cdec in ((0, cdec_f), (1, cdec_b)):
                if inter:
                    o_ref[pl.ds(r0s[d], c), cols] += _dot(qk_ref[d, pl.ds(r0s[d], c), cols],
                                                          states[d].astype(BF16))
                out.append(cdec * states[d] + upd[d])
            return tuple(out)

        _loop(n_chunks // group, prepare, 0)
        zero = jnp.zeros((DK_RET, DV_RET), F32)
        init = (s0_ref[0, 0, 0, hh], s0_ref[0, 0, 1, hh]) if has_s0 else (zero, zero)
        s_f, s_b = _loop(n_chunks, advance, init, unroll=4)
        if state_slots is not None:
            slot, n_slots = state_slots
            st_ref[0, slot, 0, hh] = s_f
            st_ref[0, slot, 1, hh] = s_b
            for other in range(n_slots):
                if other != slot:
                    st_ref[0, other, :, hh] = jnp.zeros((2, DK_RET, DV_RET), F32)


def _ret_scan(proj, decay, t, batch, heads, row_block0, rope, s0, s0_layer, state_buf, state_layer):
    w = heads * DK_RET
    col0 = POOL_WIDTH // w
    per = RET_WIDTH // w
    in_specs = [
        pl.BlockSpec(memory_space=pltpu.SMEM),
        pl.BlockSpec((t, w), lambda b, h: (row_block0 + b, col0 + h)),
        pl.BlockSpec((t, w), lambda b, h: (row_block0 + b, col0 + per + h)),
        pl.BlockSpec((t, w), lambda b, h: (row_block0 + b, col0 + 2 * per + h)),
    ]
    args = [decay, proj, proj, proj]
    if rope is not None:
        assert heads == 1
        in_specs += [pl.BlockSpec((t, DK_RET), lambda b, h: (0, 0))] * 2
        args += list(rope)
    if s0 is not None:
        in_specs.append(pl.BlockSpec((1, 1, 2, heads, DK_RET, DV_RET), lambda b, h: (b, s0_layer, 0, h, 0, 0)))
        args.append(s0)
    out_specs = [pl.BlockSpec((t, w), lambda b, h: (b, h))]
    out_shape = [jax.ShapeDtypeStruct((batch * t, RET_WIDTH), F32)]
    aliases, state_slots = {}, None
    if state_layer is not None:
        state_slots = (state_layer, N_EVEN) if state_buf is None else (0, 1)
        slot0 = 0 if state_buf is None else state_layer
        out_specs.append(pl.BlockSpec((1, state_slots[1], 2, heads, DK_RET, DV_RET),
                                      lambda b, h: (b, slot0, 0, h, 0, 0)))
        out_shape.append(jax.ShapeDtypeStruct((batch, N_EVEN, 2, H_RET, DK_RET, DV_RET), F32))
        if state_buf is not None:
            aliases = {len(args): 1}
            in_specs.append(pl.BlockSpec(memory_space=pl.ANY))
            args.append(state_buf)
    return pl.pallas_call(
        functools.partial(_ret_kernel, n_chunks=t // RET_CHUNK, heads=heads, use_rope=rope is not None,
                          has_s0=s0 is not None, state_slots=state_slots, aliased=bool(aliases)),
        grid=(batch, H_RET // heads),
        in_specs=in_specs,
        out_specs=out_specs,
        out_shape=out_shape,
        input_output_aliases=aliases,
        scratch_shapes=[pltpu.VMEM((4, t, w), BF16)],
        compiler_params=_params("parallel", "parallel"),
        name="ret_scan",
    )(*args)


def _gla_kernel(*refs, n_blocks, seqs, dvb, has_s0, state_slots, aliased):
    refs = list(refs)
    q_ref, k_ref, v_ref, z_ref, g2_ref, gb_ref = refs[:6]
    refs = refs[6:]
    if has_s0:
        s0_ref = refs[0]
        refs = refs[1:]
    if aliased:
        refs = refs[1:]
    o_ref = refs[0]
    refs = refs[1:]
    if state_slots is not None:
        st_ref = refs[0]
        refs = refs[1:]
    qd_ref, kt_ref, a_ref, dec_ref, s_ref = refs

    c = CHUNK
    blk = GLA_BLK
    cpb = blk // c
    tot_blocks = seqs * n_blocks

    @pl.when(pl.program_id(2) == 0)
    def _():
        ii = lax.broadcasted_iota(jnp.int32, (blk, blk), 0)
        jj = lax.broadcasted_iota(jnp.int32, (blk, blk), 1)
        same = (ii // c) == (jj // c)
        masks = (jnp.logical_and(same, ii >= jj), jnp.logical_and(same, ii <= jj))
        tris = tuple(jnp.tile(jnp.where(m, 1.0, 0.0).astype(BF16), (1, 2)) for m in masks)

        group = 2 if tot_blocks % 2 == 0 else 1
        chains = [(j, d) for j in range(group) for d in (0, 1)]

        def build(gi, carry):
            bis = [gi * group + j for j in range(group)]
            r0s = [bi * blk if isinstance(bi, int) else pl.multiple_of(bi * blk, blk) for bi in bis]
            q = [q_ref[pl.ds(r0, blk), :].astype(F32) * (DK_GLA ** -0.5) for r0 in r0s]
            k = [k_ref[pl.ds(r0, blk), :].astype(F32) for r0 in r0s]
            z = [_dot(z_ref[pl.ds(r0s[j], blk), :], g2_ref[0, d]) + gb_ref[0, d] for j, d in chains]
            lg = [jnp.concatenate(_split_bf16(_log_sigmoid(zz) * (LOG2E / GLA_TAU)), axis=0) for zz in z]
            g = [_dot(tris[d], lg[n]) for n, (j, d) in enumerate(chains)]
            qds, kds, cross = [], [], []
            for n, (j, d) in enumerate(chains):
                edge = c - 1 if d == 0 else 0
                gt = [g[n][ci * c + edge:ci * c + edge + 1, :] for ci in range(cpb)]
                g_tot = jnp.concatenate([jnp.broadcast_to(gt[ci], (c, DK_GLA)) for ci in range(cpb)], axis=0)
                qd_f32 = q[j] * jnp.exp2(g[n])
                qd = qd_f32.astype(BF16)
                kds.append((k[j] * jnp.exp2(-g[n])).astype(BF16))
                qds.append(qd)
                kt = k[j] * jnp.exp2(g_tot - g[n])
                scan = list(range(cpb)) if d == 0 else list(reversed(range(cpb)))
                pos = {ci: p for p, ci in enumerate(scan)}

                def span(lo, hi):
                    parts = [gt[scan[p]] for p in range(lo, hi)]
                    return functools.reduce(lambda x, y: x + y, parts) if parts else None

                def scaled(x, e):
                    return x if e is None else x * jnp.broadcast_to(jnp.exp2(e), x.shape)

                rows = lambda x, ci: x[ci * c:(ci + 1) * c, :]
                qd_ref[d, pl.ds(r0s[j], blk), :] = jnp.concatenate(
                    [scaled(rows(qd_f32, ci), span(0, pos[ci])) for ci in range(cpb)], axis=0).astype(BF16)
                kt_ref[d, pl.ds(r0s[j], blk), :] = jnp.concatenate(
                    [scaled(rows(kt, ci), span(pos[ci] + 1, cpb)) for ci in range(cpb)], axis=0).astype(BF16)
                dec_ref[d * tot_blocks + bis[j]] = jnp.exp2(jnp.broadcast_to(span(0, cpb), (8, DK_GLA)))
                zero_rows = jnp.zeros((c, DK_GLA), BF16)
                for p in range(1, cpb):
                    src = jnp.concatenate(
                        [scaled(rows(kt, cj), span(pos[cj] + 1, p)).astype(BF16) if pos[cj] < p else zero_rows
                         for cj in range(cpb)], axis=0)
                    cross.append((n, scan[p], qd[scan[p] * c:(scan[p] + 1) * c, :], src))
            a = [_dot_nt(qds[n], kds[n]) for n in range(len(chains))]
            cross = [(n, ci, _dot_nt(qrows, src)) for n, ci, qrows, src in cross]
            for j in range(group):
                total = jnp.where(masks[0], a[2 * j], 0.0) + jnp.where(masks[1], a[2 * j + 1], 0.0)
                for d in (0, 1):
                    by_chunk = {ci: sc for n, ci, sc in cross if n == 2 * j + d}
                    total = total + jnp.concatenate(
                        [by_chunk.get(ci, jnp.zeros((c, blk), F32)) for ci in range(cpb)], axis=0)
                a_ref[bis[j]] = total.astype(BF16)
            return carry

        _loop(tot_blocks // group, build, 0)

    def intra(bi, carry):
        r0 = bi * blk if isinstance(bi, int) else pl.multiple_of(bi * blk, blk)
        o_ref[pl.ds(r0, blk), :] = _dot(a_ref[bi], v_ref[pl.ds(r0, blk), :])
        return carry

    _loop(tot_blocks, intra, 0, unroll=2)

    runs = [(sq, d) for sq in range(seqs) for d in (0, 1)]
    for sq, d in runs:
        s_ref[2 * sq + d] = s0_ref[sq, 0, d, 0].T if has_s0 else jnp.zeros((dvb, DK_GLA), F32)

    inter = has_s0 or n_blocks > 1

    def advance(n, carry):
        ms = [sq * n_blocks + (n if d == 0 else n_blocks - 1 - n) for sq, d in runs]
        r0s = [m * blk if isinstance(m, int) else pl.multiple_of(m * blk, blk) for m in ms]
        upd = [_dot_tn(v_ref[pl.ds(r0s[n_], blk), :], kt_ref[d, pl.ds(r0s[n_], blk), :])
               for n_, (sq, d) in enumerate(runs)]
        for n_, (sq, d) in enumerate(runs):
            s = s_ref[2 * sq + d]
            if inter:
                o_ref[pl.ds(r0s[n_], blk), :] += _dot_nt(qd_ref[d, pl.ds(r0s[n_], blk), :], s.astype(BF16))
            s_ref[2 * sq + d] = s * dec_ref[d * tot_blocks + ms[n_]][0:1, :] + upd[n_]
        return carry

    _loop(n_blocks, advance, 0, unroll=2)
    if state_slots is not None:
        slot, n_slots = state_slots
        for sq, d in runs:
            st_ref[sq, slot, d, 0] = s_ref[2 * sq + d].T
            for other in range(n_slots):
                if other != slot:
                    st_ref[sq, other, d, 0] = jnp.zeros((DK_GLA, dvb), F32)


def _gla_scan(proj, g2, gb, t, batch, seqs, dvb, row_block0, s0, s0_layer, state_buf, state_layer):
    nd = DV_GLA // dvb
    kcol = GLA_DK_TOTAL // DK_GLA
    vcol = 2 * GLA_DK_TOTAL // dvb
    n_blocks = t // GLA_BLK
    rows = seqs * t
    in_specs = [
        pl.BlockSpec((rows, DK_GLA), lambda b, h, d: (row_block0 + b, h)),
        pl.BlockSpec((rows, DK_GLA), lambda b, h, d: (row_block0 + b, kcol + h)),
        pl.BlockSpec((rows, dvb), lambda b, h, d: (row_block0 + b, vcol + h * nd + d)),
        pl.BlockSpec((rows, 128), lambda b, h, d: (row_block0 + b, ODD_IN // 128)),
        pl.BlockSpec((1, 2, 128, DK_GLA), lambda b, h, d: (h, 0, 0, 0)),
        pl.BlockSpec((1, 2, 1, DK_GLA), lambda b, h, d: (h, 0, 0, 0)),
    ]
    args = [proj, proj, proj, proj, g2, gb]
    if s0 is not None:
        assert seqs == 1
        in_specs.append(pl.BlockSpec((1, 1, 2, 1, DK_GLA, dvb), lambda b, h, d: (b, s0_layer, 0, h, 0, d)))
        args.append(s0)
    out_specs = [pl.BlockSpec((rows, dvb), lambda b, h, d: (b, h * nd + d))]
    out_shape = [jax.ShapeDtypeStruct((batch * t, GLA_DV_TOTAL), F32)]
    aliases, state_slots = {}, None
    if state_layer is not None:
        state_slots = (state_layer, N_ODD) if state_buf is None else (0, 1)
        slot0 = 0 if state_buf is None else state_layer
        out_specs.append(pl.BlockSpec((seqs, state_slots[1], 2, 1, DK_GLA, dvb),
                                      lambda b, h, d: (b, slot0, 0, h, 0, d)))
        out_shape.append(jax.ShapeDtypeStruct((batch, N_ODD, 2, H_GLA, DK_GLA, DV_GLA), F32))
        if state_buf is not None:
            aliases = {len(args): 1}
            in_specs.append(pl.BlockSpec(memory_space=pl.ANY))
            args.append(state_buf)
    return pl.pallas_call(
        functools.partial(_gla_kernel, n_blocks=n_blocks, seqs=seqs, dvb=dvb, has_s0=s0 is not None,
                          state_slots=state_slots, aliased=bool(aliases)),
        grid=(batch // seqs, H_GLA, nd),
        in_specs=in_specs,
        out_specs=out_specs,
        out_shape=out_shape,
        input_output_aliases=aliases,
        scratch_shapes=[
            pltpu.VMEM((2, rows, DK_GLA), BF16),
            pltpu.VMEM((2, rows, DK_GLA), BF16),
            pltpu.VMEM((rows // GLA_BLK, GLA_BLK, GLA_BLK), BF16),
            pltpu.VMEM((2 * rows // GLA_BLK, 8, DK_GLA), F32),
            pltpu.VMEM((2 * seqs, dvb, DK_GLA), F32),
        ],
        compiler_params=_params("parallel", "parallel", "arbitrary"),
        name="gla_scan",
    )(*args)


def _out_proj_kernel(*refs, n_heads, width, with_pool):
    refs = list(refs)
    if with_pool:
        u_ref, up_ref, un_ref, b0_ref, bp_ref, bn_ref, pw_ref, psc_ref = refs[:8]
        refs = refs[8:]
    op_ref, os_ref, gate_ref, gn_ref, w_ref = refs[:5]
    *x_refs, g1_ref, out_ref = refs[5:]
    xp_ref, xs_ref = x_refs if len(x_refs) == 2 else (x_refs[0], x_refs[0])
    i = pl.program_id(0)
    in_prompt = i < N_PROMPT // TM

    pooled = []
    if with_pool:
        sub = TM // POOL_TM
        tiles = []
        for s in range(sub):
            r0 = s * POOL_TM
            up = up_ref[...] if s == 0 else u_ref[r0 - POOL_HALO:r0, :]
            un = un_ref[...] if s == sub - 1 else u_ref[r0 + POOL_TM:r0 + POOL_TM + POOL_HALO, :]
            tiles.append((i * sub + s, u_ref[r0:r0 + POOL_TM, :], up, un))
        pooled = [jnp.concatenate(_pool_tiles(tiles, b0_ref, bp_ref, bn_ref, pw_ref, psc_ref), axis=0)]

    def run(o_ref, x_ref):
        parts = list(pooled)
        for hd in range(n_heads):
            cols = slice(hd * width, (hd + 1) * width)
            gt = gate_ref[:, cols].astype(F32)
            parts.append((_rms(o_ref[:, cols]) * gn_ref[:, cols] * _silu(gt)).astype(BF16))
        a = jnp.concatenate(parts, axis=1)
        out_ref[...] = x_ref[...] + g1_ref[0] * _dot(a, w_ref[...])

    @pl.when(in_prompt)
    def _():
        run(op_ref, xp_ref)

    @pl.when(jnp.logical_not(in_prompt))
    def _():
        run(os_ref, xs_ref)


def _out_proj(xs, mod, layer, w, w_layer, proj, gate_col_block, norm_g, o_prompt, o_sample, n_heads, pool=None):
    width = o_prompt.shape[1]
    x_specs = [pl.BlockSpec((TM, D_MODEL), lambda i: (i, 0))] if len(xs) == 1 else _group_specs(TM, D_MODEL)
    in_specs = _group_specs(TM, width) + [
        pl.BlockSpec((TM, width), lambda i: (i, gate_col_block)),
        pl.BlockSpec((1, width), lambda i: (0, 0)),
        pl.BlockSpec((None, D_MODEL, D_MODEL), lambda i: (w_layer, 0, 0)),
    ] + x_specs + [_mod_spec(layer, 2, TM)]
    args = [o_prompt, o_sample, proj, norm_g.reshape(1, width), w, *xs, mod]
    if pool is not None:
        b0, bp, bn = _pool_bands()
        halo_blocks = TM // POOL_HALO
        last = N_TOK // POOL_HALO - 1
        full = lambda shape: pl.BlockSpec(shape, lambda i: (0,) * len(shape))
        in_specs = [
            pl.BlockSpec((TM, POOL_WIDTH), lambda i: (i, 0)),
            pl.BlockSpec((POOL_HALO, POOL_WIDTH), lambda i: (jnp.maximum(i * halo_blocks - 1, 0), 0)),
            pl.BlockSpec((POOL_HALO, POOL_WIDTH), lambda i: (jnp.minimum((i + 1) * halo_blocks, last), 0)),
            full(b0.shape), full(bp.shape), full(bn.shape),
            pl.BlockSpec((None,) + pool[0].shape[1:], lambda i: (w_layer, 0, 0, 0)),
            pl.BlockSpec((None, 1, POOL_WIDTH), lambda i: (w_layer, 0, 0)),
        ] + in_specs
        args = [proj, proj, proj, b0, bp, bn, pool[0], pool[1].reshape(N_EVEN, 1, POOL_WIDTH)] + args
    return pl.pallas_call(
        functools.partial(_out_proj_kernel, n_heads=n_heads, width=width // n_heads, with_pool=pool is not None),
        grid=(N_TOK // TM,),
        in_specs=in_specs,
        out_specs=pl.BlockSpec((TM, D_MODEL), lambda i: (i, 0)),
        out_shape=jax.ShapeDtypeStruct((N_TOK, D_MODEL), F32),
        compiler_params=_params("parallel"),
        name="out_proj",
    )(*args)


def _ffn_kernel(x_ref, xp_ref, xn_ref, gn_ref, sh_ref, sc_ref, gt_ref, wa_ref, wb_ref, cv_ref,
                wd_ref, fg_ref, *rest, final):
    if final:
        op_ref, o_ref, h_ref, hh_ref = rest
    else:
        o_ref, h_ref, hh_ref = rest
    i = pl.program_id(0)
    f = pl.program_id(1)
    tm = TM_FFN

    def modnorm(x):
        return (_rms(x) * gn_ref[0] * (1.0 + sc_ref[0]) + sh_ref[0]).astype(BF16)

    @pl.when(f == 0)
    def _():
        h_ref[...] = modnorm(x_ref[...])
        hh_ref[...] = modnorm(jnp.concatenate([xp_ref[...], xn_ref[...]], axis=0))
        o_ref[...] = jnp.zeros_like(o_ref)

    a = _dot(h_ref[...], wa_ref[...])
    b = _dot(h_ref[...], wb_ref[...])
    a_halo = _dot(hh_ref[...], wa_ref[...])

    seq = jnp.where(i * tm < N_PROMPT, SEQ, DEC_SEQ)
    t = lax.broadcasted_iota(jnp.int32, (tm, TF), 0)
    pos = (i * tm + t) & (seq - 1)
    a_prev = jnp.where(t == 0, a_halo[7:8, :], pltpu.roll(a, 1, axis=0))
    a_prev = jnp.where(pos == 0, 0.0, a_prev)
    a_next = jnp.where(t == tm - 1, a_halo[8:9, :], pltpu.roll(a, tm - 1, axis=0))
    a_next = jnp.where(pos == seq - 1, 0.0, a_next)
    cv = cv_ref[:, pl.ds(pl.multiple_of(f * TF, TF), TF)]
    conv = a_prev * cv[0:1, :] + a * cv[1:2, :] + a_next * cv[2:3, :] + cv[3:4, :]
    o_ref[...] += _dot((_silu(conv) * b).astype(BF16), wd_ref[...])

    @pl.when(f == pl.num_programs(1) - 1)
    def _():
        y = x_ref[...] + gt_ref[0] * o_ref[...]
        if not final:
            o_ref[...] = y
        else:
            y = _rms(y) * fg_ref[...]
            in_prompt = i < N_PROMPT // tm

            @pl.when(in_prompt)
            def _():
                op_ref[...] = y

            @pl.when(jnp.logical_not(in_prompt))
            def _():
                o_ref[...] = y


def _ffn(x, mod, layer, norm_g, w_up, conv_w, conv_b, w_down, final_g, final):
    m = x.shape[0]
    nf = D_FF // TF
    tm = TM_FFN
    halo = 8
    last = m // halo - 1
    if final:
        p_tiles = N_PROMPT // tm
        out_specs = [pl.BlockSpec((tm, D_MODEL), lambda i, f: (jnp.minimum(i, p_tiles - 1), 0),
                                  pipeline_mode=pl.Buffered(1)),
                     pl.BlockSpec((tm, D_MODEL), lambda i, f: (jnp.maximum(i - p_tiles, 0), 0),
                                  pipeline_mode=pl.Buffered(1))]
        out_shape = [jax.ShapeDtypeStruct((N_PROMPT, D_MODEL), F32),
                     jax.ShapeDtypeStruct((N_SAMPLE, D_MODEL), F32)]
    else:
        out_specs = pl.BlockSpec((tm, D_MODEL), lambda i, f: (i, 0))
        out_shape = jax.ShapeDtypeStruct((m, D_MODEL), F32)
    return pl.pallas_call(
        functools.partial(_ffn_kernel, final=final),
        grid=(m // tm, nf),
        in_specs=[
            pl.BlockSpec((tm, D_MODEL), lambda i, f: (i, 0), pipeline_mode=pl.Buffered(1)),
            pl.BlockSpec((halo, D_MODEL), lambda i, f: (jnp.maximum(i * (tm // halo) - 1, 0), 0)),
            pl.BlockSpec((halo, D_MODEL), lambda i, f: (jnp.minimum((i + 1) * (tm // halo), last), 0)),
            pl.BlockSpec((1, 1, D_MODEL), lambda i, f: (layer, 0, 0)),
            _mod_spec(layer, 3, tm),
            _mod_spec(layer, 4, tm),
            _mod_spec(layer, 5, tm),
            pl.BlockSpec((None, D_MODEL, TF), lambda i, f: (layer, 0, f)),
            pl.BlockSpec((None, D_MODEL, TF), lambda i, f: (layer, 0, nf + f)),
            pl.BlockSpec((None, CONV_W + 1, D_FF), lambda i, f: (layer, 0, 0)),
            pl.BlockSpec((None, TF, D_MODEL), lambda i, f: (layer, f, 0)),
            pl.BlockSpec((1, D_MODEL), lambda i, f: (0, 0)),
        ],
        out_specs=out_specs,
        out_shape=out_shape,
        scratch_shapes=[
            pltpu.VMEM((tm, D_MODEL), BF16),
            pltpu.VMEM((2 * halo, D_MODEL), BF16),
        ],
        compiler_params=_params("arbitrary", "arbitrary"),
        name="ffn",
    )(x, x, x, norm_g, mod, mod, mod, w_up, w_up,
      jnp.concatenate([conv_w, conv_b.reshape(DEPTH, 1, D_FF)], axis=1), w_down, final_g.reshape(1, D_MODEL))


def _gla_gate_weights(gw2, gb):
    r = GLA_GATE_RANK
    g2 = jnp.zeros((2, 128, GLA_DK_TOTAL), F32)
    g2 = g2.at[0, 0:r].set(gw2[0]).at[1, r:2 * r].set(gw2[1])
    g2 = g2.reshape(2, 128, H_GLA, DK_GLA).transpose(2, 0, 1, 3).astype(BF16)
    gbh = gb.reshape(2, H_GLA, 1, DK_GLA).transpose(1, 0, 2, 3)
    return g2, gbh


def kernel(x_prompt, x_sample, state_ret, state_gla, c, c_ctx, ada_w, ada_b, norm1_g, norm2_g,
           even_w_in, pool_w, pool_scale, ret_decay, ret_norm_g, even_w_out, odd_w_in, gla_gw1,
           gla_gw2, gla_gb, gla_norm_g, odd_w_out, ffn_w_up, ffn_conv_w, ffn_conv_b, ffn_w_down, final_g):
    xs = (x_prompt.reshape(N_PROMPT, D_MODEL), x_sample.reshape(N_SAMPLE, D_MODEL))
    cvec = jnp.concatenate([c_ctx[None, :], c, jnp.zeros((COND_PAD - N_COND, D_MODEL), F32)], axis=0)
    mod = _ada_mod(cvec, ada_w, ada_b)
    n1 = norm1_g.reshape(DEPTH, 1, D_MODEL)
    n2 = norm2_g.reshape(DEPTH, 1, D_MODEL)
    rope = _rope_tables(DEC_SEQ)
    p_blocks = N_PROMPT // DEC_SEQ

    w_even_in, w_even_out = even_w_in.astype(BF16), even_w_out.astype(BF16)
    w_odd_in, w_odd_out = odd_w_in.astype(BF16), odd_w_out.astype(BF16)
    w_up, w_down, w_pool = ffn_w_up.astype(BF16), ffn_w_down.astype(BF16), pool_w.astype(BF16)
    w_gate1 = jnp.concatenate([gla_gw1[:, 0], gla_gw1[:, 1],
                               jnp.zeros((N_ODD, D_MODEL, 128 - 2 * GLA_GATE_RANK), F32)], axis=2).astype(BF16)

    ret_states = gla_states = None
    for l in range(DEPTH):
        if l % 2 == 0:
            i = l // 2
            proj = _in_proj(xs, n1, mod, l, (w_even_in,), i)
            o_p, ret_states = _ret_scan(proj, ret_decay[i], SEQ, BATCH, H_RET, 0, None, None, 0, ret_states, i)
            (o_s,) = _ret_scan(proj, ret_decay[i], DEC_SEQ, DEC_BATCH, 1, p_blocks, rope, state_ret, i, None, None)
            x = _out_proj(xs, mod, l, w_even_out, i, proj, EVEN_IN // RET_WIDTH - 1,
                          ret_norm_g[i], o_p, o_s, H_RET, (w_pool, pool_scale))
        else:
            j = l // 2
            proj = _in_proj(xs, n1, mod, l, (w_odd_in, w_gate1), j)
            g2, gbh = _gla_gate_weights(gla_gw2[j], gla_gb[j])
            o_p, gla_states = _gla_scan(proj, g2, gbh, SEQ, BATCH, GLA_PROMPT_SEQS, DV_GLA, 0, None, 0,
                                        gla_states, j)
            (o_s,) = _gla_scan(proj, g2, gbh, DEC_SEQ, DEC_BATCH, 1, DV_GLA, p_blocks, state_gla, j,
                               None, None)
            x = _out_proj(xs, mod, l, w_odd_out, j, proj, ODD_IN // GLA_DV_TOTAL - 1,
                          gla_norm_g[j], o_p, o_s, H_GLA)
        xs = _ffn(x, mod, l, n2, w_up, ffn_conv_w, ffn_conv_b, w_down, final_g, l == DEPTH - 1)
        xs = tuple(xs) if l == DEPTH - 1 else (xs,)

    y_prompt = xs[0].reshape(BATCH, SEQ, D_MODEL)
    y_sample = xs[1].reshape(DEC_BATCH, DEC_SEQ, D_MODEL)
    return (y_prompt, y_sample, ret_states, gla_states)
```

```python
import functools

import numpy as np
import jax
import jax.numpy as jnp
from jax import lax
from jax.experimental import pallas as pl
from jax.experimental.pallas import tpu as pltpu

F32 = jnp.float32
BF16 = jnp.bfloat16

D_MODEL = 2048
BATCH = 16
SEQ = 256
DEPTH = 4
DEC_BATCH = 2
DEC_SEQ = 4096
GRID_W = 64
N_EVEN = (DEPTH + 1) // 2
N_ODD = DEPTH // 2
POOL_WIDTH = D_MODEL // 2
POOL_GROUPS = 4
POOL_GC = POOL_WIDTH // POOL_GROUPS
POOL_WINDOWS = (2, 4, 8, 16)
RET_WIDTH = D_MODEL // 2
H_RET = 8
DK_RET = RET_WIDTH // H_RET
DV_RET = RET_WIDTH // H_RET
ROPE_BASE = 10000.0
H_GLA = 4
GLA_DK_TOTAL = D_MODEL // 2
GLA_DV_TOTAL = D_MODEL
DK_GLA = GLA_DK_TOTAL // H_GLA
DV_GLA = GLA_DV_TOTAL // H_GLA
GLA_GATE_RANK = 16
GLA_TAU = 16.0
D_FF = 5632
CONV_W = 3
CHUNK = 64
EPS = 1e-6
EVEN_IN = POOL_WIDTH + 4 * RET_WIDTH
ODD_IN = 2 * GLA_DK_TOTAL + 2 * GLA_DV_TOTAL

N_PROMPT = BATCH * SEQ
N_SAMPLE = DEC_BATCH * DEC_SEQ
N_TOK = N_PROMPT + N_SAMPLE
N_COND = 1 + DEC_BATCH
COND_PAD = 8

VMEM_LIMIT = 56 * 1024 * 1024

TM = 512
IN_CHUNK = 512
TM_FFN = 1024
TF = 512
POOL_TM = 256
POOL_HALO = 16
ADA_TN = 1024
RET_CHUNK = 256
GLA_BLK = 256
GLA_PROMPT_SEQS = 2


def _params(*sem):
    return pltpu.CompilerParams(dimension_semantics=sem, vmem_limit_bytes=VMEM_LIMIT)


def _cond_of_tile(i, tm):
    r0 = i * tm
    return jnp.where(r0 < N_PROMPT, 0, 1 + (r0 - N_PROMPT) // DEC_SEQ)


LOG2E = 1.4426950408889634


def _log_sigmoid(x):
    return jnp.minimum(x, 0.0) - jnp.log(1.0 + jnp.exp2(jnp.abs(x) * (-LOG2E)))


def _silu(x):
    return x * jax.nn.sigmoid(x)


def _rms(x):
    return x * lax.rsqrt(jnp.mean(x * x, axis=-1, keepdims=True) + EPS)


def _dot(a, b):
    return jnp.dot(a, b, preferred_element_type=F32)


def _dot_nt(a, b):
    return lax.dot_general(a, b, (((1,), (1,)), ((), ())), preferred_element_type=F32)


def _dot_tn(a, b):
    return lax.dot_general(a, b, (((0,), (0,)), ((), ())), preferred_element_type=F32)


def _split_bf16(x):
    hi = x.astype(BF16)
    lo = (x - hi.astype(F32)).astype(BF16)
    return hi, lo


def _loop(n, body, init, unroll=1):
    if n == 1:
        return body(0, init)
    return lax.fori_loop(0, n, body, init, unroll=unroll)


def _ada_kernel(c_ref, w_ref, b_ref, o_ref):
    s = _silu(c_ref[...]).astype(BF16)
    o_ref[0] = _dot(s, w_ref[0].astype(BF16)) + b_ref[0]


def _ada_mod(cvec, ada_w, ada_b):
    n = 6 * D_MODEL
    mod = pl.pallas_call(
        _ada_kernel,
        grid=(DEPTH, n // ADA_TN),
        in_specs=[
            pl.BlockSpec((COND_PAD, D_MODEL), lambda l, j: (0, 0)),
            pl.BlockSpec((1, D_MODEL, ADA_TN), lambda l, j: (l, 0, j)),
            pl.BlockSpec((1, 1, ADA_TN), lambda l, j: (l, 0, j)),
        ],
        out_specs=pl.BlockSpec((1, COND_PAD, ADA_TN), lambda l, j: (l, 0, j)),
        out_shape=jax.ShapeDtypeStruct((DEPTH, COND_PAD, n), F32),
        compiler_params=_params("parallel", "parallel"),
        name="ada_mod",
    )(cvec, ada_w, ada_b.reshape(DEPTH, 1, n))
    mod = mod[:, :N_COND].reshape(DEPTH, N_COND, 6, D_MODEL).transpose(0, 2, 1, 3)
    return mod.reshape(DEPTH * 6 * N_COND, 1, D_MODEL)


def _mod_spec(layer, part, tm):
    base = (layer * 6 + part) * N_COND
    return pl.BlockSpec((1, 1, D_MODEL), lambda *idx: (base + _cond_of_tile(idx[0], tm), 0, 0))


def _group_specs(tm, width):
    p_tiles = N_PROMPT // tm
    return [pl.BlockSpec((tm, width), lambda *idx: (jnp.minimum(idx[0], p_tiles - 1), 0)),
            pl.BlockSpec((tm, width), lambda *idx: (jnp.maximum(idx[0] - p_tiles, 0), 0))]


def _in_proj_kernel(*refs, n_x, n_w):
    x_refs, (g_ref, sh_ref, sc_ref), refs = refs[:n_x], refs[n_x:n_x + 3], refs[n_x + 3:]
    w_refs, (o_ref, h_ref) = refs[:n_w], refs[n_w:]

    if len(x_refs) == 1:
        x = x_refs[0][...]
    else:
        x = jnp.where(pl.program_id(0) < N_PROMPT // TM, x_refs[0][...], x_refs[1][...])
    h = _rms(x) * g_ref[0]
    h_ref[...] = (h * (1.0 + sc_ref[0]) + sh_ref[0]).astype(BF16)
    col = 0
    for w_ref in w_refs:
        for c0 in range(0, w_ref.shape[1], IN_CHUNK):
            c1 = min(c0 + IN_CHUNK, w_ref.shape[1])
            o_ref[:, col + c0:col + c1] = _dot(h_ref[...], w_ref[:, c0:c1]).astype(o_ref.dtype)
        col += w_ref.shape[1]


def _in_proj(xs, norm_g, mod, layer, ws, w_layer):
    n = sum(w.shape[2] for w in ws)
    x_specs = [pl.BlockSpec((TM, D_MODEL), lambda i: (i, 0))] if len(xs) == 1 else _group_specs(TM, D_MODEL)
    return pl.pallas_call(
        functools.partial(_in_proj_kernel, n_x=len(xs), n_w=len(ws)),
        grid=(N_TOK // TM,),
        in_specs=x_specs + [
            pl.BlockSpec((1, 1, D_MODEL), lambda i: (layer, 0, 0)),
            _mod_spec(layer, 0, TM),
            _mod_spec(layer, 1, TM),
        ] + [pl.BlockSpec((None, D_MODEL, w.shape[2]), lambda i: (w_layer, 0, 0), pipeline_mode=pl.Buffered(1))
             for w in ws],
        out_specs=pl.BlockSpec((TM, n), lambda i: (i, 0)),
        out_shape=jax.ShapeDtypeStruct((N_TOK, n), BF16),
        scratch_shapes=[pltpu.VMEM((TM, D_MODEL), BF16)],
        compiler_params=_params("arbitrary"),
        name="in_proj",
    )(*xs, norm_g, mod, mod, *ws)


def _pool_bands():
    t = POOL_TM
    b0 = np.zeros((POOL_GROUPS, t, t), np.float32)
    bp = np.zeros((POOL_GROUPS, t, POOL_HALO), np.float32)
    bn = np.zeros((POOL_GROUPS, t, POOL_HALO), np.float32)
    for g, win in enumerate(POOL_WINDOWS):
        for r in range(t):
            for s in range(r - win // 2, r + win - win // 2):
                if s < 0:
                    bp[g, r, s + POOL_HALO] = 1.0
                elif s >= t:
                    bn[g, r, s - t] = 1.0
                else:
                    b0[g, r, s] = 1.0
    return jnp.asarray(b0, BF16), jnp.asarray(bp, BF16), jnp.asarray(bn, BF16)


def _pool_tiles(tiles, b0_ref, bp_ref, bn_ref, pw_ref, sc_ref):
    tiles_per_seq = DEC_SEQ // POOL_TM
    t = lax.broadcasted_iota(jnp.int32, (POOL_TM, POOL_GC), 0)
    sums, cnts, us = [], [], []
    for tile, u, up, un in tiles:
        in_prompt = tile < N_PROMPT // POOL_TM
        pos = (tile - N_PROMPT // POOL_TM) % tiles_per_seq
        is_start = jnp.logical_or(in_prompt, pos == 0)
        is_end = jnp.logical_or(in_prompt, pos == tiles_per_seq - 1)
        up = jnp.where(is_start, jnp.zeros_like(up), up)
        un = jnp.where(is_end, jnp.zeros_like(un), un)
        for g, win in enumerate(POOL_WINDOWS):
            cols = slice(g * POOL_GC, (g + 1) * POOL_GC)
            us.append(u[:, cols])
            sums.append(_dot(b0_ref[g], u[:, cols]) + _dot(bp_ref[g], up[:, cols]) + _dot(bn_ref[g], un[:, cols]))
            cut_lo = jnp.where(is_start, jnp.maximum(win // 2 - t, 0), 0)
            cut_hi = jnp.where(is_end, jnp.maximum(t + (win - win // 2) - POOL_TM, 0), 0)
            cnts.append((win - cut_lo - cut_hi).astype(F32))
    pooled = [(s / cnt - u.astype(F32)).astype(BF16) for s, cnt, u in zip(sums, cnts, us)]
    ys = [_dot(p, pw_ref[n % POOL_GROUPS]) for n, p in enumerate(pooled)]
    out = []
    for j in range(len(tiles)):
        out.append(jnp.concatenate(
            [(ys[j * POOL_GROUPS + g] * sc_ref[:, g * POOL_GC:(g + 1) * POOL_GC]).astype(BF16)
             for g in range(POOL_GROUPS)], axis=1))
    return out


def _rope_tables(t):
    nf = DK_RET // 4
    rows = t // GRID_W
    r = jnp.repeat(jnp.arange(rows), GRID_W).astype(F32)
    col = jnp.tile(jnp.arange(GRID_W), rows).astype(F32)
    inv = ROPE_BASE ** (-jnp.arange(nf, dtype=F32) / nf)
    ar, ac = r[:, None] * inv, col[:, None] * inv
    cos = jnp.concatenate([jnp.cos(ar), jnp.cos(ar), jnp.cos(ac), jnp.cos(ac)], axis=1)
    sin = jnp.concatenate([-jnp.sin(ar), jnp.sin(ar), -jnp.sin(ac), jnp.sin(ac)], axis=1)
    return cos, sin


def _ret_kernel(*refs, n_chunks, heads, use_rope, has_s0, state_slots, aliased):
    refs = list(refs)
    dec_ref, q_ref, k_ref, v_ref = refs[:4]
    refs = refs[4:]
    if use_rope:
        cos_ref, sin_ref = refs[:2]
        refs = refs[2:]
    if has_s0:
        s0_ref = refs[0]
        refs = refs[1:]
    if aliased:
        refs = refs[1:]
    o_ref = refs[0]
    refs = refs[1:]
    if state_slots is not None:
        st_ref = refs[0]
        refs = refs[1:]
    (qk_ref,) = refs

    c = RET_CHUNK
    nf = DK_RET // 4
    head0 = pl.program_id(1) * heads
    inter = has_s0 or n_chunks > 1

    row = lax.broadcasted_iota(jnp.int32, (c, DK_RET), 0).astype(F32)
    ii = lax.broadcasted_iota(jnp.int32, (c, c), 0)
    jj = lax.broadcasted_iota(jnp.int32, (c, c), 1)
    dist = (ii - jj).astype(F32)
    lane = lax.broadcasted_iota(jnp.int32, (c, DK_RET), 1)
    first_half = (lane % (2 * nf)) < nf

    def rope(x, r0):
        if not use_rope:
            return x
        partner = jnp.where(first_half, pltpu.roll(x, DK_RET - nf, axis=1), pltpu.roll(x, nf, axis=1))
        return x * cos_ref[pl.ds(r0, c), :] + partner * sin_ref[pl.ds(r0, c), :]

    for hh in range(heads):
        cols = slice(hh * DK_RET, (hh + 1) * DK_RET)

        def lam(d, shape):
            return _log_sigmoid(jnp.full(shape, dec_ref[d, head0 + hh], F32))

        lam_f, lam_b = lam(0, (c, DK_RET)), lam(1, (c, DK_RET))
        dq_f = jnp.exp((row + 1.0) * lam_f)
        dk_f = jnp.exp((c - 1.0 - row) * lam_f)
        dq_b = jnp.exp((c - row) * lam_b)
        dk_b = jnp.exp(row * lam_b)
        cdec_f = jnp.exp(float(c) * lam(0, (DK_RET, DV_RET)))
        cdec_b = jnp.exp(float(c) * lam(1, (DK_RET, DV_RET)))
        dmat = jnp.where(ii > jj, jnp.exp(dist * lam(0, (c, c))),
                         jnp.where(ii < jj, jnp.exp(-dist * lam(1, (c, c))), 2.0))

        group = 4 if n_chunks % 4 == 0 else 1

        def prepare(gi, carry):
            ns = [gi * group + j for j in range(group)]
            r0s = [n * c if isinstance(n, int) else pl.multiple_of(n * c, c) for n in ns]
            q = [rope(q_ref[pl.ds(r0, c), cols].astype(F32), r0) for r0 in r0s]
            k = [rope(k_ref[pl.ds(r0, c), cols].astype(F32) * (DK_RET ** -0.5), r0) for r0 in r0s]
            a = [(_dot_nt(q[j].astype(BF16), k[j].astype(BF16)) * dmat).astype(BF16) for j in range(group)]
            for j, r0 in enumerate(r0s):
                qk_ref[0, pl.ds(r0, c), cols] = (q[j] * dq_f).astype(BF16)
                qk_ref[1, pl.ds(r0, c), cols] = (q[j] * dq_b).astype(BF16)
                qk_ref[2, pl.ds(r0, c), cols] = (k[j] * dk_f).astype(BF16)
                qk_ref[3, pl.ds(r0, c), cols] = (k[j] * dk_b).astype(BF16)
            o = [_dot(a[j], v_ref[pl.ds(r0s[j], c), cols]) for j in range(group)]
            for j, r0 in enumerate(r0s):
                o_ref[pl.ds(r0, c), cols] = o[j]
            return carry

        def advance(n, states):
            ms = (n, n_chunks - 1 - n)
            r0s = [m * c if isinstance(m, int) else pl.multiple_of(m * c, c) for m in ms]
            upd = [_dot_tn(qk_ref[2 + d, pl.ds(r0s[d], c), cols], v_ref[pl.ds(r0s[d], c), cols]) for d in (0, 1)]
            out = []
            for d, cdec in ((0, cdec_f), (1, cdec_b)):
                if inter:
                    o_ref[pl.ds(r0s[d], c), cols] += _dot(qk_ref[d, pl.ds(r0s[d], c), cols],
                                                          states[d].astype(BF16))
                out.append(cdec * states[d] + upd[d])
            return tuple(out)

        _loop(n_chunks // group, prepare, 0)
        zero = jnp.zeros((DK_RET, DV_RET), F32)
        init = (s0_ref[0, 0, 0, hh], s0_ref[0, 0, 1, hh]) if has_s0 else (zero, zero)
        s_f, s_b = _loop(n_chunks, advance, init, unroll=8)
        if state_slots is not None:
            slot, n_slots = state_slots
            st_ref[0, slot, 0, hh] = s_f
            st_ref[0, slot, 1, hh] = s_b
            for other in range(n_slots):
                if other != slot:
                    st_ref[0, other, :, hh] = jnp.zeros((2, DK_RET, DV_RET), F32)


def _ret_scan(proj, decay, t, batch, heads, row_block0, rope, s0, s0_layer, state_buf, state_layer):
    w = heads * DK_RET
    col0 = POOL_WIDTH // w
    per = RET_WIDTH // w
    in_specs = [
        pl.BlockSpec(memory_space=pltpu.SMEM),
        pl.BlockSpec((t, w), lambda b, h: (row_block0 + b, col0 + h)),
        pl.BlockSpec((t, w), lambda b, h: (row_block0 + b, col0 + per + h)),
        pl.BlockSpec((t, w), lambda b, h: (row_block0 + b, col0 + 2 * per + h)),
    ]
    args = [decay, proj, proj, proj]
    if rope is not None:
        assert heads == 1
        in_specs += [pl.BlockSpec((t, DK_RET), lambda b, h: (0, 0))] * 2
        args += list(rope)
    if s0 is not None:
        in_specs.append(pl.BlockSpec((1, 1, 2, heads, DK_RET, DV_RET), lambda b, h: (b, s0_layer, 0, h, 0, 0)))
        args.append(s0)
    out_specs = [pl.BlockSpec((t, w), lambda b, h: (b, h))]
    out_shape = [jax.ShapeDtypeStruct((batch * t, RET_WIDTH), F32)]
    aliases, state_slots = {}, None
    if state_layer is not None:
        state_slots = (state_layer, N_EVEN) if state_buf is None else (0, 1)
        slot0 = 0 if state_buf is None else state_layer
        out_specs.append(pl.BlockSpec((1, state_slots[1], 2, heads, DK_RET, DV_RET),
                                      lambda b, h: (b, slot0, 0, h, 0, 0)))
        out_shape.append(jax.ShapeDtypeStruct((batch, N_EVEN, 2, H_RET, DK_RET, DV_RET), F32))
        if state_buf is not None:
            aliases = {len(args): 1}
            in_specs.append(pl.BlockSpec(memory_space=pl.ANY))
            args.append(state_buf)
    return pl.pallas_call(
        functools.partial(_ret_kernel, n_chunks=t // RET_CHUNK, heads=heads, use_rope=rope is not None,
                          has_s0=s0 is not None, state_slots=state_slots, aliased=bool(aliases)),
        grid=(batch, H_RET // heads),
        in_specs=in_specs,
        out_specs=out_specs,
        out_shape=out_shape,
        input_output_aliases=aliases,
        scratch_shapes=[pltpu.VMEM((4, t, w), BF16)],
        compiler_params=_params("parallel", "parallel"),
        name="ret_scan",
    )(*args)


def _gla_kernel(*refs, n_blocks, seqs, dvb, has_s0, state_slots, aliased):
    refs = list(refs)
    q_ref, k_ref, v_ref, z_ref, g2_ref, gb_ref = refs[:6]
    refs = refs[6:]
    if has_s0:
        s0_ref = refs[0]
        refs = refs[1:]
    if aliased:
        refs = refs[1:]
    o_ref = refs[0]
    refs = refs[1:]
    if state_slots is not None:
        st_ref = refs[0]
        refs = refs[1:]
    qd_ref, kt_ref, a_ref, dec_ref, s_ref = refs

    c = CHUNK
    blk = GLA_BLK
    cpb = blk // c
    tot_blocks = seqs * n_blocks

    @pl.when(pl.program_id(2) == 0)
    def _():
        ii = lax.broadcasted_iota(jnp.int32, (blk, blk), 0)
        jj = lax.broadcasted_iota(jnp.int32, (blk, blk), 1)
        same = (ii // c) == (jj // c)
        masks = (jnp.logical_and(same, ii >= jj), jnp.logical_and(same, ii <= jj))
        tris = tuple(jnp.tile(jnp.where(m, 1.0, 0.0).astype(BF16), (1, 2)) for m in masks)

        group = 2 if tot_blocks % 2 == 0 else 1
        chains = [(j, d) for j in range(group) for d in (0, 1)]

        def build(gi, carry):
            bis = [gi * group + j for j in range(group)]
            r0s = [bi * blk if isinstance(bi, int) else pl.multiple_of(bi * blk, blk) for bi in bis]
            q = [q_ref[pl.ds(r0, blk), :].astype(F32) * (DK_GLA ** -0.5) for r0 in r0s]
            k = [k_ref[pl.ds(r0, blk), :].astype(F32) for r0 in r0s]
            z = [_dot(z_ref[pl.ds(r0s[j], blk), :], g2_ref[0, d]) + gb_ref[0, d] for j, d in chains]
            lg = [jnp.concatenate(_split_bf16(_log_sigmoid(zz) * (LOG2E / GLA_TAU)), axis=0) for zz in z]
            g = [_dot(tris[d], lg[n]) for n, (j, d) in enumerate(chains)]
            qds, kds, cross = [], [], []
            for n, (j, d) in enumerate(chains):
                edge = c - 1 if d == 0 else 0
                gt = [g[n][ci * c + edge:ci * c + edge + 1, :] for ci in range(cpb)]
                g_tot = jnp.concatenate([jnp.broadcast_to(gt[ci], (c, DK_GLA)) for ci in range(cpb)], axis=0)
                qd_f32 = q[j] * jnp.exp2(g[n])
                qd = qd_f32.astype(BF16)
                kds.append((k[j] * jnp.exp2(-g[n])).astype(BF16))
                qds.append(qd)
                kt = k[j] * jnp.exp2(g_tot - g[n])
                scan = list(range(cpb)) if d == 0 else list(reversed(range(cpb)))
                pos = {ci: p for p, ci in enumerate(scan)}

                def span(lo, hi):
                    parts = [gt[scan[p]] for p in range(lo, hi)]
                    return functools.reduce(lambda x, y: x + y, parts) if parts else None

                def scaled(x, e):
                    return x if e is None else x * jnp.broadcast_to(jnp.exp2(e), x.shape)

                rows = lambda x, ci: x[ci * c:(ci + 1) * c, :]
                qd_ref[d, pl.ds(r0s[j], blk), :] = jnp.concatenate(
                    [scaled(rows(qd_f32, ci), span(0, pos[ci])) for ci in range(cpb)], axis=0).astype(BF16)
                kt_ref[d, pl.ds(r0s[j], blk), :] = jnp.concatenate(
                    [scaled(rows(kt, ci), span(pos[ci] + 1, cpb)) for ci in range(cpb)], axis=0).astype(BF16)
                dec_ref[d * tot_blocks + bis[j]] = jnp.exp2(jnp.broadcast_to(span(0, cpb), (8, DK_GLA)))
                zero_rows = jnp.zeros((c, DK_GLA), BF16)
                for p in range(1, cpb):
                    src = jnp.concatenate(
                        [scaled(rows(kt, cj), span(pos[cj] + 1, p)).astype(BF16) if pos[cj] < p else zero_rows
                         for cj in range(cpb)], axis=0)
                    cross.append((n, scan[p], qd[scan[p] * c:(scan[p] + 1) * c, :], src))
            a = [_dot_nt(qds[n], kds[n]) for n in range(len(chains))]
            cross = [(n, ci, _dot_nt(qrows, src)) for n, ci, qrows, src in cross]
            for j in range(group):
                total = jnp.where(masks[0], a[2 * j], 0.0) + jnp.where(masks[1], a[2 * j + 1], 0.0)
                for d in (0, 1):
                    by_chunk = {ci: sc for n, ci, sc in cross if n == 2 * j + d}
                    total = total + jnp.concatenate(
                        [by_chunk.get(ci, jnp.zeros((c, blk), F32)) for ci in range(cpb)], axis=0)
                a_ref[bis[j]] = total.astype(BF16)
            return carry

        _loop(tot_blocks // group, build, 0)

    def intra(bi, carry):
        r0 = bi * blk if isinstance(bi, int) else pl.multiple_of(bi * blk, blk)
        o_ref[pl.ds(r0, blk), :] = _dot(a_ref[bi], v_ref[pl.ds(r0, blk), :])
        return carry

    _loop(tot_blocks, intra, 0, unroll=2)

    runs = [(sq, d) for sq in range(seqs) for d in (0, 1)]
    for sq, d in runs:
        s_ref[2 * sq + d] = s0_ref[sq, 0, d, 0].T if has_s0 else jnp.zeros((dvb, DK_GLA), F32)

    inter = has_s0 or n_blocks > 1

    def advance(n, carry):
        ms = [sq * n_blocks + (n if d == 0 else n_blocks - 1 - n) for sq, d in runs]
        r0s = [m * blk if isinstance(m, int) else pl.multiple_of(m * blk, blk) for m in ms]
        upd = [_dot_tn(v_ref[pl.ds(r0s[n_], blk), :], kt_ref[d, pl.ds(r0s[n_], blk), :])
               for n_, (sq, d) in enumerate(runs)]
        for n_, (sq, d) in enumerate(runs):
            s = s_ref[2 * sq + d]
            if inter:
                o_ref[pl.ds(r0s[n_], blk), :] += _dot_nt(qd_ref[d, pl.ds(r0s[n_], blk), :], s.astype(BF16))
            s_ref[2 * sq + d] = s * dec_ref[d * tot_blocks + ms[n_]][0:1, :] + upd[n_]
        return carry

    _loop(n_blocks, advance, 0, unroll=4)
    if state_slots is not None:
        slot, n_slots = state_slots
        for sq, d in runs:
            st_ref[sq, slot, d, 0] = s_ref[2 * sq + d].T
            for other in range(n_slots):
                if other != slot:
                    st_ref[sq, other, d, 0] = jnp.zeros((DK_GLA, dvb), F32)


def _gla_scan(proj, g2, gb, t, batch, seqs, dvb, row_block0, s0, s0_layer, state_buf, state_layer):
    nd = DV_GLA // dvb
    kcol = GLA_DK_TOTAL // DK_GLA
    vcol = 2 * GLA_DK_TOTAL // dvb
    n_blocks = t // GLA_BLK
    rows = seqs * t
    in_specs = [
        pl.BlockSpec((rows, DK_GLA), lambda b, h, d: (row_block0 + b, h)),
        pl.BlockSpec((rows, DK_GLA), lambda b, h, d: (row_block0 + b, kcol + h)),
        pl.BlockSpec((rows, dvb), lambda b, h, d: (row_block0 + b, vcol + h * nd + d)),
        pl.BlockSpec((rows, 128), lambda b, h, d: (row_block0 + b, ODD_IN // 128)),
        pl.BlockSpec((1, 2, 128, DK_GLA), lambda b, h, d: (h, 0, 0, 0)),
        pl.BlockSpec((1, 2, 1, DK_GLA), lambda b, h, d: (h, 0, 0, 0)),
    ]
    args = [proj, proj, proj, proj, g2, gb]
    if s0 is not None:
        assert seqs == 1
        in_specs.append(pl.BlockSpec((1, 1, 2, 1, DK_GLA, dvb), lambda b, h, d: (b, s0_layer, 0, h, 0, d)))
        args.append(s0)
    out_specs = [pl.BlockSpec((rows, dvb), lambda b, h, d: (b, h * nd + d))]
    out_shape = [jax.ShapeDtypeStruct((batch * t, GLA_DV_TOTAL), F32)]
    aliases, state_slots = {}, None
    if state_layer is not None:
        state_slots = (state_layer, N_ODD) if state_buf is None else (0, 1)
        slot0 = 0 if state_buf is None else state_layer
        out_specs.append(pl.BlockSpec((seqs, state_slots[1], 2, 1, DK_GLA, dvb),
                                      lambda b, h, d: (b, slot0, 0, h, 0, d)))
        out_shape.append(jax.ShapeDtypeStruct((batch, N_ODD, 2, H_GLA, DK_GLA, DV_GLA), F32))
        if state_buf is not None:
            aliases = {len(args): 1}
            in_specs.append(pl.BlockSpec(memory_space=pl.ANY))
            args.append(state_buf)
    return pl.pallas_call(
        functools.partial(_gla_kernel, n_blocks=n_blocks, seqs=seqs, dvb=dvb, has_s0=s0 is not None,
                          state_slots=state_slots, aliased=bool(aliases)),
        grid=(batch // seqs, H_GLA, nd),
        in_specs=in_specs,
        out_specs=out_specs,
        out_shape=out_shape,
        input_output_aliases=aliases,
        scratch_shapes=[
            pltpu.VMEM((2, rows, DK_GLA), BF16),
            pltpu.VMEM((2, rows, DK_GLA), BF16),
            pltpu.VMEM((rows // GLA_BLK, GLA_BLK, GLA_BLK), BF16),
            pltpu.VMEM((2 * rows // GLA_BLK, 8, DK_GLA), F32),
            pltpu.VMEM((2 * seqs, dvb, DK_GLA), F32),
        ],
        compiler_params=_params("parallel", "parallel", "arbitrary"),
        name="gla_scan",
    )(*args)


def _out_proj_kernel(*refs, n_heads, width, with_pool):
    refs = list(refs)
    if with_pool:
        u_ref, up_ref, un_ref, b0_ref, bp_ref, bn_ref, pw_ref, psc_ref = refs[:8]
        refs = refs[8:]
    op_ref, os_ref, gate_ref, gn_ref, w_ref = refs[:5]
    *x_refs, g1_ref, out_ref = refs[5:]
    xp_ref, xs_ref = x_refs if len(x_refs) == 2 else (x_refs[0], x_refs[0])
    i = pl.program_id(0)
    in_prompt = i < N_PROMPT // TM

    pooled = []
    if with_pool:
        sub = TM // POOL_TM
        tiles = []
        for s in range(sub):
            r0 = s * POOL_TM
            up = up_ref[...] if s == 0 else u_ref[r0 - POOL_HALO:r0, :]
            un = un_ref[...] if s == sub - 1 else u_ref[r0 + POOL_TM:r0 + POOL_TM + POOL_HALO, :]
            tiles.append((i * sub + s, u_ref[r0:r0 + POOL_TM, :], up, un))
        pooled = [jnp.concatenate(_pool_tiles(tiles, b0_ref, bp_ref, bn_ref, pw_ref, psc_ref), axis=0)]

    def run(o_ref, x_ref):
        parts = list(pooled)
        for hd in range(n_heads):
            cols = slice(hd * width, (hd + 1) * width)
            gt = gate_ref[:, cols].astype(F32)
            parts.append((_rms(o_ref[:, cols]) * gn_ref[:, cols] * _silu(gt)).astype(BF16))
        a = jnp.concatenate(parts, axis=1)
        out_ref[...] = x_ref[...] + g1_ref[0] * _dot(a, w_ref[...])

    @pl.when(in_prompt)
    def _():
        run(op_ref, xp_ref)

    @pl.when(jnp.logical_not(in_prompt))
    def _():
        run(os_ref, xs_ref)


def _out_proj(xs, mod, layer, w, w_layer, proj, gate_col_block, norm_g, o_prompt, o_sample, n_heads, pool=None):
    width = o_prompt.shape[1]
    x_specs = [pl.BlockSpec((TM, D_MODEL), lambda i: (i, 0))] if len(xs) == 1 else _group_specs(TM, D_MODEL)
    in_specs = _group_specs(TM, width) + [
        pl.BlockSpec((TM, width), lambda i: (i, gate_col_block)),
        pl.BlockSpec((1, width), lambda i: (0, 0)),
        pl.BlockSpec((None, D_MODEL, D_MODEL), lambda i: (w_layer, 0, 0)),
    ] + x_specs + [_mod_spec(layer, 2, TM)]
    args = [o_prompt, o_sample, proj, norm_g.reshape(1, width), w, *xs, mod]
    if pool is not None:
        b0, bp, bn = _pool_bands()
        halo_blocks = TM // POOL_HALO
        last = N_TOK // POOL_HALO - 1
        full = lambda shape: pl.BlockSpec(shape, lambda i: (0,) * len(shape))
        in_specs = [
            pl.BlockSpec((TM, POOL_WIDTH), lambda i: (i, 0)),
            pl.BlockSpec((POOL_HALO, POOL_WIDTH), lambda i: (jnp.maximum(i * halo_blocks - 1, 0), 0)),
            pl.BlockSpec((POOL_HALO, POOL_WIDTH), lambda i: (jnp.minimum((i + 1) * halo_blocks, last), 0)),
            full(b0.shape), full(bp.shape), full(bn.shape),
            pl.BlockSpec((None,) + pool[0].shape[1:], lambda i: (w_layer, 0, 0, 0)),
            pl.BlockSpec((None, 1, POOL_WIDTH), lambda i: (w_layer, 0, 0)),
        ] + in_specs
        args = [proj, proj, proj, b0, bp, bn, pool[0], pool[1].reshape(N_EVEN, 1, POOL_WIDTH)] + args
    return pl.pallas_call(
        functools.partial(_out_proj_kernel, n_heads=n_heads, width=width // n_heads, with_pool=pool is not None),
        grid=(N_TOK // TM,),
        in_specs=in_specs,
        out_specs=pl.BlockSpec((TM, D_MODEL), lambda i: (i, 0)),
        out_shape=jax.ShapeDtypeStruct((N_TOK, D_MODEL), F32),
        compiler_params=_params("parallel"),
        name="out_proj",
    )(*args)


def _ffn_kernel(x_ref, xp_ref, xn_ref, gn_ref, sh_ref, sc_ref, gt_ref, wa_ref, wb_ref, cv_ref,
                wd_ref, fg_ref, *rest, final):
    if final:
        op_ref, o_ref, h_ref, hh_ref = rest
    else:
        o_ref, h_ref, hh_ref = rest
    i = pl.program_id(0)
    f = pl.program_id(1)
    tm = TM_FFN

    def modnorm(x):
        return (_rms(x) * gn_ref[0] * (1.0 + sc_ref[0]) + sh_ref[0]).astype(BF16)

    @pl.when(f == 0)
    def _():
        h_ref[...] = modnorm(x_ref[...])
        hh_ref[...] = modnorm(jnp.concatenate([xp_ref[...], xn_ref[...]], axis=0))
        o_ref[...] = jnp.zeros_like(o_ref)

    a = _dot(h_ref[...], wa_ref[...])
    b = _dot(h_ref[...], wb_ref[...])
    a_halo = _dot(hh_ref[...], wa_ref[...])

    seq = jnp.where(i * tm < N_PROMPT, SEQ, DEC_SEQ)
    t = lax.broadcasted_iota(jnp.int32, (tm, TF), 0)
    pos = (i * tm + t) & (seq - 1)
    a_prev = jnp.where(t == 0, a_halo[7:8, :], pltpu.roll(a, 1, axis=0))
    a_prev = jnp.where(pos == 0, 0.0, a_prev)
    a_next = jnp.where(t == tm - 1, a_halo[8:9, :], pltpu.roll(a, tm - 1, axis=0))
    a_next = jnp.where(pos == seq - 1, 0.0, a_next)
    cv = cv_ref[:, pl.ds(pl.multiple_of(f * TF, TF), TF)]
    conv = a_prev * cv[0:1, :] + a * cv[1:2, :] + a_next * cv[2:3, :] + cv[3:4, :]
    o_ref[...] += _dot((_silu(conv) * b).astype(BF16), wd_ref[...])

    @pl.when(f == pl.num_programs(1) - 1)
    def _():
        y = x_ref[...] + gt_ref[0] * o_ref[...]
        if not final:
            o_ref[...] = y
        else:
            y = _rms(y) * fg_ref[...]
            in_prompt = i < N_PROMPT // tm

            @pl.when(in_prompt)
            def _():
                op_ref[...] = y

            @pl.when(jnp.logical_not(in_prompt))
            def _():
                o_ref[...] = y


def _ffn(x, mod, layer, norm_g, w_up, conv_w, conv_b, w_down, final_g, final):
    m = x.shape[0]
    nf = D_FF // TF
    tm = TM_FFN
    halo = 8
    last = m // halo - 1
    if final:
        p_tiles = N_PROMPT // tm
        out_specs = [pl.BlockSpec((tm, D_MODEL), lambda i, f: (jnp.minimum(i, p_tiles - 1), 0),
                                  pipeline_mode=pl.Buffered(1)),
                     pl.BlockSpec((tm, D_MODEL), lambda i, f: (jnp.maximum(i - p_tiles, 0), 0),
                                  pipeline_mode=pl.Buffered(1))]
        out_shape = [jax.ShapeDtypeStruct((N_PROMPT, D_MODEL), F32),
                     jax.ShapeDtypeStruct((N_SAMPLE, D_MODEL), F32)]
    else:
        out_specs = pl.BlockSpec((tm, D_MODEL), lambda i, f: (i, 0))
        out_shape = jax.ShapeDtypeStruct((m, D_MODEL), F32)
    return pl.pallas_call(
        functools.partial(_ffn_kernel, final=final),
        grid=(m // tm, nf),
        in_specs=[
            pl.BlockSpec((tm, D_MODEL), lambda i, f: (i, 0), pipeline_mode=pl.Buffered(1)),
            pl.BlockSpec((halo, D_MODEL), lambda i, f: (jnp.maximum(i * (tm // halo) - 1, 0), 0)),
            pl.BlockSpec((halo, D_MODEL), lambda i, f: (jnp.minimum((i + 1) * (tm // halo), last), 0)),
            pl.BlockSpec((1, 1, D_MODEL), lambda i, f: (layer, 0, 0)),
            _mod_spec(layer, 3, tm),
            _mod_spec(layer, 4, tm),
            _mod_spec(layer, 5, tm),
            pl.BlockSpec((None, D_MODEL, TF), lambda i, f: (layer, 0, f)),
            pl.BlockSpec((None, D_MODEL, TF), lambda i, f: (layer, 0, nf + f)),
            pl.BlockSpec((None, CONV_W + 1, D_FF), lambda i, f: (layer, 0, 0)),
            pl.BlockSpec((None, TF, D_MODEL), lambda i, f: (layer, f, 0)),
            pl.BlockSpec((1, D_MODEL), lambda i, f: (0, 0)),
        ],
        out_specs=out_specs,
        out_shape=out_shape,
        scratch_shapes=[
            pltpu.VMEM((tm, D_MODEL), BF16),
            pltpu.VMEM((2 * halo, D_MODEL), BF16),
        ],
        compiler_params=_params("arbitrary", "arbitrary"),
        name="ffn",
    )(x, x, x, norm_g, mod, mod, mod, w_up, w_up,
      jnp.concatenate([conv_w, conv_b.reshape(DEPTH, 1, D_FF)], axis=1), w_down, final_g.reshape(1, D_MODEL))


def _gla_gate_weights(gw2, gb):
    r = GLA_GATE_RANK
    g2 = jnp.zeros((2, 128, GLA_DK_TOTAL), F32)
    g2 = g2.at[0, 0:r].set(gw2[0]).at[1, r:2 * r].set(gw2[1])
    g2 = g2.reshape(2, 128, H_GLA, DK_GLA).transpose(2, 0, 1, 3).astype(BF16)
    gbh = gb.reshape(2, H_GLA, 1, DK_GLA).transpose(1, 0, 2, 3)
    return g2, gbh


def kernel(x_prompt, x_sample, state_ret, state_gla, c, c_ctx, ada_w, ada_b, norm1_g, norm2_g,
           even_w_in, pool_w, pool_scale, ret_decay, ret_norm_g, even_w_out, odd_w_in, gla_gw1,
           gla_gw2, gla_gb, gla_norm_g, odd_w_out, ffn_w_up, ffn_conv_w, ffn_conv_b, ffn_w_down, final_g):
    xs = (x_prompt.reshape(N_PROMPT, D_MODEL), x_sample.reshape(N_SAMPLE, D_MODEL))
    cvec = jnp.concatenate([c_ctx[None, :], c, jnp.zeros((COND_PAD - N_COND, D_MODEL), F32)], axis=0)
    mod = _ada_mod(cvec, ada_w, ada_b)
    n1 = norm1_g.reshape(DEPTH, 1, D_MODEL)
    n2 = norm2_g.reshape(DEPTH, 1, D_MODEL)
    rope = _rope_tables(DEC_SEQ)
    p_blocks = N_PROMPT // DEC_SEQ

    w_even_in, w_even_out = even_w_in.astype(BF16), even_w_out.astype(BF16)
    w_odd_in, w_odd_out = odd_w_in.astype(BF16), odd_w_out.astype(BF16)
    w_up, w_down, w_pool = ffn_w_up.astype(BF16), ffn_w_down.astype(BF16), pool_w.astype(BF16)
    w_gate1 = jnp.concatenate([gla_gw1[:, 0], gla_gw1[:, 1],
                               jnp.zeros((N_ODD, D_MODEL, 128 - 2 * GLA_GATE_RANK), F32)], axis=2).astype(BF16)

    ret_states = gla_states = None
    for l in range(DEPTH):
        if l % 2 == 0:
            i = l // 2
            proj = _in_proj(xs, n1, mod, l, (w_even_in,), i)
            o_p, ret_states = _ret_scan(proj, ret_decay[i], SEQ, BATCH, H_RET, 0, None, None, 0, ret_states, i)
            (o_s,) = _ret_scan(proj, ret_decay[i], DEC_SEQ, DEC_BATCH, 1, p_blocks, rope, state_ret, i, None, None)
            x = _out_proj(xs, mod, l, w_even_out, i, proj, EVEN_IN // RET_WIDTH - 1,
                          ret_norm_g[i], o_p, o_s, H_RET, (w_pool, pool_scale))
        else:
            j = l // 2
            proj = _in_proj(xs, n1, mod, l, (w_odd_in, w_gate1), j)
            g2, gbh = _gla_gate_weights(gla_gw2[j], gla_gb[j])
            o_p, gla_states = _gla_scan(proj, g2, gbh, SEQ, BATCH, GLA_PROMPT_SEQS, DV_GLA, 0, None, 0,
                                        gla_states, j)
            (o_s,) = _gla_scan(proj, g2, gbh, DEC_SEQ, DEC_BATCH, 1, DV_GLA, p_blocks, state_gla, j,
                               None, None)
            x = _out_proj(xs, mod, l, w_odd_out, j, proj, ODD_IN // GLA_DV_TOTAL - 1,
                          gla_norm_g[j], o_p, o_s, H_GLA)
        xs = _ffn(x, mod, l, n2, w_up, ffn_conv_w, ffn_conv_b, w_down, final_g, l == DEPTH - 1)
        xs = tuple(xs) if l == DEPTH - 1 else (xs,)

    y_prompt = xs[0].reshape(BATCH, SEQ, D_MODEL)
    y_sample = xs[1].reshape(DEC_BATCH, DEC_SEQ, D_MODEL)
    return (y_prompt, y_sample, ret_states, gla_states)
```

```python
import functools

import numpy as np
import jax
import jax.numpy as jnp
from jax import lax
from jax.experimental import pallas as pl
from jax.experimental.pallas import tpu as pltpu

F32 = jnp.float32
BF16 = jnp.bfloat16

D_MODEL = 2048
BATCH = 16
SEQ = 256
DEPTH = 4
DEC_BATCH = 2
DEC_SEQ = 4096
GRID_W = 64
N_EVEN = (DEPTH + 1) // 2
N_ODD = DEPTH // 2
POOL_WIDTH = D_MODEL // 2
POOL_GROUPS = 4
POOL_GC = POOL_WIDTH // POOL_GROUPS
POOL_WINDOWS = (2, 4, 8, 16)
RET_WIDTH = D_MODEL // 2
H_RET = 8
DK_RET = RET_WIDTH // H_RET
DV_RET = RET_WIDTH // H_RET
ROPE_BASE = 10000.0
H_GLA = 4
GLA_DK_TOTAL = D_MODEL // 2
GLA_DV_TOTAL = D_MODEL
DK_GLA = GLA_DK_TOTAL // H_GLA
DV_GLA = GLA_DV_TOTAL // H_GLA
GLA_GATE_RANK = 16
GLA_TAU = 16.0
D_FF = 5632
CONV_W = 3
CHUNK = 64
EPS = 1e-6
EVEN_IN = POOL_WIDTH + 4 * RET_WIDTH
ODD_IN = 2 * GLA_DK_TOTAL + 2 * GLA_DV_TOTAL

N_PROMPT = BATCH * SEQ
N_SAMPLE = DEC_BATCH * DEC_SEQ
N_TOK = N_PROMPT + N_SAMPLE
N_COND = 1 + DEC_BATCH
COND_PAD = 8

VMEM_LIMIT = 56 * 1024 * 1024

TM = 512
IN_CHUNK = 512
TM_FFN = 1024
TF = 512
POOL_TM = 256
POOL_HALO = 16
ADA_TN = 1024
RET_CHUNK = 256
GLA_BLK = 256
GLA_PROMPT_SEQS = 2


def _params(*sem):
    return pltpu.CompilerParams(dimension_semantics=sem, vmem_limit_bytes=VMEM_LIMIT)


def _cond_of_tile(i, tm):
    r0 = i * tm
    return jnp.where(r0 < N_PROMPT, 0, 1 + (r0 - N_PROMPT) // DEC_SEQ)


LOG2E = 1.4426950408889634


def _log_sigmoid(x):
    return jnp.minimum(x, 0.0) - jnp.log(1.0 + jnp.exp2(jnp.abs(x) * (-LOG2E)))


def _silu(x):
    return x * jax.nn.sigmoid(x)


def _rms(x):
    return x * lax.rsqrt(jnp.mean(x * x, axis=-1, keepdims=True) + EPS)


def _dot(a, b):
    return jnp.dot(a, b, preferred_element_type=F32)


def _dot_nt(a, b):
    return lax.dot_general(a, b, (((1,), (1,)), ((), ())), preferred_element_type=F32)


def _dot_tn(a, b):
    return lax.dot_general(a, b, (((0,), (0,)), ((), ())), preferred_element_type=F32)


def _split_bf16(x):
    hi = x.astype(BF16)
    lo = (x - hi.astype(F32)).astype(BF16)
    return hi, lo


def _loop(n, body, init, unroll=1):
    if n == 1:
        return body(0, init)
    return lax.fori_loop(0, n, body, init, unroll=unroll)


def _ada_kernel(c_ref, w_ref, b_ref, o_ref):
    s = _silu(c_ref[...]).astype(BF16)
    o_ref[0] = _dot(s, w_ref[0].astype(BF16)) + b_ref[0]


def _ada_mod(cvec, ada_w, ada_b):
    n = 6 * D_MODEL
    mod = pl.pallas_call(
        _ada_kernel,
        grid=(DEPTH, n // ADA_TN),
        in_specs=[
            pl.BlockSpec((COND_PAD, D_MODEL), lambda l, j: (0, 0)),
            pl.BlockSpec((1, D_MODEL, ADA_TN), lambda l, j: (l, 0, j)),
            pl.BlockSpec((1, 1, ADA_TN), lambda l, j: (l, 0, j)),
        ],
        out_specs=pl.BlockSpec((1, COND_PAD, ADA_TN), lambda l, j: (l, 0, j)),
        out_shape=jax.ShapeDtypeStruct((DEPTH, COND_PAD, n), F32),
        compiler_params=_params("parallel", "parallel"),
        name="ada_mod",
    )(cvec, ada_w, ada_b.reshape(DEPTH, 1, n))
    mod = mod[:, :N_COND].reshape(DEPTH, N_COND, 6, D_MODEL).transpose(0, 2, 1, 3)
    return mod.reshape(DEPTH * 6 * N_COND, 1, D_MODEL)


def _mod_spec(layer, part, tm):
    base = (layer * 6 + part) * N_COND
    return pl.BlockSpec((1, 1, D_MODEL), lambda *idx: (base + _cond_of_tile(idx[0], tm), 0, 0))


def _group_specs(tm, width):
    p_tiles = N_PROMPT // tm
    return [pl.BlockSpec((tm, width), lambda *idx: (jnp.minimum(idx[0], p_tiles - 1), 0)),
            pl.BlockSpec((tm, width), lambda *idx: (jnp.maximum(idx[0] - p_tiles, 0), 0))]


def _in_proj_kernel(*refs, n_x, n_w):
    x_refs, (g_ref, sh_ref, sc_ref), refs = refs[:n_x], refs[n_x:n_x + 3], refs[n_x + 3:]
    w_refs, (o_ref, h_ref) = refs[:n_w], refs[n_w:]

    if len(x_refs) == 1:
        x = x_refs[0][...]
    else:
        x = jnp.where(pl.program_id(0) < N_PROMPT // TM, x_refs[0][...], x_refs[1][...])
    h = _rms(x) * g_ref[0]
    h_ref[...] = (h * (1.0 + sc_ref[0]) + sh_ref[0]).astype(BF16)
    col = 0
    for w_ref in w_refs:
        for c0 in range(0, w_ref.shape[1], IN_CHUNK):
            c1 = min(c0 + IN_CHUNK, w_ref.shape[1])
            o_ref[:, col + c0:col + c1] = _dot(h_ref[...], w_ref[:, c0:c1]).astype(o_ref.dtype)
        col += w_ref.shape[1]


def _in_proj(xs, norm_g, mod, layer, ws, w_layer):
    n = sum(w.shape[2] for w in ws)
    x_specs = [pl.BlockSpec((TM, D_MODEL), lambda i: (i, 0))] if len(xs) == 1 else _group_specs(TM, D_MODEL)
    return pl.pallas_call(
        functools.partial(_in_proj_kernel, n_x=len(xs), n_w=len(ws)),
        grid=(N_TOK // TM,),
        in_specs=x_specs + [
            pl.BlockSpec((1, 1, D_MODEL), lambda i: (layer, 0, 0)),
            _mod_spec(layer, 0, TM),
            _mod_spec(layer, 1, TM),
        ] + [pl.BlockSpec((None, D_MODEL, w.shape[2]), lambda i: (w_layer, 0, 0), pipeline_mode=pl.Buffered(1))
             for w in ws],
        out_specs=pl.BlockSpec((TM, n), lambda i: (i, 0)),
        out_shape=jax.ShapeDtypeStruct((N_TOK, n), BF16),
        scratch_shapes=[pltpu.VMEM((TM, D_MODEL), BF16)],
        compiler_params=_params("arbitrary"),
        name="in_proj",
    )(*xs, norm_g, mod, mod, *ws)


def _pool_bands():
    t = POOL_TM
    b0 = np.zeros((POOL_GROUPS, t, t), np.float32)
    bp = np.zeros((POOL_GROUPS, t, POOL_HALO), np.float32)
    bn = np.zeros((POOL_GROUPS, t, POOL_HALO), np.float32)
    for g, win in enumerate(POOL_WINDOWS):
        for r in range(t):
            for s in range(r - win // 2, r + win - win // 2):
                if s < 0:
                    bp[g, r, s + POOL_HALO] = 1.0
                elif s >= t:
                    bn[g, r, s - t] = 1.0
                else:
                    b0[g, r, s] = 1.0
    return jnp.asarray(b0, BF16), jnp.asarray(bp, BF16), jnp.asarray(bn, BF16)


def _pool_tiles(tiles, b0_ref, bp_ref, bn_ref, pw_ref, sc_ref):
    tiles_per_seq = DEC_SEQ // POOL_TM
    t = lax.broadcasted_iota(jnp.int32, (POOL_TM, POOL_GC), 0)
    sums, cnts, us = [], [], []
    for tile, u, up, un in tiles:
        in_prompt = tile < N_PROMPT // POOL_TM
        pos = (tile - N_PROMPT // POOL_TM) % tiles_per_seq
        is_start = jnp.logical_or(in_prompt, pos == 0)
        is_end = jnp.logical_or(in_prompt, pos == tiles_per_seq - 1)
        up = jnp.where(is_start, jnp.zeros_like(up), up)
        un = jnp.where(is_end, jnp.zeros_like(un), un)
        for g, win in enumerate(POOL_WINDOWS):
            cols = slice(g * POOL_GC, (g + 1) * POOL_GC)
            us.append(u[:, cols])
            sums.append(_dot(b0_ref[g], u[:, cols]) + _dot(bp_ref[g], up[:, cols]) + _dot(bn_ref[g], un[:, cols]))
            cut_lo = jnp.where(is_start, jnp.maximum(win // 2 - t, 0), 0)
            cut_hi = jnp.where(is_end, jnp.maximum(t + (win - win // 2) - POOL_TM, 0), 0)
            cnts.append((win - cut_lo - cut_hi).astype(F32))
    pooled = [(s / cnt - u.astype(F32)).astype(BF16) for s, cnt, u in zip(sums, cnts, us)]
    ys = [_dot(p, pw_ref[n % POOL_GROUPS]) for n, p in enumerate(pooled)]
    out = []
    for j in range(len(tiles)):
        out.append(jnp.concatenate(
            [(ys[j * POOL_GROUPS + g] * sc_ref[:, g * POOL_GC:(g + 1) * POOL_GC]).astype(BF16)
             for g in range(POOL_GROUPS)], axis=1))
    return out


def _rope_tables(t):
    nf = DK_RET // 4
    rows = t // GRID_W
    r = jnp.repeat(jnp.arange(rows), GRID_W).astype(F32)
    col = jnp.tile(jnp.arange(GRID_W), rows).astype(F32)
    inv = ROPE_BASE ** (-jnp.arange(nf, dtype=F32) / nf)
    ar, ac = r[:, None] * inv, col[:, None] * inv
    cos = jnp.concatenate([jnp.cos(ar), jnp.cos(ar), jnp.cos(ac), jnp.cos(ac)], axis=1)
    sin = jnp.concatenate([-jnp.sin(ar), jnp.sin(ar), -jnp.sin(ac), jnp.sin(ac)], axis=1)
    return cos, sin


def _ret_kernel(*refs, n_chunks, heads, use_rope, has_s0, state_slots, aliased):
    refs = list(refs)
    dec_ref, q_ref, k_ref, v_ref = refs[:4]
    refs = refs[4:]
    if use_rope:
        cos_ref, sin_ref = refs[:2]
        refs = refs[2:]
    if has_s0:
        s0_ref = refs[0]
        refs = refs[1:]
    if aliased:
        refs = refs[1:]
    o_ref = refs[0]
    refs = refs[1:]
    if state_slots is not None:
        st_ref = refs[0]
        refs = refs[1:]
    (qk_ref,) = refs

    c = RET_CHUNK
    nf = DK_RET // 4
    head0 = pl.program_id(1) * heads
    inter = has_s0 or n_chunks > 1

    row = lax.broadcasted_iota(jnp.int32, (c, DK_RET), 0).astype(F32)
    ii = lax.broadcasted_iota(jnp.int32, (c, c), 0)
    jj = lax.broadcasted_iota(jnp.int32, (c, c), 1)
    dist = (ii - jj).astype(F32)
    if use_rope:
        src = lax.broadcasted_iota(jnp.int32, (DK_RET, DK_RET), 0)
        dst = lax.broadcasted_iota(jnp.int32, (DK_RET, DK_RET), 1)
        partner = jnp.where((dst % (2 * nf)) < nf, dst + nf, dst - nf)
        perm = jnp.where(src == partner, 1.0, 0.0).astype(BF16)

    def rope(xb, r0):
        x = xb.astype(F32)
        if not use_rope:
            return x
        return x * cos_ref[pl.ds(r0, c), :] + _dot(xb, perm) * sin_ref[pl.ds(r0, c), :]

    for hh in range(heads):
        cols = slice(hh * DK_RET, (hh + 1) * DK_RET)

        def lam(d, shape):
            return _log_sigmoid(jnp.full(shape, dec_ref[d, head0 + hh], F32))

        lam_f, lam_b = lam(0, (c, DK_RET)), lam(1, (c, DK_RET))
        dq_f = jnp.exp((row + 1.0) * lam_f)
        dk_f = jnp.exp((c - 1.0 - row) * lam_f)
        dq_b = jnp.exp((c - row) * lam_b)
        dk_b = jnp.exp(row * lam_b)
        cdec_f = jnp.exp(float(c) * lam(0, (DK_RET, DV_RET)))
        cdec_b = jnp.exp(float(c) * lam(1, (DK_RET, DV_RET)))
        dmat = jnp.where(ii > jj, jnp.exp(dist * lam(0, (c, c))),
                         jnp.where(ii < jj, jnp.exp(-dist * lam(1, (c, c))), 2.0))

        group = 4 if n_chunks % 4 == 0 else 1

        def prepare(gi, carry):
            ns = [gi * group + j for j in range(group)]
            r0s = [n * c if isinstance(n, int) else pl.multiple_of(n * c, c) for n in ns]
            q = [rope(q_ref[pl.ds(r0, c), cols], r0) for r0 in r0s]
            k = [rope(k_ref[pl.ds(r0, c), cols], r0) * (DK_RET ** -0.5) for r0 in r0s]
            a = [(_dot_nt(q[j].astype(BF16), k[j].astype(BF16)) * dmat).astype(BF16) for j in range(group)]
            for j, r0 in enumerate(r0s):
                qk_ref[0, pl.ds(r0, c), cols] = (q[j] * dq_f).astype(BF16)
                qk_ref[1, pl.ds(r0, c), cols] = (q[j] * dq_b).astype(BF16)
                qk_ref[2, pl.ds(r0, c), cols] = (k[j] * dk_f).astype(BF16)
                qk_ref[3, pl.ds(r0, c), cols] = (k[j] * dk_b).astype(BF16)
            o = [_dot(a[j], v_ref[pl.ds(r0s[j], c), cols]) for j in range(group)]
            for j, r0 in enumerate(r0s):
                o_ref[pl.ds(r0, c), cols] = o[j]
            return carry

        def advance(n, states):
            ms = (n, n_chunks - 1 - n)
            r0s = [m * c if isinstance(m, int) else pl.multiple_of(m * c, c) for m in ms]
            upd = [_dot_tn(qk_ref[2 + d, pl.ds(r0s[d], c), cols], v_ref[pl.ds(r0s[d], c), cols]) for d in (0, 1)]
            out = []
            for d, cdec in ((0, cdec_f), (1, cdec_b)):
                if inter:
                    o_ref[pl.ds(r0s[d], c), cols] += _dot(qk_ref[d, pl.ds(r0s[d], c), cols],
                                                          states[d].astype(BF16))
                out.append(cdec * states[d] + upd[d])
            return tuple(out)

        _loop(n_chunks // group, prepare, 0)
        zero = jnp.zeros((DK_RET, DV_RET), F32)
        init = (s0_ref[0, 0, 0, hh], s0_ref[0, 0, 1, hh]) if has_s0 else (zero, zero)
        s_f, s_b = _loop(n_chunks, advance, init, unroll=4)
        if state_slots is not None:
            slot, n_slots = state_slots
            st_ref[0, slot, 0, hh] = s_f
            st_ref[0, slot, 1, hh] = s_b
            for other in range(n_slots):
                if other != slot:
                    st_ref[0, other, :, hh] = jnp.zeros((2, DK_RET, DV_RET), F32)


def _ret_scan(proj, decay, t, batch, heads, row_block0, rope, s0, s0_layer, state_buf, state_layer):
    w = heads * DK_RET
    col0 = POOL_WIDTH // w
    per = RET_WIDTH // w
    in_specs = [
        pl.BlockSpec(memory_space=pltpu.SMEM),
        pl.BlockSpec((t, w), lambda b, h: (row_block0 + b, col0 + h)),
        pl.BlockSpec((t, w), lambda b, h: (row_block0 + b, col0 + per + h)),
        pl.BlockSpec((t, w), lambda b, h: (row_block0 + b, col0 + 2 * per + h)),
    ]
    args = [decay, proj, proj, proj]
    if rope is not None:
        assert heads == 1
        in_specs += [pl.BlockSpec((t, DK_RET), lambda b, h: (0, 0))] * 2
        args += list(rope)
    if s0 is not None:
        in_specs.append(pl.BlockSpec((1, 1, 2, heads, DK_RET, DV_RET), lambda b, h: (b, s0_layer, 0, h, 0, 0)))
        args.append(s0)
    out_specs = [pl.BlockSpec((t, w), lambda b, h: (b, h))]
    out_shape = [jax.ShapeDtypeStruct((batch * t, RET_WIDTH), F32)]
    aliases, state_slots = {}, None
    if state_layer is not None:
        state_slots = (state_layer, N_EVEN) if state_buf is None else (0, 1)
        slot0 = 0 if state_buf is None else state_layer
        out_specs.append(pl.BlockSpec((1, state_slots[1], 2, heads, DK_RET, DV_RET),
                                      lambda b, h: (b, slot0, 0, h, 0, 0)))
        out_shape.append(jax.ShapeDtypeStruct((batch, N_EVEN, 2, H_RET, DK_RET, DV_RET), F32))
        if state_buf is not None:
            aliases = {len(args): 1}
            in_specs.append(pl.BlockSpec(memory_space=pl.ANY))
            args.append(state_buf)
    return pl.pallas_call(
        functools.partial(_ret_kernel, n_chunks=t // RET_CHUNK, heads=heads, use_rope=rope is not None,
                          has_s0=s0 is not None, state_slots=state_slots, aliased=bool(aliases)),
        grid=(batch, H_RET // heads),
        in_specs=in_specs,
        out_specs=out_specs,
        out_shape=out_shape,
        input_output_aliases=aliases,
        scratch_shapes=[pltpu.VMEM((4, t, w), BF16)],
        compiler_params=_params("parallel", "parallel"),
        name="ret_scan",
    )(*args)


def _gla_kernel(*refs, n_blocks, seqs, dvb, has_s0, state_slots, aliased):
    refs = list(refs)
    q_ref, k_ref, v_ref, z_ref, g2_ref, gb_ref = refs[:6]
    refs = refs[6:]
    if has_s0:
        s0_ref = refs[0]
        refs = refs[1:]
    if aliased:
        refs = refs[1:]
    o_ref = refs[0]
    refs = refs[1:]
    if state_slots is not None:
        st_ref = refs[0]
        refs = refs[1:]
    qd_ref, kt_ref, a_ref, dec_ref, s_ref = refs

    c = CHUNK
    blk = GLA_BLK
    cpb = blk // c
    tot_blocks = seqs * n_blocks

    @pl.when(pl.program_id(2) == 0)
    def _():
        ii = lax.broadcasted_iota(jnp.int32, (blk, blk), 0)
        jj = lax.broadcasted_iota(jnp.int32, (blk, blk), 1)
        same = (ii // c) == (jj // c)
        masks = (jnp.logical_and(same, ii >= jj), jnp.logical_and(same, ii <= jj))
        tris = tuple(jnp.tile(jnp.where(m, 1.0, 0.0).astype(BF16), (1, 2)) for m in masks)

        group = 2 if tot_blocks % 2 == 0 else 1
        chains = [(j, d) for j in range(group) for d in (0, 1)]

        def build(gi, carry):
            bis = [gi * group + j for j in range(group)]
            r0s = [bi * blk if isinstance(bi, int) else pl.multiple_of(bi * blk, blk) for bi in bis]
            q = [q_ref[pl.ds(r0, blk), :].astype(F32) * (DK_GLA ** -0.5) for r0 in r0s]
            k = [k_ref[pl.ds(r0, blk), :].astype(F32) for r0 in r0s]
            z = [_dot(z_ref[pl.ds(r0s[j], blk), :], g2_ref[0, d]) + gb_ref[0, d] for j, d in chains]
            lg = [jnp.concatenate(_split_bf16(_log_sigmoid(zz) * (LOG2E / GLA_TAU)), axis=0) for zz in z]
            g = [_dot(tris[d], lg[n]) for n, (j, d) in enumerate(chains)]
            qds, kds, cross = [], [], []
            for n, (j, d) in enumerate(chains):
                edge = c - 1 if d == 0 else 0
                gt = [g[n][ci * c + edge:ci * c + edge + 1, :] for ci in range(cpb)]
                g_tot = jnp.concatenate([jnp.broadcast_to(gt[ci], (c, DK_GLA)) for ci in range(cpb)], axis=0)
                qd_f32 = q[j] * jnp.exp2(g[n])
                qd = qd_f32.astype(BF16)
                kds.append((k[j] * jnp.exp2(-g[n])).astype(BF16))
                qds.append(qd)
                kt = k[j] * jnp.exp2(g_tot - g[n])
                scan = list(range(cpb)) if d == 0 else list(reversed(range(cpb)))
                pos = {ci: p for p, ci in enumerate(scan)}

                def span(lo, hi):
                    parts = [gt[scan[p]] for p in range(lo, hi)]
                    return functools.reduce(lambda x, y: x + y, parts) if parts else None

                def scaled(x, e):
                    return x if e is None else x * jnp.broadcast_to(jnp.exp2(e), x.shape)

                rows = lambda x, ci: x[ci * c:(ci + 1) * c, :]
                qd_ref[d, pl.ds(r0s[j], blk), :] = jnp.concatenate(
                    [scaled(rows(qd_f32, ci), span(0, pos[ci])) for ci in range(cpb)], axis=0).astype(BF16)
                kt_ref[d, pl.ds(r0s[j], blk), :] = jnp.concatenate(
                    [scaled(rows(kt, ci), span(pos[ci] + 1, cpb)) for ci in range(cpb)], axis=0).astype(BF16)
                dec_ref[d * tot_blocks + bis[j]] = jnp.exp2(jnp.broadcast_to(span(0, cpb), (8, DK_GLA)))
                zero_rows = jnp.zeros((c, DK_GLA), BF16)
                for p in range(1, cpb):
                    src = jnp.concatenate(
                        [scaled(rows(kt, cj), span(pos[cj] + 1, p)).astype(BF16) if pos[cj] < p else zero_rows
                         for cj in range(cpb)], axis=0)
                    cross.append((n, scan[p], qd[scan[p] * c:(scan[p] + 1) * c, :], src))
            a = [_dot_nt(qds[n], kds[n]) for n in range(len(chains))]
            cross = [(n, ci, _dot_nt(qrows, src)) for n, ci, qrows, src in cross]
            for j in range(group):
                total = jnp.where(masks[0], a[2 * j], 0.0) + jnp.where(masks[1], a[2 * j + 1], 0.0)
                for d in (0, 1):
                    by_chunk = {ci: sc for n, ci, sc in cross if n == 2 * j + d}
                    total = total + jnp.concatenate(
                        [by_chunk.get(ci, jnp.zeros((c, blk), F32)) for ci in range(cpb)], axis=0)
                a_ref[bis[j]] = total.astype(BF16)
            return carry

        _loop(tot_blocks // group, build, 0)

    def intra(bi, carry):
        r0 = bi * blk if isinstance(bi, int) else pl.multiple_of(bi * blk, blk)
        o_ref[pl.ds(r0, blk), :] = _dot(a_ref[bi], v_ref[pl.ds(r0, blk), :])
        return carry

    _loop(tot_blocks, intra, 0, unroll=2)

    runs = [(sq, d) for sq in range(seqs) for d in (0, 1)]
    for sq, d in runs:
        s_ref[2 * sq + d] = s0_ref[sq, 0, d, 0].T if has_s0 else jnp.zeros((dvb, DK_GLA), F32)

    inter = has_s0 or n_blocks > 1

    def advance(n, carry):
        ms = [sq * n_blocks + (n if d == 0 else n_blocks - 1 - n) for sq, d in runs]
        r0s = [m * blk if isinstance(m, int) else pl.multiple_of(m * blk, blk) for m in ms]
        upd = [_dot_tn(v_ref[pl.ds(r0s[n_], blk), :], kt_ref[d, pl.ds(r0s[n_], blk), :])
               for n_, (sq, d) in enumerate(runs)]
        for n_, (sq, d) in enumerate(runs):
            s = s_ref[2 * sq + d]
            if inter:
                o_ref[pl.ds(r0s[n_], blk), :] += _dot_nt(qd_ref[d, pl.ds(r0s[n_], blk), :], s.astype(BF16))
            s_ref[2 * sq + d] = s * dec_ref[d * tot_blocks + ms[n_]][0:1, :] + upd[n_]
        return carry

    _loop(n_blocks, advance, 0, unroll=2)
    if state_slots is not None:
        slot, n_slots = state_slots
        for sq, d in runs:
            st_ref[sq, slot, d, 0] = s_ref[2 * sq + d].T
            for other in range(n_slots):
                if other != slot:
                    st_ref[sq, other, d, 0] = jnp.zeros((DK_GLA, dvb), F32)


def _gla_scan(proj, g2, gb, t, batch, seqs, dvb, row_block0, s0, s0_layer, state_buf, state_layer):
    nd = DV_GLA // dvb
    kcol = GLA_DK_TOTAL // DK_GLA
    vcol = 2 * GLA_DK_TOTAL // dvb
    n_blocks = t // GLA_BLK
    rows = seqs * t
    in_specs = [
        pl.BlockSpec((rows, DK_GLA), lambda b, h, d: (row_block0 + b, h)),
        pl.BlockSpec((rows, DK_GLA), lambda b, h, d: (row_block0 + b, kcol + h)),
        pl.BlockSpec((rows, dvb), lambda b, h, d: (row_block0 + b, vcol + h * nd + d)),
        pl.BlockSpec((rows, 128), lambda b, h, d: (row_block0 + b, ODD_IN // 128)),
        pl.BlockSpec((1, 2, 128, DK_GLA), lambda b, h, d: (h, 0, 0, 0)),
        pl.BlockSpec((1, 2, 1, DK_GLA), lambda b, h, d: (h, 0, 0, 0)),
    ]
    args = [proj, proj, proj, proj, g2, gb]
    if s0 is not None:
        assert seqs == 1
        in_specs.append(pl.BlockSpec((1, 1, 2, 1, DK_GLA, dvb), lambda b, h, d: (b, s0_layer, 0, h, 0, d)))
        args.append(s0)
    out_specs = [pl.BlockSpec((rows, dvb), lambda b, h, d: (b, h * nd + d))]
    out_shape = [jax.ShapeDtypeStruct((batch * t, GLA_DV_TOTAL), F32)]
    aliases, state_slots = {}, None
    if state_layer is not None:
        state_slots = (state_layer, N_ODD) if state_buf is None else (0, 1)
        slot0 = 0 if state_buf is None else state_layer
        out_specs.append(pl.BlockSpec((seqs, state_slots[1], 2, 1, DK_GLA, dvb),
                                      lambda b, h, d: (b, slot0, 0, h, 0, d)))
        out_shape.append(jax.ShapeDtypeStruct((batch, N_ODD, 2, H_GLA, DK_GLA, DV_GLA), F32))
        if state_buf is not None:
            aliases = {len(args): 1}
            in_specs.append(pl.BlockSpec(memory_space=pl.ANY))
            args.append(state_buf)
    return pl.pallas_call(
        functools.partial(_gla_kernel, n_blocks=n_blocks, seqs=seqs, dvb=dvb, has_s0=s0 is not None,
                          state_slots=state_slots, aliased=bool(aliases)),
        grid=(batch // seqs, H_GLA, nd),
        in_specs=in_specs,
        out_specs=out_specs,
        out_shape=out_shape,
        input_output_aliases=aliases,
        scratch_shapes=[
            pltpu.VMEM((2, rows, DK_GLA), BF16),
            pltpu.VMEM((2, rows, DK_GLA), BF16),
            pltpu.VMEM((rows // GLA_BLK, GLA_BLK, GLA_BLK), BF16),
            pltpu.VMEM((2 * rows // GLA_BLK, 8, DK_GLA), F32),
            pltpu.VMEM((2 * seqs, dvb, DK_GLA), F32),
        ],
        compiler_params=_params("parallel", "parallel", "arbitrary"),
        name="gla_scan",
    )(*args)


def _out_proj_kernel(*refs, n_heads, width, with_pool):
    refs = list(refs)
    if with_pool:
        u_ref, up_ref, un_ref, b0_ref, bp_ref, bn_ref, pw_ref, psc_ref = refs[:8]
        refs = refs[8:]
    op_ref, os_ref, gate_ref, gn_ref, w_ref = refs[:5]
    *x_refs, g1_ref, out_ref = refs[5:]
    xp_ref, xs_ref = x_refs if len(x_refs) == 2 else (x_refs[0], x_refs[0])
    i = pl.program_id(0)
    in_prompt = i < N_PROMPT // TM

    pooled = []
    if with_pool:
        sub = TM // POOL_TM
        tiles = []
        for s in range(sub):
            r0 = s * POOL_TM
            up = up_ref[...] if s == 0 else u_ref[r0 - POOL_HALO:r0, :]
            un = un_ref[...] if s == sub - 1 else u_ref[r0 + POOL_TM:r0 + POOL_TM + POOL_HALO, :]
            tiles.append((i * sub + s, u_ref[r0:r0 + POOL_TM, :], up, un))
        pooled = [jnp.concatenate(_pool_tiles(tiles, b0_ref, bp_ref, bn_ref, pw_ref, psc_ref), axis=0)]

    def run(o_ref, x_ref):
        parts = list(pooled)
        for hd in range(n_heads):
            cols = slice(hd * width, (hd + 1) * width)
            gt = gate_ref[:, cols].astype(F32)
            parts.append((_rms(o_ref[:, cols]) * gn_ref[:, cols] * _silu(gt)).astype(BF16))
        a = jnp.concatenate(parts, axis=1)
        out_ref[...] = x_ref[...] + g1_ref[0] * _dot(a, w_ref[...])

    @pl.when(in_prompt)
    def _():
        run(op_ref, xp_ref)

    @pl.when(jnp.logical_not(in_prompt))
    def _():
        run(os_ref, xs_ref)


def _out_proj(xs, mod, layer, w, w_layer, proj, gate_col_block, norm_g, o_prompt, o_sample, n_heads, pool=None):
    width = o_prompt.shape[1]
    x_specs = [pl.BlockSpec((TM, D_MODEL), lambda i: (i, 0))] if len(xs) == 1 else _group_specs(TM, D_MODEL)
    in_specs = _group_specs(TM, width) + [
        pl.BlockSpec((TM, width), lambda i: (i, gate_col_block)),
        pl.BlockSpec((1, width), lambda i: (0, 0)),
        pl.BlockSpec((None, D_MODEL, D_MODEL), lambda i: (w_layer, 0, 0)),
    ] + x_specs + [_mod_spec(layer, 2, TM)]
    args = [o_prompt, o_sample, proj, norm_g.reshape(1, width), w, *xs, mod]
    if pool is not None:
        b0, bp, bn = _pool_bands()
        halo_blocks = TM // POOL_HALO
        last = N_TOK // POOL_HALO - 1
        full = lambda shape: pl.BlockSpec(shape, lambda i: (0,) * len(shape))
        in_specs = [
            pl.BlockSpec((TM, POOL_WIDTH), lambda i: (i, 0)),
            pl.BlockSpec((POOL_HALO, POOL_WIDTH), lambda i: (jnp.maximum(i * halo_blocks - 1, 0), 0)),
            pl.BlockSpec((POOL_HALO, POOL_WIDTH), lambda i: (jnp.minimum((i + 1) * halo_blocks, last), 0)),
            full(b0.shape), full(bp.shape), full(bn.shape),
            pl.BlockSpec((None,) + pool[0].shape[1:], lambda i: (w_layer, 0, 0, 0)),
            pl.BlockSpec((None, 1, POOL_WIDTH), lambda i: (w_layer, 0, 0)),
        ] + in_specs
        args = [proj, proj, proj, b0, bp, bn, pool[0], pool[1].reshape(N_EVEN, 1, POOL_WIDTH)] + args
    return pl.pallas_call(
        functools.partial(_out_proj_kernel, n_heads=n_heads, width=width // n_heads, with_pool=pool is not None),
        grid=(N_TOK // TM,),
        in_specs=in_specs,
        out_specs=pl.BlockSpec((TM, D_MODEL), lambda i: (i, 0)),
        out_shape=jax.ShapeDtypeStruct((N_TOK, D_MODEL), F32),
        compiler_params=_params("parallel"),
        name="out_proj",
    )(*args)


def _ffn_kernel(x_ref, xp_ref, xn_ref, gn_ref, sh_ref, sc_ref, gt_ref, wa_ref, wb_ref, cv_ref,
                wd_ref, fg_ref, *rest, final):
    if final:
        op_ref, o_ref, h_ref, hh_ref = rest
    else:
        o_ref, h_ref, hh_ref = rest
    i = pl.program_id(0)
    f = pl.program_id(1)
    tm = TM_FFN

    def modnorm(x):
        return (_rms(x) * gn_ref[0] * (1.0 + sc_ref[0]) + sh_ref[0]).astype(BF16)

    @pl.when(f == 0)
    def _():
        h_ref[...] = modnorm(x_ref[...])
        hh_ref[...] = modnorm(jnp.concatenate([xp_ref[...], xn_ref[...]], axis=0))
        o_ref[...] = jnp.zeros_like(o_ref)

    a = _dot(h_ref[...], wa_ref[...])
    b = _dot(h_ref[...], wb_ref[...])
    a_halo = _dot(hh_ref[...], wa_ref[...])

    seq = jnp.where(i * tm < N_PROMPT, SEQ, DEC_SEQ)
    t = lax.broadcasted_iota(jnp.int32, (tm, TF), 0)
    pos = (i * tm + t) & (seq - 1)
    a_prev = jnp.where(t == 0, a_halo[7:8, :], pltpu.roll(a, 1, axis=0))
    a_prev = jnp.where(pos == 0, 0.0, a_prev)
    a_next = jnp.where(t == tm - 1, a_halo[8:9, :], pltpu.roll(a, tm - 1, axis=0))
    a_next = jnp.where(pos == seq - 1, 0.0, a_next)
    cv = cv_ref[:, pl.ds(pl.multiple_of(f * TF, TF), TF)]
    conv = a_prev * cv[0:1, :] + a * cv[1:2, :] + a_next * cv[2:3, :] + cv[3:4, :]
    o_ref[...] += _dot((_silu(conv) * b).astype(BF16), wd_ref[...])

    @pl.when(f == pl.num_programs(1) - 1)
    def _():
        y = x_ref[...] + gt_ref[0] * o_ref[...]
        if not final:
            o_ref[...] = y
        else:
            y = _rms(y) * fg_ref[...]
            in_prompt = i < N_PROMPT // tm

            @pl.when(in_prompt)
            def _():
                op_ref[...] = y

            @pl.when(jnp.logical_not(in_prompt))
            def _():
                o_ref[...] = y


def _ffn(x, mod, layer, norm_g, w_up, conv_w, conv_b, w_down, final_g, final):
    m = x.shape[0]
    nf = D_FF // TF
    tm = TM_FFN
    halo = 8
    last = m // halo - 1
    if final:
        p_tiles = N_PROMPT // tm
        out_specs = [pl.BlockSpec((tm, D_MODEL), lambda i, f: (jnp.minimum(i, p_tiles - 1), 0),
                                  pipeline_mode=pl.Buffered(1)),
                     pl.BlockSpec((tm, D_MODEL), lambda i, f: (jnp.maximum(i - p_tiles, 0), 0),
                                  pipeline_mode=pl.Buffered(1))]
        out_shape = [jax.ShapeDtypeStruct((N_PROMPT, D_MODEL), F32),
                     jax.ShapeDtypeStruct((N_SAMPLE, D_MODEL), F32)]
    else:
        out_specs = pl.BlockSpec((tm, D_MODEL), lambda i, f: (i, 0))
        out_shape = jax.ShapeDtypeStruct((m, D_MODEL), F32)
    return pl.pallas_call(
        functools.partial(_ffn_kernel, final=final),
        grid=(m // tm, nf),
        in_specs=[
            pl.BlockSpec((tm, D_MODEL), lambda i, f: (i, 0), pipeline_mode=pl.Buffered(1)),
            pl.BlockSpec((halo, D_MODEL), lambda i, f: (jnp.maximum(i * (tm // halo) - 1, 0), 0)),
            pl.BlockSpec((halo, D_MODEL), lambda i, f: (jnp.minimum((i + 1) * (tm // halo), last), 0)),
            pl.BlockSpec((1, 1, D_MODEL), lambda i, f: (layer, 0, 0)),
            _mod_spec(layer, 3, tm),
            _mod_spec(layer, 4, tm),
            _mod_spec(layer, 5, tm),
            pl.BlockSpec((None, D_MODEL, TF), lambda i, f: (layer, 0, f)),
            pl.BlockSpec((None, D_MODEL, TF), lambda i, f: (layer, 0, nf + f)),
            pl.BlockSpec((None, CONV_W + 1, D_FF), lambda i, f: (layer, 0, 0)),
            pl.BlockSpec((None, TF, D_MODEL), lambda i, f: (layer, f, 0)),
            pl.BlockSpec((1, D_MODEL), lambda i, f: (0, 0)),
        ],
        out_specs=out_specs,
        out_shape=out_shape,
        scratch_shapes=[
            pltpu.VMEM((tm, D_MODEL), BF16),
            pltpu.VMEM((2 * halo, D_MODEL), BF16),
        ],
        compiler_params=_params("arbitrary", "arbitrary"),
        name="ffn",
    )(x, x, x, norm_g, mod, mod, mod, w_up, w_up,
      jnp.concatenate([conv_w, conv_b.reshape(DEPTH, 1, D_FF)], axis=1), w_down, final_g.reshape(1, D_MODEL))


def _gla_gate_weights(gw2, gb):
    r = GLA_GATE_RANK
    g2 = jnp.zeros((2, 128, GLA_DK_TOTAL), F32)
    g2 = g2.at[0, 0:r].set(gw2[0]).at[1, r:2 * r].set(gw2[1])
    g2 = g2.reshape(2, 128, H_GLA, DK_GLA).transpose(2, 0, 1, 3).astype(BF16)
    gbh = gb.reshape(2, H_GLA, 1, DK_GLA).transpose(1, 0, 2, 3)
    return g2, gbh


def kernel(x_prompt, x_sample, state_ret, state_gla, c, c_ctx, ada_w, ada_b, norm1_g, norm2_g,
           even_w_in, pool_w, pool_scale, ret_decay, ret_norm_g, even_w_out, odd_w_in, gla_gw1,
           gla_gw2, gla_gb, gla_norm_g, odd_w_out, ffn_w_up, ffn_conv_w, ffn_conv_b, ffn_w_down, final_g):
    xs = (x_prompt.reshape(N_PROMPT, D_MODEL), x_sample.reshape(N_SAMPLE, D_MODEL))
    cvec = jnp.concatenate([c_ctx[None, :], c, jnp.zeros((COND_PAD - N_COND, D_MODEL), F32)], axis=0)
    mod = _ada_mod(cvec, ada_w, ada_b)
    n1 = norm1_g.reshape(DEPTH, 1, D_MODEL)
    n2 = norm2_g.reshape(DEPTH, 1, D_MODEL)
    rope = _rope_tables(DEC_SEQ)
    p_blocks = N_PROMPT // DEC_SEQ

    w_even_in, w_even_out = even_w_in.astype(BF16), even_w_out.astype(BF16)
    w_odd_in, w_odd_out = odd_w_in.astype(BF16), odd_w_out.astype(BF16)
    w_up, w_down, w_pool = ffn_w_up.astype(BF16), ffn_w_down.astype(BF16), pool_w.astype(BF16)
    w_gate1 = jnp.concatenate([gla_gw1[:, 0], gla_gw1[:, 1],
                               jnp.zeros((N_ODD, D_MODEL, 128 - 2 * GLA_GATE_RANK), F32)], axis=2).astype(BF16)

    ret_states = gla_states = None
    for l in range(DEPTH):
        if l % 2 == 0:
            i = l // 2
            proj = _in_proj(xs, n1, mod, l, (w_even_in,), i)
            o_p, ret_states = _ret_scan(proj, ret_decay[i], SEQ, BATCH, H_RET, 0, None, None, 0, ret_states, i)
            (o_s,) = _ret_scan(proj, ret_decay[i], DEC_SEQ, DEC_BATCH, 1, p_blocks, rope, state_ret, i, None, None)
            x = _out_proj(xs, mod, l, w_even_out, i, proj, EVEN_IN // RET_WIDTH - 1,
                          ret_norm_g[i], o_p, o_s, H_RET, (w_pool, pool_scale))
        else:
            j = l // 2
            proj = _in_proj(xs, n1, mod, l, (w_odd_in, w_gate1), j)
            g2, gbh = _gla_gate_weights(gla_gw2[j], gla_gb[j])
            o_p, gla_states = _gla_scan(proj, g2, gbh, SEQ, BATCH, GLA_PROMPT_SEQS, DV_GLA, 0, None, 0,
                                        gla_states, j)
            (o_s,) = _gla_scan(proj, g2, gbh, DEC_SEQ, DEC_BATCH, 1, DV_GLA, p_blocks, state_gla, j,
                               None, None)
            x = _out_proj(xs, mod, l, w_odd_out, j, proj, ODD_IN // GLA_DV_TOTAL - 1,
                          gla_norm_g[j], o_p, o_s, H_GLA)
        xs = _ffn(x, mod, l, n2, w_up, ffn_conv_w, ffn_conv_b, w_down, final_g, l == DEPTH - 1)
        xs = tuple(xs) if l == DEPTH - 1 else (xs,)

    y_prompt = xs[0].reshape(BATCH, SEQ, D_MODEL)
    y_sample = xs[1].reshape(DEC_BATCH, DEC_SEQ, D_MODEL)
    return (y_prompt, y_sample, ret_states, gla_states)
```
